```python
import math
import jax
import jax.numpy as jnp
from jax import lax
import numpy as np

D_MODEL = 2048
BATCH = 1
SEQ = 8192
DEPTH = 2
DEC_BATCH = 16
DEC_SEQ = 32
PAST_LEN = 2048

CHUNK = 64
N_A_LAYERS = DEPTH // 2
N_B_LAYERS = DEPTH - N_A_LAYERS
EPS = 1e-6
D_INNER = 2 * D_MODEL
SSM_HEAD_DIM = 64
SSM_HEADS = D_INNER // SSM_HEAD_DIM
SSM_STATE = 128
SSM_GROUPS = 8
CONV_W = 4
CONV_DIM = D_INNER + 2 * SSM_GROUPS * SSM_STATE
SSM_IN_DIM = 2 * D_INNER + 2 * SSM_GROUPS * SSM_STATE + SSM_HEADS
SSD_CHUNK = CHUNK
MLA_HEADS = 16
Q_RANK = 512
KV_RANK = 512
QK_NOPE = 128
QK_ROPE = 64
V_DIM = 128
ROPE_THETA = 10000.0
ATTN_SCALE = (QK_NOPE + QK_ROPE) ** -0.5
Q_BLOCK = 128
MOE_GROUPS = 4
EXPERTS_PER_GROUP = 8
N_EXPERTS = MOE_GROUPS * EXPERTS_PER_GROUP
TOP_K = 2
D_EXPERT = 512
MOE_BLOCK = 128
PLE_DIM = 256

kernel_name = 'yoco_ssd_mla_hmoe_stream_step'


def rmsnorm(x, g):
    xf = x.astype(jnp.float32)
    y = xf * lax.rsqrt(jnp.mean(xf * xf, axis=-1, keepdims=True) + EPS)
    return (y * g.astype(jnp.float32)).astype(x.dtype)


def gated_rmsnorm(y, z, g):
    v = y * jax.nn.silu(z.astype(jnp.float32))
    vg = v.reshape(v.shape[:-1] + (SSM_GROUPS, D_INNER // SSM_GROUPS))
    vg = vg * lax.rsqrt(jnp.mean(vg * vg, axis=-1, keepdims=True) + EPS)
    return (vg.reshape(v.shape) * g.astype(jnp.float32)).astype(z.dtype)


def rope(x, pos):
    half = x.shape[-1] // 2
    inv = ROPE_THETA ** (-jnp.arange(half, dtype=jnp.float32) / half)
    ang = pos.astype(jnp.float32)[:, None] * inv[None, :]
    shape = (1, pos.shape[0]) + (1,) * (x.ndim - 3) + (half,)
    cos = jnp.cos(ang).reshape(shape)
    sin = jnp.sin(ang).reshape(shape)
    x1 = x[..., :half].astype(jnp.float32)
    x2 = x[..., half:].astype(jnp.float32)
    return jnp.concatenate([x1 * cos - x2 * sin, x1 * sin + x2 * cos], axis=-1).astype(x.dtype)


def ssd_scan(xs, dt, a, bm, cm, h0):
    b, L, H, P = xs.shape
    G, N = bm.shape[-2:]
    R = H // G
    lc = min(SSD_CHUNK, L)
    nc = L // lc
    f32 = jnp.float32
    xdt = (xs.astype(f32) * dt[..., None]).reshape(b, nc, lc, G, R, P)
    acum = jnp.cumsum((dt * a).reshape(b, nc, lc, G, R), axis=2)
    bc = bm.astype(f32).reshape(b, nc, lc, G, N)
    cc = cm.astype(f32).reshape(b, nc, lc, G, N)
    causal = jnp.tril(jnp.ones((lc, lc), bool))[:, :, None, None]
    decay = jnp.exp(jnp.where(causal, acum[:, :, :, None] - acum[:, :, None, :], -jnp.inf))
    cb = jnp.einsum('bclgn,bcsgn->bclsg', cc, bc)
    y_diag = jnp.einsum('bclsgr,bcsgrp->bclgrp', cb[..., None] * decay, xdt)

    def chunk_step(h, inp):
        bk, ck, xk, ak = inp
        y_off = jnp.einsum('blgn,bgrpn->blgrp', ck, h) * jnp.exp(ak)[..., None]
        w_end = jnp.exp(ak[:, -1:] - ak)
        h = h * jnp.exp(ak[:, -1])[..., None, None] + jnp.einsum('blgn,blgrp->bgrpn', bk, xk * w_end[..., None])
        return h, y_off

    to_c = lambda t: jnp.moveaxis(t, 1, 0)
    h_last, y_off = lax.scan(chunk_step, h0.astype(f32).reshape(b, G, R, P, N),
                             (to_c(bc), to_c(cc), to_c(xdt), to_c(acum)))
    y = (y_diag + jnp.moveaxis(y_off, 0, 1)).reshape(b, L, H, P)
    return y, h_last.reshape(b, H, P, N)


def mamba_mixer(xn, conv_state, ssm_state, w_in, w_conv, b_conv, dt_bias, a_log, d_skip, g_norm, w_out):
    b, L, _ = xn.shape
    zxbcdt = xn @ w_in
    z = zxbcdt[..., :D_INNER]
    xbc = zxbcdt[..., D_INNER:D_INNER + CONV_DIM]
    dt_raw = zxbcdt[..., D_INNER + CONV_DIM:]
    xpad = jnp.concatenate([conv_state.astype(xbc.dtype), xbc], axis=1)
    new_conv = xpad[:, xpad.shape[1] - (CONV_W - 1):]
    conv = b_conv + sum(xpad[:, k:k + L] * w_conv[k] for k in range(CONV_W))
    xbc = jax.nn.silu(conv)
    xs = xbc[..., :D_INNER].reshape(b, L, SSM_HEADS, SSM_HEAD_DIM)
    bm = xbc[..., D_INNER:D_INNER + SSM_GROUPS * SSM_STATE].reshape(b, L, SSM_GROUPS, SSM_STATE)
    cm = xbc[..., D_INNER + SSM_GROUPS * SSM_STATE:].reshape(b, L, SSM_GROUPS, SSM_STATE)
    dt = jax.nn.softplus((dt_raw + dt_bias).astype(jnp.float32))
    a = -jnp.exp(a_log.astype(jnp.float32))
    y, new_ssm = ssd_scan(xs, dt, a, bm, cm, ssm_state)
    y = y + xs.astype(jnp.float32) * d_skip.astype(jnp.float32)[:, None]
    out = gated_rmsnorm(y.reshape(b, L, D_INNER), z, g_norm) @ w_out
    return out, new_conv, new_ssm.astype(xn.dtype)


def block_attention(q, k, v, q_pos, k_pos):
    b, L, H, dk = q.shape
    hk = k.shape[2]
    r = H // hk
    qb = min(Q_BLOCK, L)
    nb = L // qb
    qs = jnp.moveaxis(q.reshape(b, nb, qb, hk, r, dk), 1, 0)
    ps = q_pos.reshape(nb, qb)
    k_chunk = k_pos // CHUNK

    def one_block(args):
        qblk, pblk = args
        s = jnp.einsum('bqkrd,bskd->bkrqs', qblk, k).astype(jnp.float32) * ATTN_SCALE
        mask = k_chunk[None, :] <= (pblk // CHUNK)[:, None]
        s = jnp.where(mask, s, -jnp.inf)
        pr = jax.nn.softmax(s, axis=-1).astype(v.dtype)
        return jnp.einsum('bkrqs,bskd->bqkrd', pr, v)

    o = lax.map(one_block, (qs, ps))
    return jnp.moveaxis(o, 0, 1).reshape(b, L, H, v.shape[-1])


def shared_kv(h, pos, g_kv_in, w_dkv, g_kv):
    c = rmsnorm(h, g_kv_in) @ w_dkv
    c_kv = rmsnorm(c[..., :KV_RANK], g_kv)
    k_r = rope(c[..., KV_RANK:], pos)
    return c_kv, k_r


def mla_mixer(xn, q_pos, c_all, kr_all, k_pos, w_dq, g_q, w_uq, w_o, w_uk, w_uv, absorb):
    b, L, _ = xn.shape
    q = (rmsnorm(xn @ w_dq, g_q) @ w_uq).reshape(b, L, MLA_HEADS, QK_NOPE + QK_ROPE)
    q_nope = q[..., :QK_NOPE]
    q_rope = rope(q[..., QK_NOPE:], q_pos)
    if absorb:
        q_cat = jnp.concatenate([jnp.einsum('blhd,chd->blhc', q_nope, w_uk), q_rope], axis=-1)
        k_cat = jnp.concatenate([c_all, kr_all], axis=-1)[:, :, None]
        o_lat = block_attention(q_cat, k_cat, c_all[:, :, None], q_pos, k_pos)
        o = jnp.einsum('blhc,chd->blhd', o_lat, w_uv)
    else:
        S = c_all.shape[1]
        k_nope = jnp.einsum('bsc,chd->bshd', c_all, w_uk)
        k_cat = jnp.concatenate([k_nope, jnp.broadcast_to(kr_all[:, :, None], (b, S, MLA_HEADS, QK_ROPE))], axis=-1)
        v = jnp.einsum('bsc,chd->bshd', c_all, w_uv)
        o = block_attention(jnp.concatenate([q_nope, q_rope], axis=-1), k_cat, v, q_pos, k_pos)
    return o.reshape(b, L, MLA_HEADS * V_DIM) @ w_o


def hier_route(xt, w_grp, b_grp, w_exp, b_exp):
    T = xt.shape[0]
    lg = (xt @ w_grp + b_grp).astype(jnp.float32)
    g_sel = jnp.argmax(lg, axis=-1)
    p_sel = jnp.take_along_axis(jax.nn.softmax(lg, axis=-1), g_sel[:, None], axis=-1)
    le = (xt @ w_exp + b_exp).astype(jnp.float32).reshape(T, MOE_GROUPS, EXPERTS_PER_GROUP)
    le = jnp.take_along_axis(le, g_sel[:, None, None], axis=1)[:, 0]
    top_v, top_i = lax.top_k(le, TOP_K)
    gates = jax.nn.softmax(top_v, axis=-1) * p_sel
    idx = (g_sel[:, None] * EXPERTS_PER_GROUP + top_i).astype(jnp.int32)
    return idx, gates


def moe_ffn(xt, idx, gates, w_gate, w_up, w_down):
    T, D = xt.shape
    E = w_gate.shape[0]
    A = T * TOP_K
    flat_e = idx.reshape(A)
    order = jnp.argsort(flat_e)
    e_sorted = flat_e[order]
    counts = jnp.zeros((E,), jnp.int32).at[flat_e].add(1)
    padded = (counts + MOE_BLOCK - 1) // MOE_BLOCK * MOE_BLOCK
    pad_end = jnp.cumsum(padded)
    pad_start = pad_end - padded
    start = jnp.cumsum(counts) - counts
    dest = pad_start[e_sorted] + jnp.arange(A, dtype=jnp.int32) - start[e_sorted]
    nblk = -(-A // MOE_BLOCK) + E
    rows = nblk * MOE_BLOCK
    row_token = jnp.full((rows,), T, jnp.int32).at[dest].set((order // TOP_K).astype(jnp.int32))
    x_pad = jnp.concatenate([xt, jnp.zeros((1, D), xt.dtype)], axis=0)
    xb = x_pad[row_token].reshape(nblk, MOE_BLOCK, D)
    blk_e = jnp.minimum(jnp.searchsorted(pad_end, jnp.arange(nblk, dtype=jnp.int32) * MOE_BLOCK, side='right'), E - 1)

    def expert_block(args):
        xblk, e = args
        hid = jax.nn.silu(xblk @ w_gate[e]) * (xblk @ w_up[e])
        return hid @ w_down[e]

    yb = lax.map(expert_block, (xb, blk_e)).reshape(rows, D)
    y_assign = jnp.zeros((A, D), yb.dtype).at[order].set(yb[dest])
    return jnp.einsum('tkd,tk->td', y_assign.reshape(T, TOP_K, D), gates.astype(yb.dtype))


def _trunk(x, p, conv_st, ssm_st, past_ckv, past_kr, absorb, prm):
    b, L, _ = x.shape
    past = 0 if past_ckv is None else past_ckv.shape[1]
    q_pos = past + jnp.arange(L, dtype=jnp.int32)
    k_pos = jnp.arange(past + L, dtype=jnp.int32)
    h = x
    conv_new, ssm_new = [], []
    c_all = kr_all = ckv_new = kr_new = None
    for i in range(DEPTH):
        xn = rmsnorm(h, prm['g_mix'][i])
        if i < N_A_LAYERS:
            o, cs, ss = mamba_mixer(xn, conv_st[i], ssm_st[i], prm['w_ssm_in'][i], prm['w_conv'][i],
                                    prm['b_conv'][i], prm['dt_bias'][i], prm['a_log'][i], prm['d_skip'][i],
                                    prm['g_ssm_norm'][i], prm['w_ssm_out'][i])
            conv_new.append(cs)
            ssm_new.append(ss)
        else:
            j = i - N_A_LAYERS
            o = mla_mixer(xn, q_pos, c_all, kr_all, k_pos, prm['w_dq'][j], prm['g_q'][j], prm['w_uq'][j],
                          prm['w_o'][j], prm['w_uk'], prm['w_uv'], absorb)
        h = h + o
        xt = rmsnorm(h, prm['g_ffn'][i]).reshape(b * L, D_MODEL)
        idx, gates = hier_route(xt, prm['w_router_grp'][i], prm['b_router_grp'][i],
                                prm['w_router_exp'][i], prm['b_router_exp'][i])
        h = h + moe_ffn(xt, idx, gates, prm['w_exp_gate'][i], prm['w_exp_up'][i],
                        prm['w_exp_down'][i]).reshape(b, L, D_MODEL)
        gate = jax.nn.sigmoid(rmsnorm(h, prm['g_ple'][i]) @ prm['w_ple_gate'][i])
        h = h + gate * (p[i] @ prm['w_ple_proj'][i])
        if i == N_A_LAYERS - 1:
            ckv_new, kr_new = shared_kv(h, q_pos, prm['g_kv_in'], prm['w_dkv'], prm['g_kv'])
            if past_ckv is None:
                c_all, kr_all = ckv_new, kr_new
            else:
                c_all = jnp.concatenate([past_ckv.astype(ckv_new.dtype), ckv_new], axis=1)
                kr_all = jnp.concatenate([past_kr.astype(kr_new.dtype), kr_new], axis=1)
    return rmsnorm(h, prm['g_final']), jnp.stack(conv_new), jnp.stack(ssm_new), ckv_new, kr_new


def setup_inputs(seed: int = 0) -> dict:
    key = jax.random.key(seed)
    keys = jax.random.split(key, 64)
    ks = iter([keys[i] for i in range(64)])

    def nrm(shape, scale):
        return jax.random.normal(next(ks), shape, jnp.float32) * scale

    def gain(shape):
        return 1.0 + nrm(shape, 0.05)

    na, nb = N_A_LAYERS, N_B_LAYERS
    dt0 = jnp.exp(jax.random.uniform(next(ks), (na, SSM_HEADS), jnp.float32, math.log(1e-3), math.log(1e-1)))
    return {
        'x_prompt': nrm((BATCH, SEQ, D_MODEL), 1.0),
        'x_sample': nrm((DEC_BATCH, DEC_SEQ, D_MODEL), 1.0),
        'state_conv': nrm((na, DEC_BATCH, CONV_W - 1, CONV_DIM), 1.0),
        'state_ssm': nrm((na, DEC_BATCH, SSM_HEADS, SSM_HEAD_DIM, SSM_STATE), 0.1),
        'cache_kv_latent': nrm((DEC_BATCH, PAST_LEN, KV_RANK), 1.0),
        'cache_k_rope': nrm((DEC_BATCH, PAST_LEN, QK_ROPE), 1.0),
        'p_prompt': nrm((DEPTH, BATCH, SEQ, PLE_DIM), 1.0),
        'p_sample': nrm((DEPTH, DEC_BATCH, DEC_SEQ, PLE_DIM), 1.0),
        'g_mix': gain((DEPTH, D_MODEL)),
        'w_ssm_in': nrm((na, D_MODEL, SSM_IN_DIM), D_MODEL ** -0.5),
        'w_conv': nrm((na, CONV_W, CONV_DIM), CONV_W ** -0.5),
        'b_conv': nrm((na, CONV_DIM), 0.02),
        'dt_bias': dt0 + jnp.log(-jnp.expm1(-dt0)),
        'a_log': jnp.log(jax.random.uniform(next(ks), (na, SSM_HEADS), jnp.float32, 1.0, 16.0)),
        'd_skip': 1.0 + nrm((na, SSM_HEADS), 0.1),
        'g_ssm_norm': gain((na, D_INNER)),
        'w_ssm_out': nrm((na, D_INNER, D_MODEL), D_INNER ** -0.5),
        'g_kv_in': gain((D_MODEL,)),
        'w_dkv': nrm((D_MODEL, KV_RANK + QK_ROPE), D_MODEL ** -0.5),
        'g_kv': gain((KV_RANK,)),
        'w_uk': nrm((KV_RANK, MLA_HEADS, QK_NOPE), KV_RANK ** -0.5),
        'w_uv': nrm((KV_RANK, MLA_HEADS, V_DIM), KV_RANK ** -0.5),
        'w_dq': nrm((nb, D_MODEL, Q_RANK), D_MODEL ** -0.5),
        'g_q': gain((nb, Q_RANK)),
        'w_uq': nrm((nb, Q_RANK, MLA_HEADS * (QK_NOPE + QK_ROPE)), Q_RANK ** -0.5),
        'w_o': nrm((nb, MLA_HEADS * V_DIM, D_MODEL), (MLA_HEADS * V_DIM) ** -0.5),
        'g_ffn': gain((DEPTH, D_MODEL)),
        'w_router_grp': nrm((DEPTH, D_MODEL, MOE_GROUPS), D_MODEL ** -0.5),
        'b_router_grp': nrm((DEPTH, MOE_GROUPS), 0.01),
        'w_router_exp': nrm((DEPTH, D_MODEL, N_EXPERTS), D_MODEL ** -0.5),
        'b_router_exp': nrm((DEPTH, N_EXPERTS), 0.01),
        'w_exp_gate': nrm((DEPTH, N_EXPERTS, D_MODEL, D_EXPERT), D_MODEL ** -0.5),
        'w_exp_up': nrm((DEPTH, N_EXPERTS, D_MODEL, D_EXPERT), D_MODEL ** -0.5),
        'w_exp_down': nrm((DEPTH, N_EXPERTS, D_EXPERT, D_MODEL), D_EXPERT ** -0.5),
        'g_ple': gain((DEPTH, D_MODEL)),
        'w_ple_gate': nrm((DEPTH, D_MODEL, D_MODEL), D_MODEL ** -0.5),
        'w_ple_proj': nrm((DEPTH, PLE_DIM, D_MODEL), PLE_DIM ** -0.5),
        'g_final': gain((D_MODEL,)),
    }


def reference(x_prompt, x_sample, state_conv, state_ssm, cache_kv_latent, cache_k_rope, p_prompt, p_sample,
              g_mix, w_ssm_in, w_conv, b_conv, dt_bias, a_log, d_skip, g_ssm_norm, w_ssm_out,
              g_kv_in, w_dkv, g_kv, w_uk, w_uv, w_dq, g_q, w_uq, w_o,
              g_ffn, w_router_grp, b_router_grp, w_router_exp, b_router_exp, w_exp_gate, w_exp_up, w_exp_down,
              g_ple, w_ple_gate, w_ple_proj, g_final):
    prm = dict(g_mix=g_mix, w_ssm_in=w_ssm_in, w_conv=w_conv, b_conv=b_conv, dt_bias=dt_bias, a_log=a_log,
               d_skip=d_skip, g_ssm_norm=g_ssm_norm, w_ssm_out=w_ssm_out, g_kv_in=g_kv_in, w_dkv=w_dkv,
               g_kv=g_kv, w_uk=w_uk, w_uv=w_uv, w_dq=w_dq, g_q=g_q, w_uq=w_uq, w_o=w_o, g_ffn=g_ffn,
               w_router_grp=w_router_grp, b_router_grp=b_router_grp, w_router_exp=w_router_exp,
               b_router_exp=b_router_exp, w_exp_gate=w_exp_gate, w_exp_up=w_exp_up, w_exp_down=w_exp_down,
               g_ple=g_ple, w_ple_gate=w_ple_gate, w_ple_proj=w_ple_proj, g_final=g_final)
    bp = x_prompt.shape[0]
    zero_conv = jnp.zeros((N_A_LAYERS, bp, CONV_W - 1, CONV_DIM), x_prompt.dtype)
    zero_ssm = jnp.zeros((N_A_LAYERS, bp, SSM_HEADS, SSM_HEAD_DIM, SSM_STATE), x_prompt.dtype)
    y_prompt, conv_p, ssm_p, ckv_p, kr_p = _trunk(x_prompt, p_prompt, zero_conv, zero_ssm, None, None, False, prm)
    y_sample, conv_s, ssm_s, ckv_s, kr_s = _trunk(x_sample, p_sample, state_conv, state_ssm,
                                                  cache_kv_latent, cache_k_rope, True, prm)
    return (y_prompt, y_sample, conv_p, ssm_p, ckv_p, kr_p, conv_s, ssm_s, ckv_s, kr_s)
```

```python
import functools
import math

import numpy as np
import jax
import jax.numpy as jnp
from jax import lax
from jax.experimental import pallas as pl
from jax.experimental.pallas import tpu as pltpu

F32 = jnp.float32
BF16 = jnp.bfloat16
HIGHEST = lax.Precision.HIGHEST

EPS = 1e-6
CHUNK = 64
D_MODEL = 2048
D_INNER = 2 * D_MODEL
SSM_HEAD_DIM = 64
SSM_HEADS = D_INNER // SSM_HEAD_DIM
SSM_STATE = 128
SSM_GROUPS = 8
HEADS_PER_GROUP = SSM_HEADS // SSM_GROUPS
GROUP_WIDTH = D_INNER // SSM_GROUPS
CONV_W = 4
CONV_DIM = D_INNER + 2 * SSM_GROUPS * SSM_STATE
MLA_HEADS = 16
Q_RANK = 512
KV_RANK = 512
QK_NOPE = 128
QK_ROPE = 64
QK_DIM = QK_NOPE + QK_ROPE
V_DIM = 128
ROPE_THETA = 10000.0
ATTN_SCALE = QK_DIM ** -0.5
MOE_GROUPS = 4
EXPERTS_PER_GROUP = 8
N_EXPERTS = MOE_GROUPS * EXPERTS_PER_GROUP
TOP_K = 2
D_EXPERT = 512

LANES = 128
SUBLANES = 8
VMEM_LIMIT = 56 * 1024 * 1024
SSD_Q = 128
MOE_BM = 128
LAT_PAD = 640


def _cparams(sem):
    return pltpu.CompilerParams(dimension_semantics=sem, vmem_limit_bytes=VMEM_LIMIT)


def _sigmoid(v):
    return 1.0 / (1.0 + jnp.exp(-v))


def _silu(v):
    return v * _sigmoid(v)


def _softplus(v):
    return jnp.maximum(v, 0.0) + jnp.log1p(jnp.exp(-jnp.abs(v)))


def _rms_rows(x):
    return x * lax.rsqrt(jnp.mean(x * x, axis=-1, keepdims=True) + EPS)


def _mm_body(*refs, nx, nw, ne, no, prologue, epilogue, x_per_j, emit_xn, precision):
    x_refs = refs[:nx]
    w_refs = refs[nx:nx + nw]
    e_refs = refs[nx + nw:nx + nw + ne]
    o_refs = refs[nx + nw + ne:nx + nw + ne + no]
    rest = refs[nx + nw + ne + no:]
    j = pl.program_id(1)
    if x_per_j:
        xn = prologue(*[r[...] for r in x_refs])
    else:
        xn_ref = rest[-1]

        @pl.when(j == 0)
        def _():
            v = prologue(*[r[...] for r in x_refs])
            xn_ref[...] = v.astype(xn_ref.dtype)
            if emit_xn:
                rest[0][...] = v.astype(rest[0].dtype)

        xn = xn_ref[...]
    accs = [jnp.dot(xn, w[...], preferred_element_type=F32, precision=precision) for w in w_refs]
    outs = epilogue(accs, [e[...] for e in e_refs])
    for o_ref, o in zip(o_refs, outs):
        o_ref[...] = o.astype(o_ref.dtype)


def _fused_mm(x_args, w_args, e_args, out_defs, *, grid, prologue, epilogue, xn_shape=None,
              xn_dtype=BF16, x_per_j=False, xn_out=None, precision=None, name=None):
    arrays = [a for a, _ in x_args + w_args + e_args]
    in_specs = [s for _, s in x_args + w_args + e_args]
    out_shape = [d for d, _ in out_defs]
    out_specs = [s for _, s in out_defs]
    emit_xn = xn_out is not None
    if emit_xn:
        out_shape.append(xn_out[0])
        out_specs.append(xn_out[1])
    scratch = [] if x_per_j else [pltpu.VMEM(xn_shape, xn_dtype)]
    body = functools.partial(
        _mm_body, nx=len(x_args), nw=len(w_args), ne=len(e_args), no=len(out_defs),
        prologue=prologue, epilogue=epilogue, x_per_j=x_per_j, emit_xn=emit_xn, precision=precision)
    return pl.pallas_call(
        body, grid=grid, in_specs=in_specs, out_specs=out_specs, out_shape=out_shape,
        scratch_shapes=scratch, compiler_params=_cparams(("parallel", "arbitrary")), name=name,
    )(*arrays)


def _pro_rms(x, g):
    return _rms_rows(x.astype(F32)) * g


def _pro_cast(x):
    return x


def _pro_gated(y, z, g):
    v = y.astype(F32) * _silu(z.astype(F32))
    parts = [_rms_rows(v[:, k * GROUP_WIDTH:(k + 1) * GROUP_WIDTH]) for k in range(SSM_GROUPS)]
    return jnp.concatenate(parts, axis=-1) * g


def _epi_id(accs, es):
    return accs


def _epi_res(accs, es):
    return [es[0] + accs[0]]


def _epi_ple(accs, es):
    h, pp = es
    return [h + _sigmoid(accs[0]) * pp]


def _epi_rms_out(accs, es):
    return [_rms_rows(accs[0]) * es[0]]


def _epi_rot(accs, es):
    cos, sin = es
    return [accs[0] * cos + accs[1] * sin]


def _epi_route(accs, es):
    lg = accs[0] + es[0]
    lane = lax.broadcasted_iota(jnp.int32, lg.shape, 1).astype(F32)
    neg = -jnp.inf
    big = 1.0e4
    is_grp = lane < MOE_GROUPS
    gl = jnp.where(is_grp, lg, neg)
    mg = jnp.max(gl, axis=-1, keepdims=True)
    g_sel = jnp.min(jnp.where(gl == mg, lane, big), axis=-1, keepdims=True)
    p_sel = 1.0 / jnp.sum(jnp.where(is_grp, jnp.exp(gl - mg), 0.0), axis=-1, keepdims=True)
    lo = MOE_GROUPS + g_sel * EXPERTS_PER_GROUP
    in_grp = jnp.where(lane >= lo, jnp.where(lane < lo + EXPERTS_PER_GROUP, 1.0, 0.0), 0.0) > 0.5
    el = jnp.where(in_grp, lg, neg)
    v1 = jnp.max(el, axis=-1, keepdims=True)
    i1 = jnp.min(jnp.where(el == v1, lane, big), axis=-1, keepdims=True)
    el2 = jnp.where(lane == i1, neg, el)
    v2 = jnp.max(el2, axis=-1, keepdims=True)
    i2 = jnp.min(jnp.where(el2 == v2, jnp.where(lane == i1, big, lane), big), axis=-1, keepdims=True)
    e21 = jnp.exp(v2 - v1)
    g1 = p_sel / (1.0 + e21)
    g2 = p_sel * e21 / (1.0 + e21)
    idx = jnp.where(lane == 0.0, i1 - MOE_GROUPS, jnp.where(lane == 1.0, i2 - MOE_GROUPS, 0.0))
    gates = jnp.where(lane == 0.0, g1, jnp.where(lane == 1.0, g2, 0.0))
    return [idx.astype(jnp.int32), gates]


def _mm_rows(x_list, consts, w, extras, *, prologue, epilogue, out_dtype, tm, tn, name,
             xn_dtype=BF16, precision=None, emit_xn_dtype=None, e_consts=()):
    m = x_list[0].shape[0]
    k, n = w.shape
    grid = (m // tm, n // tn)
    x_args = [(x, pl.BlockSpec((tm, x.shape[1]), lambda i, j: (i, 0))) for x in x_list]
    x_args += [(c, pl.BlockSpec((1, c.shape[1]), lambda i, j: (0, 0))) for c in consts]
    w_args = [(w, pl.BlockSpec((k, tn), lambda i, j: (0, j)))]
    e_args = [(c, pl.BlockSpec((1, tn), lambda i, j: (0, j))) for c in e_consts]
    e_args += [(e, pl.BlockSpec((tm, tn), lambda i, j: (i, j))) for e in extras]
    out_defs = [(jax.ShapeDtypeStruct((m, n), out_dtype), pl.BlockSpec((tm, tn), lambda i, j: (i, j)))]
    xn_out = None
    if emit_xn_dtype is not None:
        xn_out = (jax.ShapeDtypeStruct((m, k), emit_xn_dtype), pl.BlockSpec((tm, k), lambda i, j: (i, 0)))
    return _fused_mm(x_args, w_args, e_args, out_defs, grid=grid, prologue=prologue, epilogue=epilogue,
                     xn_shape=(tm, k), xn_dtype=xn_dtype, xn_out=xn_out, precision=precision, name=name)


def _conv_body(x_ref, prev_ref, st_ref, w_ref, b_ref, o_ref, ext):
    i = pl.program_id(1)
    tl = x_ref.shape[0]
    ext[SUBLANES:SUBLANES + tl, :] = x_ref[...]

    @pl.when(i == 0)
    def _():
        ext[0:SUBLANES, :] = st_ref[...]

    @pl.when(i > 0)
    def _():
        ext[0:SUBLANES, :] = prev_ref[...]

    acc = b_ref[...]
    for k in range(CONV_W):
        off = SUBLANES - (CONV_W - 1) + k
        acc = acc + ext[off:off + tl, :] * w_ref[k:k + 1, :]
    o_ref[...] = _silu(acc).astype(o_ref.dtype)


def _conv_silu(x3, batch0, nb, L, st, w, b, *, tl, tc):
    c = x3.shape[2]
    grid = (nb, L // tl, c // tc)
    per8 = tl // SUBLANES
    return pl.pallas_call(
        _conv_body, grid=grid,
        in_specs=[
            pl.BlockSpec((None, tl, tc), lambda bb, i, j: (batch0 + bb, i, j)),
            pl.BlockSpec((None, SUBLANES, tc), lambda bb, i, j: (batch0 + bb, jnp.maximum(i * per8 - 1, 0), j)),
            pl.BlockSpec((None, SUBLANES, tc), lambda bb, i, j: (bb, 0, j)),
            pl.BlockSpec((CONV_W, tc), lambda bb, i, j: (0, j)),
            pl.BlockSpec((1, tc), lambda bb, i, j: (0, j)),
        ],
        out_specs=pl.BlockSpec((None, tl, tc), lambda bb, i, j: (bb, i, j)),
        out_shape=jax.ShapeDtypeStruct((nb, L, c), BF16),
        scratch_shapes=[pltpu.VMEM((tl + SUBLANES, tc), F32)],
        compiler_params=_cparams(("parallel", "parallel", "parallel")), name="conv_silu",
    )(x3, x3, st, w, b)


def _ssd_body(xs_ref, b_ref, c_ref, dt_ref, dtT_ref, bias_ref, biasT_ref, a_ref, aT_ref, dsk_ref, s0_ref,
              y_ref, so_ref, s_scr, *, q):
    c = pl.program_id(2)

    @pl.when(c == 0)
    def _():
        s_scr[...] = s0_ref[...]

    dt = _softplus(dt_ref[...] + bias_ref[...])
    dta = dt * a_ref[...]
    dta_t = _softplus(dtT_ref[...] + biasT_ref[...]) * aT_ref[...]
    row = lax.broadcasted_iota(jnp.int32, (q, q), 0)
    col = lax.broadcasted_iota(jnp.int32, (q, q), 1)
    causal = row >= col
    tri = jnp.where(causal, 1.0, 0.0)
    tri_t = jnp.where(row <= col, 1.0, 0.0)
    acum = jnp.dot(tri, dta, preferred_element_type=F32, precision=HIGHEST)
    acum_t = jnp.dot(dta_t, tri_t, preferred_element_type=F32, precision=HIGHEST)
    bm = b_ref[...]
    cm = c_ref[...]
    cb = lax.dot_general(cm, bm, (((1,), (1,)), ((), ())), preferred_element_type=F32)
    s_prev = s_scr[...]
    y_off = jnp.dot(cm, s_prev.astype(BF16), preferred_element_type=F32)
    left = lax.broadcasted_iota(jnp.int32, (q, LANES), 1) < SSM_HEAD_DIM
    xw_parts, dec_parts = [], []
    for j in range(HEADS_PER_GROUP // 2):
        h0, h1 = 2 * j, 2 * j + 1
        sl = slice(j * LANES, (j + 1) * LANES)
        col0, col1 = acum[:, h0:h0 + 1], acum[:, h1:h1 + 1]
        pa = jnp.where(left, col0, col1)
        dtp = jnp.where(left, dt[:, h0:h0 + 1], dt[:, h1:h1 + 1])
        x = xs_ref[:, sl].astype(F32)
        xdt = x * dtp
        m0 = (jnp.exp(jnp.where(causal, col0 - acum_t[h0:h0 + 1, :], -jnp.inf)) * cb).astype(BF16)
        m1 = (jnp.exp(jnp.where(causal, col1 - acum_t[h1:h1 + 1, :], -jnp.inf)) * cb).astype(BF16)
        x_l = jnp.where(left, xdt, 0.0).astype(BF16)
        x_r = jnp.where(left, 0.0, xdt).astype(BF16)
        y_diag = (jnp.dot(m0, x_l, preferred_element_type=F32)
                  + jnp.dot(m1, x_r, preferred_element_type=F32))
        y = y_diag + y_off[:, sl] * jnp.exp(pa) + x * dsk_ref[:, sl]
        y_ref[:, sl] = y.astype(y_ref.dtype)
        last = pa[q - 1:q, :]
        xw_parts.append((xdt * jnp.exp(last - pa)).astype(BF16))
        dec_parts.append(jnp.exp(last))
    xw = jnp.concatenate(xw_parts, axis=1)
    dec = jnp.concatenate(dec_parts, axis=1)
    s_new = s_prev * dec + lax.dot_general(bm, xw, (((0,), (0,)), ((), ())), preferred_element_type=F32)
    s_scr[...] = s_new
    so_ref[...] = s_new


def _ssd_scan(act, dtg, dtg_t, bias, bias_t, a, a_t, dsk, s0, *, q):
    nb, L, _ = act.shape
    g_, r_, n_ = SSM_GROUPS, HEADS_PER_GROUP, SSM_STATE
    gw = GROUP_WIDTH
    grid = (nb, g_, L // q)
    b_off = D_INNER // n_
    c_off = b_off + g_
    return pl.pallas_call(
        functools.partial(_ssd_body, q=q), grid=grid,
        in_specs=[
            pl.BlockSpec((None, q, gw), lambda b, g, c: (b, c, g)),
            pl.BlockSpec((None, q, n_), lambda b, g, c: (b, c, b_off + g)),
            pl.BlockSpec((None, q, n_), lambda b, g, c: (b, c, c_off + g)),
            pl.BlockSpec((None, None, q, r_), lambda b, g, c: (b, g, c, 0)),
            pl.BlockSpec((None, None, r_, q), lambda b, g, c: (b, g, 0, c)),
            pl.BlockSpec((None, 1, r_), lambda b, g, c: (g, 0, 0)),
            pl.BlockSpec((None, r_, 1), lambda b, g, c: (g, 0, 0)),
            pl.BlockSpec((None, 1, r_), lambda b, g, c: (g, 0, 0)),
            pl.BlockSpec((None, r_, 1), lambda b, g, c: (g, 0, 0)),
            pl.BlockSpec((1, gw), lambda b, g, c: (0, g)),
            pl.BlockSpec((None, None, n_, gw), lambda b, g, c: (b, g, 0, 0)),
        ],
        out_specs=[
            pl.BlockSpec((None, q, gw), lambda b, g, c: (b, c, g)),
            pl.BlockSpec((None, None, n_, gw), lambda b, g, c: (b, g, 0, 0)),
        ],
        out_shape=[jax.ShapeDtypeStruct((nb, L, D_INNER), BF16),
                   jax.ShapeDtypeStruct((nb, g_, n_, gw), F32)],
        scratch_shapes=[pltpu.VMEM((n_, gw), F32)],
        compiler_params=_cparams(("parallel", "parallel", "arbitrary")), name="ssd_scan",
    )(act, act, act, dtg, dtg_t, bias, bias_t, a, a_t, dsk, s0)


def _gather_rows(tok_ref, base, src_hbm, dst, sem, n):
    def body(r, carry):
        t = tok_ref[base + r]
        pltpu.make_async_copy(src_hbm.at[pl.ds(t, 1)], dst.at[pl.ds(r, 1)], sem).start()
        return carry

    lax.fori_loop(0, n, body, 0)


def _moe_body(blk_e_ref, tok_ref, nused_ref, xt_hbm, wg_ref, wu_ref, wd_ref, yb_ref,
              xbuf, sem, wg_bf, wu_bf, wd_bf, *, bm, nblk):
    b = pl.program_id(0)
    slot = b % 2
    nused = nused_ref[0]

    @pl.when(jnp.logical_and(b == 0, nused > 0))
    def _():
        _gather_rows(tok_ref, 0, xt_hbm, xbuf.at[0], sem.at[0], bm)

    @pl.when(b + 1 < nused)
    def _():
        _gather_rows(tok_ref, (b + 1) * bm, xt_hbm, xbuf.at[1 - slot], sem.at[1 - slot], bm)

    prev_e = blk_e_ref[jnp.maximum(b - 1, 0)]
    changed = jnp.logical_or(b == 0, blk_e_ref[b] != prev_e)

    @pl.when(jnp.logical_and(changed, b < nused))
    def _():
        wg_bf[...] = wg_ref[...].astype(BF16)
        wu_bf[...] = wu_ref[...].astype(BF16)
        wd_bf[...] = wd_ref[...].astype(BF16)

    @pl.when(b < nused)
    def _():
        pltpu.make_async_copy(xbuf.at[slot], xbuf.at[slot], sem.at[slot]).wait()
        x = xbuf[slot].astype(BF16)
        g = jnp.dot(x, wg_bf[...], preferred_element_type=F32)
        u = jnp.dot(x, wu_bf[...], preferred_element_type=F32)
        hid = (_silu(g) * u).astype(BF16)
        yb_ref[...] = jnp.dot(hid, wd_bf[...], preferred_element_type=F32)

    @pl.when(b >= nused)
    def _():
        yb_ref[...] = jnp.zeros(yb_ref.shape, yb_ref.dtype)


def _moe_experts(xt, blk_e, row_token, nused, w_gate, w_up, w_down, *, bm):
    nblk = blk_e.shape[0]
    d = xt.shape[1]
    de = w_gate.shape[2]
    gs = pltpu.PrefetchScalarGridSpec(
        num_scalar_prefetch=3, grid=(nblk,),
        in_specs=[
            pl.BlockSpec(memory_space=pl.ANY),
            pl.BlockSpec((None, d, de), lambda b, be, tok, nu: (be[b], 0, 0)),
            pl.BlockSpec((None, d, de), lambda b, be, tok, nu: (be[b], 0, 0)),
            pl.BlockSpec((None, de, d), lambda b, be, tok, nu: (be[b], 0, 0)),
        ],
        out_specs=pl.BlockSpec((bm, d), lambda b, be, tok, nu: (b, 0)),
        scratch_shapes=[
            pltpu.VMEM((2, bm, d), F32),
            pltpu.SemaphoreType.DMA((2,)),
            pltpu.VMEM((d, de), BF16),
            pltpu.VMEM((d, de), BF16),
            pltpu.VMEM((de, d), BF16),
        ],
    )
    return pl.pallas_call(
        functools.partial(_moe_body, bm=bm, nblk=nblk), grid_spec=gs,
        out_shape=jax.ShapeDtypeStruct((nblk * bm, d), F32),
        compiler_params=_cparams(("arbitrary",)), name="moe_experts",
    )(blk_e, row_token, nused, xt, w_gate, w_up, w_down)


def _combine_body(dest_ref, h_ref, g_ref, yb_hbm, o_ref, buf, sem, *, tc, ntile):
    i = pl.program_id(0)
    slot = i % 2
    n = TOP_K * tc

    @pl.when(i == 0)
    def _():
        _gather_rows(dest_ref, 0, yb_hbm, buf.at[0], sem.at[0], n)

    @pl.when(i + 1 < ntile)
    def _():
        _gather_rows(dest_ref, (i + 1) * n, yb_hbm, buf.at[1 - slot], sem.at[1 - slot], n)

    pltpu.make_async_copy(buf.at[slot], buf.at[slot], sem.at[slot]).wait()
    g = g_ref[...]
    o_ref[...] = (h_ref[...] + g[:, 0:1] * buf[slot, 0:tc, :] + g[:, 1:2] * buf[slot, tc:2 * tc, :])


def _moe_combine(h, yb, dest_tiles, gates, *, tc):
    t, d = h.shape
    ntile = t // tc
    gs = pltpu.PrefetchScalarGridSpec(
        num_scalar_prefetch=1, grid=(ntile,),
        in_specs=[
            pl.BlockSpec((tc, d), lambda i, dst: (i, 0)),
            pl.BlockSpec((tc, LANES), lambda i, dst: (i, 0)),
            pl.BlockSpec(memory_space=pl.ANY),
        ],
        out_specs=pl.BlockSpec((tc, d), lambda i, dst: (i, 0)),
        scratch_shapes=[pltpu.VMEM((2, TOP_K * tc, d), F32), pltpu.SemaphoreType.DMA((2,))],
    )
    return pl.pallas_call(
        functools.partial(_combine_body, tc=tc, ntile=ntile), grid_spec=gs,
        out_shape=jax.ShapeDtypeStruct((t, d), F32),
        compiler_params=_cparams(("arbitrary",)), name="moe_combine",
    )(dest_tiles, h, gates, yb)


def _moe_plan(idx, bm, tc):
    t = idx.shape[0]
    a = t * TOP_K
    e_ = N_EXPERTS
    flat_e = idx.reshape(a)
    onehot = (flat_e[:, None] == jnp.arange(e_, dtype=jnp.int32)[None, :]).astype(jnp.int32)
    csum = jnp.cumsum(onehot, axis=0)
    rank = jnp.sum(onehot * csum, axis=1) - 1
    counts = csum[-1]
    padded = (counts + bm - 1) // bm * bm
    pad_end = jnp.cumsum(padded)
    pad_start = pad_end - padded
    dest = jnp.sum(onehot * pad_start[None, :], axis=1) + rank
    nblk = -(-a // bm) + e_
    rows = nblk * bm
    row_token = jnp.zeros((rows,), jnp.int32).at[dest].set(jnp.arange(a, dtype=jnp.int32) // TOP_K)
    blk_e = jnp.minimum(
        jnp.searchsorted(pad_end, jnp.arange(nblk, dtype=jnp.int32) * bm, side='right'), e_ - 1).astype(jnp.int32)
    nused = (pad_end[-1] // bm).astype(jnp.int32).reshape(1)
    dest_tiles = dest.reshape(t // tc, tc, TOP_K).transpose(0, 2, 1).reshape(a).astype(jnp.int32)
    return blk_e, row_token, nused, dest_tiles


def _flash_body(qi_ref, kj_ref, q_ref, k_ref, v_ref, o_ref, m_scr, l_scr, acc_scr, *, tq):
    p_id = pl.program_id(1)
    qi = qi_ref[p_id]
    kj = kj_ref[p_id]

    @pl.when(kj == 0)
    def _():
        m_scr[...] = jnp.full(m_scr.shape, -jnp.inf, F32)
        l_scr[...] = jnp.zeros(l_scr.shape, F32)
        acc_scr[...] = jnp.zeros(acc_scr.shape, F32)

    def step(diag):
        s = lax.dot_general(q_ref[...], k_ref[...], (((1,), (1,)), ((), ())),
                            preferred_element_type=F32) * ATTN_SCALE
        if diag:
            r = lax.broadcasted_iota(jnp.int32, s.shape, 0)
            c = lax.broadcasted_iota(jnp.int32, s.shape, 1)
            s = jnp.where((c // CHUNK) <= (r // CHUNK), s, -jnp.inf)
        m_prev = m_scr[...]
        m_new = jnp.maximum(m_prev, jnp.max(s, axis=-1, keepdims=True))
        alpha = jnp.exp(m_prev - m_new)
        p = jnp.exp(s - m_new)
        l_scr[...] = alpha * l_scr[...] + jnp.sum(p, axis=-1, keepdims=True)
        acc_scr[...] = alpha * acc_scr[...] + jnp.dot(p.astype(BF16), v_ref[...], preferred_element_type=F32)
        m_scr[...] = m_new

    @pl.when(kj < qi)
    def _():
        step(False)

    @pl.when(kj == qi)
    def _():
        step(True)
        o_ref[...] = (acc_scr[...] / l_scr[...]).astype(o_ref.dtype)


def _flash_prompt(q, k, v, lp, *, tq):
    nh = q.shape[0]
    nq = lp // tq
    pairs = [(a, b) for a in range(nq) for b in range(a + 1)]
    qi_tab = jnp.asarray(np.array([p[0] for p in pairs], np.int32))
    kj_tab = jnp.asarray(np.array([p[1] for p in pairs], np.int32))
    gs = pltpu.PrefetchScalarGridSpec(
        num_scalar_prefetch=2, grid=(nh, len(pairs)),
        in_specs=[
            pl.BlockSpec((None, tq, QK_DIM), lambda h, p, qi, kj: (h, qi[p], 0)),
            pl.BlockSpec((None, tq, QK_DIM), lambda h, p, qi, kj: (h, kj[p], 0)),
            pl.BlockSpec((None, tq, V_DIM), lambda h, p, qi, kj: (h, kj[p], 0)),
        ],
        out_specs=pl.BlockSpec((tq, V_DIM), lambda h, p, qi, kj: (qi[p], h)),
        scratch_shapes=[pltpu.VMEM((tq, 1), F32), pltpu.VMEM((tq, 1), F32), pltpu.VMEM((tq, V_DIM), F32)],
    )
    return pl.pallas_call(
        functools.partial(_flash_body, tq=tq), grid_spec=gs,
        out_shape=jax.ShapeDtypeStruct((lp, nh * V_DIM), BF16),
        compiler_params=_cparams(("parallel", "arbitrary")), name="flash_prompt",
    )(qi_tab, kj_tab, q, k, v)


def _attn_sample_body(q_ref, k_ref, o_ref, *, ls, past, s_valid):
    nh = q_ref.shape[0]
    q = q_ref[...].reshape(nh * ls, q_ref.shape[2])
    k = k_ref[...]
    s = lax.dot_general(q, k, (((1,), (1,)), ((), ())), preferred_element_type=F32) * ATTN_SCALE
    r = lax.broadcasted_iota(jnp.int32, s.shape, 0)
    c = lax.broadcasted_iota(jnp.int32, s.shape, 1)
    q_pos = past + r % ls
    s = jnp.where((c // CHUNK) <= (q_pos // CHUNK), s, -jnp.inf)
    s = jnp.where(c < s_valid, s, -jnp.inf)
    m = jnp.max(s, axis=-1, keepdims=True)
    p = jnp.exp(s - m)
    l = jnp.sum(p, axis=-1, keepdims=True)
    o = jnp.dot(p.astype(BF16), k[:, :KV_RANK], preferred_element_type=F32) / l
    o_ref[...] = o.reshape(nh, ls, KV_RANK).astype(o_ref.dtype)


def _attn_sample(q_abs, kc, *, ls, past, s_valid):
    nh, ts, dk = q_abs.shape
    nb, s_pad, _ = kc.shape
    return pl.pallas_call(
        functools.partial(_attn_sample_body, ls=ls, past=past, s_valid=s_valid), grid=(nb,),
        in_specs=[
            pl.BlockSpec((nh, ls, dk), lambda b: (0, b, 0)),
            pl.BlockSpec((None, s_pad, dk), lambda b: (b, 0, 0)),
        ],
        out_specs=pl.BlockSpec((nh, ls, KV_RANK), lambda b: (0, b, 0)),
        out_shape=jax.ShapeDtypeStruct((nh, ts, KV_RANK), BF16),
        compiler_params=_cparams(("parallel",)), name="attn_sample",
    )(q_abs, kc)


def _norm_body(x_ref, g_ref, o_ref):
    o_ref[...] = _rms_rows(x_ref[...]) * g_ref[...]


def _final_norm(h, g, row0, nrows, *, tm):
    d = h.shape[1]
    blk0 = row0 // tm
    return pl.pallas_call(
        _norm_body, grid=(nrows // tm,),
        in_specs=[pl.BlockSpec((tm, d), lambda i: (blk0 + i, 0)), pl.BlockSpec((1, d), lambda i: (0, 0))],
        out_specs=pl.BlockSpec((tm, d), lambda i: (i, 0)),
        out_shape=jax.ShapeDtypeStruct((nrows, d), F32),
        compiler_params=_cparams(("parallel",)), name="final_norm",
    )(h, g)


def _rope_tables(pos):
    half = QK_ROPE // 2
    inv = ROPE_THETA ** (-np.arange(half, dtype=np.float64) / half)
    ang = np.asarray(pos, np.float64)[:, None] * inv[None, :]
    cos = np.concatenate([np.cos(ang), np.cos(ang)], axis=1)
    sin = np.concatenate([-np.sin(ang), np.sin(ang)], axis=1)
    return cos.astype(np.float32), sin.astype(np.float32)


def _swap_rope_halves(w):
    half = QK_ROPE // 2
    return jnp.concatenate([w[..., half:], w[..., :half]], axis=-1)


def _moe_layer(h, g_ffn, w_rg, b_rg, w_re, b_re, w_gate, w_up, w_down, *, tm):
    t, d = h.shape
    npad = LANES - MOE_GROUPS - N_EXPERTS
    w_r = jnp.concatenate([w_rg, w_re, jnp.zeros((d, npad), F32)], axis=1)
    b_r = jnp.concatenate([b_rg, b_re, jnp.zeros((npad,), F32)]).reshape(1, LANES)
    grid = (t // tm, 1)
    x_args = [(h, pl.BlockSpec((tm, d), lambda i, j: (i, 0))),
              (g_ffn.reshape(1, d), pl.BlockSpec((1, d), lambda i, j: (0, 0)))]
    w_args = [(w_r, pl.BlockSpec((d, LANES), lambda i, j: (0, 0)))]
    e_args = [(b_r, pl.BlockSpec((1, LANES), lambda i, j: (0, 0)))]
    tile = pl.BlockSpec((tm, LANES), lambda i, j: (i, 0))
    out_defs = [(jax.ShapeDtypeStruct((t, LANES), jnp.int32), tile),
                (jax.ShapeDtypeStruct((t, LANES), F32), tile)]
    xn_out = (jax.ShapeDtypeStruct((t, d), F32), pl.BlockSpec((tm, d), lambda i, j: (i, 0)))
    idx_t, gates_t, xt = _fused_mm(x_args, w_args, e_args, out_defs, grid=grid, prologue=_pro_rms,
                                   epilogue=_epi_route, xn_shape=(tm, d), xn_dtype=F32, xn_out=xn_out,
                                   precision=HIGHEST, name="router")
    tc = min(tm, 128)
    blk_e, row_token, nused, dest_tiles = _moe_plan(idx_t[:, :TOP_K], MOE_BM, tc)
    yb = _moe_experts(xt, blk_e, row_token, nused, w_gate, w_up, w_down, bm=MOE_BM)
    return _moe_combine(h, yb, dest_tiles, gates_t, tc=tc)


def _ple_layer(h, p_rows, g_ple, w_gate, w_proj, *, tm, tn):
    t, d = h.shape
    pp = _mm_rows([p_rows.astype(BF16)], [], w_proj.astype(BF16), [], prologue=_pro_cast, epilogue=_epi_id,
                  out_dtype=F32, tm=tm, tn=tn, name="ple_proj")[0]
    return _mm_rows([h], [g_ple.reshape(1, d)], w_gate.astype(BF16), [h, pp], prologue=_pro_rms,
                    epilogue=_epi_ple, out_dtype=F32, tm=tm, tn=tn, name="ple_gate")[0]


def kernel(x_prompt, x_sample, state_conv, state_ssm, cache_kv_latent, cache_k_rope, p_prompt, p_sample,
           g_mix, w_ssm_in, w_conv, b_conv, dt_bias, a_log, d_skip, g_ssm_norm, w_ssm_out,
           g_kv_in, w_dkv, g_kv, w_uk, w_uv, w_dq, g_q, w_uq, w_o,
           g_ffn, w_router_grp, b_router_grp, w_router_exp, b_router_exp, w_exp_gate, w_exp_up, w_exp_down,
           g_ple, w_ple_gate, w_ple_proj, g_final):
    bp, lp_each, d = x_prompt.shape
    bs, ls, _ = x_sample.shape
    past = cache_kv_latent.shape[1]
    assert bp == 1 and w_ssm_in.shape[0] == 1 and w_dq.shape[0] == 1 and d == D_MODEL
    assert ls >= CONV_W - 1 and ls % SUBLANES == 0 and past % CHUNK == 0
    lp = bp * lp_each
    ts = bs * ls
    t = lp + ts
    tm = min(512, math.gcd(lp, ts))
    tn = 512
    g_, r_, n_, hd = SSM_GROUPS, HEADS_PER_GROUP, SSM_STATE, SSM_HEAD_DIM

    h = jnp.concatenate([x_prompt.reshape(lp, d), x_sample.reshape(ts, d)], axis=0)

    w_in = w_ssm_in[0]
    w_z = w_in[:, :D_INNER].astype(BF16)
    w_x = w_in[:, D_INNER:D_INNER + CONV_DIM].astype(BF16)
    w_dt = jnp.pad(w_in[:, D_INNER + CONV_DIM:], ((0, 0), (0, LANES - SSM_HEADS))).astype(BF16)
    g0 = g_mix[0].reshape(1, d)
    mm_in = functools.partial(_mm_rows, [h], [g0], prologue=_pro_rms, epilogue=_epi_id, tm=tm)
    z = mm_in(w_z, [], out_dtype=BF16, tn=tn, name="ssm_in_z")[0]
    xbc = mm_in(w_x, [], out_dtype=F32, tn=tn, name="ssm_in_xbc")[0]
    dt_raw = mm_in(w_dt, [], out_dtype=F32, tn=LANES, name="ssm_in_dt")[0][:, :SSM_HEADS]

    wc = w_conv[0]
    bc = b_conv[0].reshape(1, CONV_DIM)
    tl_p = min(512, lp)
    act_p = _conv_silu(xbc.reshape(1, t, CONV_DIM), 0, 1, lp, jnp.zeros((1, SUBLANES, CONV_DIM), F32), wc, bc,
                       tl=tl_p, tc=1536)
    st_s = jnp.pad(state_conv[0].astype(F32), ((0, 0), (SUBLANES - (CONV_W - 1), 0), (0, 0)))
    act_s = _conv_silu(xbc.reshape(t // ls, ls, CONV_DIM), lp // ls, bs, ls, st_s, wc, bc, tl=ls, tc=1536)
    conv_p = xbc[lp - (CONV_W - 1):lp].reshape(1, 1, CONV_W - 1, CONV_DIM)
    conv_s = xbc[lp:].reshape(bs, ls, CONV_DIM)[:, ls - (CONV_W - 1):].reshape(1, bs, CONV_W - 1, CONV_DIM)

    bias = dt_bias[0].astype(F32).reshape(g_, 1, r_)
    a_neg = (-jnp.exp(a_log[0].astype(F32))).reshape(g_, 1, r_)
    dsk = jnp.repeat(d_skip[0].astype(F32), hd).reshape(1, D_INNER)

    def dt_views(rows, nb, L):
        v = rows.reshape(nb, L, g_, r_).transpose(0, 2, 1, 3)
        return v, v.transpose(0, 1, 3, 2)

    def scan(act, rows, s0, q):
        nb, L, _ = act.shape
        dtg, dtg_t = dt_views(rows, nb, L)
        return _ssd_scan(act, dtg, dtg_t, bias, bias.transpose(0, 2, 1), a_neg, a_neg.transpose(0, 2, 1),
                         dsk, s0, q=q)

    def state_in(s):
        nb = s.shape[0]
        return s.astype(F32).reshape(nb, g_, r_ * hd, n_).transpose(0, 1, 3, 2)

    def state_out(s):
        nb = s.shape[0]
        return s.transpose(0, 1, 3, 2).reshape(1, nb, SSM_HEADS, hd, n_)

    y_p, s_p = scan(act_p, dt_raw[:lp], jnp.zeros((1, g_, n_, r_ * hd), F32), min(SSD_Q, lp))
    y_s, s_s = scan(act_s, dt_raw[lp:], state_in(state_ssm[0]), ls)
    ssm_p, ssm_s = state_out(s_p), state_out(s_s)

    y_all = jnp.concatenate([y_p.reshape(lp, D_INNER), y_s.reshape(ts, D_INNER)], axis=0)
    h = _mm_rows([y_all, z], [g_ssm_norm[0].reshape(1, D_INNER)], w_ssm_out[0].astype(BF16), [h],
                 prologue=_pro_gated, epilogue=_epi_res, out_dtype=F32, tm=tm, tn=tn, name="ssm_out")[0]

    def ffn_and_ple(h, i):
        h = _moe_layer(h, g_ffn[i], w_router_grp[i], b_router_grp[i], w_router_exp[i], b_router_exp[i],
                       w_exp_gate[i], w_exp_up[i], w_exp_down[i], tm=tm)
        p_rows = jnp.concatenate([p_prompt[i].reshape(lp, -1), p_sample[i].reshape(ts, -1)], axis=0)
        return _ple_layer(h, p_rows, g_ple[i], w_ple_gate[i], w_ple_proj[i], tm=tm, tn=tn)

    h = ffn_and_ple(h, 0)

    pos = np.concatenate([np.arange(lp), np.tile(past + np.arange(ls), bs)])
    cos64, sin64 = _rope_tables(pos)
    gk = g_kv_in.reshape(1, d)
    ckv = _mm_rows([h], [gk], w_dkv[:, :KV_RANK].astype(BF16), [], prologue=_pro_rms, epilogue=_epi_rms_out,
                   out_dtype=F32, tm=tm, tn=KV_RANK, name="kv_latent", e_consts=[g_kv.reshape(1, KV_RANK)])[0]
    w_r = w_dkv[:, KV_RANK:]
    row64 = pl.BlockSpec((tm, QK_ROPE), lambda i, j: (i, 0))
    w64 = pl.BlockSpec((d, QK_ROPE), lambda i, j: (0, 0))
    kr = _fused_mm(
        [(h, pl.BlockSpec((tm, d), lambda i, j: (i, 0))), (gk, pl.BlockSpec((1, d), lambda i, j: (0, 0)))],
        [(w_r.astype(BF16), w64), (_swap_rope_halves(w_r).astype(BF16), w64)],
        [(jnp.asarray(cos64), row64), (jnp.asarray(sin64), row64)],
        [(jax.ShapeDtypeStruct((t, QK_ROPE), F32), row64)],
        grid=(t // tm, 1), prologue=_pro_rms, epilogue=_epi_rot, xn_shape=(tm, d), name="k_rope")[0]

    ql = _mm_rows([h], [g_mix[1].reshape(1, d)], w_dq[0].astype(BF16), [], prologue=_pro_rms,
                  epilogue=_epi_rms_out, out_dtype=BF16, tm=tm, tn=Q_RANK, name="q_latent",
                  e_consts=[g_q[0].reshape(1, Q_RANK)])[0]
    nh = MLA_HEADS
    w_q = w_uq[0].reshape(Q_RANK, nh, QK_DIM).transpose(1, 0, 2)
    w_q_rot = jnp.concatenate([jnp.zeros((nh, Q_RANK, QK_NOPE), F32), _swap_rope_halves(w_q[..., QK_NOPE:])], -1)
    cos_q = np.concatenate([np.ones((t, QK_NOPE), np.float32), cos64], axis=1)
    sin_q = np.concatenate([np.zeros((t, QK_NOPE), np.float32), sin64], axis=1)
    rowq = pl.BlockSpec((tm, QK_DIM), lambda i, j: (i, 0))
    wq_spec = pl.BlockSpec((None, Q_RANK, QK_DIM), lambda i, j: (j, 0, 0))
    q_cat = _fused_mm(
        [(ql, pl.BlockSpec((tm, Q_RANK), lambda i, j: (i, 0)))],
        [(w_q.astype(BF16), wq_spec), (w_q_rot.astype(BF16), wq_spec)],
        [(jnp.asarray(cos_q), rowq), (jnp.asarray(sin_q), rowq)],
        [(jax.ShapeDtypeStruct((nh, t, QK_DIM), BF16), pl.BlockSpec((None, tm, QK_DIM), lambda i, j: (j, i, 0)))],
        grid=(t // tm, nh), prologue=_pro_cast, epilogue=_epi_rot, xn_shape=(tm, Q_RANK), name="q_heads")[0]

    lat_pad = LAT_PAD - KV_RANK - QK_ROPE
    ckr_p = jnp.concatenate([ckv[:lp], kr[:lp], jnp.zeros((lp, lat_pad), F32)], axis=1).astype(BF16)
    w_uk_h = w_uk.transpose(1, 0, 2)
    w_uv_h = w_uv.transpose(1, 0, 2)
    eye_r = jnp.broadcast_to(jnp.eye(QK_ROPE, dtype=F32), (nh, QK_ROPE, QK_ROPE))
    w_k_ext = jnp.concatenate([
        jnp.concatenate([w_uk_h, jnp.zeros((nh, KV_RANK, QK_ROPE), F32)], axis=2),
        jnp.concatenate([jnp.zeros((nh, QK_ROPE, QK_NOPE), F32), eye_r], axis=2),
        jnp.zeros((nh, lat_pad, QK_DIM), F32)], axis=1).astype(BF16)
    w_v_ext = jnp.concatenate([w_uv_h, jnp.zeros((nh, LAT_PAD - KV_RANK, V_DIM), F32)], axis=1).astype(BF16)
    k_cat, v_h = _fused_mm(
        [(ckr_p, pl.BlockSpec((tm, LAT_PAD), lambda i, j: (i, 0)))],
        [(w_k_ext, pl.BlockSpec((None, LAT_PAD, QK_DIM), lambda i, j: (j, 0, 0))),
         (w_v_ext, pl.BlockSpec((None, LAT_PAD, V_DIM), lambda i, j: (j, 0, 0)))],
        [],
        [(jax.ShapeDtypeStruct((nh, lp, QK_DIM), BF16), pl.BlockSpec((None, tm, QK_DIM), lambda i, j: (j, i, 0))),
         (jax.ShapeDtypeStruct((nh, lp, V_DIM), BF16), pl.BlockSpec((None, tm, V_DIM), lambda i, j: (j, i, 0)))],
        grid=(lp // tm, nh), prologue=_pro_cast, epilogue=_epi_id, xn_shape=(tm, LAT_PAD), name="kv_heads")
    o_p = _flash_prompt(q_cat, k_cat, v_h, lp, tq=min(512, lp))

    w_abs = jnp.concatenate([
        jnp.concatenate([w_uk_h.transpose(0, 2, 1), jnp.zeros((nh, QK_NOPE, LAT_PAD - KV_RANK), F32)], axis=2),
        jnp.concatenate([jnp.zeros((nh, QK_ROPE, KV_RANK), F32), eye_r,
                         jnp.zeros((nh, QK_ROPE, lat_pad), F32)], axis=2)], axis=1).astype(BF16)
    blk_s = lp // ts if lp % ts == 0 else None
    assert blk_s is not None
    q_abs = _fused_mm(
        [(q_cat, pl.BlockSpec((None, ts, QK_DIM), lambda i, j: (j, blk_s + i, 0)))],
        [(w_abs, pl.BlockSpec((None, QK_DIM, LAT_PAD), lambda i, j: (j, 0, 0)))],
        [],
        [(jax.ShapeDtypeStruct((nh, ts, LAT_PAD), BF16), pl.BlockSpec((None, ts, LAT_PAD), lambda i, j: (j, i, 0)))],
        grid=(1, nh), prologue=_pro_cast, epilogue=_epi_id, x_per_j=True, name="q_absorb")[0]
    s_valid = past + ls
    s_pad = -(-s_valid // LANES) * LANES
    kc_new = jnp.concatenate([ckv[lp:], kr[lp:]], axis=1).reshape(bs, ls, KV_RANK + QK_ROPE)
    kc = jnp.concatenate([jnp.concatenate([cache_kv_latent, cache_k_rope], axis=2), kc_new], axis=1)
    kc = jnp.pad(kc, ((0, 0), (0, s_pad - s_valid), (0, lat_pad))).astype(BF16)
    o_lat = _attn_sample(q_abs, kc, ls=ls, past=past, s_valid=s_valid)
    o_s = _fused_mm(
        [(o_lat, pl.BlockSpec((None, ts, KV_RANK), lambda i, j: (j, i, 0)))],
        [(w_uv_h.astype(BF16), pl.BlockSpec((None, KV_RANK, V_DIM), lambda i, j: (j, 0, 0)))],
        [],
        [(jax.ShapeDtypeStruct((ts, nh * V_DIM), BF16), pl.BlockSpec((ts, V_DIM), lambda i, j: (i, j)))],
        grid=(1, nh), prologue=_pro_cast, epilogue=_epi_id, x_per_j=True, name="v_absorb")[0]

    o_all = jnp.concatenate([o_p, o_s], axis=0)
    h = _mm_rows([o_all], [], w_o[0].astype(BF16), [h], prologue=_pro_cast, epilogue=_epi_res,
                 out_dtype=F32, tm=tm, tn=tn, name="attn_out")[0]
    h = ffn_and_ple(h, 1)

    gf = g_final.reshape(1, d)
    y_p_out = _final_norm(h, gf, 0, lp, tm=tm).reshape(bp, lp_each, d)
    y_s_out = _final_norm(h, gf, lp, ts, tm=tm).reshape(bs, ls, d)
    return (y_p_out, y_s_out, conv_p, ssm_p, ckv[:lp].reshape(bp, lp_each, KV_RANK),
            kr[:lp].reshape(bp, lp_each, QK_ROPE), conv_s, ssm_s, ckv[lp:].reshape(bs, ls, KV_RANK),
            kr[lp:].reshape(bs, ls, QK_ROPE))
```

```python
import functools
import math

import numpy as np
import jax
import jax.numpy as jnp
from jax import lax
from jax.experimental import pallas as pl
from jax.experimental.pallas import tpu as pltpu

F32 = jnp.float32
BF16 = jnp.bfloat16
HIGHEST = lax.Precision.HIGHEST

EPS = 1e-6
CHUNK = 64
D_MODEL = 2048
D_INNER = 2 * D_MODEL
SSM_HEAD_DIM = 64
SSM_HEADS = D_INNER // SSM_HEAD_DIM
SSM_STATE = 128
SSM_GROUPS = 8
HEADS_PER_GROUP = SSM_HEADS // SSM_GROUPS
GROUP_WIDTH = D_INNER // SSM_GROUPS
CONV_W = 4
CONV_DIM = D_INNER + 2 * SSM_GROUPS * SSM_STATE
MLA_HEADS = 16
Q_RANK = 512
KV_RANK = 512
QK_NOPE = 128
QK_ROPE = 64
QK_DIM = QK_NOPE + QK_ROPE
V_DIM = 128
ROPE_THETA = 10000.0
ATTN_SCALE = QK_DIM ** -0.5
QK_PRESCALE = ATTN_SCALE * math.log2(math.e)
MOE_GROUPS = 4
EXPERTS_PER_GROUP = 8
N_EXPERTS = MOE_GROUPS * EXPERTS_PER_GROUP
TOP_K = 2
D_EXPERT = 512

LANES = 128
SUBLANES = 8
VMEM_LIMIT = 56 * 1024 * 1024
SSD_Q = 128
MOE_BM = 256
LAT_PAD = 640


def _cparams(sem):
    return pltpu.CompilerParams(dimension_semantics=sem, vmem_limit_bytes=VMEM_LIMIT)


def _sigmoid(v):
    return 1.0 / (1.0 + jnp.exp(-v))


def _silu(v):
    return v * _sigmoid(v)


def _softplus(v):
    return jnp.maximum(v, 0.0) + jnp.log1p(jnp.exp(-jnp.abs(v)))


def _rms_rows(x):
    return x * lax.rsqrt(jnp.mean(x * x, axis=-1, keepdims=True) + EPS)


def _mm_body(*refs, nx, nw, ne, no, prologue, epilogue, x_per_j, emit_xn, precision):
    x_refs = refs[:nx]
    w_refs = refs[nx:nx + nw]
    e_refs = refs[nx + nw:nx + nw + ne]
    o_refs = refs[nx + nw + ne:nx + nw + ne + no]
    rest = refs[nx + nw + ne + no:]
    j = pl.program_id(1)
    if x_per_j:
        xn = prologue(*[r[...] for r in x_refs])
    else:
        xn_ref = rest[-1]

        @pl.when(j == 0)
        def _():
            v = prologue(*[r[...] for r in x_refs])
            xn_ref[...] = v.astype(xn_ref.dtype)
            if emit_xn:
                rest[0][...] = v.astype(rest[0].dtype)

        xn = xn_ref[...]
    accs = [jnp.dot(xn, w[...], preferred_element_type=F32, precision=precision) for w in w_refs]
    outs = epilogue(accs, [e[...] for e in e_refs])
    for o_ref, o in zip(o_refs, outs):
        o_ref[...] = o.astype(o_ref.dtype)


def _fused_mm(x_args, w_args, e_args, out_defs, *, grid, prologue, epilogue, xn_shape=None,
              xn_dtype=BF16, x_per_j=False, xn_out=None, precision=None, name=None):
    arrays = [a for a, _ in x_args + w_args + e_args]
    in_specs = [s for _, s in x_args + w_args + e_args]
    out_shape = [d for d, _ in out_defs]
    out_specs = [s for _, s in out_defs]
    emit_xn = xn_out is not None
    if emit_xn:
        out_shape.append(xn_out[0])
        out_specs.append(xn_out[1])
    scratch = [] if x_per_j else [pltpu.VMEM(xn_shape, xn_dtype)]
    body = functools.partial(
        _mm_body, nx=len(x_args), nw=len(w_args), ne=len(e_args), no=len(out_defs),
        prologue=prologue, epilogue=epilogue, x_per_j=x_per_j, emit_xn=emit_xn, precision=precision)
    return pl.pallas_call(
        body, grid=grid, in_specs=in_specs, out_specs=out_specs, out_shape=out_shape,
        scratch_shapes=scratch, compiler_params=_cparams(("parallel", "arbitrary")), name=name,
    )(*arrays)


def _pro_rms(x, g):
    return _rms_rows(x.astype(F32)) * g


def _pro_cast(x):
    return x


def _pro_gated(y, z, g):
    v = y.astype(F32) * _silu(z.astype(F32))
    parts = [_rms_rows(v[:, k * GROUP_WIDTH:(k + 1) * GROUP_WIDTH]) for k in range(SSM_GROUPS)]
    return jnp.concatenate(parts, axis=-1) * g


def _epi_id(accs, es):
    return accs


def _epi_res(accs, es):
    return [es[0] + accs[0]]


def _epi_ple(accs, es):
    h, pp = es
    return [h + _sigmoid(accs[0]) * pp]


def _epi_rms_out(accs, es):
    return [_rms_rows(accs[0]) * es[0]]


def _epi_rot(accs, es):
    cos, sin = es
    return [accs[0] * cos + accs[1] * sin]


def _epi_route(accs, es):
    lg = accs[0] + es[0]
    lane = lax.broadcasted_iota(jnp.int32, lg.shape, 1).astype(F32)
    neg = -jnp.inf
    big = 1.0e4
    is_grp = lane < MOE_GROUPS
    gl = jnp.where(is_grp, lg, neg)
    mg = jnp.max(gl, axis=-1, keepdims=True)
    g_sel = jnp.min(jnp.where(gl == mg, lane, big), axis=-1, keepdims=True)
    p_sel = 1.0 / jnp.sum(jnp.where(is_grp, jnp.exp(gl - mg), 0.0), axis=-1, keepdims=True)
    lo = MOE_GROUPS + g_sel * EXPERTS_PER_GROUP
    in_grp = jnp.where(lane >= lo, jnp.where(lane < lo + EXPERTS_PER_GROUP, 1.0, 0.0), 0.0) > 0.5
    el = jnp.where(in_grp, lg, neg)
    v1 = jnp.max(el, axis=-1, keepdims=True)
    i1 = jnp.min(jnp.where(el == v1, lane, big), axis=-1, keepdims=True)
    el2 = jnp.where(lane == i1, neg, el)
    v2 = jnp.max(el2, axis=-1, keepdims=True)
    i2 = jnp.min(jnp.where(el2 == v2, jnp.where(lane == i1, big, lane), big), axis=-1, keepdims=True)
    e21 = jnp.exp(v2 - v1)
    g1 = p_sel / (1.0 + e21)
    g2 = p_sel * e21 / (1.0 + e21)
    idx = jnp.where(lane == 0.0, i1 - MOE_GROUPS, jnp.where(lane == 1.0, i2 - MOE_GROUPS, 0.0))
    gates = jnp.where(lane == 0.0, g1, jnp.where(lane == 1.0, g2, 0.0))
    return [idx.astype(jnp.int32), gates]


def _mm_rows(x_list, consts, w, extras, *, prologue, epilogue, out_dtype, tm, tn, name,
             xn_dtype=BF16, precision=None, emit_xn_dtype=None, e_consts=()):
    m = x_list[0].shape[0]
    k, n = w.shape
    grid = (m // tm, n // tn)
    x_args = [(x, pl.BlockSpec((tm, x.shape[1]), lambda i, j: (i, 0))) for x in x_list]
    x_args += [(c, pl.BlockSpec((1, c.shape[1]), lambda i, j: (0, 0))) for c in consts]
    w_args = [(w, pl.BlockSpec((k, tn), lambda i, j: (0, j)))]
    e_args = [(c, pl.BlockSpec((1, tn), lambda i, j: (0, j))) for c in e_consts]
    e_args += [(e, pl.BlockSpec((tm, tn), lambda i, j: (i, j))) for e in extras]
    out_defs = [(jax.ShapeDtypeStruct((m, n), out_dtype), pl.BlockSpec((tm, tn), lambda i, j: (i, j)))]
    xn_out = None
    if emit_xn_dtype is not None:
        xn_out = (jax.ShapeDtypeStruct((m, k), emit_xn_dtype), pl.BlockSpec((tm, k), lambda i, j: (i, 0)))
    return _fused_mm(x_args, w_args, e_args, out_defs, grid=grid, prologue=prologue, epilogue=epilogue,
                     xn_shape=(tm, k), xn_dtype=xn_dtype, xn_out=xn_out, precision=precision, name=name)


def _conv_body(x_ref, prev_ref, st_ref, w_ref, b_ref, o_ref, ext):
    i = pl.program_id(1)
    tl = x_ref.shape[0]
    ext[SUBLANES:SUBLANES + tl, :] = x_ref[...]

    @pl.when(i == 0)
    def _():
        ext[0:SUBLANES, :] = st_ref[...]

    @pl.when(i > 0)
    def _():
        ext[0:SUBLANES, :] = prev_ref[...]

    acc = b_ref[...]
    for k in range(CONV_W):
        off = SUBLANES - (CONV_W - 1) + k
        acc = acc + ext[off:off + tl, :] * w_ref[k:k + 1, :]
    o_ref[...] = _silu(acc).astype(o_ref.dtype)


def _conv_silu(x3, batch0, nb, L, st, w, b, *, tl, tc):
    c = x3.shape[2]
    grid = (nb, L // tl, c // tc)
    per8 = tl // SUBLANES
    return pl.pallas_call(
        _conv_body, grid=grid,
        in_specs=[
            pl.BlockSpec((None, tl, tc), lambda bb, i, j: (batch0 + bb, i, j)),
            pl.BlockSpec((None, SUBLANES, tc), lambda bb, i, j: (batch0 + bb, jnp.maximum(i * per8 - 1, 0), j)),
            pl.BlockSpec((None, SUBLANES, tc), lambda bb, i, j: (bb, 0, j)),
            pl.BlockSpec((CONV_W, tc), lambda bb, i, j: (0, j)),
            pl.BlockSpec((1, tc), lambda bb, i, j: (0, j)),
        ],
        out_specs=pl.BlockSpec((None, tl, tc), lambda bb, i, j: (bb, i, j)),
        out_shape=jax.ShapeDtypeStruct((nb, L, c), BF16),
        scratch_shapes=[pltpu.VMEM((tl + SUBLANES, tc), F32)],
        compiler_params=_cparams(("parallel", "parallel", "parallel")), name="conv_silu",
    )(x3, x3, st, w, b)


def _ssd_body(xs_ref, b_ref, c_ref, dt_ref, dtT_ref, bias_ref, biasT_ref, a_ref, aT_ref, dsk_ref, s0_ref,
              y_ref, so_ref, s_scr, *, q):
    c = pl.program_id(2)

    @pl.when(c == 0)
    def _():
        s_scr[...] = s0_ref[...]

    dt = _softplus(dt_ref[...] + bias_ref[...])
    dta = dt * a_ref[...]
    dta_t = _softplus(dtT_ref[...] + biasT_ref[...]) * aT_ref[...]
    row = lax.broadcasted_iota(jnp.int32, (q, q), 0)
    col = lax.broadcasted_iota(jnp.int32, (q, q), 1)
    causal = row >= col
    tri = jnp.where(causal, 1.0, 0.0)
    tri_t = jnp.where(row <= col, 1.0, 0.0)
    acum = jnp.dot(tri, dta, preferred_element_type=F32, precision=HIGHEST)
    acum_t = jnp.dot(dta_t, tri_t, preferred_element_type=F32, precision=HIGHEST)
    bm = b_ref[...]
    cm = c_ref[...]
    cb = lax.dot_general(cm, bm, (((1,), (1,)), ((), ())), preferred_element_type=F32)
    s_prev = s_scr[...]
    y_off = jnp.dot(cm, s_prev.astype(BF16), preferred_element_type=F32)
    left = lax.broadcasted_iota(jnp.int32, (q, LANES), 1) < SSM_HEAD_DIM
    xw_parts, dec_parts = [], []
    for j in range(HEADS_PER_GROUP // 2):
        h0, h1 = 2 * j, 2 * j + 1
        sl = slice(j * LANES, (j + 1) * LANES)
        col0, col1 = acum[:, h0:h0 + 1], acum[:, h1:h1 + 1]
        pa = jnp.where(left, col0, col1)
        dtp = jnp.where(left, dt[:, h0:h0 + 1], dt[:, h1:h1 + 1])
        x = xs_ref[:, sl].astype(F32)
        xdt = x * dtp
        m0 = (jnp.exp(jnp.where(causal, col0 - acum_t[h0:h0 + 1, :], -jnp.inf)) * cb).astype(BF16)
        m1 = (jnp.exp(jnp.where(causal, col1 - acum_t[h1:h1 + 1, :], -jnp.inf)) * cb).astype(BF16)
        x_l = jnp.where(left, xdt, 0.0).astype(BF16)
        x_r = jnp.where(left, 0.0, xdt).astype(BF16)
        y_diag = (jnp.dot(m0, x_l, preferred_element_type=F32)
                  + jnp.dot(m1, x_r, preferred_element_type=F32))
        y = y_diag + y_off[:, sl] * jnp.exp(pa) + x * dsk_ref[:, sl]
        y_ref[:, sl] = y.astype(y_ref.dtype)
        last = pa[q - 1:q, :]
        xw_parts.append((xdt * jnp.exp(last - pa)).astype(BF16))
        dec_parts.append(jnp.exp(last))
    xw = jnp.concatenate(xw_parts, axis=1)
    dec = jnp.concatenate(dec_parts, axis=1)
    s_new = s_prev * dec + lax.dot_general(bm, xw, (((0,), (0,)), ((), ())), preferred_element_type=F32)
    s_scr[...] = s_new
    so_ref[...] = s_new


def _ssd_scan(act, dtg, dtg_t, bias, bias_t, a, a_t, dsk, s0, *, q):
    nb, L, _ = act.shape
    g_, r_, n_ = SSM_GROUPS, HEADS_PER_GROUP, SSM_STATE
    gw = GROUP_WIDTH
    grid = (nb, g_, L // q)
    b_off = D_INNER // n_
    c_off = b_off + g_
    return pl.pallas_call(
        functools.partial(_ssd_body, q=q), grid=grid,
        in_specs=[
            pl.BlockSpec((None, q, gw), lambda b, g, c: (b, c, g)),
            pl.BlockSpec((None, q, n_), lambda b, g, c: (b, c, b_off + g)),
            pl.BlockSpec((None, q, n_), lambda b, g, c: (b, c, c_off + g)),
            pl.BlockSpec((None, None, q, r_), lambda b, g, c: (b, g, c, 0)),
            pl.BlockSpec((None, None, r_, q), lambda b, g, c: (b, g, 0, c)),
            pl.BlockSpec((None, 1, r_), lambda b, g, c: (g, 0, 0)),
            pl.BlockSpec((None, r_, 1), lambda b, g, c: (g, 0, 0)),
            pl.BlockSpec((None, 1, r_), lambda b, g, c: (g, 0, 0)),
            pl.BlockSpec((None, r_, 1), lambda b, g, c: (g, 0, 0)),
            pl.BlockSpec((1, gw), lambda b, g, c: (0, g)),
            pl.BlockSpec((None, None, n_, gw), lambda b, g, c: (b, g, 0, 0)),
        ],
        out_specs=[
            pl.BlockSpec((None, q, gw), lambda b, g, c: (b, c, g)),
            pl.BlockSpec((None, None, n_, gw), lambda b, g, c: (b, g, 0, 0)),
        ],
        out_shape=[jax.ShapeDtypeStruct((nb, L, D_INNER), BF16),
                   jax.ShapeDtypeStruct((nb, g_, n_, gw), F32)],
        scratch_shapes=[pltpu.VMEM((n_, gw), F32)],
        compiler_params=_cparams(("parallel", "parallel", "arbitrary")), name="ssd_scan",
    )(act, act, act, dtg, dtg_t, bias, bias_t, a, a_t, dsk, s0)


def _gather_rows(tok_ref, base, src_hbm, dst, sem, n):
    def body(r, carry):
        t = tok_ref[base + r]
        pltpu.make_async_copy(src_hbm.at[pl.ds(t, 1)], dst.at[pl.ds(r, 1)], sem).start()
        return carry

    lax.fori_loop(0, n, body, 0)


def _moe_body(blk_e_ref, nxt_e_ref, tok_ref, nused_ref, xt_hbm, wg_hbm, wu_hbm, wd_hbm, yb_ref,
              xbuf, xsem, wg32, wu32, wd32, wsem, wg_bf, wu_bf, wd_bf, xb, *, bm, layer):
    b = pl.program_id(0)
    slot = b % 2
    nused = nused_ref[0]
    e_cur = blk_e_ref[b]

    def weight_copies(e):
        return (pltpu.make_async_copy(wg_hbm.at[layer, e], wg32, wsem.at[0]),
                pltpu.make_async_copy(wu_hbm.at[layer, e], wu32, wsem.at[1]),
                pltpu.make_async_copy(wd_hbm.at[layer, e], wd32, wsem.at[2]))

    def gather_unrolled(base, slot_):
        for r in range(bm):
            t = tok_ref[base + r]
            pltpu.make_async_copy(xt_hbm.at[pl.ds(t, 1)], xbuf.at[slot_, pl.ds(r, 1)], xsem.at[slot_]).start()

    @pl.when(jnp.logical_and(b == 0, nused > 0))
    def _():
        for cp in weight_copies(e_cur):
            cp.start()
        _gather_rows(tok_ref, 0, xt_hbm, xbuf.at[0], xsem.at[0], bm)

    prev_e = blk_e_ref[jnp.maximum(b - 1, 0)]
    changed = jnp.logical_or(b == 0, e_cur != prev_e)

    @pl.when(jnp.logical_and(changed, b < nused))
    def _():
        for cp in weight_copies(e_cur):
            cp.wait()
        wg_bf[...] = wg32[...].astype(BF16)
        wu_bf[...] = wu32[...].astype(BF16)
        wd_bf[...] = wd32[...].astype(BF16)
        nxt = nxt_e_ref[b]

        @pl.when(nxt >= 0)
        def _():
            for cp in weight_copies(nxt):
                cp.start()

    def compute(prefetch):
        pltpu.make_async_copy(xbuf.at[slot], xbuf.at[slot], xsem.at[slot]).wait()
        xb[...] = xbuf[slot].astype(BF16)
        if prefetch:
            gather_unrolled((b + 1) * bm, 1 - slot)
        x = xb[...]
        g = jnp.dot(x, wg_bf[...], preferred_element_type=F32)
        u = jnp.dot(x, wu_bf[...], preferred_element_type=F32)
        hid = (_silu(g) * u).astype(BF16)
        yb_ref[...] = jnp.dot(hid, wd_bf[...], preferred_element_type=F32)

    @pl.when(b + 1 < nused)
    def _():
        compute(True)

    @pl.when(b + 1 == nused)
    def _():
        compute(False)

    @pl.when(b >= nused)
    def _():
        yb_ref[...] = jnp.zeros(yb_ref.shape, yb_ref.dtype)


def _moe_experts(xt, blk_e, nxt_e, row_token, nused, w_gate, w_up, w_down, *, bm, layer):
    nblk = blk_e.shape[0]
    d = xt.shape[1]
    de = w_gate.shape[3]
    anyspec = pl.BlockSpec(memory_space=pl.ANY)
    gs = pltpu.PrefetchScalarGridSpec(
        num_scalar_prefetch=4, grid=(nblk,),
        in_specs=[anyspec, anyspec, anyspec, anyspec],
        out_specs=pl.BlockSpec((bm, d), lambda b, be, nx, tok, nu: (b, 0)),
        scratch_shapes=[
            pltpu.VMEM((2, bm, d), F32),
            pltpu.SemaphoreType.DMA((2,)),
            pltpu.VMEM((d, de), F32),
            pltpu.VMEM((d, de), F32),
            pltpu.VMEM((de, d), F32),
            pltpu.SemaphoreType.DMA((3,)),
            pltpu.VMEM((d, de), BF16),
            pltpu.VMEM((d, de), BF16),
            pltpu.VMEM((de, d), BF16),
            pltpu.VMEM((bm, d), BF16),
        ],
    )
    return pl.pallas_call(
        functools.partial(_moe_body, bm=bm, layer=layer), grid_spec=gs,
        out_shape=jax.ShapeDtypeStruct((nblk * bm, d), F32),
        compiler_params=_cparams(("arbitrary",)), name="moe_experts",
    )(blk_e, nxt_e, row_token, nused, xt, w_gate, w_up, w_down)


def _combine_body(dest_ref, h_ref, g_ref, yb_hbm, o_ref, buf, sem, *, tc, ntile):
    i = pl.program_id(0)
    slot = i % 2
    n = TOP_K * tc

    @pl.when(i == 0)
    def _():
        _gather_rows(dest_ref, 0, yb_hbm, buf.at[0], sem.at[0], n)

    @pl.when(i + 1 < ntile)
    def _():
        _gather_rows(dest_ref, (i + 1) * n, yb_hbm, buf.at[1 - slot], sem.at[1 - slot], n)

    pltpu.make_async_copy(buf.at[slot], buf.at[slot], sem.at[slot]).wait()
    g = g_ref[...]
    o_ref[...] = (h_ref[...] + g[:, 0:1] * buf[slot, 0:tc, :] + g[:, 1:2] * buf[slot, tc:2 * tc, :])


def _moe_combine(h, yb, dest_tiles, gates, *, tc):
    t, d = h.shape
    ntile = t // tc
    gs = pltpu.PrefetchScalarGridSpec(
        num_scalar_prefetch=1, grid=(ntile,),
        in_specs=[
            pl.BlockSpec((tc, d), lambda i, dst: (i, 0)),
            pl.BlockSpec((tc, LANES), lambda i, dst: (i, 0)),
            pl.BlockSpec(memory_space=pl.ANY),
        ],
        out_specs=pl.BlockSpec((tc, d), lambda i, dst: (i, 0)),
        scratch_shapes=[pltpu.VMEM((2, TOP_K * tc, d), F32), pltpu.SemaphoreType.DMA((2,))],
    )
    return pl.pallas_call(
        functools.partial(_combine_body, tc=tc, ntile=ntile), grid_spec=gs,
        out_shape=jax.ShapeDtypeStruct((t, d), F32),
        compiler_params=_cparams(("arbitrary",)), name="moe_combine",
    )(dest_tiles, h, gates, yb)


def _moe_plan(idx, bm, tc):
    t = idx.shape[0]
    a = t * TOP_K
    e_ = N_EXPERTS
    flat_e = idx.reshape(a)
    onehot = (flat_e[:, None] == jnp.arange(e_, dtype=jnp.int32)[None, :]).astype(jnp.int32)
    csum = jnp.cumsum(onehot, axis=0)
    rank = jnp.sum(onehot * csum, axis=1) - 1
    counts = csum[-1]
    padded = (counts + bm - 1) // bm * bm
    pad_end = jnp.cumsum(padded)
    pad_start = pad_end - padded
    dest = jnp.sum(onehot * pad_start[None, :], axis=1) + rank
    nblk = -(-a // bm) + e_
    rows = nblk * bm
    row_token = jnp.zeros((rows,), jnp.int32).at[dest].set(jnp.arange(a, dtype=jnp.int32) // TOP_K)
    blk_row0 = jnp.arange(nblk, dtype=jnp.int32) * bm
    blk_e = jnp.minimum(jnp.sum((pad_end[None, :] <= blk_row0[:, None]).astype(jnp.int32), axis=1), e_ - 1)
    ids = jnp.arange(e_, dtype=jnp.int32)
    later_used = jnp.logical_and(ids[None, :] > ids[:, None], (counts > 0)[None, :])
    nxt_of_e = jnp.min(jnp.where(later_used, ids[None, :], e_), axis=1)
    nxt_of_e = jnp.where(nxt_of_e == e_, -1, nxt_of_e)
    nxt_e = jnp.sum(jnp.where(blk_e[:, None] == ids[None, :], nxt_of_e[None, :], 0), axis=1).astype(jnp.int32)
    nused = (pad_end[-1] // bm).astype(jnp.int32).reshape(1)
    dest_tiles = dest.reshape(t // tc, tc, TOP_K).transpose(0, 2, 1).reshape(a).astype(jnp.int32)
    return blk_e.astype(jnp.int32), nxt_e, row_token, nused, dest_tiles


def _flash_body(qi_ref, kj_ref, q_ref, k_ref, v_ref, o_ref, m_scr, l_scr, acc_scr, *, hb):
    p_id = pl.program_id(1)
    qi = qi_ref[p_id]
    kj = kj_ref[p_id]
    tk = k_ref.shape[1]

    @pl.when(kj == 0)
    def _():
        m_scr[...] = jnp.full(m_scr.shape, -jnp.inf, F32)
        l_scr[...] = jnp.zeros(l_scr.shape, F32)
        acc_scr[...] = jnp.zeros(acc_scr.shape, F32)

    def step(diag):
        for hh in range(hb):
            s = lax.dot_general(q_ref[hh], k_ref[hh], (((1,), (1,)), ((), ())), preferred_element_type=F32)
            if diag:
                r = lax.broadcasted_iota(jnp.int32, s.shape, 0)
                c = lax.broadcasted_iota(jnp.int32, s.shape, 1)
                s = jnp.where((c // CHUNK) <= (r // CHUNK), s, -jnp.inf)
            m_prev = m_scr[hh]
            m_next = jnp.maximum(m_prev, jnp.max(s, axis=1, keepdims=True))
            p = jnp.exp2(s - jnp.concatenate([m_next] * (tk // LANES), axis=1))
            alpha = jnp.exp2(m_prev - m_next)
            l_scr[hh] = alpha * l_scr[hh] + jnp.sum(p, axis=1, keepdims=True)
            acc_scr[hh] = alpha * acc_scr[hh] + jnp.dot(p.astype(BF16), v_ref[hh], preferred_element_type=F32)
            m_scr[hh] = m_next

    @pl.when(kj < qi)
    def _():
        step(False)

    @pl.when(kj == qi)
    def _():
        step(True)
        for hh in range(hb):
            o_ref[:, hh * V_DIM:(hh + 1) * V_DIM] = (acc_scr[hh] / l_scr[hh]).astype(o_ref.dtype)


def _flash_prompt(q, k, v, lp, *, tq, hb):
    assert V_DIM == LANES
    nh = q.shape[0]
    nq = lp // tq
    pairs = [(a, b) for a in range(nq) for b in range(a + 1)]
    qi_tab = jnp.asarray(np.array([p[0] for p in pairs], np.int32))
    kj_tab = jnp.asarray(np.array([p[1] for p in pairs], np.int32))
    gs = pltpu.PrefetchScalarGridSpec(
        num_scalar_prefetch=2, grid=(nh // hb, len(pairs)),
        in_specs=[
            pl.BlockSpec((hb, tq, QK_DIM), lambda h, p, qi, kj: (h, qi[p], 0)),
            pl.BlockSpec((hb, tq, QK_DIM), lambda h, p, qi, kj: (h, kj[p], 0)),
            pl.BlockSpec((hb, tq, V_DIM), lambda h, p, qi, kj: (h, kj[p], 0)),
        ],
        out_specs=pl.BlockSpec((tq, hb * V_DIM), lambda h, p, qi, kj: (qi[p], h)),
        scratch_shapes=[pltpu.VMEM((hb, tq, LANES), F32), pltpu.VMEM((hb, tq, LANES), F32),
                        pltpu.VMEM((hb, tq, V_DIM), F32)],
    )
    return pl.pallas_call(
        functools.partial(_flash_body, hb=hb), grid_spec=gs,
        out_shape=jax.ShapeDtypeStruct((lp, nh * V_DIM), BF16),
        compiler_params=_cparams(("parallel", "arbitrary")), name="flash_prompt",
    )(qi_tab, kj_tab, q, k, v)


def _attn_sample_body(q_ref, k_ref, o_ref, *, ls, past, s_valid):
    nh = q_ref.shape[0]
    q = q_ref[...].reshape(nh * ls, q_ref.shape[2])
    k = k_ref[...]
    s = lax.dot_general(q, k, (((1,), (1,)), ((), ())), preferred_element_type=F32)
    r = lax.broadcasted_iota(jnp.int32, s.shape, 0)
    c = lax.broadcasted_iota(jnp.int32, s.shape, 1)
    q_pos = past + r % ls
    s = jnp.where((c // CHUNK) <= (q_pos // CHUNK), s, -jnp.inf)
    s = jnp.where(c < s_valid, s, -jnp.inf)
    m = jnp.max(s, axis=-1, keepdims=True)
    p = jnp.exp2(s - m)
    l = jnp.sum(p, axis=-1, keepdims=True)
    o = jnp.dot(p.astype(BF16), k[:, :KV_RANK], preferred_element_type=F32) / l
    o_ref[...] = o.reshape(nh, ls, KV_RANK).astype(o_ref.dtype)


def _attn_sample(q_abs, kc, *, ls, past, s_valid):
    nh, ts, dk = q_abs.shape
    nb, s_pad, _ = kc.shape
    return pl.pallas_call(
        functools.partial(_attn_sample_body, ls=ls, past=past, s_valid=s_valid), grid=(nb,),
        in_specs=[
            pl.BlockSpec((nh, ls, dk), lambda b: (0, b, 0)),
            pl.BlockSpec((None, s_pad, dk), lambda b: (b, 0, 0)),
        ],
        out_specs=pl.BlockSpec((nh, ls, KV_RANK), lambda b: (0, b, 0)),
        out_shape=jax.ShapeDtypeStruct((nh, ts, KV_RANK), BF16),
        compiler_params=_cparams(("parallel",)), name="attn_sample",
    )(q_abs, kc)


def _norm_body(x_ref, g_ref, o_ref):
    o_ref[...] = _rms_rows(x_ref[...]) * g_ref[...]


def _final_norm(h, g, row0, nrows, *, tm):
    d = h.shape[1]
    blk0 = row0 // tm
    return pl.pallas_call(
        _norm_body, grid=(nrows // tm,),
        in_specs=[pl.BlockSpec((tm, d), lambda i: (blk0 + i, 0)), pl.BlockSpec((1, d), lambda i: (0, 0))],
        out_specs=pl.BlockSpec((tm, d), lambda i: (i, 0)),
        out_shape=jax.ShapeDtypeStruct((nrows, d), F32),
        compiler_params=_cparams(("parallel",)), name="final_norm",
    )(h, g)


def _rope_tables(pos):
    half = QK_ROPE // 2
    inv = ROPE_THETA ** (-np.arange(half, dtype=np.float64) / half)
    ang = np.asarray(pos, np.float64)[:, None] * inv[None, :]
    cos = np.concatenate([np.cos(ang), np.cos(ang)], axis=1)
    sin = np.concatenate([-np.sin(ang), np.sin(ang)], axis=1)
    return cos.astype(np.float32), sin.astype(np.float32)


def _swap_rope_halves(w):
    half = QK_ROPE // 2
    return jnp.concatenate([w[..., half:], w[..., :half]], axis=-1)


def _moe_layer(h, g_ffn, w_rg, b_rg, w_re, b_re, w_gate, w_up, w_down, *, tm, layer):
    t, d = h.shape
    npad = LANES - MOE_GROUPS - N_EXPERTS
    w_r = jnp.concatenate([w_rg, w_re, jnp.zeros((d, npad), F32)], axis=1)
    b_r = jnp.concatenate([b_rg, b_re, jnp.zeros((npad,), F32)]).reshape(1, LANES)
    grid = (t // tm, 1)
    x_args = [(h, pl.BlockSpec((tm, d), lambda i, j: (i, 0))),
              (g_ffn.reshape(1, d), pl.BlockSpec((1, d), lambda i, j: (0, 0)))]
    w_args = [(w_r, pl.BlockSpec((d, LANES), lambda i, j: (0, 0)))]
    e_args = [(b_r, pl.BlockSpec((1, LANES), lambda i, j: (0, 0)))]
    tile = pl.BlockSpec((tm, LANES), lambda i, j: (i, 0))
    out_defs = [(jax.ShapeDtypeStruct((t, LANES), jnp.int32), tile),
                (jax.ShapeDtypeStruct((t, LANES), F32), tile)]
    xn_out = (jax.ShapeDtypeStruct((t, d), F32), pl.BlockSpec((tm, d), lambda i, j: (i, 0)))
    idx_t, gates_t, xt = _fused_mm(x_args, w_args, e_args, out_defs, grid=grid, prologue=_pro_rms,
                                   epilogue=_epi_route, xn_shape=(tm, d), xn_dtype=F32, xn_out=xn_out,
                                   precision=HIGHEST, name="router")
    tc = min(tm, 128)
    blk_e, nxt_e, row_token, nused, dest_tiles = _moe_plan(idx_t[:, :TOP_K], MOE_BM, tc)
    yb = _moe_experts(xt, blk_e, nxt_e, row_token, nused, w_gate, w_up, w_down, bm=MOE_BM, layer=layer)
    return _moe_combine(h, yb, dest_tiles, gates_t, tc=tc)


def _ple_layer(h, p_rows, g_ple, w_gate, w_proj, *, tm, tn):
    t, d = h.shape
    pp = _mm_rows([p_rows.astype(BF16)], [], w_proj.astype(BF16), [], prologue=_pro_cast, epilogue=_epi_id,
                  out_dtype=F32, tm=tm, tn=tn, name="ple_proj")[0]
    return _mm_rows([h], [g_ple.reshape(1, d)], w_gate.astype(BF16), [h, pp], prologue=_pro_rms,
                    epilogue=_epi_ple, out_dtype=F32, tm=tm, tn=tn, name="ple_gate")[0]


def kernel(x_prompt, x_sample, state_conv, state_ssm, cache_kv_latent, cache_k_rope, p_prompt, p_sample,
           g_mix, w_ssm_in, w_conv, b_conv, dt_bias, a_log, d_skip, g_ssm_norm, w_ssm_out,
           g_kv_in, w_dkv, g_kv, w_uk, w_uv, w_dq, g_q, w_uq, w_o,
           g_ffn, w_router_grp, b_router_grp, w_router_exp, b_router_exp, w_exp_gate, w_exp_up, w_exp_down,
           g_ple, w_ple_gate, w_ple_proj, g_final):
    bp, lp_each, d = x_prompt.shape
    bs, ls, _ = x_sample.shape
    past = cache_kv_latent.shape[1]
    assert bp == 1 and w_ssm_in.shape[0] == 1 and w_dq.shape[0] == 1 and d == D_MODEL
    assert ls >= CONV_W - 1 and ls % SUBLANES == 0 and past % CHUNK == 0
    lp = bp * lp_each
    ts = bs * ls
    t = lp + ts
    tm = min(512, math.gcd(lp, ts))
    tn = 512
    g_, r_, n_, hd = SSM_GROUPS, HEADS_PER_GROUP, SSM_STATE, SSM_HEAD_DIM

    h = jnp.concatenate([x_prompt.reshape(lp, d), x_sample.reshape(ts, d)], axis=0)

    w_in = w_ssm_in[0]
    w_z = w_in[:, :D_INNER].astype(BF16)
    w_x = w_in[:, D_INNER:D_INNER + CONV_DIM].astype(BF16)
    w_dt = jnp.pad(w_in[:, D_INNER + CONV_DIM:], ((0, 0), (0, LANES - SSM_HEADS))).astype(BF16)
    g0 = g_mix[0].reshape(1, d)
    mm_in = functools.partial(_mm_rows, [h], [g0], prologue=_pro_rms, epilogue=_epi_id, tm=tm)
    z = mm_in(w_z, [], out_dtype=BF16, tn=tn, name="ssm_in_z")[0]
    xbc = mm_in(w_x, [], out_dtype=F32, tn=tn, name="ssm_in_xbc")[0]
    dt_raw = mm_in(w_dt, [], out_dtype=F32, tn=LANES, name="ssm_in_dt")[0][:, :SSM_HEADS]

    wc = w_conv[0]
    bc = b_conv[0].reshape(1, CONV_DIM)
    tl_p = min(512, lp)
    act_p = _conv_silu(xbc.reshape(1, t, CONV_DIM), 0, 1, lp, jnp.zeros((1, SUBLANES, CONV_DIM), F32), wc, bc,
                       tl=tl_p, tc=1536)
    st_s = jnp.pad(state_conv[0].astype(F32), ((0, 0), (SUBLANES - (CONV_W - 1), 0), (0, 0)))
    act_s = _conv_silu(xbc.reshape(t // ls, ls, CONV_DIM), lp // ls, bs, ls, st_s, wc, bc, tl=ls, tc=1536)
    conv_p = xbc[lp - (CONV_W - 1):lp].reshape(1, 1, CONV_W - 1, CONV_DIM)
    conv_s = xbc[lp:].reshape(bs, ls, CONV_DIM)[:, ls - (CONV_W - 1):].reshape(1, bs, CONV_W - 1, CONV_DIM)

    bias = dt_bias[0].astype(F32).reshape(g_, 1, r_)
    a_neg = (-jnp.exp(a_log[0].astype(F32))).reshape(g_, 1, r_)
    dsk = jnp.repeat(d_skip[0].astype(F32), hd).reshape(1, D_INNER)

    def dt_views(rows, nb, L):
        v = rows.reshape(nb, L, g_, r_).transpose(0, 2, 1, 3)
        return v, v.transpose(0, 1, 3, 2)

    def scan(act, rows, s0, q):
        nb, L, _ = act.shape
        dtg, dtg_t = dt_views(rows, nb, L)
        return _ssd_scan(act, dtg, dtg_t, bias, bias.transpose(0, 2, 1), a_neg, a_neg.transpose(0, 2, 1),
                         dsk, s0, q=q)

    def state_in(s):
        nb = s.shape[0]
        return s.astype(F32).reshape(nb, g_, r_ * hd, n_).transpose(0, 1, 3, 2)

    def state_out(s):
        nb = s.shape[0]
        return s.transpose(0, 1, 3, 2).reshape(1, nb, SSM_HEADS, hd, n_)

    y_p, s_p = scan(act_p, dt_raw[:lp], jnp.zeros((1, g_, n_, r_ * hd), F32), min(SSD_Q, lp))
    y_s, s_s = scan(act_s, dt_raw[lp:], state_in(state_ssm[0]), ls)
    ssm_p, ssm_s = state_out(s_p), state_out(s_s)

    y_all = jnp.concatenate([y_p.reshape(lp, D_INNER), y_s.reshape(ts, D_INNER)], axis=0)
    h = _mm_rows([y_all, z], [g_ssm_norm[0].reshape(1, D_INNER)], w_ssm_out[0].astype(BF16), [h],
                 prologue=_pro_gated, epilogue=_epi_res, out_dtype=F32, tm=tm, tn=tn, name="ssm_out")[0]

    def ffn_and_ple(h, i):
        h = _moe_layer(h, g_ffn[i], w_router_grp[i], b_router_grp[i], w_router_exp[i], b_router_exp[i],
                       w_exp_gate, w_exp_up, w_exp_down, tm=tm, layer=i)
        p_rows = jnp.concatenate([p_prompt[i].reshape(lp, -1), p_sample[i].reshape(ts, -1)], axis=0)
        return _ple_layer(h, p_rows, g_ple[i], w_ple_gate[i], w_ple_proj[i], tm=tm, tn=tn)

    h = ffn_and_ple(h, 0)

    pos = np.concatenate([np.arange(lp), np.tile(past + np.arange(ls), bs)])
    cos64, sin64 = _rope_tables(pos)
    gk = g_kv_in.reshape(1, d)
    ckv = _mm_rows([h], [gk], w_dkv[:, :KV_RANK].astype(BF16), [], prologue=_pro_rms, epilogue=_epi_rms_out,
                   out_dtype=F32, tm=tm, tn=KV_RANK, name="kv_latent", e_consts=[g_kv.reshape(1, KV_RANK)])[0]
    w_r = w_dkv[:, KV_RANK:]
    row64 = pl.BlockSpec((tm, QK_ROPE), lambda i, j: (i, 0))
    w64 = pl.BlockSpec((d, QK_ROPE), lambda i, j: (0, 0))
    kr = _fused_mm(
        [(h, pl.BlockSpec((tm, d), lambda i, j: (i, 0))), (gk, pl.BlockSpec((1, d), lambda i, j: (0, 0)))],
        [(w_r.astype(BF16), w64), (_swap_rope_halves(w_r).astype(BF16), w64)],
        [(jnp.asarray(cos64), row64), (jnp.asarray(sin64), row64)],
        [(jax.ShapeDtypeStruct((t, QK_ROPE), F32), row64)],
        grid=(t // tm, 1), prologue=_pro_rms, epilogue=_epi_rot, xn_shape=(tm, d), name="k_rope")[0]

    ql = _mm_rows([h], [g_mix[1].reshape(1, d)], w_dq[0].astype(BF16), [], prologue=_pro_rms,
                  epilogue=_epi_rms_out, out_dtype=BF16, tm=tm, tn=Q_RANK, name="q_latent",
                  e_consts=[g_q[0].reshape(1, Q_RANK)])[0]
    nh = MLA_HEADS
    w_q = w_uq[0].reshape(Q_RANK, nh, QK_DIM).transpose(1, 0, 2)
    w_q_rot = jnp.concatenate([jnp.zeros((nh, Q_RANK, QK_NOPE), F32), _swap_rope_halves(w_q[..., QK_NOPE:])], -1)
    cos_q = np.concatenate([np.ones((t, QK_NOPE), np.float32), cos64], axis=1) * np.float32(QK_PRESCALE)
    sin_q = np.concatenate([np.zeros((t, QK_NOPE), np.float32), sin64], axis=1) * np.float32(QK_PRESCALE)
    rowq = pl.BlockSpec((tm, QK_DIM), lambda i, j: (i, 0))
    wq_spec = pl.BlockSpec((None, Q_RANK, QK_DIM), lambda i, j: (j, 0, 0))
    q_cat = _fused_mm(
        [(ql, pl.BlockSpec((tm, Q_RANK), lambda i, j: (i, 0)))],
        [(w_q.astype(BF16), wq_spec), (w_q_rot.astype(BF16), wq_spec)],
        [(jnp.asarray(cos_q), rowq), (jnp.asarray(sin_q), rowq)],
        [(jax.ShapeDtypeStruct((nh, t, QK_DIM), BF16), pl.BlockSpec((None, tm, QK_DIM), lambda i, j: (j, i, 0)))],
        grid=(t // tm, nh), prologue=_pro_cast, epilogue=_epi_rot, xn_shape=(tm, Q_RANK), name="q_heads")[0]

    lat_pad = LAT_PAD - KV_RANK - QK_ROPE
    ckr_p = jnp.concatenate([ckv[:lp], kr[:lp], jnp.zeros((lp, lat_pad), F32)], axis=1).astype(BF16)
    w_uk_h = w_uk.transpose(1, 0, 2)
    w_uv_h = w_uv.transpose(1, 0, 2)
    eye_r = jnp.broadcast_to(jnp.eye(QK_ROPE, dtype=F32), (nh, QK_ROPE, QK_ROPE))
    w_k_ext = jnp.concatenate([
        jnp.concatenate([w_uk_h, jnp.zeros((nh, KV_RANK, QK_ROPE), F32)], axis=2),
        jnp.concatenate([jnp.zeros((nh, QK_ROPE, QK_NOPE), F32), eye_r], axis=2),
        jnp.zeros((nh, lat_pad, QK_DIM), F32)], axis=1).astype(BF16)
    w_v_ext = jnp.concatenate([w_uv_h, jnp.zeros((nh, LAT_PAD - KV_RANK, V_DIM), F32)], axis=1).astype(BF16)
    k_cat, v_h = _fused_mm(
        [(ckr_p, pl.BlockSpec((tm, LAT_PAD), lambda i, j: (i, 0)))],
        [(w_k_ext, pl.BlockSpec((None, LAT_PAD, QK_DIM), lambda i, j: (j, 0, 0))),
         (w_v_ext, pl.BlockSpec((None, LAT_PAD, V_DIM), lambda i, j: (j, 0, 0)))],
        [],
        [(jax.ShapeDtypeStruct((nh, lp, QK_DIM), BF16), pl.BlockSpec((None, tm, QK_DIM), lambda i, j: (j, i, 0))),
         (jax.ShapeDtypeStruct((nh, lp, V_DIM), BF16), pl.BlockSpec((None, tm, V_DIM), lambda i, j: (j, i, 0)))],
        grid=(lp // tm, nh), prologue=_pro_cast, epilogue=_epi_id, xn_shape=(tm, LAT_PAD), name="kv_heads")
    o_p = _flash_prompt(q_cat, k_cat, v_h, lp, tq=min(512, lp), hb=4)

    w_abs = jnp.concatenate([
        jnp.concatenate([w_uk_h.transpose(0, 2, 1), jnp.zeros((nh, QK_NOPE, LAT_PAD - KV_RANK), F32)], axis=2),
        jnp.concatenate([jnp.zeros((nh, QK_ROPE, KV_RANK), F32), eye_r,
                         jnp.zeros((nh, QK_ROPE, lat_pad), F32)], axis=2)], axis=1).astype(BF16)
    blk_s = lp // ts if lp % ts == 0 else None
    assert blk_s is not None
    q_abs = _fused_mm(
        [(q_cat, pl.BlockSpec((None, ts, QK_DIM), lambda i, j: (j, blk_s + i, 0)))],
        [(w_abs, pl.BlockSpec((None, QK_DIM, LAT_PAD), lambda i, j: (j, 0, 0)))],
        [],
        [(jax.ShapeDtypeStruct((nh, ts, LAT_PAD), BF16), pl.BlockSpec((None, ts, LAT_PAD), lambda i, j: (j, i, 0)))],
        grid=(1, nh), prologue=_pro_cast, epilogue=_epi_id, x_per_j=True, name="q_absorb")[0]
    s_valid = past + ls
    s_pad = -(-s_valid // LANES) * LANES
    kc_new = jnp.concatenate([ckv[lp:], kr[lp:]], axis=1).reshape(bs, ls, KV_RANK + QK_ROPE)
    kc = jnp.concatenate([jnp.concatenate([cache_kv_latent, cache_k_rope], axis=2), kc_new], axis=1)
    kc = jnp.pad(kc, ((0, 0), (0, s_pad - s_valid), (0, lat_pad))).astype(BF16)
    o_lat = _attn_sample(q_abs, kc, ls=ls, past=past, s_valid=s_valid)
    o_s = _fused_mm(
        [(o_lat, pl.BlockSpec((None, ts, KV_RANK), lambda i, j: (j, i, 0)))],
        [(w_uv_h.astype(BF16), pl.BlockSpec((None, KV_RANK, V_DIM), lambda i, j: (j, 0, 0)))],
        [],
        [(jax.ShapeDtypeStruct((ts, nh * V_DIM), BF16), pl.BlockSpec((ts, V_DIM), lambda i, j: (i, j)))],
        grid=(1, nh), prologue=_pro_cast, epilogue=_epi_id, x_per_j=True, name="v_absorb")[0]

    o_all = jnp.concatenate([o_p, o_s], axis=0)
    h = _mm_rows([o_all], [], w_o[0].astype(BF16), [h], prologue=_pro_cast, epilogue=_epi_res,
                 out_dtype=F32, tm=tm, tn=tn, name="attn_out")[0]
    h = ffn_and_ple(h, 1)

    gf = g_final.reshape(1, d)
    y_p_out = _final_norm(h, gf, 0, lp, tm=tm).reshape(bp, lp_each, d)
    y_s_out = _final_norm(h, gf, lp, ts, tm=tm).reshape(bs, ls, d)
    return (y_p_out, y_s_out, conv_p, ssm_p, ckv[:lp].reshape(bp, lp_each, KV_RANK),
            kr[:lp].reshape(bp, lp_each, QK_ROPE), conv_s, ssm_s, ckv[lp:].reshape(bs, ls, KV_RANK),
            kr[lp:].reshape(bs, ls, QK_ROPE))
```

```python
import functools
import math

import numpy as np
import jax
import jax.numpy as jnp
from jax import lax
from jax.experimental import pallas as pl
from jax.experimental.pallas import tpu as pltpu

F32 = jnp.float32
BF16 = jnp.bfloat16
HIGHEST = lax.Precision.HIGHEST

EPS = 1e-6
CHUNK = 64
D_MODEL = 2048
D_INNER = 2 * D_MODEL
SSM_HEAD_DIM = 64
SSM_HEADS = D_INNER // SSM_HEAD_DIM
SSM_STATE = 128
SSM_GROUPS = 8
HEADS_PER_GROUP = SSM_HEADS // SSM_GROUPS
GROUP_WIDTH = D_INNER // SSM_GROUPS
CONV_W = 4
CONV_DIM = D_INNER + 2 * SSM_GROUPS * SSM_STATE
MLA_HEADS = 16
Q_RANK = 512
KV_RANK = 512
QK_NOPE = 128
QK_ROPE = 64
QK_DIM = QK_NOPE + QK_ROPE
V_DIM = 128
ROPE_THETA = 10000.0
ATTN_SCALE = QK_DIM ** -0.5
QK_PRESCALE = ATTN_SCALE * math.log2(math.e)
MOE_GROUPS = 4
EXPERTS_PER_GROUP = 8
N_EXPERTS = MOE_GROUPS * EXPERTS_PER_GROUP
TOP_K = 2
D_EXPERT = 512

LANES = 128
SUBLANES = 8
VMEM_LIMIT = 56 * 1024 * 1024
SSD_Q = 128
MOE_BM = 256
LAT_PAD = 640


def _cparams(sem):
    return pltpu.CompilerParams(dimension_semantics=sem, vmem_limit_bytes=VMEM_LIMIT)


def _sigmoid(v):
    return 1.0 / (1.0 + jnp.exp(-v))


def _silu(v):
    return v * _sigmoid(v)


def _softplus(v):
    return jnp.maximum(v, 0.0) + jnp.log1p(jnp.exp(-jnp.abs(v)))


def _rms_rows(x):
    return x * lax.rsqrt(jnp.mean(x * x, axis=-1, keepdims=True) + EPS)


def _mm_body(*refs, nx, nw, ne, no, prologue, epilogue, x_per_j, emit_xn, precision):
    x_refs = refs[:nx]
    w_refs = refs[nx:nx + nw]
    e_refs = refs[nx + nw:nx + nw + ne]
    o_refs = refs[nx + nw + ne:nx + nw + ne + no]
    rest = refs[nx + nw + ne + no:]
    j = pl.program_id(1)
    if x_per_j:
        xn = prologue(*[r[...] for r in x_refs])
    else:
        xn_ref = rest[-1]

        @pl.when(j == 0)
        def _():
            v = prologue(*[r[...] for r in x_refs])
            xn_ref[...] = v.astype(xn_ref.dtype)
            if emit_xn:
                rest[0][...] = v.astype(rest[0].dtype)

        xn = xn_ref[...]
    accs = [jnp.dot(xn, w[...], preferred_element_type=F32, precision=precision) for w in w_refs]
    outs = epilogue(accs, [e[...] for e in e_refs])
    for o_ref, o in zip(o_refs, outs):
        o_ref[...] = o.astype(o_ref.dtype)


def _fused_mm(x_args, w_args, e_args, out_defs, *, grid, prologue, epilogue, xn_shape=None,
              xn_dtype=BF16, x_per_j=False, xn_out=None, precision=None, name=None):
    arrays = [a for a, _ in x_args + w_args + e_args]
    in_specs = [s for _, s in x_args + w_args + e_args]
    out_shape = [d for d, _ in out_defs]
    out_specs = [s for _, s in out_defs]
    emit_xn = xn_out is not None
    if emit_xn:
        out_shape.append(xn_out[0])
        out_specs.append(xn_out[1])
    scratch = [] if x_per_j else [pltpu.VMEM(xn_shape, xn_dtype)]
    body = functools.partial(
        _mm_body, nx=len(x_args), nw=len(w_args), ne=len(e_args), no=len(out_defs),
        prologue=prologue, epilogue=epilogue, x_per_j=x_per_j, emit_xn=emit_xn, precision=precision)
    return pl.pallas_call(
        body, grid=grid, in_specs=in_specs, out_specs=out_specs, out_shape=out_shape,
        scratch_shapes=scratch, compiler_params=_cparams(("parallel", "arbitrary")), name=name,
    )(*arrays)


def _pro_rms(x, g):
    return _rms_rows(x.astype(F32)) * g


def _pro_cast(x):
    return x


def _pro_gated(y, z, g):
    v = y.astype(F32) * _silu(z.astype(F32))
    parts = [_rms_rows(v[:, k * GROUP_WIDTH:(k + 1) * GROUP_WIDTH]) for k in range(SSM_GROUPS)]
    return jnp.concatenate(parts, axis=-1) * g


def _epi_id(accs, es):
    return accs


def _epi_res(accs, es):
    return [es[0] + accs[0]]


def _epi_ple(accs, es):
    return [es[0] + _sigmoid(accs[0]) * accs[1]]


def _epi_rms_out(accs, es):
    return [_rms_rows(accs[0]) * es[0]]


def _epi_rot(accs, es):
    cos, sin = es
    return [accs[0] * cos + accs[1] * sin]


def _epi_route(accs, es):
    lg = accs[0] + es[0]
    lane = lax.broadcasted_iota(jnp.int32, lg.shape, 1).astype(F32)
    neg = -jnp.inf
    big = 1.0e4
    is_grp = lane < MOE_GROUPS
    gl = jnp.where(is_grp, lg, neg)
    mg = jnp.max(gl, axis=-1, keepdims=True)
    g_sel = jnp.min(jnp.where(gl == mg, lane, big), axis=-1, keepdims=True)
    p_sel = 1.0 / jnp.sum(jnp.where(is_grp, jnp.exp(gl - mg), 0.0), axis=-1, keepdims=True)
    lo = MOE_GROUPS + g_sel * EXPERTS_PER_GROUP
    in_grp = jnp.where(lane >= lo, jnp.where(lane < lo + EXPERTS_PER_GROUP, 1.0, 0.0), 0.0) > 0.5
    el = jnp.where(in_grp, lg, neg)
    v1 = jnp.max(el, axis=-1, keepdims=True)
    i1 = jnp.min(jnp.where(el == v1, lane, big), axis=-1, keepdims=True)
    el2 = jnp.where(lane == i1, neg, el)
    v2 = jnp.max(el2, axis=-1, keepdims=True)
    i2 = jnp.min(jnp.where(el2 == v2, jnp.where(lane == i1, big, lane), big), axis=-1, keepdims=True)
    e21 = jnp.exp(v2 - v1)
    g1 = p_sel / (1.0 + e21)
    g2 = p_sel * e21 / (1.0 + e21)
    idx = jnp.where(lane == 0.0, i1 - MOE_GROUPS, jnp.where(lane == 1.0, i2 - MOE_GROUPS, 0.0))
    gates = jnp.where(lane == 0.0, g1, jnp.where(lane == 1.0, g2, 0.0))
    return [idx.astype(jnp.int32), gates]


def _mm_rows(x_list, consts, w, extras, *, prologue, epilogue, out_dtype, tm, tn, name,
             xn_dtype=BF16, precision=None, emit_xn_dtype=None, e_consts=()):
    m = x_list[0].shape[0]
    k, n = w.shape
    grid = (m // tm, n // tn)
    x_args = [(x, pl.BlockSpec((tm, x.shape[1]), lambda i, j: (i, 0))) for x in x_list]
    x_args += [(c, pl.BlockSpec((1, c.shape[1]), lambda i, j: (0, 0))) for c in consts]
    w_args = [(w, pl.BlockSpec((k, tn), lambda i, j: (0, j)))]
    e_args = [(c, pl.BlockSpec((1, tn), lambda i, j: (0, j))) for c in e_consts]
    e_args += [(e, pl.BlockSpec((tm, tn), lambda i, j: (i, j))) for e in extras]
    out_defs = [(jax.ShapeDtypeStruct((m, n), out_dtype), pl.BlockSpec((tm, tn), lambda i, j: (i, j)))]
    xn_out = None
    if emit_xn_dtype is not None:
        xn_out = (jax.ShapeDtypeStruct((m, k), emit_xn_dtype), pl.BlockSpec((tm, k), lambda i, j: (i, 0)))
    return _fused_mm(x_args, w_args, e_args, out_defs, grid=grid, prologue=prologue, epilogue=epilogue,
                     xn_shape=(tm, k), xn_dtype=xn_dtype, xn_out=xn_out, precision=precision, name=name)


def _mm_res_body(*refs, nx, nl, nw, ne, lhs_of_w, res_from_x, prologue, epilogue, cw):
    x_refs = refs[:nx]
    l_refs = refs[nx:nx + nl]
    w_refs = refs[nx + nl:nx + nl + nw]
    e_refs = refs[nx + nl + nw:nx + nl + nw + ne]
    o_ref = refs[nx + nl + nw + ne]
    xn_ref = refs[-1]
    xn_ref[...] = prologue(*[r[...] for r in x_refs]).astype(xn_ref.dtype)
    lhs = [xn_ref] + list(l_refs)
    for c in range(o_ref.shape[1] // cw):
        sl = slice(c * cw, (c + 1) * cw)
        accs = [jnp.dot(lhs[li][...], w[:, sl], preferred_element_type=F32) for w, li in zip(w_refs, lhs_of_w)]
        es = ([x_refs[0][:, sl]] if res_from_x else []) + [e[:, sl] for e in e_refs]
        o_ref[:, sl] = epilogue(accs, es)[0].astype(o_ref.dtype)


def _mm_resident(x_list, consts, lhs_list, ws, lhs_of_w, extras, *, prologue, epilogue, out_dtype, tm, name,
                 res_from_x=False, cw=512):
    m = x_list[0].shape[0]
    k = consts[0].shape[1] if consts else x_list[0].shape[1]
    n = ws[0].shape[1]
    row = lambda a: pl.BlockSpec((tm, a.shape[1]), lambda i: (i, 0))
    whole = lambda a: pl.BlockSpec(a.shape, lambda i: (0, 0), pipeline_mode=pl.Buffered(1))
    arrays = list(x_list) + list(consts) + list(lhs_list) + list(ws) + list(extras)
    in_specs = ([row(a) for a in x_list] + [whole(a) for a in consts] + [row(a) for a in lhs_list]
                + [whole(a) for a in ws] + [row(a) for a in extras])
    body = functools.partial(
        _mm_res_body, nx=len(x_list) + len(consts), nl=len(lhs_list), nw=len(ws), ne=len(extras),
        lhs_of_w=tuple(lhs_of_w), res_from_x=res_from_x, prologue=prologue, epilogue=epilogue, cw=cw)
    return pl.pallas_call(
        body, grid=(m // tm,), in_specs=in_specs,
        out_specs=pl.BlockSpec((tm, n), lambda i: (i, 0)),
        out_shape=jax.ShapeDtypeStruct((m, n), out_dtype),
        scratch_shapes=[pltpu.VMEM((tm, k), BF16)],
        compiler_params=_cparams(("parallel",)), name=name,
    )(*arrays)


def _heads_body(*refs, nw, ne, hpb, epilogue):
    x_ref = refs[0]
    w_refs = refs[1:1 + nw]
    e_refs = refs[1 + nw:1 + nw + ne]
    o_refs = refs[1 + nw + ne:]
    x = x_ref[...]
    es = [e[...] for e in e_refs]
    for hh in range(hpb):
        outs = epilogue([jnp.dot(x, w[hh], preferred_element_type=F32) for w in w_refs], es)
        for o_ref, o in zip(o_refs, outs):
            o_ref[hh] = o.astype(o_ref.dtype)


def _heads_mm(x, ws, es, out_dims, *, tm, hpb, epilogue, name):
    m, k = x.shape
    nh = ws[0].shape[0]
    in_specs = [pl.BlockSpec((tm, k), lambda i, j: (i, 0))]
    in_specs += [pl.BlockSpec((hpb, k, w.shape[2]), lambda i, j: (j, 0, 0)) for w in ws]
    in_specs += [pl.BlockSpec((tm, e.shape[1]), lambda i, j: (i, 0)) for e in es]
    return pl.pallas_call(
        functools.partial(_heads_body, nw=len(ws), ne=len(es), hpb=hpb, epilogue=epilogue),
        grid=(m // tm, nh // hpb), in_specs=in_specs,
        out_specs=[pl.BlockSpec((hpb, tm, n), lambda i, j: (j, i, 0)) for n in out_dims],
        out_shape=[jax.ShapeDtypeStruct((nh, m, n), BF16) for n in out_dims],
        compiler_params=_cparams(("parallel", "arbitrary")), name=name,
    )(x, *ws, *es)


def _conv_body(x_ref, prev_ref, st_ref, w_ref, b_ref, o_ref, ext):
    i = pl.program_id(1)
    tl = x_ref.shape[0]
    ext[SUBLANES:SUBLANES + tl, :] = x_ref[...]

    @pl.when(i == 0)
    def _():
        ext[0:SUBLANES, :] = st_ref[...]

    @pl.when(i > 0)
    def _():
        ext[0:SUBLANES, :] = prev_ref[...]

    acc = b_ref[...]
    for k in range(CONV_W):
        off = SUBLANES - (CONV_W - 1) + k
        acc = acc + ext[off:off + tl, :] * w_ref[k:k + 1, :]
    o_ref[...] = _silu(acc).astype(o_ref.dtype)


def _conv_silu(x3, batch0, nb, L, st, w, b, *, tl, tc):
    c = CONV_DIM
    grid = (nb, L // tl, c // tc)
    per8 = tl // SUBLANES
    return pl.pallas_call(
        _conv_body, grid=grid,
        in_specs=[
            pl.BlockSpec((None, tl, tc), lambda bb, i, j: (batch0 + bb, i, j)),
            pl.BlockSpec((None, SUBLANES, tc), lambda bb, i, j: (batch0 + bb, jnp.maximum(i * per8 - 1, 0), j)),
            pl.BlockSpec((None, SUBLANES, tc), lambda bb, i, j: (bb, 0, j)),
            pl.BlockSpec((CONV_W, tc), lambda bb, i, j: (0, j)),
            pl.BlockSpec((1, tc), lambda bb, i, j: (0, j)),
        ],
        out_specs=pl.BlockSpec((None, tl, tc), lambda bb, i, j: (bb, i, j)),
        out_shape=jax.ShapeDtypeStruct((nb, L, c), BF16),
        scratch_shapes=[pltpu.VMEM((tl + SUBLANES, tc), F32)],
        compiler_params=_cparams(("parallel", "parallel", "parallel")), name="conv_silu",
    )(x3, x3, st, w, b)


def _ssd_body(xs_ref, b_ref, c_ref, dt_ref, dtT_ref, bias_ref, biasT_ref, a_ref, aT_ref, dsk_ref, s0_ref,
              y_ref, so_ref, s_scr, *, q):
    c = pl.program_id(2)

    @pl.when(c == 0)
    def _():
        s_scr[...] = s0_ref[...]

    dt = _softplus(dt_ref[...] + bias_ref[...])
    dta = dt * a_ref[...]
    dta_t = _softplus(dtT_ref[...] + biasT_ref[...]) * aT_ref[...]
    row = lax.broadcasted_iota(jnp.int32, (q, q), 0)
    col = lax.broadcasted_iota(jnp.int32, (q, q), 1)
    causal = row >= col
    tri = jnp.where(causal, 1.0, 0.0)
    tri_t = jnp.where(row <= col, 1.0, 0.0)
    acum = jnp.dot(tri, dta, preferred_element_type=F32, precision=HIGHEST)
    acum_t = jnp.dot(dta_t, tri_t, preferred_element_type=F32, precision=HIGHEST)
    bm = b_ref[...]
    cm = c_ref[...]
    cb = lax.dot_general(cm, bm, (((1,), (1,)), ((), ())), preferred_element_type=F32)
    s_prev = s_scr[...]
    y_off = jnp.dot(cm, s_prev.astype(BF16), preferred_element_type=F32)
    left = lax.broadcasted_iota(jnp.int32, (q, LANES), 1) < SSM_HEAD_DIM
    xw_parts, dec_parts = [], []
    for j in range(HEADS_PER_GROUP // 2):
        h0, h1 = 2 * j, 2 * j + 1
        sl = slice(j * LANES, (j + 1) * LANES)
        col0, col1 = acum[:, h0:h0 + 1], acum[:, h1:h1 + 1]
        pa = jnp.where(left, col0, col1)
        dtp = jnp.where(left, dt[:, h0:h0 + 1], dt[:, h1:h1 + 1])
        x = xs_ref[:, sl].astype(F32)
        xdt = x * dtp
        m0 = (jnp.exp(jnp.where(causal, col0 - acum_t[h0:h0 + 1, :], -jnp.inf)) * cb).astype(BF16)
        m1 = (jnp.exp(jnp.where(causal, col1 - acum_t[h1:h1 + 1, :], -jnp.inf)) * cb).astype(BF16)
        x_l = jnp.where(left, xdt, 0.0).astype(BF16)
        x_r = jnp.where(left, 0.0, xdt).astype(BF16)
        y_diag = (jnp.dot(m0, x_l, preferred_element_type=F32)
                  + jnp.dot(m1, x_r, preferred_element_type=F32))
        y = y_diag + y_off[:, sl] * jnp.exp(pa) + x * dsk_ref[:, sl]
        y_ref[:, sl] = y.astype(y_ref.dtype)
        last = pa[q - 1:q, :]
        xw_parts.append((xdt * jnp.exp(last - pa)).astype(BF16))
        dec_parts.append(jnp.exp(last))
    xw = jnp.concatenate(xw_parts, axis=1)
    dec = jnp.concatenate(dec_parts, axis=1)
    s_new = s_prev * dec + lax.dot_general(bm, xw, (((0,), (0,)), ((), ())), preferred_element_type=F32)
    s_scr[...] = s_new
    so_ref[...] = s_new


def _ssd_scan(act, dtg, dtg_t, bias, bias_t, a, a_t, dsk, s0, *, q):
    nb, L, _ = act.shape
    g_, r_, n_ = SSM_GROUPS, HEADS_PER_GROUP, SSM_STATE
    gw = GROUP_WIDTH
    grid = (nb, g_, L // q)
    b_off = D_INNER // n_
    c_off = b_off + g_
    return pl.pallas_call(
        functools.partial(_ssd_body, q=q), grid=grid,
        in_specs=[
            pl.BlockSpec((None, q, gw), lambda b, g, c: (b, c, g)),
            pl.BlockSpec((None, q, n_), lambda b, g, c: (b, c, b_off + g)),
            pl.BlockSpec((None, q, n_), lambda b, g, c: (b, c, c_off + g)),
            pl.BlockSpec((None, None, q, r_), lambda b, g, c: (b, g, c, 0)),
            pl.BlockSpec((None, None, r_, q), lambda b, g, c: (b, g, 0, c)),
            pl.BlockSpec((None, 1, r_), lambda b, g, c: (g, 0, 0)),
            pl.BlockSpec((None, r_, 1), lambda b, g, c: (g, 0, 0)),
            pl.BlockSpec((None, 1, r_), lambda b, g, c: (g, 0, 0)),
            pl.BlockSpec((None, r_, 1), lambda b, g, c: (g, 0, 0)),
            pl.BlockSpec((1, gw), lambda b, g, c: (0, g)),
            pl.BlockSpec((None, None, n_, gw), lambda b, g, c: (b, g, 0, 0)),
        ],
        out_specs=[
            pl.BlockSpec((None, q, gw), lambda b, g, c: (b, c, g)),
            pl.BlockSpec((None, None, n_, gw), lambda b, g, c: (b, g, 0, 0)),
        ],
        out_shape=[jax.ShapeDtypeStruct((nb, L, D_INNER), BF16),
                   jax.ShapeDtypeStruct((nb, g_, n_, gw), F32)],
        scratch_shapes=[pltpu.VMEM((n_, gw), F32)],
        compiler_params=_cparams(("parallel", "parallel", "arbitrary")), name="ssd_scan",
    )(act, act, act, dtg, dtg_t, bias, bias_t, a, a_t, dsk, s0)


def _gather_rows(tok_ref, base, src_hbm, dst, sem, n):
    def body(r, carry):
        t = tok_ref[base + r]
        pltpu.make_async_copy(src_hbm.at[pl.ds(t, 1)], dst.at[pl.ds(r, 1)], sem).start()
        return carry

    lax.fori_loop(0, n, body, 0)


def _gather_rows_unrolled(tok_ref, base, src_hbm, dst, sem, n):
    for r in range(n):
        t = tok_ref[base + r]
        pltpu.make_async_copy(src_hbm.at[pl.ds(t, 1)], dst.at[pl.ds(r, 1)], sem).start(priority=r % 2)


def _moe_body(blk_e_ref, nxt_e_ref, tok_ref, nused_ref, xt_hbm, wg_hbm, wu_hbm, wd_hbm, yb_ref,
              xbuf, xsem, wg32, wu32, wd32, wsem, wg_bf, wu_bf, wd_bf, xb, *, bm, layer):
    b = pl.program_id(0)
    slot = b % 2
    nused = nused_ref[0]
    e_cur = blk_e_ref[b]

    def weight_copies(e):
        return (pltpu.make_async_copy(wg_hbm.at[layer, e], wg32, wsem.at[0]),
                pltpu.make_async_copy(wu_hbm.at[layer, e], wu32, wsem.at[1]),
                pltpu.make_async_copy(wd_hbm.at[layer, e], wd32, wsem.at[2]))

    def gather_unrolled(base, slot_):
        for r in range(bm):
            t = tok_ref[base + r]
            pltpu.make_async_copy(xt_hbm.at[pl.ds(t, 1)], xbuf.at[slot_, pl.ds(r, 1)], xsem.at[slot_]).start()

    @pl.when(jnp.logical_and(b == 0, nused > 0))
    def _():
        for cp in weight_copies(e_cur):
            cp.start(priority=1)
        _gather_rows(tok_ref, 0, xt_hbm, xbuf.at[0], xsem.at[0], bm)

    prev_e = blk_e_ref[jnp.maximum(b - 1, 0)]
    changed = jnp.logical_or(b == 0, e_cur != prev_e)

    @pl.when(jnp.logical_and(changed, b < nused))
    def _():
        for cp in weight_copies(e_cur):
            cp.wait()
        wg_bf[...] = wg32[...].astype(BF16)
        wu_bf[...] = wu32[...].astype(BF16)
        wd_bf[...] = wd32[...].astype(BF16)
        nxt = nxt_e_ref[b]

        @pl.when(nxt >= 0)
        def _():
            for cp in weight_copies(nxt):
                cp.start(priority=1)

    def compute(prefetch):
        pltpu.make_async_copy(xbuf.at[slot], xbuf.at[slot], xsem.at[slot]).wait()
        xb[...] = xbuf[slot].astype(BF16)
        if prefetch:
            gather_unrolled((b + 1) * bm, 1 - slot)
        x = xb[...]
        g = jnp.dot(x, wg_bf[...], preferred_element_type=F32)
        u = jnp.dot(x, wu_bf[...], preferred_element_type=F32)
        hid = (_silu(g) * u).astype(BF16)
        yb_ref[...] = jnp.dot(hid, wd_bf[...], preferred_element_type=F32)

    @pl.when(b + 1 < nused)
    def _():
        compute(True)

    @pl.when(b + 1 == nused)
    def _():
        compute(False)

    @pl.when(b >= nused)
    def _():
        yb_ref[...] = jnp.zeros(yb_ref.shape, yb_ref.dtype)


def _moe_experts(xt, blk_e, nxt_e, row_token, nused, w_gate, w_up, w_down, *, bm, layer):
    nblk = blk_e.shape[0]
    d = xt.shape[1]
    de = w_gate.shape[3]
    anyspec = pl.BlockSpec(memory_space=pl.ANY)
    gs = pltpu.PrefetchScalarGridSpec(
        num_scalar_prefetch=4, grid=(nblk,),
        in_specs=[anyspec, anyspec, anyspec, anyspec],
        out_specs=pl.BlockSpec((bm, d), lambda b, be, nx, tok, nu: (b, 0)),
        scratch_shapes=[
            pltpu.VMEM((2, bm, d), F32),
            pltpu.SemaphoreType.DMA((2,)),
            pltpu.VMEM((d, de), F32),
            pltpu.VMEM((d, de), F32),
            pltpu.VMEM((de, d), F32),
            pltpu.SemaphoreType.DMA((3,)),
            pltpu.VMEM((d, de), BF16),
            pltpu.VMEM((d, de), BF16),
            pltpu.VMEM((de, d), BF16),
            pltpu.VMEM((bm, d), BF16),
        ],
    )
    return pl.pallas_call(
        functools.partial(_moe_body, bm=bm, layer=layer), grid_spec=gs,
        out_shape=jax.ShapeDtypeStruct((nblk * bm, d), F32),
        compiler_params=_cparams(("arbitrary",)), name="moe_experts",
    )(blk_e, nxt_e, row_token, nused, xt, w_gate, w_up, w_down)


def _combine_body(dest_ref, h_ref, g_ref, yb_hbm, o_ref, buf, sem, *, tc, ntile):
    i = pl.program_id(0)
    slot = i % 2
    n = TOP_K * tc

    @pl.when(i == 0)
    def _():
        _gather_rows(dest_ref, 0, yb_hbm, buf.at[0], sem.at[0], n)

    @pl.when(i + 1 < ntile)
    def _():
        _gather_rows_unrolled(dest_ref, (i + 1) * n, yb_hbm, buf.at[1 - slot], sem.at[1 - slot], n)

    pltpu.make_async_copy(buf.at[slot], buf.at[slot], sem.at[slot]).wait()
    g = g_ref[...]
    o_ref[...] = (h_ref[...] + g[:, 0:1] * buf[slot, 0:tc, :] + g[:, 1:2] * buf[slot, tc:2 * tc, :])


def _moe_combine(h, yb, dest_tiles, gates, *, tc):
    t, d = h.shape
    ntile = t // tc
    gs = pltpu.PrefetchScalarGridSpec(
        num_scalar_prefetch=1, grid=(ntile,),
        in_specs=[
            pl.BlockSpec((tc, d), lambda i, dst: (i, 0)),
            pl.BlockSpec((tc, LANES), lambda i, dst: (i, 0)),
            pl.BlockSpec(memory_space=pl.ANY),
        ],
        out_specs=pl.BlockSpec((tc, d), lambda i, dst: (i, 0)),
        scratch_shapes=[pltpu.VMEM((2, TOP_K * tc, d), F32), pltpu.SemaphoreType.DMA((2,))],
    )
    return pl.pallas_call(
        functools.partial(_combine_body, tc=tc, ntile=ntile), grid_spec=gs,
        out_shape=jax.ShapeDtypeStruct((t, d), F32),
        compiler_params=_cparams(("arbitrary",)), name="moe_combine",
    )(dest_tiles, h, gates, yb)


def _moe_plan(idx, bm, tc):
    t = idx.shape[0]
    a = t * TOP_K
    e_ = N_EXPERTS
    flat_e = idx.reshape(a)
    onehot = (flat_e[:, None] == jnp.arange(e_, dtype=jnp.int32)[None, :]).astype(jnp.int32)
    csum = jnp.cumsum(onehot, axis=0)
    rank = jnp.sum(onehot * csum, axis=1) - 1
    counts = csum[-1]
    padded = (counts + bm - 1) // bm * bm
    pad_end = jnp.cumsum(padded)
    pad_start = pad_end - padded
    dest = jnp.sum(onehot * pad_start[None, :], axis=1) + rank
    nblk = -(-a // bm) + e_
    rows = nblk * bm
    row_token = jnp.zeros((rows,), jnp.int32).at[dest].set(jnp.arange(a, dtype=jnp.int32) // TOP_K)
    blk_row0 = jnp.arange(nblk, dtype=jnp.int32) * bm
    blk_e = jnp.minimum(jnp.sum((pad_end[None, :] <= blk_row0[:, None]).astype(jnp.int32), axis=1), e_ - 1)
    ids = jnp.arange(e_, dtype=jnp.int32)
    later_used = jnp.logical_and(ids[None, :] > ids[:, None], (counts > 0)[None, :])
    nxt_of_e = jnp.min(jnp.where(later_used, ids[None, :], e_), axis=1)
    nxt_of_e = jnp.where(nxt_of_e == e_, -1, nxt_of_e)
    nxt_e = jnp.sum(jnp.where(blk_e[:, None] == ids[None, :], nxt_of_e[None, :], 0), axis=1).astype(jnp.int32)
    nused = (pad_end[-1] // bm).astype(jnp.int32).reshape(1)
    dest_tiles = dest.reshape(t // tc, tc, TOP_K).transpose(0, 2, 1).reshape(a).astype(jnp.int32)
    return blk_e.astype(jnp.int32), nxt_e, row_token, nused, dest_tiles


def _flash_body(qi_ref, kj_ref, q_ref, k_ref, v_ref, o_ref, m_scr, l_scr, acc_scr, *, hb):
    p_id = pl.program_id(1)
    qi = qi_ref[p_id]
    kj = kj_ref[p_id]
    tk = k_ref.shape[1]

    @pl.when(kj == 0)
    def _():
        m_scr[...] = jnp.full(m_scr.shape, -jnp.inf, F32)
        l_scr[...] = jnp.zeros(l_scr.shape, F32)
        acc_scr[...] = jnp.zeros(acc_scr.shape, F32)

    def step(diag):
        for hh in range(hb):
            s = lax.dot_general(q_ref[hh], k_ref[hh], (((1,), (1,)), ((), ())), preferred_element_type=F32)
            if diag:
                r = lax.broadcasted_iota(jnp.int32, s.shape, 0)
                c = lax.broadcasted_iota(jnp.int32, s.shape, 1)
                s = jnp.where((c // CHUNK) <= (r // CHUNK), s, -jnp.inf)
            m_prev = m_scr[hh]
            m_next = jnp.maximum(m_prev, jnp.max(s, axis=1, keepdims=True))
            p = jnp.exp2(s - jnp.concatenate([m_next] * (tk // LANES), axis=1))
            alpha = jnp.exp2(m_prev - m_next)
            l_scr[hh] = alpha * l_scr[hh] + jnp.sum(p, axis=1, keepdims=True)
            acc_scr[hh] = alpha * acc_scr[hh] + jnp.dot(p.astype(BF16), v_ref[hh], preferred_element_type=F32)
            m_scr[hh] = m_next

    @pl.when(kj < qi)
    def _():
        step(False)

    @pl.when(kj == qi)
    def _():
        step(True)
        for hh in range(hb):
            o_ref[:, hh * V_DIM:(hh + 1) * V_DIM] = (acc_scr[hh] / l_scr[hh]).astype(o_ref.dtype)


def _flash_prompt(q, k, v, lp, *, tq, hb):
    assert V_DIM == LANES
    nh = q.shape[0]
    nq = lp // tq
    pairs = [(a, b) for a in range(nq) for b in range(a + 1)]
    qi_tab = jnp.asarray(np.array([p[0] for p in pairs], np.int32))
    kj_tab = jnp.asarray(np.array([p[1] for p in pairs], np.int32))
    gs = pltpu.PrefetchScalarGridSpec(
        num_scalar_prefetch=2, grid=(nh // hb, len(pairs)),
        in_specs=[
            pl.BlockSpec((hb, tq, QK_DIM), lambda h, p, qi, kj: (h, qi[p], 0)),
            pl.BlockSpec((hb, tq, QK_DIM), lambda h, p, qi, kj: (h, kj[p], 0)),
            pl.BlockSpec((hb, tq, V_DIM), lambda h, p, qi, kj: (h, kj[p], 0)),
        ],
        out_specs=pl.BlockSpec((tq, hb * V_DIM), lambda h, p, qi, kj: (qi[p], h)),
        scratch_shapes=[pltpu.VMEM((hb, tq, LANES), F32), pltpu.VMEM((hb, tq, LANES), F32),
                        pltpu.VMEM((hb, tq, V_DIM), F32)],
    )
    return pl.pallas_call(
        functools.partial(_flash_body, hb=hb), grid_spec=gs,
        out_shape=jax.ShapeDtypeStruct((lp, nh * V_DIM), BF16),
        compiler_params=_cparams(("parallel", "arbitrary")), name="flash_prompt",
    )(qi_tab, kj_tab, q, k, v)


def _attn_sample_body(q_ref, k_ref, o_ref, *, ls, past, s_valid):
    nh = q_ref.shape[0]
    q = q_ref[...].reshape(nh * ls, q_ref.shape[2])
    k = k_ref[...]
    s = lax.dot_general(q, k, (((1,), (1,)), ((), ())), preferred_element_type=F32)
    r = lax.broadcasted_iota(jnp.int32, s.shape, 0)
    c = lax.broadcasted_iota(jnp.int32, s.shape, 1)
    q_pos = past + r % ls
    s = jnp.where((c // CHUNK) <= (q_pos // CHUNK), s, -jnp.inf)
    s = jnp.where(c < s_valid, s, -jnp.inf)
    m = jnp.max(s, axis=-1, keepdims=True)
    p = jnp.exp2(s - m)
    l = jnp.sum(p, axis=-1, keepdims=True)
    o = jnp.dot(p.astype(BF16), k[:, :KV_RANK], preferred_element_type=F32) / l
    o_ref[...] = o.reshape(nh, ls, KV_RANK).astype(o_ref.dtype)


def _attn_sample(q_abs, kc, *, ls, past, s_valid):
    nh, ts, dk = q_abs.shape
    nb, s_pad, _ = kc.shape
    return pl.pallas_call(
        functools.partial(_attn_sample_body, ls=ls, past=past, s_valid=s_valid), grid=(nb,),
        in_specs=[
            pl.BlockSpec((nh, ls, dk), lambda b: (0, b, 0)),
            pl.BlockSpec((None, s_pad, dk), lambda b: (b, 0, 0)),
        ],
        out_specs=pl.BlockSpec((nh, ls, KV_RANK), lambda b: (0, b, 0)),
        out_shape=jax.ShapeDtypeStruct((nh, ts, KV_RANK), BF16),
        compiler_params=_cparams(("parallel",)), name="attn_sample",
    )(q_abs, kc)


def _norm_body(x_ref, g_ref, o_ref):
    o_ref[...] = _rms_rows(x_ref[...]) * g_ref[...]


def _final_norm(h, g, row0, nrows, *, tm):
    d = h.shape[1]
    blk0 = row0 // tm
    return pl.pallas_call(
        _norm_body, grid=(nrows // tm,),
        in_specs=[pl.BlockSpec((tm, d), lambda i: (blk0 + i, 0)), pl.BlockSpec((1, d), lambda i: (0, 0))],
        out_specs=pl.BlockSpec((tm, d), lambda i: (i, 0)),
        out_shape=jax.ShapeDtypeStruct((nrows, d), F32),
        compiler_params=_cparams(("parallel",)), name="final_norm",
    )(h, g)


def _rope_tables(pos):
    half = QK_ROPE // 2
    inv = ROPE_THETA ** (-np.arange(half, dtype=np.float64) / half)
    ang = np.asarray(pos, np.float64)[:, None] * inv[None, :]
    cos = np.concatenate([np.cos(ang), np.cos(ang)], axis=1)
    sin = np.concatenate([-np.sin(ang), np.sin(ang)], axis=1)
    return cos.astype(np.float32), sin.astype(np.float32)


def _swap_rope_halves(w):
    half = QK_ROPE // 2
    return jnp.concatenate([w[..., half:], w[..., :half]], axis=-1)


def _moe_layer(h, g_ffn, w_rg, b_rg, w_re, b_re, w_gate, w_up, w_down, *, tm, layer):
    t, d = h.shape
    npad = LANES - MOE_GROUPS - N_EXPERTS
    w_r = jnp.concatenate([w_rg, w_re, jnp.zeros((d, npad), F32)], axis=1)
    b_r = jnp.concatenate([b_rg, b_re, jnp.zeros((npad,), F32)]).reshape(1, LANES)
    grid = (t // tm, 1)
    x_args = [(h, pl.BlockSpec((tm, d), lambda i, j: (i, 0))),
              (g_ffn.reshape(1, d), pl.BlockSpec((1, d), lambda i, j: (0, 0)))]
    w_args = [(w_r, pl.BlockSpec((d, LANES), lambda i, j: (0, 0)))]
    e_args = [(b_r, pl.BlockSpec((1, LANES), lambda i, j: (0, 0)))]
    tile = pl.BlockSpec((tm, LANES), lambda i, j: (i, 0))
    out_defs = [(jax.ShapeDtypeStruct((t, LANES), jnp.int32), tile),
                (jax.ShapeDtypeStruct((t, LANES), F32), tile)]
    xn_out = (jax.ShapeDtypeStruct((t, d), F32), pl.BlockSpec((tm, d), lambda i, j: (i, 0)))
    idx_t, gates_t, xt = _fused_mm(x_args, w_args, e_args, out_defs, grid=grid, prologue=_pro_rms,
                                   epilogue=_epi_route, xn_shape=(tm, d), xn_dtype=F32, xn_out=xn_out,
                                   precision=HIGHEST, name="router")
    tc = min(tm, 128)
    blk_e, nxt_e, row_token, nused, dest_tiles = _moe_plan(idx_t[:, :TOP_K], MOE_BM, tc)
    yb = _moe_experts(xt, blk_e, nxt_e, row_token, nused, w_gate, w_up, w_down, bm=MOE_BM, layer=layer)
    return _moe_combine(h, yb, dest_tiles, gates_t, tc=tc)


def _ple_layer(h, p_rows, g_ple, w_gate, w_proj, *, tm, tn):
    t, d = h.shape
    return _mm_resident([h], [g_ple.reshape(1, d)], [p_rows.astype(BF16)], [w_gate.astype(BF16), w_proj.astype(BF16)],
                        [0, 1], [], prologue=_pro_rms, epilogue=_epi_ple, out_dtype=F32, tm=tm, name="ple_gate",
                        res_from_x=True)


def kernel(x_prompt, x_sample, state_conv, state_ssm, cache_kv_latent, cache_k_rope, p_prompt, p_sample,
           g_mix, w_ssm_in, w_conv, b_conv, dt_bias, a_log, d_skip, g_ssm_norm, w_ssm_out,
           g_kv_in, w_dkv, g_kv, w_uk, w_uv, w_dq, g_q, w_uq, w_o,
           g_ffn, w_router_grp, b_router_grp, w_router_exp, b_router_exp, w_exp_gate, w_exp_up, w_exp_down,
           g_ple, w_ple_gate, w_ple_proj, g_final):
    bp, lp_each, d = x_prompt.shape
    bs, ls, _ = x_sample.shape
    past = cache_kv_latent.shape[1]
    assert bp == 1 and w_ssm_in.shape[0] == 1 and w_dq.shape[0] == 1 and d == D_MODEL
    assert ls >= CONV_W - 1 and ls % SUBLANES == 0 and past % CHUNK == 0
    lp = bp * lp_each
    ts = bs * ls
    t = lp + ts
    tm = min(512, math.gcd(lp, ts))
    tn = 512
    g_, r_, n_, hd = SSM_GROUPS, HEADS_PER_GROUP, SSM_STATE, SSM_HEAD_DIM

    h = jnp.concatenate([x_prompt.reshape(lp, d), x_sample.reshape(ts, d)], axis=0)

    w_in = w_ssm_in[0]
    w_z = w_in[:, :D_INNER].astype(BF16)
    xw = CONV_DIM + LANES
    w_x = jnp.pad(w_in[:, D_INNER:], ((0, 0), (0, LANES - SSM_HEADS))).astype(BF16)
    g0 = g_mix[0].reshape(1, d)
    mm_in = functools.partial(_mm_rows, [h], [g0], prologue=_pro_rms, epilogue=_epi_id, tm=tm)
    z = mm_in(w_z, [], out_dtype=BF16, tn=1024, name="ssm_in_z")[0]
    xbc = mm_in(w_x, [], out_dtype=F32, tn=xw // 7, name="ssm_in_xbc")[0]
    dt_raw = xbc[:, CONV_DIM:CONV_DIM + SSM_HEADS]

    wc = w_conv[0]
    bc = b_conv[0].reshape(1, CONV_DIM)
    tl_p = min(512, lp)
    act_p = _conv_silu(xbc.reshape(1, t, xw), 0, 1, lp, jnp.zeros((1, SUBLANES, CONV_DIM), F32), wc, bc,
                       tl=tl_p, tc=1536)
    st_s = jnp.pad(state_conv[0].astype(F32), ((0, 0), (SUBLANES - (CONV_W - 1), 0), (0, 0)))
    act_s = _conv_silu(xbc.reshape(t // ls, ls, xw), lp // ls, bs, ls, st_s, wc, bc, tl=ls, tc=1536)
    conv_p = xbc[lp - (CONV_W - 1):lp, :CONV_DIM].reshape(1, 1, CONV_W - 1, CONV_DIM)
    conv_s = xbc[lp:, :CONV_DIM].reshape(bs, ls, CONV_DIM)[:, ls - (CONV_W - 1):].reshape(
        1, bs, CONV_W - 1, CONV_DIM)

    bias = dt_bias[0].astype(F32).reshape(g_, 1, r_)
    a_neg = (-jnp.exp(a_log[0].astype(F32))).reshape(g_, 1, r_)
    dsk = jnp.repeat(d_skip[0].astype(F32), hd).reshape(1, D_INNER)

    def dt_views(rows, nb, L):
        v = rows.reshape(nb, L, g_, r_).transpose(0, 2, 1, 3)
        return v, v.transpose(0, 1, 3, 2)

    def scan(act, rows, s0, q):
        nb, L, _ = act.shape
        dtg, dtg_t = dt_views(rows, nb, L)
        return _ssd_scan(act, dtg, dtg_t, bias, bias.transpose(0, 2, 1), a_neg, a_neg.transpose(0, 2, 1),
                         dsk, s0, q=q)

    def state_in(s):
        nb = s.shape[0]
        return s.astype(F32).reshape(nb, g_, r_ * hd, n_).transpose(0, 1, 3, 2)

    def state_out(s):
        nb = s.shape[0]
        return s.transpose(0, 1, 3, 2).reshape(1, nb, SSM_HEADS, hd, n_)

    y_p, s_p = scan(act_p, dt_raw[:lp], jnp.zeros((1, g_, n_, r_ * hd), F32), min(SSD_Q, lp))
    y_s, s_s = scan(act_s, dt_raw[lp:], state_in(state_ssm[0]), ls)
    ssm_p, ssm_s = state_out(s_p), state_out(s_s)

    y_all = jnp.concatenate([y_p.reshape(lp, D_INNER), y_s.reshape(ts, D_INNER)], axis=0)
    h = _mm_resident([y_all, z], [g_ssm_norm[0].reshape(1, D_INNER)], [], [w_ssm_out[0].astype(BF16)], [0], [h],
                     prologue=_pro_gated, epilogue=_epi_res, out_dtype=F32, tm=tm // 2, name="ssm_out")

    def ffn_and_ple(h, i):
        h = _moe_layer(h, g_ffn[i], w_router_grp[i], b_router_grp[i], w_router_exp[i], b_router_exp[i],
                       w_exp_gate, w_exp_up, w_exp_down, tm=tm, layer=i)
        p_rows = jnp.concatenate([p_prompt[i].reshape(lp, -1), p_sample[i].reshape(ts, -1)], axis=0)
        return _ple_layer(h, p_rows, g_ple[i], w_ple_gate[i], w_ple_proj[i], tm=tm, tn=tn)

    h = ffn_and_ple(h, 0)

    pos = np.concatenate([np.arange(lp), np.tile(past + np.arange(ls), bs)])
    cos64, sin64 = _rope_tables(pos)
    gk = g_kv_in.reshape(1, d)
    ckv = _mm_rows([h], [gk], w_dkv[:, :KV_RANK].astype(BF16), [], prologue=_pro_rms, epilogue=_epi_rms_out,
                   out_dtype=F32, tm=tm, tn=KV_RANK, name="kv_latent", e_consts=[g_kv.reshape(1, KV_RANK)])[0]
    w_r = w_dkv[:, KV_RANK:]
    row64 = pl.BlockSpec((tm, QK_ROPE), lambda i, j: (i, 0))
    w64 = pl.BlockSpec((d, QK_ROPE), lambda i, j: (0, 0))
    kr = _fused_mm(
        [(h, pl.BlockSpec((tm, d), lambda i, j: (i, 0))), (gk, pl.BlockSpec((1, d), lambda i, j: (0, 0)))],
        [(w_r.astype(BF16), w64), (_swap_rope_halves(w_r).astype(BF16), w64)],
        [(jnp.asarray(cos64), row64), (jnp.asarray(sin64), row64)],
        [(jax.ShapeDtypeStruct((t, QK_ROPE), F32), row64)],
        grid=(t // tm, 1), prologue=_pro_rms, epilogue=_epi_rot, xn_shape=(tm, d), name="k_rope")[0]

    ql = _mm_rows([h], [g_mix[1].reshape(1, d)], w_dq[0].astype(BF16), [], prologue=_pro_rms,
                  epilogue=_epi_rms_out, out_dtype=BF16, tm=tm, tn=Q_RANK, name="q_latent",
                  e_consts=[g_q[0].reshape(1, Q_RANK)])[0]
    nh = MLA_HEADS
    w_q = w_uq[0].reshape(Q_RANK, nh, QK_DIM).transpose(1, 0, 2)
    w_q_rot = jnp.concatenate([jnp.zeros((nh, Q_RANK, QK_NOPE), F32), _swap_rope_halves(w_q[..., QK_NOPE:])], -1)
    cos_q = np.concatenate([np.ones((t, QK_NOPE), np.float32), cos64], axis=1) * np.float32(QK_PRESCALE)
    sin_q = np.concatenate([np.zeros((t, QK_NOPE), np.float32), sin64], axis=1) * np.float32(QK_PRESCALE)
    q_cat = _heads_mm(ql, [w_q.astype(BF16), w_q_rot.astype(BF16)], [jnp.asarray(cos_q), jnp.asarray(sin_q)],
                      [QK_DIM], tm=tm, hpb=8, epilogue=_epi_rot, name="q_heads")[0]

    lat_pad = LAT_PAD - KV_RANK - QK_ROPE
    ckr_p = jnp.concatenate([ckv[:lp], kr[:lp], jnp.zeros((lp, lat_pad), F32)], axis=1).astype(BF16)
    w_uk_h = w_uk.transpose(1, 0, 2)
    w_uv_h = w_uv.transpose(1, 0, 2)
    eye_r = jnp.broadcast_to(jnp.eye(QK_ROPE, dtype=F32), (nh, QK_ROPE, QK_ROPE))
    w_k_ext = jnp.concatenate([
        jnp.concatenate([w_uk_h, jnp.zeros((nh, KV_RANK, QK_ROPE), F32)], axis=2),
        jnp.concatenate([jnp.zeros((nh, QK_ROPE, QK_NOPE), F32), eye_r], axis=2),
        jnp.zeros((nh, lat_pad, QK_DIM), F32)], axis=1).astype(BF16)
    w_v_ext = jnp.concatenate([w_uv_h, jnp.zeros((nh, LAT_PAD - KV_RANK, V_DIM), F32)], axis=1).astype(BF16)
    k_cat, v_h = _heads_mm(ckr_p, [w_k_ext, w_v_ext], [], [QK_DIM, V_DIM], tm=tm, hpb=8,
                           epilogue=_epi_id, name="kv_heads")
    o_p = _flash_prompt(q_cat, k_cat, v_h, lp, tq=min(512, lp), hb=4)

    w_abs = jnp.concatenate([
        jnp.concatenate([w_uk_h.transpose(0, 2, 1), jnp.zeros((nh, QK_NOPE, LAT_PAD - KV_RANK), F32)], axis=2),
        jnp.concatenate([jnp.zeros((nh, QK_ROPE, KV_RANK), F32), eye_r,
                         jnp.zeros((nh, QK_ROPE, lat_pad), F32)], axis=2)], axis=1).astype(BF16)
    blk_s = lp // ts if lp % ts == 0 else None
    assert blk_s is not None
    q_abs = _fused_mm(
        [(q_cat, pl.BlockSpec((None, ts, QK_DIM), lambda i, j: (j, blk_s + i, 0)))],
        [(w_abs, pl.BlockSpec((None, QK_DIM, LAT_PAD), lambda i, j: (j, 0, 0)))],
        [],
        [(jax.ShapeDtypeStruct((nh, ts, LAT_PAD), BF16), pl.BlockSpec((None, ts, LAT_PAD), lambda i, j: (j, i, 0)))],
        grid=(1, nh), prologue=_pro_cast, epilogue=_epi_id, x_per_j=True, name="q_absorb")[0]
    s_valid = past + ls
    s_pad = -(-s_valid // LANES) * LANES
    kc_new = jnp.concatenate([ckv[lp:], kr[lp:]], axis=1).reshape(bs, ls, KV_RANK + QK_ROPE)
    kc = jnp.concatenate([jnp.concatenate([cache_kv_latent, cache_k_rope], axis=2), kc_new], axis=1)
    kc = jnp.pad(kc, ((0, 0), (0, s_pad - s_valid), (0, lat_pad))).astype(BF16)
    o_lat = _attn_sample(q_abs, kc, ls=ls, past=past, s_valid=s_valid)
    o_s = _fused_mm(
        [(o_lat, pl.BlockSpec((None, ts, KV_RANK), lambda i, j: (j, i, 0)))],
        [(w_uv_h.astype(BF16), pl.BlockSpec((None, KV_RANK, V_DIM), lambda i, j: (j, 0, 0)))],
        [],
        [(jax.ShapeDtypeStruct((ts, nh * V_DIM), BF16), pl.BlockSpec((ts, V_DIM), lambda i, j: (i, j)))],
        grid=(1, nh), prologue=_pro_cast, epilogue=_epi_id, x_per_j=True, name="v_absorb")[0]

    o_all = jnp.concatenate([o_p, o_s], axis=0)
    h = _mm_resident([o_all], [], [], [w_o[0].astype(BF16)], [0], [h], prologue=_pro_cast, epilogue=_epi_res,
                     out_dtype=F32, tm=tm, name="attn_out")
    h = ffn_and_ple(h, 1)

    gf = g_final.reshape(1, d)
    y_p_out = _final_norm(h, gf, 0, lp, tm=tm).reshape(bp, lp_each, d)
    y_s_out = _final_norm(h, gf, lp, ts, tm=tm).reshape(bs, ls, d)
    return (y_p_out, y_s_out, conv_p, ssm_p, ckv[:lp].reshape(bp, lp_each, KV_RANK),
            kr[:lp].reshape(bp, lp_each, QK_ROPE), conv_s, ssm_s, ckv[lp:].reshape(bs, ls, KV_RANK),
            kr[lp:].reshape(bs, ls, QK_ROPE))
```

```python
import functools
import math

import numpy as np
import jax
import jax.numpy as jnp
from jax import lax
from jax.experimental import pallas as pl
from jax.experimental.pallas import tpu as pltpu

F32 = jnp.float32
BF16 = jnp.bfloat16
HIGHEST = lax.Precision.HIGHEST

EPS = 1e-6
CHUNK = 64
D_MODEL = 2048
D_INNER = 2 * D_MODEL
SSM_HEAD_DIM = 64
SSM_HEADS = D_INNER // SSM_HEAD_DIM
SSM_STATE = 128
SSM_GROUPS = 8
HEADS_PER_GROUP = SSM_HEADS // SSM_GROUPS
GROUP_WIDTH = D_INNER // SSM_GROUPS
CONV_W = 4
CONV_DIM = D_INNER + 2 * SSM_GROUPS * SSM_STATE
MLA_HEADS = 16
Q_RANK = 512
KV_RANK = 512
QK_NOPE = 128
QK_ROPE = 64
QK_DIM = QK_NOPE + QK_ROPE
V_DIM = 128
ROPE_THETA = 10000.0
ATTN_SCALE = QK_DIM ** -0.5
QK_PRESCALE = ATTN_SCALE * math.log2(math.e)
MOE_GROUPS = 4
EXPERTS_PER_GROUP = 8
N_EXPERTS = MOE_GROUPS * EXPERTS_PER_GROUP
TOP_K = 2
D_EXPERT = 512

LANES = 128
SUBLANES = 8
VMEM_LIMIT = 56 * 1024 * 1024
SSD_Q = 128
MOE_BM = 256
LAT_PAD = 640
SSM_IN_XW = 6400


def _cparams(sem):
    return pltpu.CompilerParams(dimension_semantics=sem, vmem_limit_bytes=VMEM_LIMIT)


def _sigmoid(v):
    return 1.0 / (1.0 + jnp.exp(-v))


def _silu(v):
    return v * _sigmoid(v)


def _softplus(v):
    return jnp.maximum(v, 0.0) + jnp.log1p(jnp.exp(-jnp.abs(v)))


def _rms_rows(x):
    return x * lax.rsqrt(jnp.mean(x * x, axis=-1, keepdims=True) + EPS)


def _mm_body(*refs, nx, nw, ne, no, prologue, epilogue, x_per_j, emit_xn, xn_emit, precision):
    x_refs = refs[:nx]
    w_refs = refs[nx:nx + nw]
    e_refs = refs[nx + nw:nx + nw + ne]
    o_refs = refs[nx + nw + ne:nx + nw + ne + no]
    rest = refs[nx + nw + ne + no:]
    j = pl.program_id(1)
    if x_per_j:
        xn = prologue(*[r[...] for r in x_refs])
    else:
        xn_ref = rest[-1]

        @pl.when(j == 0)
        def _():
            v = prologue(*[r[...] for r in x_refs])
            xn_ref[...] = v.astype(xn_ref.dtype)
            if emit_xn:
                rest[0][...] = xn_emit(v).astype(rest[0].dtype)

        xn = xn_ref[...]
    accs = [jnp.dot(xn, w[...], preferred_element_type=F32, precision=precision) for w in w_refs]
    outs = epilogue(accs, [e[...] for e in e_refs])
    for o_ref, o in zip(o_refs, outs):
        o_ref[...] = o.astype(o_ref.dtype)


def _fused_mm(x_args, w_args, e_args, out_defs, *, grid, prologue, epilogue, xn_shape=None,
              xn_dtype=BF16, x_per_j=False, xn_out=None, xn_emit=None, precision=None, name=None):
    arrays = [a for a, _ in x_args + w_args + e_args]
    in_specs = [s for _, s in x_args + w_args + e_args]
    out_shape = [d for d, _ in out_defs]
    out_specs = [s for _, s in out_defs]
    emit_xn = xn_out is not None
    if emit_xn:
        out_shape.append(xn_out[0])
        out_specs.append(xn_out[1])
    scratch = [] if x_per_j else [pltpu.VMEM(xn_shape, xn_dtype)]
    body = functools.partial(
        _mm_body, nx=len(x_args), nw=len(w_args), ne=len(e_args), no=len(out_defs),
        prologue=prologue, epilogue=epilogue, x_per_j=x_per_j, emit_xn=emit_xn,
        xn_emit=xn_emit or (lambda v: v), precision=precision)
    return pl.pallas_call(
        body, grid=grid, in_specs=in_specs, out_specs=out_specs, out_shape=out_shape,
        scratch_shapes=scratch, compiler_params=_cparams(("parallel", "arbitrary")), name=name,
    )(*arrays)


def _pro_rms(x, g):
    return _rms_rows(x.astype(F32)) * g


def _pro_cast(x):
    return x


def _pack_bf16_pairs(v):
    k = v.shape[1] // 2
    lo = lax.bitcast_convert_type(v[:, :k].astype(BF16).astype(F32), jnp.uint32) >> 16
    hi = lax.bitcast_convert_type(v[:, k:].astype(BF16).astype(F32), jnp.uint32) & jnp.uint32(0xFFFF0000)
    return hi | lo


def _pro_gated(y, z, g):
    v = y.astype(F32) * _silu(z.astype(F32))
    parts = [_rms_rows(v[:, k * GROUP_WIDTH:(k + 1) * GROUP_WIDTH]) for k in range(SSM_GROUPS)]
    return jnp.concatenate(parts, axis=-1) * g


def _epi_id(accs, es):
    return accs


def _epi_res(accs, es):
    return [es[0] + accs[0]]


def _epi_ple(accs, es):
    return [es[0] + _sigmoid(accs[0]) * accs[1]]


def _epi_rms_out(accs, es):
    return [_rms_rows(accs[0]) * es[0]]


def _epi_rot(accs, es):
    cos, sin = es
    return [accs[0] * cos + accs[1] * sin]


def _epi_route(accs, es):
    lg = accs[0] + es[0]
    lane = lax.broadcasted_iota(jnp.int32, lg.shape, 1).astype(F32)
    neg = -jnp.inf
    big = 1.0e4
    is_grp = lane < MOE_GROUPS
    gl = jnp.where(is_grp, lg, neg)
    mg = jnp.max(gl, axis=-1, keepdims=True)
    g_sel = jnp.min(jnp.where(gl == mg, lane, big), axis=-1, keepdims=True)
    p_sel = 1.0 / jnp.sum(jnp.where(is_grp, jnp.exp(gl - mg), 0.0), axis=-1, keepdims=True)
    lo = MOE_GROUPS + g_sel * EXPERTS_PER_GROUP
    in_grp = jnp.where(lane >= lo, jnp.where(lane < lo + EXPERTS_PER_GROUP, 1.0, 0.0), 0.0) > 0.5
    el = jnp.where(in_grp, lg, neg)
    v1 = jnp.max(el, axis=-1, keepdims=True)
    i1 = jnp.min(jnp.where(el == v1, lane, big), axis=-1, keepdims=True)
    el2 = jnp.where(lane == i1, neg, el)
    v2 = jnp.max(el2, axis=-1, keepdims=True)
    i2 = jnp.min(jnp.where(el2 == v2, jnp.where(lane == i1, big, lane), big), axis=-1, keepdims=True)
    e21 = jnp.exp(v2 - v1)
    g1 = p_sel / (1.0 + e21)
    g2 = p_sel * e21 / (1.0 + e21)
    idx = jnp.where(lane == 0.0, i1 - MOE_GROUPS, jnp.where(lane == 1.0, i2 - MOE_GROUPS, 0.0))
    gates = jnp.where(lane == 0.0, g1, jnp.where(lane == 1.0, g2, 0.0))
    return [idx.astype(jnp.int32), gates]


def _mm_rows(x_list, consts, w, extras, *, prologue, epilogue, out_dtype, tm, tn, name,
             xn_dtype=BF16, precision=None, emit_xn_dtype=None, e_consts=()):
    m = x_list[0].shape[0]
    k, n = w.shape
    grid = (m // tm, n // tn)
    x_args = [(x, pl.BlockSpec((tm, x.shape[1]), lambda i, j: (i, 0))) for x in x_list]
    x_args += [(c, pl.BlockSpec((1, c.shape[1]), lambda i, j: (0, 0))) for c in consts]
    w_args = [(w, pl.BlockSpec((k, tn), lambda i, j: (0, j)))]
    e_args = [(c, pl.BlockSpec((1, tn), lambda i, j: (0, j))) for c in e_consts]
    e_args += [(e, pl.BlockSpec((tm, tn), lambda i, j: (i, j))) for e in extras]
    out_defs = [(jax.ShapeDtypeStruct((m, n), out_dtype), pl.BlockSpec((tm, tn), lambda i, j: (i, j)))]
    xn_out = None
    if emit_xn_dtype is not None:
        xn_out = (jax.ShapeDtypeStruct((m, k), emit_xn_dtype), pl.BlockSpec((tm, k), lambda i, j: (i, 0)))
    return _fused_mm(x_args, w_args, e_args, out_defs, grid=grid, prologue=prologue, epilogue=epilogue,
                     xn_shape=(tm, k), xn_dtype=xn_dtype, xn_out=xn_out, precision=precision, name=name)


def _mm_res_body(*refs, nx, nl, nw, ne, lhs_of_w, res_from_x, prologue, epilogue, cw):
    x_refs = refs[:nx]
    l_refs = refs[nx:nx + nl]
    w_refs = refs[nx + nl:nx + nl + nw]
    e_refs = refs[nx + nl + nw:nx + nl + nw + ne]
    o_ref = refs[nx + nl + nw + ne]
    xn_ref = refs[-1]
    xn_ref[...] = prologue(*[r[...] for r in x_refs]).astype(xn_ref.dtype)
    lhs = [xn_ref] + list(l_refs)
    for c in range(o_ref.shape[1] // cw):
        sl = slice(c * cw, (c + 1) * cw)
        accs = [jnp.dot(lhs[li][...], w[:, sl], preferred_element_type=F32) for w, li in zip(w_refs, lhs_of_w)]
        es = ([x_refs[0][:, sl]] if res_from_x else []) + [e[:, sl] for e in e_refs]
        o_ref[:, sl] = epilogue(accs, es)[0].astype(o_ref.dtype)


def _mm_resident(x_list, consts, lhs_list, ws, lhs_of_w, extras, *, prologue, epilogue, out_dtype, tm, name,
                 res_from_x=False, cw=512):
    m = x_list[0].shape[0]
    k = consts[0].shape[1] if consts else x_list[0].shape[1]
    n = ws[0].shape[1]
    row = lambda a: pl.BlockSpec((tm, a.shape[1]), lambda i: (i, 0))
    whole = lambda a: pl.BlockSpec(a.shape, lambda i: (0, 0), pipeline_mode=pl.Buffered(1))
    arrays = list(x_list) + list(consts) + list(lhs_list) + list(ws) + list(extras)
    in_specs = ([row(a) for a in x_list] + [whole(a) for a in consts] + [row(a) for a in lhs_list]
                + [whole(a) for a in ws] + [row(a) for a in extras])
    body = functools.partial(
        _mm_res_body, nx=len(x_list) + len(consts), nl=len(lhs_list), nw=len(ws), ne=len(extras),
        lhs_of_w=tuple(lhs_of_w), res_from_x=res_from_x, prologue=prologue, epilogue=epilogue, cw=cw)
    return pl.pallas_call(
        body, grid=(m // tm,), in_specs=in_specs,
        out_specs=pl.BlockSpec((tm, n), lambda i: (i, 0)),
        out_shape=jax.ShapeDtypeStruct((m, n), out_dtype),
        scratch_shapes=[pltpu.VMEM((tm, k), BF16)],
        compiler_params=_cparams(("parallel",)), name=name,
    )(*arrays)


def _heads_body(*refs, nw, ne, hpb, epilogue):
    x_ref = refs[0]
    w_refs = refs[1:1 + nw]
    e_refs = refs[1 + nw:1 + nw + ne]
    o_refs = refs[1 + nw + ne:]
    x = x_ref[...]
    es = [e[...] for e in e_refs]
    for hh in range(hpb):
        outs = epilogue([jnp.dot(x, w[hh], preferred_element_type=F32) for w in w_refs], es)
        for o_ref, o in zip(o_refs, outs):
            o_ref[hh] = o.astype(o_ref.dtype)


def _heads_mm(x, ws, es, out_dims, *, tm, hpb, epilogue, name):
    m, k = x.shape
    nh = ws[0].shape[0]
    in_specs = [pl.BlockSpec((tm, k), lambda i, j: (i, 0))]
    in_specs += [pl.BlockSpec((hpb, k, w.shape[2]), lambda i, j: (j, 0, 0)) for w in ws]
    in_specs += [pl.BlockSpec((tm, e.shape[1]), lambda i, j: (i, 0)) for e in es]
    return pl.pallas_call(
        functools.partial(_heads_body, nw=len(ws), ne=len(es), hpb=hpb, epilogue=epilogue),
        grid=(m // tm, nh // hpb), in_specs=in_specs,
        out_specs=[pl.BlockSpec((hpb, tm, n), lambda i, j: (j, i, 0)) for n in out_dims],
        out_shape=[jax.ShapeDtypeStruct((nh, m, n), BF16) for n in out_dims],
        compiler_params=_cparams(("parallel", "arbitrary")), name=name,
    )(x, *ws, *es)


def _conv_body(x_ref, prev_ref, st_ref, w_ref, b_ref, o_ref, ext):
    i = pl.program_id(1)
    tl = x_ref.shape[0]
    ext[SUBLANES:SUBLANES + tl, :] = x_ref[...]

    @pl.when(i == 0)
    def _():
        ext[0:SUBLANES, :] = st_ref[...]

    @pl.when(i > 0)
    def _():
        ext[0:SUBLANES, :] = prev_ref[...]

    acc = b_ref[...]
    for k in range(CONV_W):
        off = SUBLANES - (CONV_W - 1) + k
        acc = acc + ext[off:off + tl, :] * w_ref[k:k + 1, :]
    o_ref[...] = _silu(acc).astype(o_ref.dtype)


def _conv_silu(x3, batch0, nb, L, st, w, b, *, tl, tc):
    c = CONV_DIM
    grid = (nb, L // tl, c // tc)
    per8 = tl // SUBLANES
    return pl.pallas_call(
        _conv_body, grid=grid,
        in_specs=[
            pl.BlockSpec((None, tl, tc), lambda bb, i, j: (batch0 + bb, i, j)),
            pl.BlockSpec((None, SUBLANES, tc), lambda bb, i, j: (batch0 + bb, jnp.maximum(i * per8 - 1, 0), j)),
            pl.BlockSpec((None, SUBLANES, tc), lambda bb, i, j: (bb, 0, j)),
            pl.BlockSpec((CONV_W, tc), lambda bb, i, j: (0, j)),
            pl.BlockSpec((1, tc), lambda bb, i, j: (0, j)),
        ],
        out_specs=pl.BlockSpec((None, tl, tc), lambda bb, i, j: (bb, i, j)),
        out_shape=jax.ShapeDtypeStruct((nb, L, c), BF16),
        scratch_shapes=[pltpu.VMEM((tl + SUBLANES, tc), F32)],
        compiler_params=_cparams(("parallel", "parallel", "parallel")), name="conv_silu",
    )(x3, x3, st, w, b)


def _ssd_body(xs_ref, b_ref, c_ref, dt_ref, dtT_ref, bias_ref, biasT_ref, a_ref, aT_ref, dsk_ref, s0_ref,
              y_ref, so_ref, s_scr, *, q):
    c = pl.program_id(2)

    @pl.when(c == 0)
    def _():
        s_scr[...] = s0_ref[...]

    dt = _softplus(dt_ref[...] + bias_ref[...])
    dta = dt * a_ref[...]
    dta_t = _softplus(dtT_ref[...] + biasT_ref[...]) * aT_ref[...]
    row = lax.broadcasted_iota(jnp.int32, (q, q), 0)
    col = lax.broadcasted_iota(jnp.int32, (q, q), 1)
    causal = row >= col
    tri = jnp.where(causal, 1.0, 0.0)
    tri_t = jnp.where(row <= col, 1.0, 0.0)
    acum = jnp.dot(tri, dta, preferred_element_type=F32, precision=HIGHEST)
    acum_t = jnp.dot(dta_t, tri_t, preferred_element_type=F32, precision=HIGHEST)
    bm = b_ref[...]
    cm = c_ref[...]
    cb = lax.dot_general(cm, bm, (((1,), (1,)), ((), ())), preferred_element_type=F32)
    s_prev = s_scr[...]
    y_off = jnp.dot(cm, s_prev.astype(BF16), preferred_element_type=F32)
    left = lax.broadcasted_iota(jnp.int32, (q, LANES), 1) < SSM_HEAD_DIM
    xw_parts, dec_parts = [], []
    for j in range(HEADS_PER_GROUP // 2):
        h0, h1 = 2 * j, 2 * j + 1
        sl = slice(j * LANES, (j + 1) * LANES)
        col0, col1 = acum[:, h0:h0 + 1], acum[:, h1:h1 + 1]
        pa = jnp.where(left, col0, col1)
        dtp = jnp.where(left, dt[:, h0:h0 + 1], dt[:, h1:h1 + 1])
        x = xs_ref[:, sl].astype(F32)
        xdt = x * dtp
        m0 = (jnp.exp(jnp.where(causal, col0 - acum_t[h0:h0 + 1, :], -jnp.inf)) * cb).astype(BF16)
        m1 = (jnp.exp(jnp.where(causal, col1 - acum_t[h1:h1 + 1, :], -jnp.inf)) * cb).astype(BF16)
        x_l = jnp.where(left, xdt, 0.0).astype(BF16)
        x_r = jnp.where(left, 0.0, xdt).astype(BF16)
        y_diag = (jnp.dot(m0, x_l, preferred_element_type=F32)
                  + jnp.dot(m1, x_r, preferred_element_type=F32))
        y = y_diag + y_off[:, sl] * jnp.exp(pa) + x * dsk_ref[:, sl]
        y_ref[:, sl] = y.astype(y_ref.dtype)
        last = pa[q - 1:q, :]
        xw_parts.append((xdt * jnp.exp(last - pa)).astype(BF16))
        dec_parts.append(jnp.exp(last))
    xw = jnp.concatenate(xw_parts, axis=1)
    dec = jnp.concatenate(dec_parts, axis=1)
    s_new = s_prev * dec + lax.dot_general(bm, xw, (((0,), (0,)), ((), ())), preferred_element_type=F32)
    s_scr[...] = s_new
    so_ref[...] = s_new


def _ssd_scan(act, dtg, dtg_t, bias, bias_t, a, a_t, dsk, s0, *, q):
    nb, L, _ = act.shape
    g_, r_, n_ = SSM_GROUPS, HEADS_PER_GROUP, SSM_STATE
    gw = GROUP_WIDTH
    grid = (nb, g_, L // q)
    b_off = D_INNER // n_
    c_off = b_off + g_
    return pl.pallas_call(
        functools.partial(_ssd_body, q=q), grid=grid,
        in_specs=[
            pl.BlockSpec((None, q, gw), lambda b, g, c: (b, c, g)),
            pl.BlockSpec((None, q, n_), lambda b, g, c: (b, c, b_off + g)),
            pl.BlockSpec((None, q, n_), lambda b, g, c: (b, c, c_off + g)),
            pl.BlockSpec((None, None, q, r_), lambda b, g, c: (b, g, c, 0)),
            pl.BlockSpec((None, None, r_, q), lambda b, g, c: (b, g, 0, c)),
            pl.BlockSpec((None, 1, r_), lambda b, g, c: (g, 0, 0)),
            pl.BlockSpec((None, r_, 1), lambda b, g, c: (g, 0, 0)),
            pl.BlockSpec((None, 1, r_), lambda b, g, c: (g, 0, 0)),
            pl.BlockSpec((None, r_, 1), lambda b, g, c: (g, 0, 0)),
            pl.BlockSpec((1, gw), lambda b, g, c: (0, g)),
            pl.BlockSpec((None, None, n_, gw), lambda b, g, c: (b, g, 0, 0)),
        ],
        out_specs=[
            pl.BlockSpec((None, q, gw), lambda b, g, c: (b, c, g)),
            pl.BlockSpec((None, None, n_, gw), lambda b, g, c: (b, g, 0, 0)),
        ],
        out_shape=[jax.ShapeDtypeStruct((nb, L, D_INNER), BF16),
                   jax.ShapeDtypeStruct((nb, g_, n_, gw), F32)],
        scratch_shapes=[pltpu.VMEM((n_, gw), F32)],
        compiler_params=_cparams(("parallel", "parallel", "arbitrary")), name="ssd_scan",
    )(act, act, act, dtg, dtg_t, bias, bias_t, a, a_t, dsk, s0)


def _gather_rows(tok_ref, base, src_hbm, dst, sem, n):
    def body(r, carry):
        t = tok_ref[base + r]
        pltpu.make_async_copy(src_hbm.at[pl.ds(t, 1)], dst.at[pl.ds(r, 1)], sem).start()
        return carry

    lax.fori_loop(0, n, body, 0)


def _gather_rows_unrolled(tok_ref, base, src_hbm, dst, sem, n):
    for r in range(n):
        t = tok_ref[base + r]
        pltpu.make_async_copy(src_hbm.at[pl.ds(t, 1)], dst.at[pl.ds(r, 1)], sem).start(priority=r % 2)


def _moe_body(blk_e_ref, nxt_e_ref, tok_ref, nused_ref, xt_hbm, wg_hbm, wu_hbm, wd_hbm, yb_ref,
              xbuf, xsem, wg32, wu32, wd32, wsem, wg_bf, wu_bf, wd_bf, xb, *, bm, layer):
    b = pl.program_id(0)
    slot = b % 2
    nused = nused_ref[0]
    e_cur = blk_e_ref[b]

    def weight_copies(e):
        return (pltpu.make_async_copy(wg_hbm.at[layer, e], wg32, wsem.at[0]),
                pltpu.make_async_copy(wu_hbm.at[layer, e], wu32, wsem.at[1]),
                pltpu.make_async_copy(wd_hbm.at[layer, e], wd32, wsem.at[2]))

    def gather_unrolled(base, slot_):
        for r in range(bm):
            t = tok_ref[base + r]
            pltpu.make_async_copy(xt_hbm.at[pl.ds(t, 1)], xbuf.at[slot_, pl.ds(r, 1)], xsem.at[slot_]).start()

    @pl.when(jnp.logical_and(b == 0, nused > 0))
    def _():
        for cp in weight_copies(e_cur):
            cp.start(priority=1)
        _gather_rows(tok_ref, 0, xt_hbm, xbuf.at[0], xsem.at[0], bm)

    prev_e = blk_e_ref[jnp.maximum(b - 1, 0)]
    changed = jnp.logical_or(b == 0, e_cur != prev_e)

    @pl.when(jnp.logical_and(changed, b < nused))
    def _():
        for cp in weight_copies(e_cur):
            cp.wait()
        wg_bf[...] = wg32[...].astype(BF16)
        wu_bf[...] = wu32[...].astype(BF16)
        wd_bf[...] = wd32[...].astype(BF16)
        nxt = nxt_e_ref[b]

        @pl.when(nxt >= 0)
        def _():
            for cp in weight_copies(nxt):
                cp.start(priority=1)

    def compute(prefetch):
        pltpu.make_async_copy(xbuf.at[slot], xbuf.at[slot], xsem.at[slot]).wait()
        w = xbuf[slot]
        half = w.shape[1]
        xb[:, :half] = lax.bitcast_convert_type(w << 16, F32).astype(BF16)
        xb[:, half:] = lax.bitcast_convert_type(w & jnp.uint32(0xFFFF0000), F32).astype(BF16)
        if prefetch:
            gather_unrolled((b + 1) * bm, 1 - slot)
        x = xb[...]
        g = jnp.dot(x, wg_bf[...], preferred_element_type=F32)
        u = jnp.dot(x, wu_bf[...], preferred_element_type=F32)
        hid = (_silu(g) * u).astype(BF16)
        yb_ref[...] = jnp.dot(hid, wd_bf[...], preferred_element_type=F32)

    @pl.when(b + 1 < nused)
    def _():
        compute(True)

    @pl.when(b + 1 == nused)
    def _():
        compute(False)

    @pl.when(b >= nused)
    def _():
        yb_ref[...] = jnp.zeros(yb_ref.shape, yb_ref.dtype)


def _moe_experts(xt, blk_e, nxt_e, row_token, nused, w_gate, w_up, w_down, *, bm, layer):
    nblk = blk_e.shape[0]
    d = w_gate.shape[2]
    de = w_gate.shape[3]
    assert xt.shape[1] * 2 == d and xt.dtype == jnp.uint32
    anyspec = pl.BlockSpec(memory_space=pl.ANY)
    gs = pltpu.PrefetchScalarGridSpec(
        num_scalar_prefetch=4, grid=(nblk,),
        in_specs=[anyspec, anyspec, anyspec, anyspec],
        out_specs=pl.BlockSpec((bm, d), lambda b, be, nx, tok, nu: (b, 0)),
        scratch_shapes=[
            pltpu.VMEM((2, bm, d // 2), jnp.uint32),
            pltpu.SemaphoreType.DMA((2,)),
            pltpu.VMEM((d, de), F32),
            pltpu.VMEM((d, de), F32),
            pltpu.VMEM((de, d), F32),
            pltpu.SemaphoreType.DMA((3,)),
            pltpu.VMEM((d, de), BF16),
            pltpu.VMEM((d, de), BF16),
            pltpu.VMEM((de, d), BF16),
            pltpu.VMEM((bm, d), BF16),
        ],
    )
    return pl.pallas_call(
        functools.partial(_moe_body, bm=bm, layer=layer), grid_spec=gs,
        out_shape=jax.ShapeDtypeStruct((nblk * bm, d), F32),
        compiler_params=_cparams(("arbitrary",)), name="moe_experts",
    )(blk_e, nxt_e, row_token, nused, xt, w_gate, w_up, w_down)


def _combine_body(dest_ref, h_ref, g_ref, yb_hbm, o_ref, buf, sem, *, tc, ntile):
    i = pl.program_id(0)
    slot = i % 2
    n = TOP_K * tc

    @pl.when(i == 0)
    def _():
        _gather_rows(dest_ref, 0, yb_hbm, buf.at[0], sem.at[0], n)

    @pl.when(i + 1 < ntile)
    def _():
        _gather_rows_unrolled(dest_ref, (i + 1) * n, yb_hbm, buf.at[1 - slot], sem.at[1 - slot], n)

    pltpu.make_async_copy(buf.at[slot], buf.at[slot], sem.at[slot]).wait()
    g = g_ref[...]
    o_ref[...] = (h_ref[...] + g[:, 0:1] * buf[slot, 0:tc, :] + g[:, 1:2] * buf[slot, tc:2 * tc, :])


def _moe_combine(h, yb, dest_tiles, gates, *, tc):
    t, d = h.shape
    ntile = t // tc
    gs = pltpu.PrefetchScalarGridSpec(
        num_scalar_prefetch=1, grid=(ntile,),
        in_specs=[
            pl.BlockSpec((tc, d), lambda i, dst: (i, 0)),
            pl.BlockSpec((tc, LANES), lambda i, dst: (i, 0)),
            pl.BlockSpec(memory_space=pl.ANY),
        ],
        out_specs=pl.BlockSpec((tc, d), lambda i, dst: (i, 0)),
        scratch_shapes=[pltpu.VMEM((2, TOP_K * tc, d), F32), pltpu.SemaphoreType.DMA((2,))],
    )
    return pl.pallas_call(
        functools.partial(_combine_body, tc=tc, ntile=ntile), grid_spec=gs,
        out_shape=jax.ShapeDtypeStruct((t, d), F32),
        compiler_params=_cparams(("arbitrary",)), name="moe_combine",
    )(dest_tiles, h, gates, yb)


def _moe_plan(idx, bm, tc):
    t = idx.shape[0]
    a = t * TOP_K
    e_ = N_EXPERTS
    flat_e = idx.reshape(a)
    onehot = (flat_e[:, None] == jnp.arange(e_, dtype=jnp.int32)[None, :]).astype(jnp.int32)
    csum = jnp.cumsum(onehot, axis=0)
    rank = jnp.sum(onehot * csum, axis=1) - 1
    counts = csum[-1]
    padded = (counts + bm - 1) // bm * bm
    pad_end = jnp.cumsum(padded)
    pad_start = pad_end - padded
    dest = jnp.sum(onehot * pad_start[None, :], axis=1) + rank
    nblk = -(-a // bm) + e_
    rows = nblk * bm
    row_token = jnp.zeros((rows,), jnp.int32).at[dest].set(jnp.arange(a, dtype=jnp.int32) // TOP_K)
    blk_row0 = jnp.arange(nblk, dtype=jnp.int32) * bm
    blk_e = jnp.minimum(jnp.sum((pad_end[None, :] <= blk_row0[:, None]).astype(jnp.int32), axis=1), e_ - 1)
    ids = jnp.arange(e_, dtype=jnp.int32)
    later_used = jnp.logical_and(ids[None, :] > ids[:, None], (counts > 0)[None, :])
    nxt_of_e = jnp.min(jnp.where(later_used, ids[None, :], e_), axis=1)
    nxt_of_e = jnp.where(nxt_of_e == e_, -1, nxt_of_e)
    nxt_e = jnp.sum(jnp.where(blk_e[:, None] == ids[None, :], nxt_of_e[None, :], 0), axis=1).astype(jnp.int32)
    nused = (pad_end[-1] // bm).astype(jnp.int32).reshape(1)
    dest_tiles = dest.reshape(t // tc, tc, TOP_K).transpose(0, 2, 1).reshape(a).astype(jnp.int32)
    return blk_e.astype(jnp.int32), nxt_e, row_token, nused, dest_tiles


def _flash_body(qi_ref, kj_ref, q_ref, k_ref, v_ref, o_ref, m_scr, l_scr, acc_scr, *, hb):
    p_id = pl.program_id(1)
    qi = qi_ref[p_id]
    kj = kj_ref[p_id]
    tk = k_ref.shape[1]

    @pl.when(kj == 0)
    def _():
        m_scr[...] = jnp.full(m_scr.shape, -jnp.inf, F32)
        l_scr[...] = jnp.zeros(l_scr.shape, F32)
        acc_scr[...] = jnp.zeros(acc_scr.shape, F32)

    def step(diag):
        for hh in range(hb):
            s = lax.dot_general(q_ref[hh], k_ref[hh], (((1,), (1,)), ((), ())), preferred_element_type=F32)
            if diag:
                r = lax.broadcasted_iota(jnp.int32, s.shape, 0)
                c = lax.broadcasted_iota(jnp.int32, s.shape, 1)
                s = jnp.where((c // CHUNK) <= (r // CHUNK), s, -jnp.inf)
            m_prev = m_scr[hh]
            m_next = jnp.maximum(m_prev, jnp.max(s, axis=1, keepdims=True))
            p = jnp.exp2(s - jnp.concatenate([m_next] * (tk // LANES), axis=1))
            alpha = jnp.exp2(m_prev - m_next)
            l_scr[hh] = alpha * l_scr[hh] + jnp.sum(p, axis=1, keepdims=True)
            acc_scr[hh] = alpha * acc_scr[hh] + jnp.dot(p.astype(BF16), v_ref[hh], preferred_element_type=F32)
            m_scr[hh] = m_next

    @pl.when(kj < qi)
    def _():
        step(False)

    @pl.when(kj == qi)
    def _():
        step(True)
        for hh in range(hb):
            o_ref[:, hh * V_DIM:(hh + 1) * V_DIM] = (acc_scr[hh] / l_scr[hh]).astype(o_ref.dtype)


def _flash_prompt(q, k, v, lp, *, tq, hb):
    assert V_DIM == LANES
    nh = q.shape[0]
    nq = lp // tq
    pairs = [(a, b) for a in range(nq) for b in range(a + 1)]
    qi_tab = jnp.asarray(np.array([p[0] for p in pairs], np.int32))
    kj_tab = jnp.asarray(np.array([p[1] for p in pairs], np.int32))
    gs = pltpu.PrefetchScalarGridSpec(
        num_scalar_prefetch=2, grid=(nh // hb, len(pairs)),
        in_specs=[
            pl.BlockSpec((hb, tq, QK_DIM), lambda h, p, qi, kj: (h, qi[p], 0)),
            pl.BlockSpec((hb, tq, QK_DIM), lambda h, p, qi, kj: (h, kj[p], 0)),
            pl.BlockSpec((hb, tq, V_DIM), lambda h, p, qi, kj: (h, kj[p], 0)),
        ],
        out_specs=pl.BlockSpec((tq, hb * V_DIM), lambda h, p, qi, kj: (qi[p], h)),
        scratch_shapes=[pltpu.VMEM((hb, tq, LANES), F32), pltpu.VMEM((hb, tq, LANES), F32),
                        pltpu.VMEM((hb, tq, V_DIM), F32)],
    )
    return pl.pallas_call(
        functools.partial(_flash_body, hb=hb), grid_spec=gs,
        out_shape=jax.ShapeDtypeStruct((lp, nh * V_DIM), BF16),
        compiler_params=_cparams(("parallel", "arbitrary")), name="flash_prompt",
    )(qi_tab, kj_tab, q, k, v)


def _attn_sample_body(q_ref, k_ref, o_ref, *, ls, past, s_valid):
    nh = q_ref.shape[0]
    q = q_ref[...].reshape(nh * ls, q_ref.shape[2])
    k = k_ref[...]
    s = lax.dot_general(q, k, (((1,), (1,)), ((), ())), preferred_element_type=F32)
    r = lax.broadcasted_iota(jnp.int32, s.shape, 0)
    c = lax.broadcasted_iota(jnp.int32, s.shape, 1)
    q_pos = past + r % ls
    s = jnp.where((c // CHUNK) <= (q_pos // CHUNK), s, -jnp.inf)
    s = jnp.where(c < s_valid, s, -jnp.inf)
    m = jnp.max(s, axis=-1, keepdims=True)
    p = jnp.exp2(s - m)
    l = jnp.sum(p, axis=-1, keepdims=True)
    o = jnp.dot(p.astype(BF16), k[:, :KV_RANK], preferred_element_type=F32) / l
    o_ref[...] = o.reshape(nh, ls, KV_RANK).astype(o_ref.dtype)


def _attn_sample(q_abs, kc, *, ls, past, s_valid):
    nh, ts, dk = q_abs.shape
    nb, s_pad, _ = kc.shape
    return pl.pallas_call(
        functools.partial(_attn_sample_body, ls=ls, past=past, s_valid=s_valid), grid=(nb,),
        in_specs=[
            pl.BlockSpec((nh, ls, dk), lambda b: (0, b, 0)),
            pl.BlockSpec((None, s_pad, dk), lambda b: (b, 0, 0)),
        ],
        out_specs=pl.BlockSpec((nh, ls, KV_RANK), lambda b: (0, b, 0)),
        out_shape=jax.ShapeDtypeStruct((nh, ts, KV_RANK), BF16),
        compiler_params=_cparams(("parallel",)), name="attn_sample",
    )(q_abs, kc)


def _norm_body(x_ref, g_ref, o_ref):
    o_ref[...] = _rms_rows(x_ref[...]) * g_ref[...]


def _final_norm(h, g, row0, nrows, *, tm):
    d = h.shape[1]
    blk0 = row0 // tm
    return pl.pallas_call(
        _norm_body, grid=(nrows // tm,),
        in_specs=[pl.BlockSpec((tm, d), lambda i: (blk0 + i, 0)), pl.BlockSpec((1, d), lambda i: (0, 0))],
        out_specs=pl.BlockSpec((tm, d), lambda i: (i, 0)),
        out_shape=jax.ShapeDtypeStruct((nrows, d), F32),
        compiler_params=_cparams(("parallel",)), name="final_norm",
    )(h, g)


def _rope_tables(pos):
    half = QK_ROPE // 2
    inv = ROPE_THETA ** (-np.arange(half, dtype=np.float64) / half)
    ang = np.asarray(pos, np.float64)[:, None] * inv[None, :]
    cos = np.concatenate([np.cos(ang), np.cos(ang)], axis=1)
    sin = np.concatenate([-np.sin(ang), np.sin(ang)], axis=1)
    return cos.astype(np.float32), sin.astype(np.float32)


def _swap_rope_halves(w):
    half = QK_ROPE // 2
    return jnp.concatenate([w[..., half:], w[..., :half]], axis=-1)


def _moe_layer(h, g_ffn, w_rg, b_rg, w_re, b_re, w_gate, w_up, w_down, *, tm, layer):
    t, d = h.shape
    npad = LANES - MOE_GROUPS - N_EXPERTS
    w_r = jnp.concatenate([w_rg, w_re, jnp.zeros((d, npad), F32)], axis=1)
    b_r = jnp.concatenate([b_rg, b_re, jnp.zeros((npad,), F32)]).reshape(1, LANES)
    grid = (t // tm, 1)
    x_args = [(h, pl.BlockSpec((tm, d), lambda i, j: (i, 0))),
              (g_ffn.reshape(1, d), pl.BlockSpec((1, d), lambda i, j: (0, 0)))]
    w_args = [(w_r, pl.BlockSpec((d, LANES), lambda i, j: (0, 0)))]
    e_args = [(b_r, pl.BlockSpec((1, LANES), lambda i, j: (0, 0)))]
    tile = pl.BlockSpec((tm, LANES), lambda i, j: (i, 0))
    out_defs = [(jax.ShapeDtypeStruct((t, LANES), jnp.int32), tile),
                (jax.ShapeDtypeStruct((t, LANES), F32), tile)]
    xn_out = (jax.ShapeDtypeStruct((t, d // 2), jnp.uint32), pl.BlockSpec((tm, d // 2), lambda i, j: (i, 0)))
    idx_t, gates_t, xt = _fused_mm(x_args, w_args, e_args, out_defs, grid=grid, prologue=_pro_rms,
                                   epilogue=_epi_route, xn_shape=(tm, d), xn_dtype=F32, xn_out=xn_out,
                                   xn_emit=_pack_bf16_pairs, precision=HIGHEST, name="router")
    tc = min(tm, 128)
    blk_e, nxt_e, row_token, nused, dest_tiles = _moe_plan(idx_t[:, :TOP_K], MOE_BM, tc)
    yb = _moe_experts(xt, blk_e, nxt_e, row_token, nused, w_gate, w_up, w_down, bm=MOE_BM, layer=layer)
    return _moe_combine(h, yb, dest_tiles, gates_t, tc=tc)


def _ple_layer(h, p_rows, g_ple, w_gate, w_proj, *, tm, tn):
    t, d = h.shape
    return _mm_resident([h], [g_ple.reshape(1, d)], [p_rows.astype(BF16)], [w_gate.astype(BF16), w_proj.astype(BF16)],
                        [0, 1], [], prologue=_pro_rms, epilogue=_epi_ple, out_dtype=F32, tm=tm, name="ple_gate",
                        res_from_x=True)


def kernel(x_prompt, x_sample, state_conv, state_ssm, cache_kv_latent, cache_k_rope, p_prompt, p_sample,
           g_mix, w_ssm_in, w_conv, b_conv, dt_bias, a_log, d_skip, g_ssm_norm, w_ssm_out,
           g_kv_in, w_dkv, g_kv, w_uk, w_uv, w_dq, g_q, w_uq, w_o,
           g_ffn, w_router_grp, b_router_grp, w_router_exp, b_router_exp, w_exp_gate, w_exp_up, w_exp_down,
           g_ple, w_ple_gate, w_ple_proj, g_final):
    bp, lp_each, d = x_prompt.shape
    bs, ls, _ = x_sample.shape
    past = cache_kv_latent.shape[1]
    assert bp == 1 and w_ssm_in.shape[0] == 1 and w_dq.shape[0] == 1 and d == D_MODEL
    assert ls >= CONV_W - 1 and ls % SUBLANES == 0 and past % CHUNK == 0
    lp = bp * lp_each
    ts = bs * ls
    t = lp + ts
    tm = min(512, math.gcd(lp, ts))
    tn = 512
    g_, r_, n_, hd = SSM_GROUPS, HEADS_PER_GROUP, SSM_STATE, SSM_HEAD_DIM

    h = jnp.concatenate([x_prompt.reshape(lp, d), x_sample.reshape(ts, d)], axis=0)

    w_in = w_ssm_in[0]
    w_z = w_in[:, :D_INNER].astype(BF16)
    xw = SSM_IN_XW
    w_x = jnp.pad(w_in[:, D_INNER:], ((0, 0), (0, xw - CONV_DIM - SSM_HEADS))).astype(BF16)
    g0 = g_mix[0].reshape(1, d)
    tm_in = max(c for c in range(16, 1153, 16) if t % c == 0)
    mm_in = functools.partial(_mm_rows, [h], [g0], prologue=_pro_rms, epilogue=_epi_id, tm=tm_in)
    z = mm_in(w_z, [], out_dtype=BF16, tn=1024, name="ssm_in_z")[0]
    xbc = mm_in(w_x, [], out_dtype=F32, tn=xw // 5, name="ssm_in_xbc")[0]
    dt_raw = xbc[:, CONV_DIM:CONV_DIM + SSM_HEADS]

    wc = w_conv[0]
    bc = b_conv[0].reshape(1, CONV_DIM)
    tl_p = min(512, lp)
    act_p = _conv_silu(xbc.reshape(1, t, xw), 0, 1, lp, jnp.zeros((1, SUBLANES, CONV_DIM), F32), wc, bc,
                       tl=tl_p, tc=1536)
    st_s = jnp.pad(state_conv[0].astype(F32), ((0, 0), (SUBLANES - (CONV_W - 1), 0), (0, 0)))
    act_s = _conv_silu(xbc.reshape(t // ls, ls, xw), lp // ls, bs, ls, st_s, wc, bc, tl=ls, tc=1536)
    conv_p = xbc[lp - (CONV_W - 1):lp, :CONV_DIM].reshape(1, 1, CONV_W - 1, CONV_DIM)
    conv_s = xbc[lp:, :CONV_DIM].reshape(bs, ls, CONV_DIM)[:, ls - (CONV_W - 1):].reshape(
        1, bs, CONV_W - 1, CONV_DIM)

    bias = dt_bias[0].astype(F32).reshape(g_, 1, r_)
    a_neg = (-jnp.exp(a_log[0].astype(F32))).reshape(g_, 1, r_)
    dsk = jnp.repeat(d_skip[0].astype(F32), hd).reshape(1, D_INNER)

    def dt_views(rows, nb, L):
        v = rows.reshape(nb, L, g_, r_).transpose(0, 2, 1, 3)
        return v, v.transpose(0, 1, 3, 2)

    def scan(act, rows, s0, q):
        nb, L, _ = act.shape
        dtg, dtg_t = dt_views(rows, nb, L)
        return _ssd_scan(act, dtg, dtg_t, bias, bias.transpose(0, 2, 1), a_neg, a_neg.transpose(0, 2, 1),
                         dsk, s0, q=q)

    def state_in(s):
        nb = s.shape[0]
        return s.astype(F32).reshape(nb, g_, r_ * hd, n_).transpose(0, 1, 3, 2)

    def state_out(s):
        nb = s.shape[0]
        return s.transpose(0, 1, 3, 2).reshape(1, nb, SSM_HEADS, hd, n_)

    y_p, s_p = scan(act_p, dt_raw[:lp], jnp.zeros((1, g_, n_, r_ * hd), F32), min(SSD_Q, lp))
    y_s, s_s = scan(act_s, dt_raw[lp:], state_in(state_ssm[0]), ls)
    ssm_p, ssm_s = state_out(s_p), state_out(s_s)

    y_all = jnp.concatenate([y_p.reshape(lp, D_INNER), y_s.reshape(ts, D_INNER)], axis=0)
    h = _mm_resident([y_all, z], [g_ssm_norm[0].reshape(1, D_INNER)], [], [w_ssm_out[0].astype(BF16)], [0], [h],
                     prologue=_pro_gated, epilogue=_epi_res, out_dtype=F32, tm=tm // 2, name="ssm_out")

    def ffn_and_ple(h, i):
        h = _moe_layer(h, g_ffn[i], w_router_grp[i], b_router_grp[i], w_router_exp[i], b_router_exp[i],
                       w_exp_gate, w_exp_up, w_exp_down, tm=tm, layer=i)
        p_rows = jnp.concatenate([p_prompt[i].reshape(lp, -1), p_sample[i].reshape(ts, -1)], axis=0)
        return _ple_layer(h, p_rows, g_ple[i], w_ple_gate[i], w_ple_proj[i], tm=tm, tn=tn)

    h = ffn_and_ple(h, 0)

    pos = np.concatenate([np.arange(lp), np.tile(past + np.arange(ls), bs)])
    cos64, sin64 = _rope_tables(pos)
    gk = g_kv_in.reshape(1, d)
    ckv = _mm_rows([h], [gk], w_dkv[:, :KV_RANK].astype(BF16), [], prologue=_pro_rms, epilogue=_epi_rms_out,
                   out_dtype=F32, tm=tm, tn=KV_RANK, name="kv_latent", e_consts=[g_kv.reshape(1, KV_RANK)])[0]
    w_r = w_dkv[:, KV_RANK:]
    row64 = pl.BlockSpec((tm, QK_ROPE), lambda i, j: (i, 0))
    w64 = pl.BlockSpec((d, QK_ROPE), lambda i, j: (0, 0))
    kr = _fused_mm(
        [(h, pl.BlockSpec((tm, d), lambda i, j: (i, 0))), (gk, pl.BlockSpec((1, d), lambda i, j: (0, 0)))],
        [(w_r.astype(BF16), w64), (_swap_rope_halves(w_r).astype(BF16), w64)],
        [(jnp.asarray(cos64), row64), (jnp.asarray(sin64), row64)],
        [(jax.ShapeDtypeStruct((t, QK_ROPE), F32), row64)],
        grid=(t // tm, 1), prologue=_pro_rms, epilogue=_epi_rot, xn_shape=(tm, d), name="k_rope")[0]

    ql = _mm_rows([h], [g_mix[1].reshape(1, d)], w_dq[0].astype(BF16), [], prologue=_pro_rms,
                  epilogue=_epi_rms_out, out_dtype=BF16, tm=tm, tn=Q_RANK, name="q_latent",
                  e_consts=[g_q[0].reshape(1, Q_RANK)])[0]
    nh = MLA_HEADS
    w_q = w_uq[0].reshape(Q_RANK, nh, QK_DIM).transpose(1, 0, 2)
    w_q_rot = jnp.concatenate([jnp.zeros((nh, Q_RANK, QK_NOPE), F32), _swap_rope_halves(w_q[..., QK_NOPE:])], -1)
    cos_q = np.concatenate([np.ones((t, QK_NOPE), np.float32), cos64], axis=1) * np.float32(QK_PRESCALE)
    sin_q = np.concatenate([np.zeros((t, QK_NOPE), np.float32), sin64], axis=1) * np.float32(QK_PRESCALE)
    q_cat = _heads_mm(ql, [w_q.astype(BF16), w_q_rot.astype(BF16)], [jnp.asarray(cos_q), jnp.asarray(sin_q)],
                      [QK_DIM], tm=tm, hpb=8, epilogue=_epi_rot, name="q_heads")[0]

    lat_pad = LAT_PAD - KV_RANK - QK_ROPE
    ckr_p = jnp.concatenate([ckv[:lp], kr[:lp], jnp.zeros((lp, lat_pad), F32)], axis=1).astype(BF16)
    w_uk_h = w_uk.transpose(1, 0, 2)
    w_uv_h = w_uv.transpose(1, 0, 2)
    eye_r = jnp.broadcast_to(jnp.eye(QK_ROPE, dtype=F32), (nh, QK_ROPE, QK_ROPE))
    w_k_ext = jnp.concatenate([
        jnp.concatenate([w_uk_h, jnp.zeros((nh, KV_RANK, QK_ROPE), F32)], axis=2),
        jnp.concatenate([jnp.zeros((nh, QK_ROPE, QK_NOPE), F32), eye_r], axis=2),
        jnp.zeros((nh, lat_pad, QK_DIM), F32)], axis=1).astype(BF16)
    w_v_ext = jnp.concatenate([w_uv_h, jnp.zeros((nh, LAT_PAD - KV_RANK, V_DIM), F32)], axis=1).astype(BF16)
    k_cat, v_h = _heads_mm(ckr_p, [w_k_ext, w_v_ext], [], [QK_DIM, V_DIM], tm=tm, hpb=8,
                           epilogue=_epi_id, name="kv_heads")
    o_p = _flash_prompt(q_cat, k_cat, v_h, lp, tq=min(512, lp), hb=4)

    w_abs = jnp.concatenate([
        jnp.concatenate([w_uk_h.transpose(0, 2, 1), jnp.zeros((nh, QK_NOPE, LAT_PAD - KV_RANK), F32)], axis=2),
        jnp.concatenate([jnp.zeros((nh, QK_ROPE, KV_RANK), F32), eye_r,
                         jnp.zeros((nh, QK_ROPE, lat_pad), F32)], axis=2)], axis=1).astype(BF16)
    blk_s = lp // ts if lp % ts == 0 else None
    assert blk_s is not None
    q_abs = _fused_mm(
        [(q_cat, pl.BlockSpec((None, ts, QK_DIM), lambda i, j: (j, blk_s + i, 0)))],
        [(w_abs, pl.BlockSpec((None, QK_DIM, LAT_PAD), lambda i, j: (j, 0, 0)))],
        [],
        [(jax.ShapeDtypeStruct((nh, ts, LAT_PAD), BF16), pl.BlockSpec((None, ts, LAT_PAD), lambda i, j: (j, i, 0)))],
        grid=(1, nh), prologue=_pro_cast, epilogue=_epi_id, x_per_j=True, name="q_absorb")[0]
    s_valid = past + ls
    s_pad = -(-s_valid // LANES) * LANES
    kc_new = jnp.concatenate([ckv[lp:], kr[lp:]], axis=1).reshape(bs, ls, KV_RANK + QK_ROPE)
    kc = jnp.concatenate([jnp.concatenate([cache_kv_latent, cache_k_rope], axis=2), kc_new], axis=1)
    kc = jnp.pad(kc, ((0, 0), (0, s_pad - s_valid), (0, lat_pad))).astype(BF16)
    o_lat = _attn_sample(q_abs, kc, ls=ls, past=past, s_valid=s_valid)
    o_s = _fused_mm(
        [(o_lat, pl.BlockSpec((None, ts, KV_RANK), lambda i, j: (j, i, 0)))],
        [(w_uv_h.astype(BF16), pl.BlockSpec((None, KV_RANK, V_DIM), lambda i, j: (j, 0, 0)))],
        [],
        [(jax.ShapeDtypeStruct((ts, nh * V_DIM), BF16), pl.BlockSpec((ts, V_DIM), lambda i, j: (i, j)))],
        grid=(1, nh), prologue=_pro_cast, epilogue=_epi_id, x_per_j=True, name="v_absorb")[0]

    o_all = jnp.concatenate([o_p, o_s], axis=0)
    h = _mm_resident([o_all], [], [], [w_o[0].astype(BF16)], [0], [h], prologue=_pro_cast, epilogue=_epi_res,
                     out_dtype=F32, tm=tm, name="attn_out")
    h = ffn_and_ple(h, 1)

    gf = g_final.reshape(1, d)
    y_p_out = _final_norm(h, gf, 0, lp, tm=tm).reshape(bp, lp_each, d)
    y_s_out = _final_norm(h, gf, lp, ts, tm=tm).reshape(bs, ls, d)
    return (y_p_out, y_s_out, conv_p, ssm_p, ckv[:lp].reshape(bp, lp_each, KV_RANK),
            kr[:lp].reshape(bp, lp_each, QK_ROPE), conv_s, ssm_s, ckv[lp:].reshape(bs, ls, KV_RANK),
            kr[lp:].reshape(bs, ls, QK_ROPE))
```

```python
import functools
import math

import numpy as np
import jax
import jax.numpy as jnp
from jax import lax
from jax.experimental import pallas as pl
from jax.experimental.pallas import tpu as pltpu

F32 = jnp.float32
BF16 = jnp.bfloat16
HIGHEST = lax.Precision.HIGHEST

EPS = 1e-6
CHUNK = 64
D_MODEL = 2048
D_INNER = 2 * D_MODEL
SSM_HEAD_DIM = 64
SSM_HEADS = D_INNER // SSM_HEAD_DIM
SSM_STATE = 128
SSM_GROUPS = 8
HEADS_PER_GROUP = SSM_HEADS // SSM_GROUPS
GROUP_WIDTH = D_INNER // SSM_GROUPS
CONV_W = 4
CONV_DIM = D_INNER + 2 * SSM_GROUPS * SSM_STATE
MLA_HEADS = 16
Q_RANK = 512
KV_RANK = 512
QK_NOPE = 128
QK_ROPE = 64
QK_DIM = QK_NOPE + QK_ROPE
V_DIM = 128
ROPE_THETA = 10000.0
ATTN_SCALE = QK_DIM ** -0.5
QK_PRESCALE = ATTN_SCALE * math.log2(math.e)
MOE_GROUPS = 4
EXPERTS_PER_GROUP = 8
N_EXPERTS = MOE_GROUPS * EXPERTS_PER_GROUP
TOP_K = 2
D_EXPERT = 512

LANES = 128
SUBLANES = 8
VMEM_LIMIT = 56 * 1024 * 1024
SSD_Q = 128
MOE_BM = 256
LAT_PAD = 640
SSM_IN_XW = 6400


def _cparams(sem):
    return pltpu.CompilerParams(dimension_semantics=sem, vmem_limit_bytes=VMEM_LIMIT)


def _sigmoid(v):
    return 1.0 / (1.0 + jnp.exp(-v))


def _silu(v):
    return v * _sigmoid(v)


def _softplus(v):
    return jnp.maximum(v, 0.0) + jnp.log1p(jnp.exp(-jnp.abs(v)))


def _rms_rows(x):
    return x * lax.rsqrt(jnp.mean(x * x, axis=-1, keepdims=True) + EPS)


def _mm_body(*refs, nx, nw, ne, no, prologue, epilogue, x_per_j, emit_xn, xn_emit, precision):
    x_refs = refs[:nx]
    w_refs = refs[nx:nx + nw]
    e_refs = refs[nx + nw:nx + nw + ne]
    o_refs = refs[nx + nw + ne:nx + nw + ne + no]
    rest = refs[nx + nw + ne + no:]
    j = pl.program_id(1)
    if x_per_j:
        xn = prologue(*[r[...] for r in x_refs])
    else:
        xn_ref = rest[-1]

        @pl.when(j == 0)
        def _():
            v = prologue(*[r[...] for r in x_refs])
            xn_ref[...] = v.astype(xn_ref.dtype)
            if emit_xn:
                rest[0][...] = xn_emit(v).astype(rest[0].dtype)

        xn = xn_ref[...]
    accs = [jnp.dot(xn, w[...], preferred_element_type=F32, precision=precision) for w in w_refs]
    outs = epilogue(accs, [e[...] for e in e_refs])
    for o_ref, o in zip(o_refs, outs):
        o_ref[...] = o.astype(o_ref.dtype)


def _fused_mm(x_args, w_args, e_args, out_defs, *, grid, prologue, epilogue, xn_shape=None,
              xn_dtype=BF16, x_per_j=False, xn_out=None, xn_emit=None, precision=None, name=None):
    arrays = [a for a, _ in x_args + w_args + e_args]
    in_specs = [s for _, s in x_args + w_args + e_args]
    out_shape = [d for d, _ in out_defs]
    out_specs = [s for _, s in out_defs]
    emit_xn = xn_out is not None
    if emit_xn:
        out_shape.append(xn_out[0])
        out_specs.append(xn_out[1])
    scratch = [] if x_per_j else [pltpu.VMEM(xn_shape, xn_dtype)]
    body = functools.partial(
        _mm_body, nx=len(x_args), nw=len(w_args), ne=len(e_args), no=len(out_defs),
        prologue=prologue, epilogue=epilogue, x_per_j=x_per_j, emit_xn=emit_xn,
        xn_emit=xn_emit or (lambda v: v), precision=precision)
    return pl.pallas_call(
        body, grid=grid, in_specs=in_specs, out_specs=out_specs, out_shape=out_shape,
        scratch_shapes=scratch, compiler_params=_cparams(("parallel", "arbitrary")), name=name,
    )(*arrays)


def _pro_rms(x, g):
    return _rms_rows(x.astype(F32)) * g


def _pro_cast(x):
    return x


def _pack_bf16_pairs(v):
    k = v.shape[1] // 2
    lo = lax.bitcast_convert_type(v[:, :k].astype(BF16).astype(F32), jnp.uint32) >> 16
    hi = lax.bitcast_convert_type(v[:, k:].astype(BF16).astype(F32), jnp.uint32) & jnp.uint32(0xFFFF0000)
    return hi | lo


def _pro_gated(y, z, g):
    v = y.astype(F32) * _silu(z.astype(F32))
    parts = [_rms_rows(v[:, k * GROUP_WIDTH:(k + 1) * GROUP_WIDTH]) for k in range(SSM_GROUPS)]
    return jnp.concatenate(parts, axis=-1) * g


def _epi_id(accs, es):
    return accs


def _epi_res(accs, es):
    return [es[0] + accs[0]]


def _epi_ple(accs, es):
    return [es[0] + _sigmoid(accs[0]) * accs[1]]


def _epi_rms_out(accs, es):
    return [_rms_rows(accs[0]) * es[0]]


def _epi_rot(accs, es):
    cos, sin = es
    return [accs[0] * cos + accs[1] * sin]


def _epi_route(accs, es):
    lg = accs[0] + es[0]
    lane = lax.broadcasted_iota(jnp.int32, lg.shape, 1).astype(F32)
    neg = -jnp.inf
    big = 1.0e4
    is_grp = lane < MOE_GROUPS
    gl = jnp.where(is_grp, lg, neg)
    mg = jnp.max(gl, axis=-1, keepdims=True)
    g_sel = jnp.min(jnp.where(gl == mg, lane, big), axis=-1, keepdims=True)
    p_sel = 1.0 / jnp.sum(jnp.where(is_grp, jnp.exp(gl - mg), 0.0), axis=-1, keepdims=True)
    lo = MOE_GROUPS + g_sel * EXPERTS_PER_GROUP
    in_grp = jnp.where(lane >= lo, jnp.where(lane < lo + EXPERTS_PER_GROUP, 1.0, 0.0), 0.0) > 0.5
    el = jnp.where(in_grp, lg, neg)
    v1 = jnp.max(el, axis=-1, keepdims=True)
    i1 = jnp.min(jnp.where(el == v1, lane, big), axis=-1, keepdims=True)
    el2 = jnp.where(lane == i1, neg, el)
    v2 = jnp.max(el2, axis=-1, keepdims=True)
    i2 = jnp.min(jnp.where(el2 == v2, jnp.where(lane == i1, big, lane), big), axis=-1, keepdims=True)
    e21 = jnp.exp(v2 - v1)
    g1 = p_sel / (1.0 + e21)
    g2 = p_sel * e21 / (1.0 + e21)
    idx = jnp.where(lane == 0.0, i1 - MOE_GROUPS, jnp.where(lane == 1.0, i2 - MOE_GROUPS, 0.0))
    gates = jnp.where(lane == 0.0, g1, jnp.where(lane == 1.0, g2, 0.0))
    return [idx.astype(jnp.int32), gates]


def _mm_rows(x_list, consts, w, extras, *, prologue, epilogue, out_dtype, tm, tn, name,
             xn_dtype=BF16, precision=None, emit_xn_dtype=None, e_consts=()):
    m = x_list[0].shape[0]
    k, n = w.shape
    grid = (m // tm, n // tn)
    x_args = [(x, pl.BlockSpec((tm, x.shape[1]), lambda i, j: (i, 0))) for x in x_list]
    x_args += [(c, pl.BlockSpec((1, c.shape[1]), lambda i, j: (0, 0))) for c in consts]
    w_args = [(w, pl.BlockSpec((k, tn), lambda i, j: (0, j)))]
    e_args = [(c, pl.BlockSpec((1, tn), lambda i, j: (0, j))) for c in e_consts]
    e_args += [(e, pl.BlockSpec((tm, tn), lambda i, j: (i, j))) for e in extras]
    out_defs = [(jax.ShapeDtypeStruct((m, n), out_dtype), pl.BlockSpec((tm, tn), lambda i, j: (i, j)))]
    xn_out = None
    if emit_xn_dtype is not None:
        xn_out = (jax.ShapeDtypeStruct((m, k), emit_xn_dtype), pl.BlockSpec((tm, k), lambda i, j: (i, 0)))
    return _fused_mm(x_args, w_args, e_args, out_defs, grid=grid, prologue=prologue, epilogue=epilogue,
                     xn_shape=(tm, k), xn_dtype=xn_dtype, xn_out=xn_out, precision=precision, name=name)


def _mm_res_body(*refs, nx, nl, nw, ne, lhs_of_w, res_from_x, prologue, epilogue, cw):
    x_refs = refs[:nx]
    l_refs = refs[nx:nx + nl]
    w_refs = refs[nx + nl:nx + nl + nw]
    e_refs = refs[nx + nl + nw:nx + nl + nw + ne]
    o_ref = refs[nx + nl + nw + ne]
    xn_ref = refs[-1]
    xn_ref[...] = prologue(*[r[...] for r in x_refs]).astype(xn_ref.dtype)
    lhs = [xn_ref] + list(l_refs)
    for c in range(o_ref.shape[1] // cw):
        sl = slice(c * cw, (c + 1) * cw)
        accs = [jnp.dot(lhs[li][...], w[:, sl], preferred_element_type=F32) for w, li in zip(w_refs, lhs_of_w)]
        es = ([x_refs[0][:, sl]] if res_from_x else []) + [e[:, sl] for e in e_refs]
        o_ref[:, sl] = epilogue(accs, es)[0].astype(o_ref.dtype)


def _mm_resident(x_list, consts, lhs_list, ws, lhs_of_w, extras, *, prologue, epilogue, out_dtype, tm, name,
                 res_from_x=False, cw=512):
    m = x_list[0].shape[0]
    k = consts[0].shape[1] if consts else x_list[0].shape[1]
    n = ws[0].shape[1]
    row = lambda a: pl.BlockSpec((tm, a.shape[1]), lambda i: (i, 0))
    whole = lambda a: pl.BlockSpec(a.shape, lambda i: (0, 0), pipeline_mode=pl.Buffered(1))
    arrays = list(x_list) + list(consts) + list(lhs_list) + list(ws) + list(extras)
    in_specs = ([row(a) for a in x_list] + [whole(a) for a in consts] + [row(a) for a in lhs_list]
                + [whole(a) for a in ws] + [row(a) for a in extras])
    body = functools.partial(
        _mm_res_body, nx=len(x_list) + len(consts), nl=len(lhs_list), nw=len(ws), ne=len(extras),
        lhs_of_w=tuple(lhs_of_w), res_from_x=res_from_x, prologue=prologue, epilogue=epilogue, cw=cw)
    return pl.pallas_call(
        body, grid=(m // tm,), in_specs=in_specs,
        out_specs=pl.BlockSpec((tm, n), lambda i: (i, 0)),
        out_shape=jax.ShapeDtypeStruct((m, n), out_dtype),
        scratch_shapes=[pltpu.VMEM((tm, k), BF16)],
        compiler_params=_cparams(("parallel",)), name=name,
    )(*arrays)


def _heads_body(*refs, nw, ne, hpb, epilogue):
    x_ref = refs[0]
    w_refs = refs[1:1 + nw]
    e_refs = refs[1 + nw:1 + nw + ne]
    o_refs = refs[1 + nw + ne:]
    x = x_ref[...]
    es = [e[...] for e in e_refs]
    for hh in range(hpb):
        outs = epilogue([jnp.dot(x, w[hh], preferred_element_type=F32) for w in w_refs], es)
        for o_ref, o in zip(o_refs, outs):
            o_ref[hh] = o.astype(o_ref.dtype)


def _heads_mm(x, ws, es, out_dims, *, tm, hpb, epilogue, name):
    m, k = x.shape
    nh = ws[0].shape[0]
    in_specs = [pl.BlockSpec((tm, k), lambda i, j: (i, 0))]
    in_specs += [pl.BlockSpec((hpb, k, w.shape[2]), lambda i, j: (j, 0, 0)) for w in ws]
    in_specs += [pl.BlockSpec((tm, e.shape[1]), lambda i, j: (i, 0)) for e in es]
    return pl.pallas_call(
        functools.partial(_heads_body, nw=len(ws), ne=len(es), hpb=hpb, epilogue=epilogue),
        grid=(m // tm, nh // hpb), in_specs=in_specs,
        out_specs=[pl.BlockSpec((hpb, tm, n), lambda i, j: (j, i, 0)) for n in out_dims],
        out_shape=[jax.ShapeDtypeStruct((nh, m, n), BF16) for n in out_dims],
        compiler_params=_cparams(("parallel", "arbitrary")), name=name,
    )(x, *ws, *es)


def _conv_silu_chunk(ext, in_ref, st_ref, w_ref, b_ref, first, q):
    @pl.when(first)
    def _():
        ext[0:SUBLANES, :] = st_ref[...]

    ext[SUBLANES:SUBLANES + q, :] = in_ref[...]
    acc = b_ref[...]
    for k in range(CONV_W):
        off = SUBLANES - (CONV_W - 1) + k
        acc = acc + ext[off:off + q, :] * w_ref[k:k + 1, :]
    ext[0:SUBLANES, :] = ext[q:q + SUBLANES, :]
    return _silu(acc)


def _ssd_body(x_ref, b_ref, c_ref, stx_ref, stb_ref, stc_ref, wx_ref, wb_ref, wc_ref, bx_ref, bb_ref, bc_ref,
              dt_ref, dtT_ref, bias_ref, biasT_ref, a_ref, aT_ref, dsk_ref, s0_ref,
              y_ref, so_ref, s_scr, ext_x, ext_b, ext_c, *, q, gps):
    c = pl.program_id(2)
    first = c == 0

    @pl.when(first)
    def _():
        s_scr[...] = s0_ref[...]

    xs_all = _conv_silu_chunk(ext_x, x_ref, stx_ref, wx_ref, bx_ref, first, q)
    bm_all = _conv_silu_chunk(ext_b, b_ref, stb_ref, wb_ref, bb_ref, first, q).astype(BF16)
    cm_all = _conv_silu_chunk(ext_c, c_ref, stc_ref, wc_ref, bc_ref, first, q).astype(BF16)

    row = lax.broadcasted_iota(jnp.int32, (q, q), 0)
    col = lax.broadcasted_iota(jnp.int32, (q, q), 1)
    causal = row >= col
    tri = jnp.where(causal, 1.0, 0.0)
    tri_t = jnp.where(row <= col, 1.0, 0.0)
    left = lax.broadcasted_iota(jnp.int32, (q, LANES), 1) < SSM_HEAD_DIM
    gw = GROUP_WIDTH
    for gi in range(gps):
        dt = _softplus(dt_ref[gi] + bias_ref[gi])
        dta = dt * a_ref[gi]
        dta_t = _softplus(dtT_ref[gi] + biasT_ref[gi]) * aT_ref[gi]
        acum = jnp.dot(tri, dta, preferred_element_type=F32, precision=HIGHEST)
        acum_t = jnp.dot(dta_t, tri_t, preferred_element_type=F32, precision=HIGHEST)
        bm = bm_all[:, gi * SSM_STATE:(gi + 1) * SSM_STATE]
        cm = cm_all[:, gi * SSM_STATE:(gi + 1) * SSM_STATE]
        cb = lax.dot_general(cm, bm, (((1,), (1,)), ((), ())), preferred_element_type=F32)
        s_prev = s_scr[gi]
        y_off = jnp.dot(cm, s_prev.astype(BF16), preferred_element_type=F32)
        xw_parts, dec_parts = [], []
        for j in range(HEADS_PER_GROUP // 2):
            h0, h1 = 2 * j, 2 * j + 1
            sl = slice(j * LANES, (j + 1) * LANES)
            sg = slice(gi * gw + j * LANES, gi * gw + (j + 1) * LANES)
            col0, col1 = acum[:, h0:h0 + 1], acum[:, h1:h1 + 1]
            pa = jnp.where(left, col0, col1)
            dtp = jnp.where(left, dt[:, h0:h0 + 1], dt[:, h1:h1 + 1])
            x = xs_all[:, sg]
            xdt = x * dtp
            m0 = (jnp.exp(jnp.where(causal, col0 - acum_t[h0:h0 + 1, :], -jnp.inf)) * cb).astype(BF16)
            m1 = (jnp.exp(jnp.where(causal, col1 - acum_t[h1:h1 + 1, :], -jnp.inf)) * cb).astype(BF16)
            x_l = jnp.where(left, xdt, 0.0).astype(BF16)
            x_r = jnp.where(left, 0.0, xdt).astype(BF16)
            y_diag = (jnp.dot(m0, x_l, preferred_element_type=F32)
                      + jnp.dot(m1, x_r, preferred_element_type=F32))
            y = y_diag + y_off[:, sl] * jnp.exp(pa) + x * dsk_ref[:, sg]
            y_ref[:, sg] = y.astype(y_ref.dtype)
            last = pa[q - 1:q, :]
            xw_parts.append((xdt * jnp.exp(last - pa)).astype(BF16))
            dec_parts.append(jnp.exp(last))
        xw = jnp.concatenate(xw_parts, axis=1)
        dec = jnp.concatenate(dec_parts, axis=1)
        s_new = s_prev * dec + lax.dot_general(bm, xw, (((0,), (0,)), ((), ())), preferred_element_type=F32)
        s_scr[gi] = s_new
        so_ref[gi] = s_new


def _ssd_scan(xbc3, batch0, nb, L, st, w_conv, b_conv, dtg, dtg_t, bias, bias_t, a, a_t, dsk, s0, *, q, gps=2):
    g_, r_, n_ = SSM_GROUPS, HEADS_PER_GROUP, SSM_STATE
    gw = GROUP_WIDTH
    xw_, bw_ = gps * gw, gps * n_
    b_off = D_INNER // bw_
    c_off = b_off + g_ // gps
    grid = (nb, g_ // gps, L // q)
    col_x = lambda b, g, c: (0, g)
    col_b = lambda b, g, c: (0, b_off + g)
    col_c = lambda b, g, c: (0, c_off + g)
    return pl.pallas_call(
        functools.partial(_ssd_body, q=q, gps=gps), grid=grid,
        in_specs=[
            pl.BlockSpec((None, q, xw_), lambda b, g, c: (batch0 + b, c, g)),
            pl.BlockSpec((None, q, bw_), lambda b, g, c: (batch0 + b, c, b_off + g)),
            pl.BlockSpec((None, q, bw_), lambda b, g, c: (batch0 + b, c, c_off + g)),
            pl.BlockSpec((None, SUBLANES, xw_), lambda b, g, c: (b, 0, g)),
            pl.BlockSpec((None, SUBLANES, bw_), lambda b, g, c: (b, 0, b_off + g)),
            pl.BlockSpec((None, SUBLANES, bw_), lambda b, g, c: (b, 0, c_off + g)),
            pl.BlockSpec((CONV_W, xw_), col_x),
            pl.BlockSpec((CONV_W, bw_), col_b),
            pl.BlockSpec((CONV_W, bw_), col_c),
            pl.BlockSpec((1, xw_), col_x),
            pl.BlockSpec((1, bw_), col_b),
            pl.BlockSpec((1, bw_), col_c),
            pl.BlockSpec((None, gps, q, r_), lambda b, g, c: (b, g, c, 0)),
            pl.BlockSpec((None, gps, r_, q), lambda b, g, c: (b, g, 0, c)),
            pl.BlockSpec((gps, 1, r_), lambda b, g, c: (g, 0, 0)),
            pl.BlockSpec((gps, r_, 1), lambda b, g, c: (g, 0, 0)),
            pl.BlockSpec((gps, 1, r_), lambda b, g, c: (g, 0, 0)),
            pl.BlockSpec((gps, r_, 1), lambda b, g, c: (g, 0, 0)),
            pl.BlockSpec((1, xw_), col_x),
            pl.BlockSpec((None, gps, n_, gw), lambda b, g, c: (b, g, 0, 0)),
        ],
        out_specs=[
            pl.BlockSpec((None, q, xw_), lambda b, g, c: (b, c, g)),
            pl.BlockSpec((None, gps, n_, gw), lambda b, g, c: (b, g, 0, 0)),
        ],
        out_shape=[jax.ShapeDtypeStruct((nb, L, D_INNER), BF16),
                   jax.ShapeDtypeStruct((nb, g_, n_, gw), F32)],
        scratch_shapes=[pltpu.VMEM((gps, n_, gw), F32), pltpu.VMEM((q + SUBLANES, xw_), F32),
                        pltpu.VMEM((q + SUBLANES, bw_), F32), pltpu.VMEM((q + SUBLANES, bw_), F32)],
        compiler_params=_cparams(("parallel", "parallel", "arbitrary")), name="ssd_scan",
    )(xbc3, xbc3, xbc3, st, st, st, w_conv, w_conv, w_conv, b_conv, b_conv, b_conv,
      dtg, dtg_t, bias, bias_t, a, a_t, dsk, s0)


def _gather_rows(tok_ref, base, src_hbm, dst, sem, n):
    def body(r, carry):
        t = tok_ref[base + r]
        pltpu.make_async_copy(src_hbm.at[pl.ds(t, 1)], dst.at[pl.ds(r, 1)], sem).start()
        return carry

    lax.fori_loop(0, n, body, 0)


def _gather_rows_unrolled(tok_ref, base, src_hbm, dst, sem, n):
    for r in range(n):
        t = tok_ref[base + r]
        pltpu.make_async_copy(src_hbm.at[pl.ds(t, 1)], dst.at[pl.ds(r, 1)], sem).start(priority=r % 2)


def _moe_body(blk_e_ref, nxt_e_ref, tok_ref, nused_ref, xt_hbm, wg_hbm, wu_hbm, wd_hbm, yb_ref,
              xbuf, xsem, wg32, wu32, wd32, wsem, wg_bf, wu_bf, wd_bf, xb, *, bm, layer):
    b = pl.program_id(0)
    slot = b % 2
    nused = nused_ref[0]
    e_cur = blk_e_ref[b]

    def weight_copies(e):
        return (pltpu.make_async_copy(wg_hbm.at[layer, e], wg32, wsem.at[0]),
                pltpu.make_async_copy(wu_hbm.at[layer, e], wu32, wsem.at[1]),
                pltpu.make_async_copy(wd_hbm.at[layer, e], wd32, wsem.at[2]))

    def gather_unrolled(base, slot_):
        for r in range(bm):
            t = tok_ref[base + r]
            pltpu.make_async_copy(xt_hbm.at[pl.ds(t, 1)], xbuf.at[slot_, pl.ds(r, 1)], xsem.at[slot_]).start(
                priority=r % 2)

    @pl.when(jnp.logical_and(b == 0, nused > 0))
    def _():
        for cp in weight_copies(e_cur):
            cp.start(priority=1)
        _gather_rows(tok_ref, 0, xt_hbm, xbuf.at[0], xsem.at[0], bm)

    prev_e = blk_e_ref[jnp.maximum(b - 1, 0)]
    changed = jnp.logical_or(b == 0, e_cur != prev_e)

    @pl.when(jnp.logical_and(changed, b < nused))
    def _():
        for cp in weight_copies(e_cur):
            cp.wait()
        wg_bf[...] = wg32[...].astype(BF16)
        wu_bf[...] = wu32[...].astype(BF16)
        wd_bf[...] = wd32[...].astype(BF16)
        nxt = nxt_e_ref[b]

        @pl.when(nxt >= 0)
        def _():
            for cp in weight_copies(nxt):
                cp.start(priority=1)

    def compute(prefetch):
        pltpu.make_async_copy(xbuf.at[slot], xbuf.at[slot], xsem.at[slot]).wait()
        w = xbuf[slot]
        half = w.shape[1]
        xb[:, :half] = lax.bitcast_convert_type(w << 16, F32).astype(BF16)
        xb[:, half:] = lax.bitcast_convert_type(w & jnp.uint32(0xFFFF0000), F32).astype(BF16)
        if prefetch:
            gather_unrolled((b + 1) * bm, 1 - slot)
        x = xb[...]
        g = jnp.dot(x, wg_bf[...], preferred_element_type=F32)
        u = jnp.dot(x, wu_bf[...], preferred_element_type=F32)
        hid = (_silu(g) * u).astype(BF16)
        yb_ref[...] = jnp.dot(hid, wd_bf[...], preferred_element_type=F32)

    @pl.when(b + 1 < nused)
    def _():
        compute(True)

    @pl.when(b + 1 == nused)
    def _():
        compute(False)

    @pl.when(b >= nused)
    def _():
        yb_ref[...] = jnp.zeros(yb_ref.shape, yb_ref.dtype)


def _moe_experts(xt, blk_e, nxt_e, row_token, nused, w_gate, w_up, w_down, *, bm, layer):
    nblk = blk_e.shape[0]
    d = w_gate.shape[2]
    de = w_gate.shape[3]
    assert xt.shape[1] * 2 == d and xt.dtype == jnp.uint32
    anyspec = pl.BlockSpec(memory_space=pl.ANY)
    gs = pltpu.PrefetchScalarGridSpec(
        num_scalar_prefetch=4, grid=(nblk,),
        in_specs=[anyspec, anyspec, anyspec, anyspec],
        out_specs=pl.BlockSpec((bm, d), lambda b, be, nx, tok, nu: (b, 0)),
        scratch_shapes=[
            pltpu.VMEM((2, bm, d // 2), jnp.uint32),
            pltpu.SemaphoreType.DMA((2,)),
            pltpu.VMEM((d, de), F32),
            pltpu.VMEM((d, de), F32),
            pltpu.VMEM((de, d), F32),
            pltpu.SemaphoreType.DMA((3,)),
            pltpu.VMEM((d, de), BF16),
            pltpu.VMEM((d, de), BF16),
            pltpu.VMEM((de, d), BF16),
            pltpu.VMEM((bm, d), BF16),
        ],
    )
    return pl.pallas_call(
        functools.partial(_moe_body, bm=bm, layer=layer), grid_spec=gs,
        out_shape=jax.ShapeDtypeStruct((nblk * bm, d), F32),
        compiler_params=_cparams(("arbitrary",)), name="moe_experts",
    )(blk_e, nxt_e, row_token, nused, xt, w_gate, w_up, w_down)


def _combine_body(dest_ref, h_ref, g_ref, yb_hbm, o_ref, buf, sem, *, tc, ntile):
    i = pl.program_id(0)
    slot = i % 2
    n = TOP_K * tc

    @pl.when(i == 0)
    def _():
        _gather_rows(dest_ref, 0, yb_hbm, buf.at[0], sem.at[0], n)

    @pl.when(i + 1 < ntile)
    def _():
        _gather_rows_unrolled(dest_ref, (i + 1) * n, yb_hbm, buf.at[1 - slot], sem.at[1 - slot], n)

    pltpu.make_async_copy(buf.at[slot], buf.at[slot], sem.at[slot]).wait()
    g = g_ref[...]
    o_ref[...] = (h_ref[...] + g[:, 0:1] * buf[slot, 0:tc, :] + g[:, 1:2] * buf[slot, tc:2 * tc, :])


def _moe_combine(h, yb, dest_tiles, gates, *, tc):
    t, d = h.shape
    ntile = t // tc
    gs = pltpu.PrefetchScalarGridSpec(
        num_scalar_prefetch=1, grid=(ntile,),
        in_specs=[
            pl.BlockSpec((tc, d), lambda i, dst: (i, 0)),
            pl.BlockSpec((tc, LANES), lambda i, dst: (i, 0)),
            pl.BlockSpec(memory_space=pl.ANY),
        ],
        out_specs=pl.BlockSpec((tc, d), lambda i, dst: (i, 0)),
        scratch_shapes=[pltpu.VMEM((2, TOP_K * tc, d), F32), pltpu.SemaphoreType.DMA((2,))],
    )
    return pl.pallas_call(
        functools.partial(_combine_body, tc=tc, ntile=ntile), grid_spec=gs,
        out_shape=jax.ShapeDtypeStruct((t, d), F32),
        compiler_params=_cparams(("arbitrary",)), name="moe_combine",
    )(dest_tiles, h, gates, yb)


def _moe_plan(idx, bm, tc):
    t = idx.shape[0]
    a = t * TOP_K
    e_ = N_EXPERTS
    flat_e = idx.reshape(a)
    onehot = (flat_e[:, None] == jnp.arange(e_, dtype=jnp.int32)[None, :]).astype(jnp.int32)
    csum = jnp.cumsum(onehot, axis=0)
    rank = jnp.sum(onehot * csum, axis=1) - 1
    counts = csum[-1]
    padded = (counts + bm - 1) // bm * bm
    pad_end = jnp.cumsum(padded)
    pad_start = pad_end - padded
    dest = jnp.sum(onehot * pad_start[None, :], axis=1) + rank
    nblk = -(-a // bm) + e_
    rows = nblk * bm
    row_token = jnp.zeros((rows,), jnp.int32).at[dest].set(jnp.arange(a, dtype=jnp.int32) // TOP_K)
    blk_row0 = jnp.arange(nblk, dtype=jnp.int32) * bm
    blk_e = jnp.minimum(jnp.sum((pad_end[None, :] <= blk_row0[:, None]).astype(jnp.int32), axis=1), e_ - 1)
    ids = jnp.arange(e_, dtype=jnp.int32)
    later_used = jnp.logical_and(ids[None, :] > ids[:, None], (counts > 0)[None, :])
    nxt_of_e = jnp.min(jnp.where(later_used, ids[None, :], e_), axis=1)
    nxt_of_e = jnp.where(nxt_of_e == e_, -1, nxt_of_e)
    nxt_e = jnp.sum(jnp.where(blk_e[:, None] == ids[None, :], nxt_of_e[None, :], 0), axis=1).astype(jnp.int32)
    nused = (pad_end[-1] // bm).astype(jnp.int32).reshape(1)
    dest_tiles = dest.reshape(t // tc, tc, TOP_K).transpose(0, 2, 1).reshape(a).astype(jnp.int32)
    return blk_e.astype(jnp.int32), nxt_e, row_token, nused, dest_tiles


def _flash_body(qi_ref, kj_ref, q_ref, k_ref, v_ref, o_ref, m_scr, l_scr, acc_scr, *, hb):
    p_id = pl.program_id(1)
    qi = qi_ref[p_id]
    kj = kj_ref[p_id]
    tk = k_ref.shape[1]

    @pl.when(kj == 0)
    def _():
        m_scr[...] = jnp.full(m_scr.shape, -jnp.inf, F32)
        l_scr[...] = jnp.zeros(l_scr.shape, F32)
        acc_scr[...] = jnp.zeros(acc_scr.shape, F32)

    def step(diag):
        for hh in range(hb):
            s = lax.dot_general(q_ref[hh], k_ref[hh], (((1,), (1,)), ((), ())), preferred_element_type=F32)
            if diag:
                r = lax.broadcasted_iota(jnp.int32, s.shape, 0)
                c = lax.broadcasted_iota(jnp.int32, s.shape, 1)
                s = jnp.where((c // CHUNK) <= (r // CHUNK), s, -jnp.inf)
            m_prev = m_scr[hh]
            m_next = jnp.maximum(m_prev, jnp.max(s, axis=1, keepdims=True))
            p = jnp.exp2(s - jnp.concatenate([m_next] * (tk // LANES), axis=1))
            alpha = jnp.exp2(m_prev - m_next)
            l_scr[hh] = alpha * l_scr[hh] + jnp.sum(p, axis=1, keepdims=True)
            acc_scr[hh] = alpha * acc_scr[hh] + jnp.dot(p.astype(BF16), v_ref[hh], preferred_element_type=F32)
            m_scr[hh] = m_next

    @pl.when(kj < qi)
    def _():
        step(False)

    @pl.when(kj == qi)
    def _():
        step(True)
        for hh in range(hb):
            o_ref[:, hh * V_DIM:(hh + 1) * V_DIM] = (acc_scr[hh] / l_scr[hh]).astype(o_ref.dtype)


def _flash_prompt(q, k, v, lp, *, tq, hb):
    assert V_DIM == LANES
    nh = q.shape[0]
    nq = lp // tq
    pairs = [(a, b) for a in range(nq) for b in range(a + 1)]
    qi_tab = jnp.asarray(np.array([p[0] for p in pairs], np.int32))
    kj_tab = jnp.asarray(np.array([p[1] for p in pairs], np.int32))
    gs = pltpu.PrefetchScalarGridSpec(
        num_scalar_prefetch=2, grid=(nh // hb, len(pairs)),
        in_specs=[
            pl.BlockSpec((hb, tq, QK_DIM), lambda h, p, qi, kj: (h, qi[p], 0)),
            pl.BlockSpec((hb, tq, QK_DIM), lambda h, p, qi, kj: (h, kj[p], 0)),
            pl.BlockSpec((hb, tq, V_DIM), lambda h, p, qi, kj: (h, kj[p], 0)),
        ],
        out_specs=pl.BlockSpec((tq, hb * V_DIM), lambda h, p, qi, kj: (qi[p], h)),
        scratch_shapes=[pltpu.VMEM((hb, tq, LANES), F32), pltpu.VMEM((hb, tq, LANES), F32),
                        pltpu.VMEM((hb, tq, V_DIM), F32)],
    )
    return pl.pallas_call(
        functools.partial(_flash_body, hb=hb), grid_spec=gs,
        out_shape=jax.ShapeDtypeStruct((lp, nh * V_DIM), BF16),
        compiler_params=_cparams(("parallel", "arbitrary")), name="flash_prompt",
    )(qi_tab, kj_tab, q, k, v)


def _attn_sample_body(q_ref, k_ref, o_ref, *, ls, past, s_valid):
    nh = q_ref.shape[0]
    q = q_ref[...].reshape(nh * ls, q_ref.shape[2])
    k = k_ref[...]
    s = lax.dot_general(q, k, (((1,), (1,)), ((), ())), preferred_element_type=F32)
    r = lax.broadcasted_iota(jnp.int32, s.shape, 0)
    c = lax.broadcasted_iota(jnp.int32, s.shape, 1)
    q_pos = past + r % ls
    s = jnp.where((c // CHUNK) <= (q_pos // CHUNK), s, -jnp.inf)
    s = jnp.where(c < s_valid, s, -jnp.inf)
    m = jnp.max(s, axis=-1, keepdims=True)
    p = jnp.exp2(s - m)
    l = jnp.sum(p, axis=-1, keepdims=True)
    o = jnp.dot(p.astype(BF16), k[:, :KV_RANK], preferred_element_type=F32) / l
    o_ref[...] = o.reshape(nh, ls, KV_RANK).astype(o_ref.dtype)


def _attn_sample(q_abs, kc, *, ls, past, s_valid):
    nh, ts, dk = q_abs.shape
    nb, s_pad, _ = kc.shape
    return pl.pallas_call(
        functools.partial(_attn_sample_body, ls=ls, past=past, s_valid=s_valid), grid=(nb,),
        in_specs=[
            pl.BlockSpec((nh, ls, dk), lambda b: (0, b, 0)),
            pl.BlockSpec((None, s_pad, dk), lambda b: (b, 0, 0)),
        ],
        out_specs=pl.BlockSpec((nh, ls, KV_RANK), lambda b: (0, b, 0)),
        out_shape=jax.ShapeDtypeStruct((nh, ts, KV_RANK), BF16),
        compiler_params=_cparams(("parallel",)), name="attn_sample",
    )(q_abs, kc)


def _norm_body(x_ref, g_ref, o_ref):
    o_ref[...] = _rms_rows(x_ref[...]) * g_ref[...]


def _final_norm(h, g, row0, nrows, *, tm):
    d = h.shape[1]
    blk0 = row0 // tm
    return pl.pallas_call(
        _norm_body, grid=(nrows // tm,),
        in_specs=[pl.BlockSpec((tm, d), lambda i: (blk0 + i, 0)), pl.BlockSpec((1, d), lambda i: (0, 0))],
        out_specs=pl.BlockSpec((tm, d), lambda i: (i, 0)),
        out_shape=jax.ShapeDtypeStruct((nrows, d), F32),
        compiler_params=_cparams(("parallel",)), name="final_norm",
    )(h, g)


def _rope_tables(pos):
    half = QK_ROPE // 2
    inv = ROPE_THETA ** (-np.arange(half, dtype=np.float64) / half)
    ang = np.asarray(pos, np.float64)[:, None] * inv[None, :]
    cos = np.concatenate([np.cos(ang), np.cos(ang)], axis=1)
    sin = np.concatenate([-np.sin(ang), np.sin(ang)], axis=1)
    return cos.astype(np.float32), sin.astype(np.float32)


def _swap_rope_halves(w):
    half = QK_ROPE // 2
    return jnp.concatenate([w[..., half:], w[..., :half]], axis=-1)


def _moe_layer(h, g_ffn, w_rg, b_rg, w_re, b_re, w_gate, w_up, w_down, *, tm, layer):
    t, d = h.shape
    npad = LANES - MOE_GROUPS - N_EXPERTS
    w_r = jnp.concatenate([w_rg, w_re, jnp.zeros((d, npad), F32)], axis=1)
    b_r = jnp.concatenate([b_rg, b_re, jnp.zeros((npad,), F32)]).reshape(1, LANES)
    grid = (t // tm, 1)
    x_args = [(h, pl.BlockSpec((tm, d), lambda i, j: (i, 0))),
              (g_ffn.reshape(1, d), pl.BlockSpec((1, d), lambda i, j: (0, 0)))]
    w_args = [(w_r, pl.BlockSpec((d, LANES), lambda i, j: (0, 0)))]
    e_args = [(b_r, pl.BlockSpec((1, LANES), lambda i, j: (0, 0)))]
    tile = pl.BlockSpec((tm, LANES), lambda i, j: (i, 0))
    out_defs = [(jax.ShapeDtypeStruct((t, LANES), jnp.int32), tile),
                (jax.ShapeDtypeStruct((t, LANES), F32), tile)]
    xn_out = (jax.ShapeDtypeStruct((t, d // 2), jnp.uint32), pl.BlockSpec((tm, d // 2), lambda i, j: (i, 0)))
    idx_t, gates_t, xt = _fused_mm(x_args, w_args, e_args, out_defs, grid=grid, prologue=_pro_rms,
                                   epilogue=_epi_route, xn_shape=(tm, d), xn_dtype=F32, xn_out=xn_out,
                                   xn_emit=_pack_bf16_pairs, precision=HIGHEST, name="router")
    tc = min(tm, 128)
    blk_e, nxt_e, row_token, nused, dest_tiles = _moe_plan(idx_t[:, :TOP_K], MOE_BM, tc)
    yb = _moe_experts(xt, blk_e, nxt_e, row_token, nused, w_gate, w_up, w_down, bm=MOE_BM, layer=layer)
    return _moe_combine(h, yb, dest_tiles, gates_t, tc=tc)


def _ple_layer(h, p_rows, g_ple, w_gate, w_proj, *, tm, tn):
    t, d = h.shape
    return _mm_resident([h], [g_ple.reshape(1, d)], [p_rows.astype(BF16)], [w_gate.astype(BF16), w_proj.astype(BF16)],
                        [0, 1], [], prologue=_pro_rms, epilogue=_epi_ple, out_dtype=F32, tm=tm, name="ple_gate",
                        res_from_x=True)


def kernel(x_prompt, x_sample, state_conv, state_ssm, cache_kv_latent, cache_k_rope, p_prompt, p_sample,
           g_mix, w_ssm_in, w_conv, b_conv, dt_bias, a_log, d_skip, g_ssm_norm, w_ssm_out,
           g_kv_in, w_dkv, g_kv, w_uk, w_uv, w_dq, g_q, w_uq, w_o,
           g_ffn, w_router_grp, b_router_grp, w_router_exp, b_router_exp, w_exp_gate, w_exp_up, w_exp_down,
           g_ple, w_ple_gate, w_ple_proj, g_final):
    bp, lp_each, d = x_prompt.shape
    bs, ls, _ = x_sample.shape
    past = cache_kv_latent.shape[1]
    assert bp == 1 and w_ssm_in.shape[0] == 1 and w_dq.shape[0] == 1 and d == D_MODEL
    assert ls >= CONV_W - 1 and ls % SUBLANES == 0 and past % CHUNK == 0
    lp = bp * lp_each
    ts = bs * ls
    t = lp + ts
    tm = min(512, math.gcd(lp, ts))
    tn = 512
    g_, r_, n_, hd = SSM_GROUPS, HEADS_PER_GROUP, SSM_STATE, SSM_HEAD_DIM

    h = jnp.concatenate([x_prompt.reshape(lp, d), x_sample.reshape(ts, d)], axis=0)

    w_in = w_ssm_in[0]
    w_z = w_in[:, :D_INNER].astype(BF16)
    xw = SSM_IN_XW
    w_x = jnp.pad(w_in[:, D_INNER:], ((0, 0), (0, xw - CONV_DIM - SSM_HEADS))).astype(BF16)
    g0 = g_mix[0].reshape(1, d)
    tm_in = max(c for c in range(16, 1153, 16) if t % c == 0)
    mm_in = functools.partial(_mm_rows, [h], [g0], prologue=_pro_rms, epilogue=_epi_id, tm=tm_in)
    z = mm_in(w_z, [], out_dtype=BF16, tn=1024, name="ssm_in_z")[0]
    xbc = mm_in(w_x, [], out_dtype=F32, tn=xw // 5, name="ssm_in_xbc")[0]
    dt_raw = xbc[:, CONV_DIM:CONV_DIM + SSM_HEADS]

    wc = w_conv[0]
    bc = b_conv[0].reshape(1, CONV_DIM)
    st_p = jnp.zeros((1, SUBLANES, CONV_DIM), F32)
    st_s = jnp.pad(state_conv[0].astype(F32), ((0, 0), (SUBLANES - (CONV_W - 1), 0), (0, 0)))
    conv_p = xbc[lp - (CONV_W - 1):lp, :CONV_DIM].reshape(1, 1, CONV_W - 1, CONV_DIM)
    conv_s = xbc[lp:, :CONV_DIM].reshape(bs, ls, CONV_DIM)[:, ls - (CONV_W - 1):].reshape(
        1, bs, CONV_W - 1, CONV_DIM)

    bias = dt_bias[0].astype(F32).reshape(g_, 1, r_)
    a_neg = (-jnp.exp(a_log[0].astype(F32))).reshape(g_, 1, r_)
    dsk = jnp.repeat(d_skip[0].astype(F32), hd).reshape(1, D_INNER)

    def dt_views(rows, nb, L):
        v = rows.reshape(nb, L, g_, r_).transpose(0, 2, 1, 3)
        return v, v.transpose(0, 1, 3, 2)

    def scan(xbc3, batch0, nb, L, st, rows, s0, q):
        dtg, dtg_t = dt_views(rows, nb, L)
        return _ssd_scan(xbc3, batch0, nb, L, st, wc, bc, dtg, dtg_t, bias, bias.transpose(0, 2, 1),
                         a_neg, a_neg.transpose(0, 2, 1), dsk, s0, q=q)

    def state_in(s):
        nb = s.shape[0]
        return s.astype(F32).reshape(nb, g_, r_ * hd, n_).transpose(0, 1, 3, 2)

    def state_out(s):
        nb = s.shape[0]
        return s.transpose(0, 1, 3, 2).reshape(1, nb, SSM_HEADS, hd, n_)

    y_p, s_p = scan(xbc.reshape(1, t, xw), 0, 1, lp, st_p, dt_raw[:lp],
                    jnp.zeros((1, g_, n_, r_ * hd), F32), min(SSD_Q, lp))
    y_s, s_s = scan(xbc.reshape(t // ls, ls, xw), lp // ls, bs, ls, st_s, dt_raw[lp:],
                    state_in(state_ssm[0]), ls)
    ssm_p, ssm_s = state_out(s_p), state_out(s_s)

    y_all = jnp.concatenate([y_p.reshape(lp, D_INNER), y_s.reshape(ts, D_INNER)], axis=0)
    h = _mm_resident([y_all, z], [g_ssm_norm[0].reshape(1, D_INNER)], [], [w_ssm_out[0].astype(BF16)], [0], [h],
                     prologue=_pro_gated, epilogue=_epi_res, out_dtype=F32, tm=tm // 2, name="ssm_out")

    def ffn_and_ple(h, i):
        h = _moe_layer(h, g_ffn[i], w_router_grp[i], b_router_grp[i], w_router_exp[i], b_router_exp[i],
                       w_exp_gate, w_exp_up, w_exp_down, tm=tm, layer=i)
        p_rows = jnp.concatenate([p_prompt[i].reshape(lp, -1), p_sample[i].reshape(ts, -1)], axis=0)
        return _ple_layer(h, p_rows, g_ple[i], w_ple_gate[i], w_ple_proj[i], tm=tm, tn=tn)

    h = ffn_and_ple(h, 0)

    pos = np.concatenate([np.arange(lp), np.tile(past + np.arange(ls), bs)])
    cos64, sin64 = _rope_tables(pos)
    gk = g_kv_in.reshape(1, d)
    ckv = _mm_rows([h], [gk], w_dkv[:, :KV_RANK].astype(BF16), [], prologue=_pro_rms, epilogue=_epi_rms_out,
                   out_dtype=F32, tm=tm, tn=KV_RANK, name="kv_latent", e_consts=[g_kv.reshape(1, KV_RANK)])[0]
    w_r = w_dkv[:, KV_RANK:]
    row64 = pl.BlockSpec((tm, QK_ROPE), lambda i, j: (i, 0))
    w64 = pl.BlockSpec((d, QK_ROPE), lambda i, j: (0, 0))
    kr = _fused_mm(
        [(h, pl.BlockSpec((tm, d), lambda i, j: (i, 0))), (gk, pl.BlockSpec((1, d), lambda i, j: (0, 0)))],
        [(w_r.astype(BF16), w64), (_swap_rope_halves(w_r).astype(BF16), w64)],
        [(jnp.asarray(cos64), row64), (jnp.asarray(sin64), row64)],
        [(jax.ShapeDtypeStruct((t, QK_ROPE), F32), row64)],
        grid=(t // tm, 1), prologue=_pro_rms, epilogue=_epi_rot, xn_shape=(tm, d), name="k_rope")[0]

    ql = _mm_rows([h], [g_mix[1].reshape(1, d)], w_dq[0].astype(BF16), [], prologue=_pro_rms,
                  epilogue=_epi_rms_out, out_dtype=BF16, tm=tm, tn=Q_RANK, name="q_latent",
                  e_consts=[g_q[0].reshape(1, Q_RANK)])[0]
    nh = MLA_HEADS
    w_q = w_uq[0].reshape(Q_RANK, nh, QK_DIM).transpose(1, 0, 2)
    w_q_rot = jnp.concatenate([jnp.zeros((nh, Q_RANK, QK_NOPE), F32), _swap_rope_halves(w_q[..., QK_NOPE:])], -1)
    cos_q = np.concatenate([np.ones((t, QK_NOPE), np.float32), cos64], axis=1) * np.float32(QK_PRESCALE)
    sin_q = np.concatenate([np.zeros((t, QK_NOPE), np.float32), sin64], axis=1) * np.float32(QK_PRESCALE)
    q_cat = _heads_mm(ql, [w_q.astype(BF16), w_q_rot.astype(BF16)], [jnp.asarray(cos_q), jnp.asarray(sin_q)],
                      [QK_DIM], tm=tm, hpb=8, epilogue=_epi_rot, name="q_heads")[0]

    lat_pad = LAT_PAD - KV_RANK - QK_ROPE
    ckr_p = jnp.concatenate([ckv[:lp], kr[:lp], jnp.zeros((lp, lat_pad), F32)], axis=1).astype(BF16)
    w_uk_h = w_uk.transpose(1, 0, 2)
    w_uv_h = w_uv.transpose(1, 0, 2)
    eye_r = jnp.broadcast_to(jnp.eye(QK_ROPE, dtype=F32), (nh, QK_ROPE, QK_ROPE))
    w_k_ext = jnp.concatenate([
        jnp.concatenate([w_uk_h, jnp.zeros((nh, KV_RANK, QK_ROPE), F32)], axis=2),
        jnp.concatenate([jnp.zeros((nh, QK_ROPE, QK_NOPE), F32), eye_r], axis=2),
        jnp.zeros((nh, lat_pad, QK_DIM), F32)], axis=1).astype(BF16)
    w_v_ext = jnp.concatenate([w_uv_h, jnp.zeros((nh, LAT_PAD - KV_RANK, V_DIM), F32)], axis=1).astype(BF16)
    k_cat, v_h = _heads_mm(ckr_p, [w_k_ext, w_v_ext], [], [QK_DIM, V_DIM], tm=tm, hpb=8,
                           epilogue=_epi_id, name="kv_heads")
    o_p = _flash_prompt(q_cat, k_cat, v_h, lp, tq=min(512, lp), hb=4)

    w_abs = jnp.concatenate([
        jnp.concatenate([w_uk_h.transpose(0, 2, 1), jnp.zeros((nh, QK_NOPE, LAT_PAD - KV_RANK), F32)], axis=2),
        jnp.concatenate([jnp.zeros((nh, QK_ROPE, KV_RANK), F32), eye_r,
                         jnp.zeros((nh, QK_ROPE, lat_pad), F32)], axis=2)], axis=1).astype(BF16)
    blk_s = lp // ts if lp % ts == 0 else None
    assert blk_s is not None
    q_abs = _fused_mm(
        [(q_cat, pl.BlockSpec((None, ts, QK_DIM), lambda i, j: (j, blk_s + i, 0)))],
        [(w_abs, pl.BlockSpec((None, QK_DIM, LAT_PAD), lambda i, j: (j, 0, 0)))],
        [],
        [(jax.ShapeDtypeStruct((nh, ts, LAT_PAD), BF16), pl.BlockSpec((None, ts, LAT_PAD), lambda i, j: (j, i, 0)))],
        grid=(1, nh), prologue=_pro_cast, epilogue=_epi_id, x_per_j=True, name="q_absorb")[0]
    s_valid = past + ls
    s_pad = -(-s_valid // LANES) * LANES
    kc_new = jnp.concatenate([ckv[lp:], kr[lp:]], axis=1).reshape(bs, ls, KV_RANK + QK_ROPE)
    kc = jnp.concatenate([jnp.concatenate([cache_kv_latent, cache_k_rope], axis=2), kc_new], axis=1)
    kc = jnp.pad(kc, ((0, 0), (0, s_pad - s_valid), (0, lat_pad))).astype(BF16)
    o_lat = _attn_sample(q_abs, kc, ls=ls, past=past, s_valid=s_valid)
    o_s = _fused_mm(
        [(o_lat, pl.BlockSpec((None, ts, KV_RANK), lambda i, j: (j, i, 0)))],
        [(w_uv_h.astype(BF16), pl.BlockSpec((None, KV_RANK, V_DIM), lambda i, j: (j, 0, 0)))],
        [],
        [(jax.ShapeDtypeStruct((ts, nh * V_DIM), BF16), pl.BlockSpec((ts, V_DIM), lambda i, j: (i, j)))],
        grid=(1, nh), prologue=_pro_cast, epilogue=_epi_id, x_per_j=True, name="v_absorb")[0]

    o_all = jnp.concatenate([o_p, o_s], axis=0)
    h = _mm_resident([o_all], [], [], [w_o[0].astype(BF16)], [0], [h], prologue=_pro_cast, epilogue=_epi_res,
                     out_dtype=F32, tm=tm, name="attn_out")
    h = ffn_and_ple(h, 1)

    gf = g_final.reshape(1, d)
    y_p_out = _final_norm(h, gf, 0, lp, tm=tm).reshape(bp, lp_each, d)
    y_s_out = _final_norm(h, gf, lp, ts, tm=tm).reshape(bs, ls, d)
    return (y_p_out, y_s_out, conv_p, ssm_p, ckv[:lp].reshape(bp, lp_each, KV_RANK),
            kr[:lp].reshape(bp, lp_each, QK_ROPE), conv_s, ssm_s, ckv[lp:].reshape(bs, ls, KV_RANK),
            kr[lp:].reshape(bs, ls, QK_ROPE))
```

```python
import functools
import math

import numpy as np
import jax
import jax.numpy as jnp
from jax import lax
from jax.experimental import pallas as pl
from jax.experimental.pallas import tpu as pltpu

F32 = jnp.float32
BF16 = jnp.bfloat16
HIGHEST = lax.Precision.HIGHEST

EPS = 1e-6
CHUNK = 64
D_MODEL = 2048
D_INNER = 2 * D_MODEL
SSM_HEAD_DIM = 64
SSM_HEADS = D_INNER // SSM_HEAD_DIM
SSM_STATE = 128
SSM_GROUPS = 8
HEADS_PER_GROUP = SSM_HEADS // SSM_GROUPS
GROUP_WIDTH = D_INNER // SSM_GROUPS
CONV_W = 4
CONV_DIM = D_INNER + 2 * SSM_GROUPS * SSM_STATE
MLA_HEADS = 16
Q_RANK = 512
KV_RANK = 512
QK_NOPE = 128
QK_ROPE = 64
QK_DIM = QK_NOPE + QK_ROPE
V_DIM = 128
ROPE_THETA = 10000.0
ATTN_SCALE = QK_DIM ** -0.5
QK_PRESCALE = ATTN_SCALE * math.log2(math.e)
MOE_GROUPS = 4
EXPERTS_PER_GROUP = 8
N_EXPERTS = MOE_GROUPS * EXPERTS_PER_GROUP
TOP_K = 2
D_EXPERT = 512

LANES = 128
SUBLANES = 8
VMEM_LIMIT = 56 * 1024 * 1024
SSD_Q = 128
MOE_BM = 256
LAT_PAD = 640
SSM_IN_XW = 6400


def _cparams(sem):
    return pltpu.CompilerParams(dimension_semantics=sem, vmem_limit_bytes=VMEM_LIMIT)


def _sigmoid(v):
    return 1.0 / (1.0 + jnp.exp(-v))


def _silu(v):
    return v * _sigmoid(v)


def _softplus(v):
    return jnp.maximum(v, 0.0) + jnp.log1p(jnp.exp(-jnp.abs(v)))


def _rms_rows(x):
    return x * lax.rsqrt(jnp.mean(x * x, axis=-1, keepdims=True) + EPS)


def _mm_body(*refs, nx, nw, ne, no, prologue, epilogue, x_per_j, emit_xn, xn_emit, precision):
    x_refs = refs[:nx]
    w_refs = refs[nx:nx + nw]
    e_refs = refs[nx + nw:nx + nw + ne]
    o_refs = refs[nx + nw + ne:nx + nw + ne + no]
    rest = refs[nx + nw + ne + no:]
    j = pl.program_id(1)
    if x_per_j:
        xn = prologue(*[r[...] for r in x_refs])
    else:
        xn_ref = rest[-1]

        @pl.when(j == 0)
        def _():
            v = prologue(*[r[...] for r in x_refs])
            xn_ref[...] = v.astype(xn_ref.dtype)
            if emit_xn:
                rest[0][...] = xn_emit(v).astype(rest[0].dtype)

        xn = xn_ref[...]
    accs = [jnp.dot(xn, w[...], preferred_element_type=F32, precision=precision) for w in w_refs]
    outs = epilogue(accs, [e[...] for e in e_refs])
    for o_ref, o in zip(o_refs, outs):
        o_ref[...] = o.astype(o_ref.dtype)


def _fused_mm(x_args, w_args, e_args, out_defs, *, grid, prologue, epilogue, xn_shape=None,
              xn_dtype=BF16, x_per_j=False, xn_out=None, xn_emit=None, precision=None, name=None):
    arrays = [a for a, _ in x_args + w_args + e_args]
    in_specs = [s for _, s in x_args + w_args + e_args]
    out_shape = [d for d, _ in out_defs]
    out_specs = [s for _, s in out_defs]
    emit_xn = xn_out is not None
    if emit_xn:
        out_shape.append(xn_out[0])
        out_specs.append(xn_out[1])
    scratch = [] if x_per_j else [pltpu.VMEM(xn_shape, xn_dtype)]
    body = functools.partial(
        _mm_body, nx=len(x_args), nw=len(w_args), ne=len(e_args), no=len(out_defs),
        prologue=prologue, epilogue=epilogue, x_per_j=x_per_j, emit_xn=emit_xn,
        xn_emit=xn_emit or (lambda v: v), precision=precision)
    return pl.pallas_call(
        body, grid=grid, in_specs=in_specs, out_specs=out_specs, out_shape=out_shape,
        scratch_shapes=scratch, compiler_params=_cparams(("parallel", "arbitrary")), name=name,
    )(*arrays)


def _pro_rms(x, g):
    return _rms_rows(x.astype(F32)) * g


def _pro_cast(x):
    return x


def _pack_bf16_pairs(v):
    k = v.shape[1] // 2
    lo = lax.bitcast_convert_type(v[:, :k].astype(BF16).astype(F32), jnp.uint32) >> 16
    hi = lax.bitcast_convert_type(v[:, k:].astype(BF16).astype(F32), jnp.uint32) & jnp.uint32(0xFFFF0000)
    return hi | lo


def _pro_gated(y, z, g):
    v = y.astype(F32) * _silu(z.astype(F32))
    parts = [_rms_rows(v[:, k * GROUP_WIDTH:(k + 1) * GROUP_WIDTH]) for k in range(SSM_GROUPS)]
    return jnp.concatenate(parts, axis=-1) * g


def _epi_id(accs, es):
    return accs


def _epi_kv_ones(accs, es):
    return [accs[0], jnp.concatenate([accs[1], jnp.ones_like(accs[1])], axis=1)]


def _epi_res(accs, es):
    return [es[0] + accs[0]]


def _epi_ple(accs, es):
    return [es[0] + _sigmoid(accs[0]) * accs[1]]


def _epi_rms_out(accs, es):
    return [_rms_rows(accs[0]) * es[0]]


def _epi_rot(accs, es):
    cos, sin = es
    return [accs[0] * cos + accs[1] * sin]


def _epi_route(accs, es):
    lg = accs[0] + es[0]
    lane = lax.broadcasted_iota(jnp.int32, lg.shape, 1).astype(F32)
    neg = -jnp.inf
    big = 1.0e4
    is_grp = lane < MOE_GROUPS
    gl = jnp.where(is_grp, lg, neg)
    mg = jnp.max(gl, axis=-1, keepdims=True)
    g_sel = jnp.min(jnp.where(gl == mg, lane, big), axis=-1, keepdims=True)
    p_sel = 1.0 / jnp.sum(jnp.where(is_grp, jnp.exp(gl - mg), 0.0), axis=-1, keepdims=True)
    lo = MOE_GROUPS + g_sel * EXPERTS_PER_GROUP
    in_grp = jnp.where(lane >= lo, jnp.where(lane < lo + EXPERTS_PER_GROUP, 1.0, 0.0), 0.0) > 0.5
    el = jnp.where(in_grp, lg, neg)
    v1 = jnp.max(el, axis=-1, keepdims=True)
    i1 = jnp.min(jnp.where(el == v1, lane, big), axis=-1, keepdims=True)
    el2 = jnp.where(lane == i1, neg, el)
    v2 = jnp.max(el2, axis=-1, keepdims=True)
    i2 = jnp.min(jnp.where(el2 == v2, jnp.where(lane == i1, big, lane), big), axis=-1, keepdims=True)
    e21 = jnp.exp(v2 - v1)
    g1 = p_sel / (1.0 + e21)
    g2 = p_sel * e21 / (1.0 + e21)
    idx = jnp.where(lane == 0.0, i1 - MOE_GROUPS, jnp.where(lane == 1.0, i2 - MOE_GROUPS, 0.0))
    gates = jnp.where(lane == 0.0, g1, jnp.where(lane == 1.0, g2, 0.0))
    return [idx.astype(jnp.int32), gates]


def _mm_rows(x_list, consts, w, extras, *, prologue, epilogue, out_dtype, tm, tn, name,
             xn_dtype=BF16, precision=None, emit_xn_dtype=None, e_consts=()):
    m = x_list[0].shape[0]
    k, n = w.shape
    grid = (m // tm, n // tn)
    x_args = [(x, pl.BlockSpec((tm, x.shape[1]), lambda i, j: (i, 0))) for x in x_list]
    x_args += [(c, pl.BlockSpec((1, c.shape[1]), lambda i, j: (0, 0))) for c in consts]
    w_args = [(w, pl.BlockSpec((k, tn), lambda i, j: (0, j)))]
    e_args = [(c, pl.BlockSpec((1, tn), lambda i, j: (0, j))) for c in e_consts]
    e_args += [(e, pl.BlockSpec((tm, tn), lambda i, j: (i, j))) for e in extras]
    out_defs = [(jax.ShapeDtypeStruct((m, n), out_dtype), pl.BlockSpec((tm, tn), lambda i, j: (i, j)))]
    xn_out = None
    if emit_xn_dtype is not None:
        xn_out = (jax.ShapeDtypeStruct((m, k), emit_xn_dtype), pl.BlockSpec((tm, k), lambda i, j: (i, 0)))
    return _fused_mm(x_args, w_args, e_args, out_defs, grid=grid, prologue=prologue, epilogue=epilogue,
                     xn_shape=(tm, k), xn_dtype=xn_dtype, xn_out=xn_out, precision=precision, name=name)


def _mm_res_body(*refs, x_split, nc, nl, nw, ne, lhs_of_w, res_from_x, prologue, epilogue, cw):
    i = pl.program_id(0)
    pos, xs = 0, []
    for first_tiles in x_split:
        if first_tiles is None:
            xs.append(refs[pos][...])
            pos += 1
        else:
            xs.append(jnp.where(i < first_tiles, refs[pos][...], refs[pos + 1][...]))
            pos += 2
    x0_ref = refs[0]
    c_refs = refs[pos:pos + nc]
    l_refs = refs[pos + nc:pos + nc + nl]
    w_refs = refs[pos + nc + nl:pos + nc + nl + nw]
    e_refs = refs[pos + nc + nl + nw:pos + nc + nl + nw + ne]
    o_ref = refs[pos + nc + nl + nw + ne]
    xn_ref = refs[-1]
    xn_ref[...] = prologue(*xs, *[r[...] for r in c_refs]).astype(xn_ref.dtype)
    lhs = [xn_ref] + list(l_refs)
    for c in range(o_ref.shape[1] // cw):
        sl = slice(c * cw, (c + 1) * cw)
        accs = [jnp.dot(lhs[li][...], w[:, sl], preferred_element_type=F32) for w, li in zip(w_refs, lhs_of_w)]
        es = ([x0_ref[:, sl]] if res_from_x else []) + [e[:, sl] for e in e_refs]
        o_ref[:, sl] = epilogue(accs, es)[0].astype(o_ref.dtype)


def _mm_resident(x_list, consts, lhs_list, ws, lhs_of_w, extras, *, prologue, epilogue, out_dtype, tm, name,
                 res_from_x=False, cw=512):
    rows = lambda x: x[0].shape[0] + x[1].shape[0] if isinstance(x, tuple) else x.shape[0]
    m = rows(x_list[0])
    k = ws[lhs_of_w.index(0)].shape[0]
    n = ws[0].shape[1]
    row = lambda a: pl.BlockSpec((tm, a.shape[1]), lambda i: (i, 0))
    whole = lambda a: pl.BlockSpec(a.shape, lambda i: (0, 0), pipeline_mode=pl.Buffered(1))
    arrays, in_specs, x_split = [], [], []
    for x in x_list:
        if isinstance(x, tuple):
            a, b = x
            na = a.shape[0] // tm
            assert a.shape[0] % tm == 0 and b.shape[0] % tm == 0
            arrays += [a, b]
            in_specs += [pl.BlockSpec((tm, a.shape[1]), lambda i, na=na: (jnp.minimum(i, na - 1), 0)),
                         pl.BlockSpec((tm, b.shape[1]), lambda i, na=na: (jnp.maximum(i - na, 0), 0))]
            x_split.append(na)
        else:
            arrays.append(x)
            in_specs.append(row(x))
            x_split.append(None)
    arrays += list(consts) + list(lhs_list) + list(ws) + list(extras)
    in_specs += ([whole(a) for a in consts] + [row(a) for a in lhs_list] + [whole(a) for a in ws]
                 + [row(a) for a in extras])
    body = functools.partial(
        _mm_res_body, x_split=tuple(x_split), nc=len(consts), nl=len(lhs_list), nw=len(ws), ne=len(extras),
        lhs_of_w=tuple(lhs_of_w), res_from_x=res_from_x, prologue=prologue, epilogue=epilogue, cw=cw)
    return pl.pallas_call(
        body, grid=(m // tm,), in_specs=in_specs,
        out_specs=pl.BlockSpec((tm, n), lambda i: (i, 0)),
        out_shape=jax.ShapeDtypeStruct((m, n), out_dtype),
        scratch_shapes=[pltpu.VMEM((tm, k), BF16)],
        compiler_params=_cparams(("parallel",)), name=name,
    )(*arrays)


def _heads_body(*refs, nw, ne, hpb, epilogue):
    x_ref = refs[0]
    w_refs = refs[1:1 + nw]
    e_refs = refs[1 + nw:1 + nw + ne]
    o_refs = refs[1 + nw + ne:]
    x = x_ref[...]
    es = [e[...] for e in e_refs]
    for hh in range(hpb):
        outs = epilogue([jnp.dot(x, w[hh], preferred_element_type=F32) for w in w_refs], es)
        for o_ref, o in zip(o_refs, outs):
            o_ref[hh] = o.astype(o_ref.dtype)


def _heads_mm(x, ws, es, out_dims, *, tm, hpb, epilogue, name):
    m, k = x.shape
    nh = ws[0].shape[0]
    in_specs = [pl.BlockSpec((tm, k), lambda i, j: (i, 0))]
    in_specs += [pl.BlockSpec((hpb, k, w.shape[2]), lambda i, j: (j, 0, 0)) for w in ws]
    in_specs += [pl.BlockSpec((tm, e.shape[1]), lambda i, j: (i, 0)) for e in es]
    return pl.pallas_call(
        functools.partial(_heads_body, nw=len(ws), ne=len(es), hpb=hpb, epilogue=epilogue),
        grid=(m // tm, nh // hpb), in_specs=in_specs,
        out_specs=[pl.BlockSpec((hpb, tm, n), lambda i, j: (j, i, 0)) for n in out_dims],
        out_shape=[jax.ShapeDtypeStruct((nh, m, n), BF16) for n in out_dims],
        compiler_params=_cparams(("parallel", "arbitrary")), name=name,
    )(x, *ws, *es)


def _conv_silu_chunk(ext, in_ref, st_ref, w_ref, b_ref, first, q):
    @pl.when(first)
    def _():
        ext[0:SUBLANES, :] = st_ref[...]

    ext[SUBLANES:SUBLANES + q, :] = in_ref[...]
    acc = b_ref[...]
    for k in range(CONV_W):
        off = SUBLANES - (CONV_W - 1) + k
        acc = acc + ext[off:off + q, :] * w_ref[k:k + 1, :]
    ext[0:SUBLANES, :] = ext[q:q + SUBLANES, :]
    return _silu(acc)


def _ssd_body(x_ref, b_ref, c_ref, stx_ref, stb_ref, stc_ref, wx_ref, wb_ref, wc_ref, bx_ref, bb_ref, bc_ref,
              dt_ref, dtT_ref, bias_ref, biasT_ref, a_ref, aT_ref, dsk_ref, s0_ref,
              y_ref, so_ref, s_scr, ext_x, ext_b, ext_c, *, q, gps):
    c = pl.program_id(2)
    first = c == 0

    @pl.when(first)
    def _():
        s_scr[...] = s0_ref[...]

    xs_all = _conv_silu_chunk(ext_x, x_ref, stx_ref, wx_ref, bx_ref, first, q)
    bm_all = _conv_silu_chunk(ext_b, b_ref, stb_ref, wb_ref, bb_ref, first, q).astype(BF16)
    cm_all = _conv_silu_chunk(ext_c, c_ref, stc_ref, wc_ref, bc_ref, first, q).astype(BF16)

    row = lax.broadcasted_iota(jnp.int32, (q, q), 0)
    col = lax.broadcasted_iota(jnp.int32, (q, q), 1)
    causal = row >= col
    tri = jnp.where(causal, 1.0, 0.0)
    tri_t = jnp.where(row <= col, 1.0, 0.0)
    left = lax.broadcasted_iota(jnp.int32, (q, LANES), 1) < SSM_HEAD_DIM
    gw = GROUP_WIDTH
    for gi in range(gps):
        dt = _softplus(dt_ref[gi] + bias_ref[gi])
        dta = dt * a_ref[gi]
        dta_t = _softplus(dtT_ref[gi] + biasT_ref[gi]) * aT_ref[gi]
        acum = jnp.dot(tri, dta, preferred_element_type=F32, precision=HIGHEST)
        acum_t = jnp.dot(dta_t, tri_t, preferred_element_type=F32, precision=HIGHEST)
        bm = bm_all[:, gi * SSM_STATE:(gi + 1) * SSM_STATE]
        cm = cm_all[:, gi * SSM_STATE:(gi + 1) * SSM_STATE]
        cb = lax.dot_general(cm, bm, (((1,), (1,)), ((), ())), preferred_element_type=F32)
        s_prev = s_scr[gi]
        y_off = jnp.dot(cm, s_prev.astype(BF16), preferred_element_type=F32)
        xw_parts, dec_parts = [], []
        for j in range(HEADS_PER_GROUP // 2):
            h0, h1 = 2 * j, 2 * j + 1
            sl = slice(j * LANES, (j + 1) * LANES)
            sg = slice(gi * gw + j * LANES, gi * gw + (j + 1) * LANES)
            col0, col1 = acum[:, h0:h0 + 1], acum[:, h1:h1 + 1]
            pa = jnp.where(left, col0, col1)
            dtp = jnp.where(left, dt[:, h0:h0 + 1], dt[:, h1:h1 + 1])
            x = xs_all[:, sg]
            xdt = x * dtp
            m0 = (jnp.exp(jnp.where(causal, col0 - acum_t[h0:h0 + 1, :], -jnp.inf)) * cb).astype(BF16)
            m1 = (jnp.exp(jnp.where(causal, col1 - acum_t[h1:h1 + 1, :], -jnp.inf)) * cb).astype(BF16)
            x_l = jnp.where(left, xdt, 0.0).astype(BF16)
            x_r = jnp.where(left, 0.0, xdt).astype(BF16)
            y_diag = (jnp.dot(m0, x_l, preferred_element_type=F32)
                      + jnp.dot(m1, x_r, preferred_element_type=F32))
            y = y_diag + y_off[:, sl] * jnp.exp(pa) + x * dsk_ref[:, sg]
            y_ref[:, sg] = y.astype(y_ref.dtype)
            last = pa[q - 1:q, :]
            xw_parts.append((xdt * jnp.exp(last - pa)).astype(BF16))
            dec_parts.append(jnp.exp(last))
        xw = jnp.concatenate(xw_parts, axis=1)
        dec = jnp.concatenate(dec_parts, axis=1)
        s_new = s_prev * dec + lax.dot_general(bm, xw, (((0,), (0,)), ((), ())), preferred_element_type=F32)
        s_scr[gi] = s_new
        so_ref[gi] = s_new


def _ssd_scan(xbc3, batch0, nb, L, st, w_conv, b_conv, dtg, dtg_t, bias, bias_t, a, a_t, dsk, s0, *, q, gps=2):
    g_, r_, n_ = SSM_GROUPS, HEADS_PER_GROUP, SSM_STATE
    gw = GROUP_WIDTH
    xw_, bw_ = gps * gw, gps * n_
    b_off = D_INNER // bw_
    c_off = b_off + g_ // gps
    grid = (nb, g_ // gps, L // q)
    col_x = lambda b, g, c: (0, g)
    col_b = lambda b, g, c: (0, b_off + g)
    col_c = lambda b, g, c: (0, c_off + g)
    return pl.pallas_call(
        functools.partial(_ssd_body, q=q, gps=gps), grid=grid,
        in_specs=[
            pl.BlockSpec((None, q, xw_), lambda b, g, c: (batch0 + b, c, g)),
            pl.BlockSpec((None, q, bw_), lambda b, g, c: (batch0 + b, c, b_off + g)),
            pl.BlockSpec((None, q, bw_), lambda b, g, c: (batch0 + b, c, c_off + g)),
            pl.BlockSpec((None, SUBLANES, xw_), lambda b, g, c: (b, 0, g)),
            pl.BlockSpec((None, SUBLANES, bw_), lambda b, g, c: (b, 0, b_off + g)),
            pl.BlockSpec((None, SUBLANES, bw_), lambda b, g, c: (b, 0, c_off + g)),
            pl.BlockSpec((CONV_W, xw_), col_x),
            pl.BlockSpec((CONV_W, bw_), col_b),
            pl.BlockSpec((CONV_W, bw_), col_c),
            pl.BlockSpec((1, xw_), col_x),
            pl.BlockSpec((1, bw_), col_b),
            pl.BlockSpec((1, bw_), col_c),
            pl.BlockSpec((None, gps, q, r_), lambda b, g, c: (b, g, c, 0)),
            pl.BlockSpec((None, gps, r_, q), lambda b, g, c: (b, g, 0, c)),
            pl.BlockSpec((gps, 1, r_), lambda b, g, c: (g, 0, 0)),
            pl.BlockSpec((gps, r_, 1), lambda b, g, c: (g, 0, 0)),
            pl.BlockSpec((gps, 1, r_), lambda b, g, c: (g, 0, 0)),
            pl.BlockSpec((gps, r_, 1), lambda b, g, c: (g, 0, 0)),
            pl.BlockSpec((1, xw_), col_x),
            pl.BlockSpec((None, gps, n_, gw), lambda b, g, c: (b, g, 0, 0)),
        ],
        out_specs=[
            pl.BlockSpec((None, q, xw_), lambda b, g, c: (b, c, g)),
            pl.BlockSpec((None, gps, n_, gw), lambda b, g, c: (b, g, 0, 0)),
        ],
        out_shape=[jax.ShapeDtypeStruct((nb, L, D_INNER), BF16),
                   jax.ShapeDtypeStruct((nb, g_, n_, gw), F32)],
        scratch_shapes=[pltpu.VMEM((gps, n_, gw), F32), pltpu.VMEM((q + SUBLANES, xw_), F32),
                        pltpu.VMEM((q + SUBLANES, bw_), F32), pltpu.VMEM((q + SUBLANES, bw_), F32)],
        compiler_params=_cparams(("parallel", "parallel", "arbitrary")), name="ssd_scan",
    )(xbc3, xbc3, xbc3, st, st, st, w_conv, w_conv, w_conv, b_conv, b_conv, b_conv,
      dtg, dtg_t, bias, bias_t, a, a_t, dsk, s0)


def _gather_rows(tok_ref, base, src_hbm, dst, sem, n):
    def body(r, carry):
        t = tok_ref[base + r]
        pltpu.make_async_copy(src_hbm.at[pl.ds(t, 1)], dst.at[pl.ds(r, 1)], sem).start()
        return carry

    lax.fori_loop(0, n, body, 0)


def _gather_rows_unrolled(tok_ref, base, src_hbm, dst, sem, n):
    for r in range(n):
        t = tok_ref[base + r]
        pltpu.make_async_copy(src_hbm.at[pl.ds(t, 1)], dst.at[pl.ds(r, 1)], sem).start(priority=r % 2)


def _moe_body(blk_e_ref, nxt_e_ref, tok_ref, nused_ref, xt_hbm, wg_hbm, wu_hbm, wd_hbm, yb_ref,
              xbuf, xsem, wg32, wu32, wd32, wsem, wg_bf, wu_bf, wd_bf, xb, *, bm, layer):
    b = pl.program_id(0)
    slot = b % 2
    nused = nused_ref[0]
    e_cur = blk_e_ref[b]

    def weight_copies(e):
        return (pltpu.make_async_copy(wg_hbm.at[layer, e], wg32, wsem.at[0]),
                pltpu.make_async_copy(wu_hbm.at[layer, e], wu32, wsem.at[1]),
                pltpu.make_async_copy(wd_hbm.at[layer, e], wd32, wsem.at[2]))

    def gather_unrolled(base, slot_):
        for r in range(bm):
            t = tok_ref[base + r]
            pltpu.make_async_copy(xt_hbm.at[pl.ds(t, 1)], xbuf.at[slot_, pl.ds(r, 1)], xsem.at[slot_]).start()

    @pl.when(jnp.logical_and(b == 0, nused > 0))
    def _():
        for cp in weight_copies(e_cur):
            cp.start(priority=1)
        _gather_rows(tok_ref, 0, xt_hbm, xbuf.at[0], xsem.at[0], bm)

    prev_e = blk_e_ref[jnp.maximum(b - 1, 0)]
    changed = jnp.logical_or(b == 0, e_cur != prev_e)

    @pl.when(jnp.logical_and(changed, b < nused))
    def _():
        for cp in weight_copies(e_cur):
            cp.wait()
        wg_bf[...] = wg32[...].astype(BF16)
        wu_bf[...] = wu32[...].astype(BF16)
        wd_bf[...] = wd32[...].astype(BF16)
        nxt = nxt_e_ref[b]

        @pl.when(nxt >= 0)
        def _():
            for cp in weight_copies(nxt):
                cp.start(priority=1)

    def compute(prefetch):
        pltpu.make_async_copy(xbuf.at[slot], xbuf.at[slot], xsem.at[slot]).wait()
        w = xbuf[slot]
        half = w.shape[1]
        xb[:, :half] = lax.bitcast_convert_type(w << 16, F32).astype(BF16)
        xb[:, half:] = lax.bitcast_convert_type(w & jnp.uint32(0xFFFF0000), F32).astype(BF16)
        if prefetch:
            gather_unrolled((b + 1) * bm, 1 - slot)
        x = xb[...]
        g = jnp.dot(x, wg_bf[...], preferred_element_type=F32)
        u = jnp.dot(x, wu_bf[...], preferred_element_type=F32)
        hid = (_silu(g) * u).astype(BF16)
        yb_ref[...] = jnp.dot(hid, wd_bf[...], preferred_element_type=F32)

    @pl.when(b + 1 < nused)
    def _():
        compute(True)

    @pl.when(b + 1 == nused)
    def _():
        compute(False)

    @pl.when(b >= nused)
    def _():
        yb_ref[...] = jnp.zeros(yb_ref.shape, yb_ref.dtype)


def _moe_experts(xt, blk_e, nxt_e, row_token, nused, w_gate, w_up, w_down, *, bm, layer):
    nblk = blk_e.shape[0]
    d = w_gate.shape[2]
    de = w_gate.shape[3]
    assert xt.shape[1] * 2 == d and xt.dtype == jnp.uint32
    anyspec = pl.BlockSpec(memory_space=pl.ANY)
    gs = pltpu.PrefetchScalarGridSpec(
        num_scalar_prefetch=4, grid=(nblk,),
        in_specs=[anyspec, anyspec, anyspec, anyspec],
        out_specs=pl.BlockSpec((bm, d), lambda b, be, nx, tok, nu: (b, 0)),
        scratch_shapes=[
            pltpu.VMEM((2, bm, d // 2), jnp.uint32),
            pltpu.SemaphoreType.DMA((2,)),
            pltpu.VMEM((d, de), F32),
            pltpu.VMEM((d, de), F32),
            pltpu.VMEM((de, d), F32),
            pltpu.SemaphoreType.DMA((3,)),
            pltpu.VMEM((d, de), BF16),
            pltpu.VMEM((d, de), BF16),
            pltpu.VMEM((de, d), BF16),
            pltpu.VMEM((bm, d), BF16),
        ],
    )
    return pl.pallas_call(
        functools.partial(_moe_body, bm=bm, layer=layer), grid_spec=gs,
        out_shape=jax.ShapeDtypeStruct((nblk * bm, d), F32),
        compiler_params=_cparams(("arbitrary",)), name="moe_experts",
    )(blk_e, nxt_e, row_token, nused, xt, w_gate, w_up, w_down)


def _combine_body(dest_ref, h_ref, g_ref, yb_hbm, o_ref, buf, sem, *, tc, ntile):
    i = pl.program_id(0)
    slot = i % 2
    n = TOP_K * tc

    @pl.when(i == 0)
    def _():
        _gather_rows(dest_ref, 0, yb_hbm, buf.at[0], sem.at[0], n)

    @pl.when(i + 1 < ntile)
    def _():
        _gather_rows_unrolled(dest_ref, (i + 1) * n, yb_hbm, buf.at[1 - slot], sem.at[1 - slot], n)

    pltpu.make_async_copy(buf.at[slot], buf.at[slot], sem.at[slot]).wait()
    g = g_ref[...]
    o_ref[...] = (h_ref[...] + g[:, 0:1] * buf[slot, 0:tc, :] + g[:, 1:2] * buf[slot, tc:2 * tc, :])


def _moe_combine(h, yb, dest_tiles, gates, *, tc):
    t, d = h.shape
    ntile = t // tc
    gs = pltpu.PrefetchScalarGridSpec(
        num_scalar_prefetch=1, grid=(ntile,),
        in_specs=[
            pl.BlockSpec((tc, d), lambda i, dst: (i, 0)),
            pl.BlockSpec((tc, LANES), lambda i, dst: (i, 0)),
            pl.BlockSpec(memory_space=pl.ANY),
        ],
        out_specs=pl.BlockSpec((tc, d), lambda i, dst: (i, 0)),
        scratch_shapes=[pltpu.VMEM((2, TOP_K * tc, d), F32), pltpu.SemaphoreType.DMA((2,))],
    )
    return pl.pallas_call(
        functools.partial(_combine_body, tc=tc, ntile=ntile), grid_spec=gs,
        out_shape=jax.ShapeDtypeStruct((t, d), F32),
        compiler_params=_cparams(("arbitrary",)), name="moe_combine",
    )(dest_tiles, h, gates, yb)


def _moe_plan(idx, bm, tc):
    t = idx.shape[0]
    a = t * TOP_K
    e_ = N_EXPERTS
    flat_e = idx.reshape(a)
    onehot = (flat_e[:, None] == jnp.arange(e_, dtype=jnp.int32)[None, :]).astype(jnp.int32)
    csum = jnp.cumsum(onehot, axis=0)
    rank = jnp.sum(onehot * csum, axis=1) - 1
    counts = csum[-1]
    padded = (counts + bm - 1) // bm * bm
    pad_end = jnp.cumsum(padded)
    pad_start = pad_end - padded
    dest = jnp.sum(onehot * pad_start[None, :], axis=1) + rank
    nblk = -(-a // bm) + e_
    rows = nblk * bm
    row_token = jnp.zeros((rows,), jnp.int32).at[dest].set(jnp.arange(a, dtype=jnp.int32) // TOP_K)
    blk_row0 = jnp.arange(nblk, dtype=jnp.int32) * bm
    blk_e = jnp.minimum(jnp.sum((pad_end[None, :] <= blk_row0[:, None]).astype(jnp.int32), axis=1), e_ - 1)
    ids = jnp.arange(e_, dtype=jnp.int32)
    later_used = jnp.logical_and(ids[None, :] > ids[:, None], (counts > 0)[None, :])
    nxt_of_e = jnp.min(jnp.where(later_used, ids[None, :], e_), axis=1)
    nxt_of_e = jnp.where(nxt_of_e == e_, -1, nxt_of_e)
    nxt_e = jnp.sum(jnp.where(blk_e[:, None] == ids[None, :], nxt_of_e[None, :], 0), axis=1).astype(jnp.int32)
    nused = (pad_end[-1] // bm).astype(jnp.int32).reshape(1)
    dest_tiles = dest.reshape(t // tc, tc, TOP_K).transpose(0, 2, 1).reshape(a).astype(jnp.int32)
    return blk_e.astype(jnp.int32), nxt_e, row_token, nused, dest_tiles


def _flash_body(qi_ref, kj_ref, q_ref, k_ref, v_ref, o_ref, m_scr, acc_scr, *, hb):
    p_id = pl.program_id(1)
    qi = qi_ref[p_id]
    kj = kj_ref[p_id]
    tk = k_ref.shape[1]

    @pl.when(kj == 0)
    def _():
        m_scr[...] = jnp.full(m_scr.shape, -jnp.inf, F32)
        acc_scr[...] = jnp.zeros(acc_scr.shape, F32)

    def step(diag):
        for hh in range(hb):
            s = lax.dot_general(q_ref[hh], k_ref[hh], (((1,), (1,)), ((), ())), preferred_element_type=F32)
            if diag:
                r = lax.broadcasted_iota(jnp.int32, s.shape, 0)
                c = lax.broadcasted_iota(jnp.int32, s.shape, 1)
                s = jnp.where((c // CHUNK) <= (r // CHUNK), s, -jnp.inf)
            m_prev = m_scr[hh]
            m_next = jnp.maximum(m_prev, jnp.max(s, axis=1, keepdims=True))
            p = jnp.exp2(s - jnp.concatenate([m_next] * (tk // LANES), axis=1))
            alpha = jnp.exp2(m_prev - m_next)
            acc_scr[hh] = (jnp.concatenate([alpha, alpha], axis=1) * acc_scr[hh]
                           + jnp.dot(p.astype(BF16), v_ref[hh], preferred_element_type=F32))
            m_scr[hh] = m_next

    @pl.when(kj < qi)
    def _():
        step(False)

    @pl.when(kj == qi)
    def _():
        step(True)
        for hh in range(hb):
            a = acc_scr[hh]
            o_ref[:, hh * V_DIM:(hh + 1) * V_DIM] = (a[:, :V_DIM] / a[:, V_DIM:]).astype(o_ref.dtype)


def _flash_prompt(q, k, v, lp, *, tq, hb):
    assert V_DIM == LANES
    nh = q.shape[0]
    nq = lp // tq
    pairs = [(a, b) for a in range(nq) for b in range(a + 1)]
    qi_tab = jnp.asarray(np.array([p[0] for p in pairs], np.int32))
    kj_tab = jnp.asarray(np.array([p[1] for p in pairs], np.int32))
    gs = pltpu.PrefetchScalarGridSpec(
        num_scalar_prefetch=2, grid=(nh // hb, len(pairs)),
        in_specs=[
            pl.BlockSpec((hb, tq, QK_DIM), lambda h, p, qi, kj: (h, qi[p], 0)),
            pl.BlockSpec((hb, tq, QK_DIM), lambda h, p, qi, kj: (h, kj[p], 0)),
            pl.BlockSpec((hb, tq, 2 * V_DIM), lambda h, p, qi, kj: (h, kj[p], 0)),
        ],
        out_specs=pl.BlockSpec((tq, hb * V_DIM), lambda h, p, qi, kj: (qi[p], h)),
        scratch_shapes=[pltpu.VMEM((hb, tq, LANES), F32), pltpu.VMEM((hb, tq, 2 * V_DIM), F32)],
    )
    return pl.pallas_call(
        functools.partial(_flash_body, hb=hb), grid_spec=gs,
        out_shape=jax.ShapeDtypeStruct((lp, nh * V_DIM), BF16),
        compiler_params=_cparams(("parallel", "arbitrary")), name="flash_prompt",
    )(qi_tab, kj_tab, q, k, v)


def _attn_sample_body(q_ref, k_ref, o_ref, *, ls, past, s_valid):
    nh = q_ref.shape[0]
    q = q_ref[...].reshape(nh * ls, q_ref.shape[2])
    k = k_ref[...]
    s = lax.dot_general(q, k, (((1,), (1,)), ((), ())), preferred_element_type=F32)
    r = lax.broadcasted_iota(jnp.int32, s.shape, 0)
    c = lax.broadcasted_iota(jnp.int32, s.shape, 1)
    q_pos = past + r % ls
    s = jnp.where((c // CHUNK) <= (q_pos // CHUNK), s, -jnp.inf)
    s = jnp.where(c < s_valid, s, -jnp.inf)
    m = jnp.max(s, axis=-1, keepdims=True)
    p = jnp.exp2(s - m)
    l = jnp.sum(p, axis=-1, keepdims=True)
    o = jnp.dot(p.astype(BF16), k[:, :KV_RANK], preferred_element_type=F32) / l
    o_ref[...] = o.reshape(nh, ls, KV_RANK).astype(o_ref.dtype)


def _attn_sample(q_abs, kc, *, ls, past, s_valid):
    nh, ts, dk = q_abs.shape
    nb, s_pad, _ = kc.shape
    return pl.pallas_call(
        functools.partial(_attn_sample_body, ls=ls, past=past, s_valid=s_valid), grid=(nb,),
        in_specs=[
            pl.BlockSpec((nh, ls, dk), lambda b: (0, b, 0)),
            pl.BlockSpec((None, s_pad, dk), lambda b: (b, 0, 0)),
        ],
        out_specs=pl.BlockSpec((nh, ls, KV_RANK), lambda b: (0, b, 0)),
        out_shape=jax.ShapeDtypeStruct((nh, ts, KV_RANK), BF16),
        compiler_params=_cparams(("parallel",)), name="attn_sample",
    )(q_abs, kc)


def _norm_body(x_ref, g_ref, o_ref):
    o_ref[...] = _rms_rows(x_ref[...]) * g_ref[...]


def _final_norm(h, g, row0, nrows, *, tm):
    d = h.shape[1]
    blk0 = row0 // tm
    return pl.pallas_call(
        _norm_body, grid=(nrows // tm,),
        in_specs=[pl.BlockSpec((tm, d), lambda i: (blk0 + i, 0)), pl.BlockSpec((1, d), lambda i: (0, 0))],
        out_specs=pl.BlockSpec((tm, d), lambda i: (i, 0)),
        out_shape=jax.ShapeDtypeStruct((nrows, d), F32),
        compiler_params=_cparams(("parallel",)), name="final_norm",
    )(h, g)


def _rope_tables(pos):
    half = QK_ROPE // 2
    inv = ROPE_THETA ** (-np.arange(half, dtype=np.float64) / half)
    ang = np.asarray(pos, np.float64)[:, None] * inv[None, :]
    cos = np.concatenate([np.cos(ang), np.cos(ang)], axis=1)
    sin = np.concatenate([-np.sin(ang), np.sin(ang)], axis=1)
    return cos.astype(np.float32), sin.astype(np.float32)


def _swap_rope_halves(w):
    half = QK_ROPE // 2
    return jnp.concatenate([w[..., half:], w[..., :half]], axis=-1)


def _moe_layer(h, g_ffn, w_rg, b_rg, w_re, b_re, w_gate, w_up, w_down, *, tm, layer):
    t, d = h.shape
    npad = LANES - MOE_GROUPS - N_EXPERTS
    w_r = jnp.concatenate([w_rg, w_re, jnp.zeros((d, npad), F32)], axis=1)
    b_r = jnp.concatenate([b_rg, b_re, jnp.zeros((npad,), F32)]).reshape(1, LANES)
    grid = (t // tm, 1)
    x_args = [(h, pl.BlockSpec((tm, d), lambda i, j: (i, 0))),
              (g_ffn.reshape(1, d), pl.BlockSpec((1, d), lambda i, j: (0, 0)))]
    w_args = [(w_r, pl.BlockSpec((d, LANES), lambda i, j: (0, 0)))]
    e_args = [(b_r, pl.BlockSpec((1, LANES), lambda i, j: (0, 0)))]
    tile = pl.BlockSpec((tm, LANES), lambda i, j: (i, 0))
    out_defs = [(jax.ShapeDtypeStruct((t, LANES), jnp.int32), tile),
                (jax.ShapeDtypeStruct((t, LANES), F32), tile)]
    xn_out = (jax.ShapeDtypeStruct((t, d // 2), jnp.uint32), pl.BlockSpec((tm, d // 2), lambda i, j: (i, 0)))
    idx_t, gates_t, xt = _fused_mm(x_args, w_args, e_args, out_defs, grid=grid, prologue=_pro_rms,
                                   epilogue=_epi_route, xn_shape=(tm, d), xn_dtype=F32, xn_out=xn_out,
                                   xn_emit=_pack_bf16_pairs, precision=HIGHEST, name="router")
    tc = min(tm, 128)
    blk_e, nxt_e, row_token, nused, dest_tiles = _moe_plan(idx_t[:, :TOP_K], MOE_BM, tc)
    yb = _moe_experts(xt, blk_e, nxt_e, row_token, nused, w_gate, w_up, w_down, bm=MOE_BM, layer=layer)
    return _moe_combine(h, yb, dest_tiles, gates_t, tc=tc)


def _ple_layer(h, p_rows, g_ple, w_gate, w_proj, *, tm, tn):
    t, d = h.shape
    return _mm_resident([h], [g_ple.reshape(1, d)], [p_rows.astype(BF16)], [w_gate.astype(BF16), w_proj.astype(BF16)],
                        [0, 1], [], prologue=_pro_rms, epilogue=_epi_ple, out_dtype=F32, tm=tm, name="ple_gate",
                        res_from_x=True)


def kernel(x_prompt, x_sample, state_conv, state_ssm, cache_kv_latent, cache_k_rope, p_prompt, p_sample,
           g_mix, w_ssm_in, w_conv, b_conv, dt_bias, a_log, d_skip, g_ssm_norm, w_ssm_out,
           g_kv_in, w_dkv, g_kv, w_uk, w_uv, w_dq, g_q, w_uq, w_o,
           g_ffn, w_router_grp, b_router_grp, w_router_exp, b_router_exp, w_exp_gate, w_exp_up, w_exp_down,
           g_ple, w_ple_gate, w_ple_proj, g_final):
    bp, lp_each, d = x_prompt.shape
    bs, ls, _ = x_sample.shape
    past = cache_kv_latent.shape[1]
    assert bp == 1 and w_ssm_in.shape[0] == 1 and w_dq.shape[0] == 1 and d == D_MODEL
    assert ls >= CONV_W - 1 and ls % SUBLANES == 0 and past % CHUNK == 0
    lp = bp * lp_each
    ts = bs * ls
    t = lp + ts
    tm = min(512, math.gcd(lp, ts))
    tn = 512
    g_, r_, n_, hd = SSM_GROUPS, HEADS_PER_GROUP, SSM_STATE, SSM_HEAD_DIM

    h = jnp.concatenate([x_prompt.reshape(lp, d), x_sample.reshape(ts, d)], axis=0)

    w_in = w_ssm_in[0]
    w_z = w_in[:, :D_INNER].astype(BF16)
    xw = SSM_IN_XW
    w_x = jnp.pad(w_in[:, D_INNER:], ((0, 0), (0, xw - CONV_DIM - SSM_HEADS))).astype(BF16)
    g0 = g_mix[0].reshape(1, d)
    tm_in = max(c for c in range(16, 1153, 16) if t % c == 0)
    mm_in = functools.partial(_mm_rows, [h], [g0], prologue=_pro_rms, epilogue=_epi_id, tm=tm_in)
    z = mm_in(w_z, [], out_dtype=BF16, tn=1024, name="ssm_in_z")[0]
    xbc = mm_in(w_x, [], out_dtype=F32, tn=xw // 5, name="ssm_in_xbc")[0]
    dt_raw = xbc[:, CONV_DIM:CONV_DIM + SSM_HEADS]

    wc = w_conv[0]
    bc = b_conv[0].reshape(1, CONV_DIM)
    st_p = jnp.zeros((1, SUBLANES, CONV_DIM), F32)
    st_s = jnp.pad(state_conv[0].astype(F32), ((0, 0), (SUBLANES - (CONV_W - 1), 0), (0, 0)))
    conv_p = xbc[lp - (CONV_W - 1):lp, :CONV_DIM].reshape(1, 1, CONV_W - 1, CONV_DIM)
    conv_s = xbc[lp:, :CONV_DIM].reshape(bs, ls, CONV_DIM)[:, ls - (CONV_W - 1):].reshape(
        1, bs, CONV_W - 1, CONV_DIM)

    bias = dt_bias[0].astype(F32).reshape(g_, 1, r_)
    a_neg = (-jnp.exp(a_log[0].astype(F32))).reshape(g_, 1, r_)
    dsk = jnp.repeat(d_skip[0].astype(F32), hd).reshape(1, D_INNER)

    def dt_views(rows, nb, L):
        v = rows.reshape(nb, L, g_, r_).transpose(0, 2, 1, 3)
        return v, v.transpose(0, 1, 3, 2)

    def scan(xbc3, batch0, nb, L, st, rows, s0, q):
        dtg, dtg_t = dt_views(rows, nb, L)
        return _ssd_scan(xbc3, batch0, nb, L, st, wc, bc, dtg, dtg_t, bias, bias.transpose(0, 2, 1),
                         a_neg, a_neg.transpose(0, 2, 1), dsk, s0, q=q)

    def state_in(s):
        nb = s.shape[0]
        return s.astype(F32).reshape(nb, g_, r_ * hd, n_).transpose(0, 1, 3, 2)

    def state_out(s):
        nb = s.shape[0]
        return s.transpose(0, 1, 3, 2).reshape(1, nb, SSM_HEADS, hd, n_)

    y_p, s_p = scan(xbc.reshape(1, t, xw), 0, 1, lp, st_p, dt_raw[:lp],
                    jnp.zeros((1, g_, n_, r_ * hd), F32), min(SSD_Q, lp))
    y_s, s_s = scan(xbc.reshape(t // ls, ls, xw), lp // ls, bs, ls, st_s, dt_raw[lp:],
                    state_in(state_ssm[0]), ls)
    ssm_p, ssm_s = state_out(s_p), state_out(s_s)
    y_all = (y_p.reshape(lp, D_INNER), y_s.reshape(ts, D_INNER))

    h = _mm_resident([y_all, z], [g_ssm_norm[0].reshape(1, D_INNER)], [], [w_ssm_out[0].astype(BF16)], [0], [h],
                     prologue=_pro_gated, epilogue=_epi_res, out_dtype=F32, tm=tm // 2, name="ssm_out")

    def ffn_and_ple(h, i):
        h = _moe_layer(h, g_ffn[i], w_router_grp[i], b_router_grp[i], w_router_exp[i], b_router_exp[i],
                       w_exp_gate, w_exp_up, w_exp_down, tm=tm, layer=i)
        p_rows = jnp.concatenate([p_prompt[i].reshape(lp, -1), p_sample[i].reshape(ts, -1)], axis=0)
        return _ple_layer(h, p_rows, g_ple[i], w_ple_gate[i], w_ple_proj[i], tm=tm, tn=tn)

    h = ffn_and_ple(h, 0)

    pos = np.concatenate([np.arange(lp), np.tile(past + np.arange(ls), bs)])
    cos64, sin64 = _rope_tables(pos)
    gk = g_kv_in.reshape(1, d)
    ckv = _mm_rows([h], [gk], w_dkv[:, :KV_RANK].astype(BF16), [], prologue=_pro_rms, epilogue=_epi_rms_out,
                   out_dtype=F32, tm=tm, tn=KV_RANK, name="kv_latent", e_consts=[g_kv.reshape(1, KV_RANK)])[0]
    w_r = w_dkv[:, KV_RANK:]
    row64 = pl.BlockSpec((tm, QK_ROPE), lambda i, j: (i, 0))
    w64 = pl.BlockSpec((d, QK_ROPE), lambda i, j: (0, 0))
    kr = _fused_mm(
        [(h, pl.BlockSpec((tm, d), lambda i, j: (i, 0))), (gk, pl.BlockSpec((1, d), lambda i, j: (0, 0)))],
        [(w_r.astype(BF16), w64), (_swap_rope_halves(w_r).astype(BF16), w64)],
        [(jnp.asarray(cos64), row64), (jnp.asarray(sin64), row64)],
        [(jax.ShapeDtypeStruct((t, QK_ROPE), F32), row64)],
        grid=(t // tm, 1), prologue=_pro_rms, epilogue=_epi_rot, xn_shape=(tm, d), name="k_rope")[0]

    ql = _mm_rows([h], [g_mix[1].reshape(1, d)], w_dq[0].astype(BF16), [], prologue=_pro_rms,
                  epilogue=_epi_rms_out, out_dtype=BF16, tm=tm, tn=Q_RANK, name="q_latent",
                  e_consts=[g_q[0].reshape(1, Q_RANK)])[0]
    nh = MLA_HEADS
    w_q = w_uq[0].reshape(Q_RANK, nh, QK_DIM).transpose(1, 0, 2)
    w_q_rot = jnp.concatenate([jnp.zeros((nh, Q_RANK, QK_NOPE), F32), _swap_rope_halves(w_q[..., QK_NOPE:])], -1)
    cos_q = np.concatenate([np.ones((t, QK_NOPE), np.float32), cos64], axis=1) * np.float32(QK_PRESCALE)
    sin_q = np.concatenate([np.zeros((t, QK_NOPE), np.float32), sin64], axis=1) * np.float32(QK_PRESCALE)
    q_cat = _heads_mm(ql, [w_q.astype(BF16), w_q_rot.astype(BF16)], [jnp.asarray(cos_q), jnp.asarray(sin_q)],
                      [QK_DIM], tm=tm, hpb=8, epilogue=_epi_rot, name="q_heads")[0]

    lat_pad = LAT_PAD - KV_RANK - QK_ROPE
    ckr_p = jnp.concatenate([ckv[:lp], kr[:lp], jnp.zeros((lp, lat_pad), F32)], axis=1).astype(BF16)
    w_uk_h = w_uk.transpose(1, 0, 2)
    w_uv_h = w_uv.transpose(1, 0, 2)
    eye_r = jnp.broadcast_to(jnp.eye(QK_ROPE, dtype=F32), (nh, QK_ROPE, QK_ROPE))
    w_k_ext = jnp.concatenate([
        jnp.concatenate([w_uk_h, jnp.zeros((nh, KV_RANK, QK_ROPE), F32)], axis=2),
        jnp.concatenate([jnp.zeros((nh, QK_ROPE, QK_NOPE), F32), eye_r], axis=2),
        jnp.zeros((nh, lat_pad, QK_DIM), F32)], axis=1).astype(BF16)
    w_v_ext = jnp.concatenate([w_uv_h, jnp.zeros((nh, LAT_PAD - KV_RANK, V_DIM), F32)], axis=1).astype(BF16)
    k_cat, v_h = _heads_mm(ckr_p, [w_k_ext, w_v_ext], [], [QK_DIM, 2 * V_DIM], tm=tm, hpb=8,
                           epilogue=_epi_kv_ones, name="kv_heads")
    o_p = _flash_prompt(q_cat, k_cat, v_h, lp, tq=min(512, lp), hb=4)

    w_abs = jnp.concatenate([
        jnp.concatenate([w_uk_h.transpose(0, 2, 1), jnp.zeros((nh, QK_NOPE, LAT_PAD - KV_RANK), F32)], axis=2),
        jnp.concatenate([jnp.zeros((nh, QK_ROPE, KV_RANK), F32), eye_r,
                         jnp.zeros((nh, QK_ROPE, lat_pad), F32)], axis=2)], axis=1).astype(BF16)
    blk_s = lp // ts if lp % ts == 0 else None
    assert blk_s is not None
    q_abs = _fused_mm(
        [(q_cat, pl.BlockSpec((None, ts, QK_DIM), lambda i, j: (j, blk_s + i, 0)))],
        [(w_abs, pl.BlockSpec((None, QK_DIM, LAT_PAD), lambda i, j: (j, 0, 0)))],
        [],
        [(jax.ShapeDtypeStruct((nh, ts, LAT_PAD), BF16), pl.BlockSpec((None, ts, LAT_PAD), lambda i, j: (j, i, 0)))],
        grid=(1, nh), prologue=_pro_cast, epilogue=_epi_id, x_per_j=True, name="q_absorb")[0]
    s_valid = past + ls
    s_pad = -(-s_valid // LANES) * LANES
    kc_new = jnp.concatenate([ckv[lp:], kr[lp:]], axis=1).reshape(bs, ls, KV_RANK + QK_ROPE)
    kc = jnp.concatenate([jnp.concatenate([cache_kv_latent, cache_k_rope], axis=2), kc_new], axis=1)
    kc = jnp.pad(kc, ((0, 0), (0, s_pad - s_valid), (0, lat_pad))).astype(BF16)
    o_lat = _attn_sample(q_abs, kc, ls=ls, past=past, s_valid=s_valid)
    o_s = _fused_mm(
        [(o_lat, pl.BlockSpec((None, ts, KV_RANK), lambda i, j: (j, i, 0)))],
        [(w_uv_h.astype(BF16), pl.BlockSpec((None, KV_RANK, V_DIM), lambda i, j: (j, 0, 0)))],
        [],
        [(jax.ShapeDtypeStruct((ts, nh * V_DIM), BF16), pl.BlockSpec((ts, V_DIM), lambda i, j: (i, j)))],
        grid=(1, nh), prologue=_pro_cast, epilogue=_epi_id, x_per_j=True, name="v_absorb")[0]
    h = _mm_resident([(o_p, o_s)], [], [], [w_o[0].astype(BF16)], [0], [h], prologue=_pro_cast, epilogue=_epi_res,
                     out_dtype=F32, tm=tm, name="attn_out")
    h = ffn_and_ple(h, 1)

    gf = g_final.reshape(1, d)
    y_p_out = _final_norm(h, gf, 0, lp, tm=tm).reshape(bp, lp_each, d)
    y_s_out = _final_norm(h, gf, lp, ts, tm=tm).reshape(bs, ls, d)
    return (y_p_out, y_s_out, conv_p, ssm_p, ckv[:lp].reshape(bp, lp_each, KV_RANK),
            kr[:lp].reshape(bp, lp_each, QK_ROPE), conv_s, ssm_s, ckv[lp:].reshape(bs, ls, KV_RANK),
            kr[lp:].reshape(bs, ls, QK_ROPE))
```

```python
import functools
import math

import numpy as np
import jax
import jax.numpy as jnp
from jax import lax
from jax.experimental import pallas as pl
from jax.experimental.pallas import tpu as pltpu

F32 = jnp.float32
BF16 = jnp.bfloat16

EPS = 1e-6
CHUNK = 64
D_MODEL = 2048
D_INNER = 2 * D_MODEL
SSM_HEAD_DIM = 64
SSM_HEADS = D_INNER // SSM_HEAD_DIM
SSM_STATE = 128
SSM_GROUPS = 8
HEADS_PER_GROUP = SSM_HEADS // SSM_GROUPS
GROUP_WIDTH = D_INNER // SSM_GROUPS
CONV_W = 4
CONV_DIM = D_INNER + 2 * SSM_GROUPS * SSM_STATE
MLA_HEADS = 16
Q_RANK = 512
KV_RANK = 512
QK_NOPE = 128
QK_ROPE = 64
QK_DIM = QK_NOPE + QK_ROPE
V_DIM = 128
ROPE_THETA = 10000.0
ATTN_SCALE = QK_DIM ** -0.5
QK_PRESCALE = ATTN_SCALE * math.log2(math.e)
MOE_GROUPS = 4
EXPERTS_PER_GROUP = 8
N_EXPERTS = MOE_GROUPS * EXPERTS_PER_GROUP
TOP_K = 2
D_EXPERT = 512

LANES = 128
SUBLANES = 8
VMEM_LIMIT = 56 * 1024 * 1024
SSD_Q = 128
MOE_BM = 256
LAT_PAD = 640
SSM_IN_XW = 6400


def _cparams(sem):
    return pltpu.CompilerParams(dimension_semantics=sem, vmem_limit_bytes=VMEM_LIMIT)


def _sigmoid(v):
    return 1.0 / (1.0 + jnp.exp(-v))


def _silu(v):
    return v * _sigmoid(v)


def _softplus(v):
    return jnp.maximum(v, 0.0) + jnp.log1p(jnp.exp(-jnp.abs(v)))


def _rms_rows(x):
    return x * lax.rsqrt(jnp.mean(x * x, axis=-1, keepdims=True) + EPS)


def _mm_body(*refs, nx, nw, ne, no, prologue, epilogue, x_per_j, emit_xn, xn_emit, xn_store, precision):
    x_refs = refs[:nx]
    w_refs = refs[nx:nx + nw]
    e_refs = refs[nx + nw:nx + nw + ne]
    o_refs = refs[nx + nw + ne:nx + nw + ne + no]
    rest = refs[nx + nw + ne + no:]
    j = pl.program_id(1)
    if x_per_j:
        xn = prologue(*[r[...] for r in x_refs])
    else:
        xn_ref = rest[-1]

        @pl.when(j == 0)
        def _():
            v = prologue(*[r[...] for r in x_refs])
            xn_ref[...] = xn_store(v).astype(xn_ref.dtype)
            if emit_xn:
                rest[0][...] = xn_emit(v).astype(rest[0].dtype)

        xn = xn_ref[...]
    accs = [jnp.dot(xn, w[...], preferred_element_type=F32, precision=precision) for w in w_refs]
    outs = epilogue(accs, [e[...] for e in e_refs])
    for o_ref, o in zip(o_refs, outs):
        o_ref[...] = o.astype(o_ref.dtype)


def _fused_mm(x_args, w_args, e_args, out_defs, *, grid, prologue, epilogue, xn_shape=None,
              xn_dtype=BF16, x_per_j=False, xn_out=None, xn_emit=None, xn_store=None, precision=None, name=None):
    arrays = [a for a, _ in x_args + w_args + e_args]
    in_specs = [s for _, s in x_args + w_args + e_args]
    out_shape = [d for d, _ in out_defs]
    out_specs = [s for _, s in out_defs]
    emit_xn = xn_out is not None
    if emit_xn:
        out_shape.append(xn_out[0])
        out_specs.append(xn_out[1])
    scratch = [] if x_per_j else [pltpu.VMEM(xn_shape, xn_dtype)]
    body = functools.partial(
        _mm_body, nx=len(x_args), nw=len(w_args), ne=len(e_args), no=len(out_defs),
        prologue=prologue, epilogue=epilogue, x_per_j=x_per_j, emit_xn=emit_xn,
        xn_emit=xn_emit or _pro_cast, xn_store=xn_store or _pro_cast, precision=precision)
    return pl.pallas_call(
        body, grid=grid, in_specs=in_specs, out_specs=out_specs, out_shape=out_shape,
        scratch_shapes=scratch, compiler_params=_cparams(("parallel", "arbitrary")), name=name,
    )(*arrays)


def _pro_rms(x, g):
    return _rms_rows(x.astype(F32)) * g


def _pro_cast(x):
    return x


def _split3_bf16(v):
    hi = v.astype(BF16)
    r1 = v - hi.astype(F32)
    mid = r1.astype(BF16)
    return hi, mid, (r1 - mid.astype(F32)).astype(BF16)


def _split_hi_lo_hi(v):
    hi = v.astype(BF16)
    lo = (v - hi.astype(F32)).astype(BF16)
    return jnp.concatenate([hi, lo, hi], axis=1)


def _pack_bf16_pairs(v):
    k = v.shape[1] // 2
    lo = lax.bitcast_convert_type(v[:, :k].astype(BF16).astype(F32), jnp.uint32) >> 16
    hi = lax.bitcast_convert_type(v[:, k:].astype(BF16).astype(F32), jnp.uint32) & jnp.uint32(0xFFFF0000)
    return hi | lo


def _pro_gated(y, z, g):
    v = y.astype(F32) * _silu(z.astype(F32))
    parts = [_rms_rows(v[:, k * GROUP_WIDTH:(k + 1) * GROUP_WIDTH]) for k in range(SSM_GROUPS)]
    return jnp.concatenate(parts, axis=-1) * g


def _epi_id(accs, es):
    return accs


def _epi_kv_ones(accs, es):
    return [accs[0], jnp.concatenate([accs[1], jnp.ones_like(accs[1])], axis=1)]


def _epi_res(accs, es):
    return [es[0] + accs[0]]


def _epi_ple(accs, es):
    return [es[0] + _sigmoid(accs[0]) * accs[1]]


def _epi_rms_out(accs, es):
    return [_rms_rows(accs[0]) * es[0]]


def _epi_rot(accs, es):
    cos, sin = es
    return [accs[0] * cos + accs[1] * sin]


def _epi_route(accs, es):
    lg = accs[0] + es[0]
    lane = lax.broadcasted_iota(jnp.int32, lg.shape, 1).astype(F32)
    neg = -jnp.inf
    big = 1.0e4
    is_grp = lane < MOE_GROUPS
    gl = jnp.where(is_grp, lg, neg)
    mg = jnp.max(gl, axis=-1, keepdims=True)
    g_sel = jnp.min(jnp.where(gl == mg, lane, big), axis=-1, keepdims=True)
    p_sel = 1.0 / jnp.sum(jnp.where(is_grp, jnp.exp(gl - mg), 0.0), axis=-1, keepdims=True)
    lo = MOE_GROUPS + g_sel * EXPERTS_PER_GROUP
    in_grp = jnp.where(lane >= lo, jnp.where(lane < lo + EXPERTS_PER_GROUP, 1.0, 0.0), 0.0) > 0.5
    el = jnp.where(in_grp, lg, neg)
    v1 = jnp.max(el, axis=-1, keepdims=True)
    i1 = jnp.min(jnp.where(el == v1, lane, big), axis=-1, keepdims=True)
    el2 = jnp.where(lane == i1, neg, el)
    v2 = jnp.max(el2, axis=-1, keepdims=True)
    i2 = jnp.min(jnp.where(el2 == v2, jnp.where(lane == i1, big, lane), big), axis=-1, keepdims=True)
    e21 = jnp.exp(v2 - v1)
    g1 = p_sel / (1.0 + e21)
    g2 = p_sel * e21 / (1.0 + e21)
    idx = jnp.where(lane == 0.0, i1 - MOE_GROUPS, jnp.where(lane == 1.0, i2 - MOE_GROUPS, 0.0))
    gates = jnp.where(lane == 0.0, g1, jnp.where(lane == 1.0, g2, 0.0))
    return [idx.astype(jnp.int32), gates]


def _mm_rows(x_list, consts, w, extras, *, prologue, epilogue, out_dtype, tm, tn, name,
             xn_dtype=BF16, precision=None, emit_xn_dtype=None, e_consts=()):
    m = x_list[0].shape[0]
    k, n = w.shape
    grid = (m // tm, n // tn)
    x_args = [(x, pl.BlockSpec((tm, x.shape[1]), lambda i, j: (i, 0))) for x in x_list]
    x_args += [(c, pl.BlockSpec((1, c.shape[1]), lambda i, j: (0, 0))) for c in consts]
    w_args = [(w, pl.BlockSpec((k, tn), lambda i, j: (0, j)))]
    e_args = [(c, pl.BlockSpec((1, tn), lambda i, j: (0, j))) for c in e_consts]
    e_args += [(e, pl.BlockSpec((tm, tn), lambda i, j: (i, j))) for e in extras]
    out_defs = [(jax.ShapeDtypeStruct((m, n), out_dtype), pl.BlockSpec((tm, tn), lambda i, j: (i, j)))]
    xn_out = None
    if emit_xn_dtype is not None:
        xn_out = (jax.ShapeDtypeStruct((m, k), emit_xn_dtype), pl.BlockSpec((tm, k), lambda i, j: (i, 0)))
    return _fused_mm(x_args, w_args, e_args, out_defs, grid=grid, prologue=prologue, epilogue=epilogue,
                     xn_shape=(tm, k), xn_dtype=xn_dtype, xn_out=xn_out, precision=precision, name=name)


def _mm_res_body(*refs, x_split, nc, nl, nw, ne, lhs_of_w, res_from_x, prologue, epilogue, cw):
    i = pl.program_id(0)
    pos, xs = 0, []
    for first_tiles in x_split:
        if first_tiles is None:
            xs.append(refs[pos][...])
            pos += 1
        else:
            xs.append(jnp.where(i < first_tiles, refs[pos][...], refs[pos + 1][...]))
            pos += 2
    x0_ref = refs[0]
    c_refs = refs[pos:pos + nc]
    l_refs = refs[pos + nc:pos + nc + nl]
    w_refs = refs[pos + nc + nl:pos + nc + nl + nw]
    e_refs = refs[pos + nc + nl + nw:pos + nc + nl + nw + ne]
    o_ref = refs[pos + nc + nl + nw + ne]
    xn_ref = refs[-1]
    xn_ref[...] = prologue(*xs, *[r[...] for r in c_refs]).astype(xn_ref.dtype)
    lhs = [xn_ref] + list(l_refs)
    for c in range(o_ref.shape[1] // cw):
        sl = slice(c * cw, (c + 1) * cw)
        accs = [jnp.dot(lhs[li][...], w[:, sl], preferred_element_type=F32) for w, li in zip(w_refs, lhs_of_w)]
        es = ([x0_ref[:, sl]] if res_from_x else []) + [e[:, sl] for e in e_refs]
        o_ref[:, sl] = epilogue(accs, es)[0].astype(o_ref.dtype)


def _mm_resident(x_list, consts, lhs_list, ws, lhs_of_w, extras, *, prologue, epilogue, out_dtype, tm, name,
                 res_from_x=False, cw=512):
    rows = lambda x: x[0].shape[0] + x[1].shape[0] if isinstance(x, tuple) else x.shape[0]
    m = rows(x_list[0])
    k = ws[lhs_of_w.index(0)].shape[0]
    n = ws[0].shape[1]
    row = lambda a: pl.BlockSpec((tm, a.shape[1]), lambda i: (i, 0))
    whole = lambda a: pl.BlockSpec(a.shape, lambda i: (0, 0), pipeline_mode=pl.Buffered(1))
    arrays, in_specs, x_split = [], [], []
    for x in x_list:
        if isinstance(x, tuple):
            a, b = x
            na = a.shape[0] // tm
            assert a.shape[0] % tm == 0 and b.shape[0] % tm == 0
            arrays += [a, b]
            in_specs += [pl.BlockSpec((tm, a.shape[1]), lambda i, na=na: (jnp.minimum(i, na - 1), 0)),
                         pl.BlockSpec((tm, b.shape[1]), lambda i, na=na: (jnp.maximum(i - na, 0), 0))]
            x_split.append(na)
        else:
            arrays.append(x)
            in_specs.append(row(x))
            x_split.append(None)
    arrays += list(consts) + list(lhs_list) + list(ws) + list(extras)
    in_specs += ([whole(a) for a in consts] + [row(a) for a in lhs_list] + [whole(a) for a in ws]
                 + [row(a) for a in extras])
    body = functools.partial(
        _mm_res_body, x_split=tuple(x_split), nc=len(consts), nl=len(lhs_list), nw=len(ws), ne=len(extras),
        lhs_of_w=tuple(lhs_of_w), res_from_x=res_from_x, prologue=prologue, epilogue=epilogue, cw=cw)
    return pl.pallas_call(
        body, grid=(m // tm,), in_specs=in_specs,
        out_specs=pl.BlockSpec((tm, n), lambda i: (i, 0)),
        out_shape=jax.ShapeDtypeStruct((m, n), out_dtype),
        scratch_shapes=[pltpu.VMEM((tm, k), BF16)],
        compiler_params=_cparams(("parallel",)), name=name,
    )(*arrays)


def _heads_body(*refs, nw, ne, hpb, epilogue):
    x_ref = refs[0]
    w_refs = refs[1:1 + nw]
    e_refs = refs[1 + nw:1 + nw + ne]
    o_refs = refs[1 + nw + ne:]
    x = x_ref[...]
    es = [e[...] for e in e_refs]
    for hh in range(hpb):
        outs = epilogue([jnp.dot(x, w[hh], preferred_element_type=F32) for w in w_refs], es)
        for o_ref, o in zip(o_refs, outs):
            o_ref[hh] = o.astype(o_ref.dtype)


def _heads_mm(x, ws, es, out_dims, *, tm, hpb, epilogue, name):
    m, k = x.shape
    nh = ws[0].shape[0]
    in_specs = [pl.BlockSpec((tm, k), lambda i, j: (i, 0))]
    in_specs += [pl.BlockSpec((hpb, k, w.shape[2]), lambda i, j: (j, 0, 0)) for w in ws]
    in_specs += [pl.BlockSpec((tm, e.shape[1]), lambda i, j: (i, 0)) for e in es]
    return pl.pallas_call(
        functools.partial(_heads_body, nw=len(ws), ne=len(es), hpb=hpb, epilogue=epilogue),
        grid=(m // tm, nh // hpb), in_specs=in_specs,
        out_specs=[pl.BlockSpec((hpb, tm, n), lambda i, j: (j, i, 0)) for n in out_dims],
        out_shape=[jax.ShapeDtypeStruct((nh, m, n), BF16) for n in out_dims],
        compiler_params=_cparams(("parallel", "arbitrary")), name=name,
    )(x, *ws, *es)


def _conv_silu_chunk(ext, in_ref, st_ref, w_ref, b_ref, first, q):
    @pl.when(first)
    def _():
        ext[0:SUBLANES, :] = st_ref[...]

    ext[SUBLANES:SUBLANES + q, :] = in_ref[...]
    acc = b_ref[...]
    for k in range(CONV_W):
        off = SUBLANES - (CONV_W - 1) + k
        acc = acc + ext[off:off + q, :] * w_ref[k:k + 1, :]
    ext[0:SUBLANES, :] = ext[q:q + SUBLANES, :]
    return _silu(acc)


def _ssd_body(x_ref, b_ref, c_ref, stx_ref, stb_ref, stc_ref, wx_ref, wb_ref, wc_ref, bx_ref, bb_ref, bc_ref,
              dt_ref, dtT_ref, bias_ref, biasT_ref, a_ref, aT_ref, dsk_ref, s0_ref,
              y_ref, so_ref, s_scr, ext_x, ext_b, ext_c, *, q, gps):
    c = pl.program_id(2)
    first = c == 0

    @pl.when(first)
    def _():
        s_scr[...] = s0_ref[...]

    xs_all = _conv_silu_chunk(ext_x, x_ref, stx_ref, wx_ref, bx_ref, first, q)
    bm_all = _conv_silu_chunk(ext_b, b_ref, stb_ref, wb_ref, bb_ref, first, q).astype(BF16)
    cm_all = _conv_silu_chunk(ext_c, c_ref, stc_ref, wc_ref, bc_ref, first, q).astype(BF16)

    row = lax.broadcasted_iota(jnp.int32, (q, q), 0)
    col = lax.broadcasted_iota(jnp.int32, (q, q), 1)
    causal = row >= col
    tri = jnp.where(causal, 1.0, 0.0).astype(BF16)
    tri_t = jnp.where(row <= col, 1.0, 0.0).astype(BF16)
    left = lax.broadcasted_iota(jnp.int32, (q, LANES), 1) < SSM_HEAD_DIM
    gw = GROUP_WIDTH
    for gi in range(gps):
        dt = _softplus(dt_ref[gi] + bias_ref[gi])
        dta = dt * a_ref[gi]
        dta_t = _softplus(dtT_ref[gi] + biasT_ref[gi]) * aT_ref[gi]
        acum = sum(jnp.dot(tri, piece, preferred_element_type=F32) for piece in _split3_bf16(dta))
        acum_t = sum(jnp.dot(piece, tri_t, preferred_element_type=F32) for piece in _split3_bf16(dta_t))
        bm = bm_all[:, gi * SSM_STATE:(gi + 1) * SSM_STATE]
        cm = cm_all[:, gi * SSM_STATE:(gi + 1) * SSM_STATE]
        cb = lax.dot_general(cm, bm, (((1,), (1,)), ((), ())), preferred_element_type=F32)
        s_prev = s_scr[gi]
        y_off = jnp.dot(cm, s_prev.astype(BF16), preferred_element_type=F32)
        xw_parts, dec_parts = [], []
        for j in range(HEADS_PER_GROUP // 2):
            h0, h1 = 2 * j, 2 * j + 1
            sl = slice(j * LANES, (j + 1) * LANES)
            sg = slice(gi * gw + j * LANES, gi * gw + (j + 1) * LANES)
            col0, col1 = acum[:, h0:h0 + 1], acum[:, h1:h1 + 1]
            pa = jnp.where(left, col0, col1)
            dtp = jnp.where(left, dt[:, h0:h0 + 1], dt[:, h1:h1 + 1])
            x = xs_all[:, sg]
            xdt = x * dtp
            m0 = (jnp.exp2(jnp.where(causal, col0 - acum_t[h0:h0 + 1, :], -jnp.inf)) * cb).astype(BF16)
            m1 = (jnp.exp2(jnp.where(causal, col1 - acum_t[h1:h1 + 1, :], -jnp.inf)) * cb).astype(BF16)
            x_l = jnp.where(left, xdt, 0.0).astype(BF16)
            x_r = jnp.where(left, 0.0, xdt).astype(BF16)
            y_diag = (jnp.dot(m0, x_l, preferred_element_type=F32)
                      + jnp.dot(m1, x_r, preferred_element_type=F32))
            y = y_diag + y_off[:, sl] * jnp.exp2(pa) + x * dsk_ref[:, sg]
            y_ref[:, sg] = y.astype(y_ref.dtype)
            last = pa[q - 1:q, :]
            xw_parts.append((xdt * jnp.exp2(last - pa)).astype(BF16))
            dec_parts.append(jnp.exp2(last))
        xw = jnp.concatenate(xw_parts, axis=1)
        dec = jnp.concatenate(dec_parts, axis=1)
        s_new = s_prev * dec + lax.dot_general(bm, xw, (((0,), (0,)), ((), ())), preferred_element_type=F32)
        s_scr[gi] = s_new
        so_ref[gi] = s_new


def _ssd_scan(xbc3, batch0, nb, L, st, w_conv, b_conv, dtg, dtg_t, bias, bias_t, a, a_t, dsk, s0, *, q, gps=2):
    g_, r_, n_ = SSM_GROUPS, HEADS_PER_GROUP, SSM_STATE
    gw = GROUP_WIDTH
    xw_, bw_ = gps * gw, gps * n_
    b_off = D_INNER // bw_
    c_off = b_off + g_ // gps
    grid = (nb, g_ // gps, L // q)
    col_x = lambda b, g, c: (0, g)
    col_b = lambda b, g, c: (0, b_off + g)
    col_c = lambda b, g, c: (0, c_off + g)
    return pl.pallas_call(
        functools.partial(_ssd_body, q=q, gps=gps), grid=grid,
        in_specs=[
            pl.BlockSpec((None, q, xw_), lambda b, g, c: (batch0 + b, c, g)),
            pl.BlockSpec((None, q, bw_), lambda b, g, c: (batch0 + b, c, b_off + g)),
            pl.BlockSpec((None, q, bw_), lambda b, g, c: (batch0 + b, c, c_off + g)),
            pl.BlockSpec((None, SUBLANES, xw_), lambda b, g, c: (b, 0, g)),
            pl.BlockSpec((None, SUBLANES, bw_), lambda b, g, c: (b, 0, b_off + g)),
            pl.BlockSpec((None, SUBLANES, bw_), lambda b, g, c: (b, 0, c_off + g)),
            pl.BlockSpec((CONV_W, xw_), col_x),
            pl.BlockSpec((CONV_W, bw_), col_b),
            pl.BlockSpec((CONV_W, bw_), col_c),
            pl.BlockSpec((1, xw_), col_x),
            pl.BlockSpec((1, bw_), col_b),
            pl.BlockSpec((1, bw_), col_c),
            pl.BlockSpec((None, gps, q, r_), lambda b, g, c: (b, g, c, 0)),
            pl.BlockSpec((None, gps, r_, q), lambda b, g, c: (b, g, 0, c)),
            pl.BlockSpec((gps, 1, r_), lambda b, g, c: (g, 0, 0)),
            pl.BlockSpec((gps, r_, 1), lambda b, g, c: (g, 0, 0)),
            pl.BlockSpec((gps, 1, r_), lambda b, g, c: (g, 0, 0)),
            pl.BlockSpec((gps, r_, 1), lambda b, g, c: (g, 0, 0)),
            pl.BlockSpec((1, xw_), col_x),
            pl.BlockSpec((None, gps, n_, gw), lambda b, g, c: (b, g, 0, 0)),
        ],
        out_specs=[
            pl.BlockSpec((None, q, xw_), lambda b, g, c: (b, c, g)),
            pl.BlockSpec((None, gps, n_, gw), lambda b, g, c: (b, g, 0, 0)),
        ],
        out_shape=[jax.ShapeDtypeStruct((nb, L, D_INNER), BF16),
                   jax.ShapeDtypeStruct((nb, g_, n_, gw), F32)],
        scratch_shapes=[pltpu.VMEM((gps, n_, gw), F32), pltpu.VMEM((q + SUBLANES, xw_), F32),
                        pltpu.VMEM((q + SUBLANES, bw_), F32), pltpu.VMEM((q + SUBLANES, bw_), F32)],
        compiler_params=_cparams(("parallel", "parallel", "arbitrary")), name="ssd_scan",
    )(xbc3, xbc3, xbc3, st, st, st, w_conv, w_conv, w_conv, b_conv, b_conv, b_conv,
      dtg, dtg_t, bias, bias_t, a, a_t, dsk, s0)


def _gather_rows(tok_ref, base, src_hbm, dst, sem, n):
    def body(r, carry):
        t = tok_ref[base + r]
        pltpu.make_async_copy(src_hbm.at[pl.ds(t, 1)], dst.at[pl.ds(r, 1)], sem).start()
        return carry

    lax.fori_loop(0, n, body, 0)


def _gather_rows_unrolled(tok_ref, base, src_hbm, dst, sem, n):
    for r in range(n):
        t = tok_ref[base + r]
        pltpu.make_async_copy(src_hbm.at[pl.ds(t, 1)], dst.at[pl.ds(r, 1)], sem).start(priority=r % 2)


def _moe_body(blk_e_ref, nxt_e_ref, tok_ref, nused_ref, xt_hbm, wg_hbm, wu_hbm, wd_hbm, yb_ref,
              xbuf, xsem, wg32, wu32, wd32, wsem, wg_bf, wu_bf, wd_bf, xb, *, bm, layer):
    b = pl.program_id(0)
    slot = b % 2
    nused = nused_ref[0]
    e_cur = blk_e_ref[b]

    def weight_copies(e):
        return (pltpu.make_async_copy(wg_hbm.at[layer, e], wg32, wsem.at[0]),
                pltpu.make_async_copy(wu_hbm.at[layer, e], wu32, wsem.at[1]),
                pltpu.make_async_copy(wd_hbm.at[layer, e], wd32, wsem.at[2]))

    def gather_unrolled(base, slot_):
        for r in range(bm):
            t = tok_ref[base + r]
            pltpu.make_async_copy(xt_hbm.at[pl.ds(t, 1)], xbuf.at[slot_, pl.ds(r, 1)], xsem.at[slot_]).start()

    @pl.when(jnp.logical_and(b == 0, nused > 0))
    def _():
        for cp in weight_copies(e_cur):
            cp.start(priority=1)
        _gather_rows(tok_ref, 0, xt_hbm, xbuf.at[0], xsem.at[0], bm)

    prev_e = blk_e_ref[jnp.maximum(b - 1, 0)]
    changed = jnp.logical_or(b == 0, e_cur != prev_e)

    @pl.when(jnp.logical_and(changed, b < nused))
    def _():
        for cp in weight_copies(e_cur):
            cp.wait()
        wg_bf[...] = wg32[...].astype(BF16)
        wu_bf[...] = wu32[...].astype(BF16)
        wd_bf[...] = wd32[...].astype(BF16)
        nxt = nxt_e_ref[b]

        @pl.when(nxt >= 0)
        def _():
            for cp in weight_copies(nxt):
                cp.start(priority=1)

    def compute(prefetch):
        pltpu.make_async_copy(xbuf.at[slot], xbuf.at[slot], xsem.at[slot]).wait()
        w = xbuf[slot]
        half = w.shape[1]
        xb[:, :half] = lax.bitcast_convert_type(w << 16, F32).astype(BF16)
        xb[:, half:] = lax.bitcast_convert_type(w & jnp.uint32(0xFFFF0000), F32).astype(BF16)
        if prefetch:
            gather_unrolled((b + 1) * bm, 1 - slot)
        x = xb[...]
        g = jnp.dot(x, wg_bf[...], preferred_element_type=F32)
        u = jnp.dot(x, wu_bf[...], preferred_element_type=F32)
        hid = (_silu(g) * u).astype(BF16)
        yb_ref[...] = jnp.dot(hid, wd_bf[...], preferred_element_type=F32)

    @pl.when(b + 1 < nused)
    def _():
        compute(True)

    @pl.when(b + 1 == nused)
    def _():
        compute(False)

    @pl.when(b >= nused)
    def _():
        yb_ref[...] = jnp.zeros(yb_ref.shape, yb_ref.dtype)


def _moe_experts(xt, blk_e, nxt_e, row_token, nused, w_gate, w_up, w_down, *, bm, layer):
    nblk = blk_e.shape[0]
    d = w_gate.shape[2]
    de = w_gate.shape[3]
    assert xt.shape[1] * 2 == d and xt.dtype == jnp.uint32
    anyspec = pl.BlockSpec(memory_space=pl.ANY)
    gs = pltpu.PrefetchScalarGridSpec(
        num_scalar_prefetch=4, grid=(nblk,),
        in_specs=[anyspec, anyspec, anyspec, anyspec],
        out_specs=pl.BlockSpec((bm, d), lambda b, be, nx, tok, nu: (b, 0)),
        scratch_shapes=[
            pltpu.VMEM((2, bm, d // 2), jnp.uint32),
            pltpu.SemaphoreType.DMA((2,)),
            pltpu.VMEM((d, de), F32),
            pltpu.VMEM((d, de), F32),
            pltpu.VMEM((de, d), F32),
            pltpu.SemaphoreType.DMA((3,)),
            pltpu.VMEM((d, de), BF16),
            pltpu.VMEM((d, de), BF16),
            pltpu.VMEM((de, d), BF16),
            pltpu.VMEM((bm, d), BF16),
        ],
    )
    return pl.pallas_call(
        functools.partial(_moe_body, bm=bm, layer=layer), grid_spec=gs,
        out_shape=jax.ShapeDtypeStruct((nblk * bm, d), F32),
        compiler_params=_cparams(("arbitrary",)), name="moe_experts",
    )(blk_e, nxt_e, row_token, nused, xt, w_gate, w_up, w_down)


def _combine_body(dest_ref, h_ref, g_ref, yb_hbm, o_ref, buf, sem, *, tc, ntile):
    i = pl.program_id(0)
    slot = i % 2
    n = TOP_K * tc

    @pl.when(i == 0)
    def _():
        _gather_rows(dest_ref, 0, yb_hbm, buf.at[0], sem.at[0], n)

    @pl.when(i + 1 < ntile)
    def _():
        _gather_rows_unrolled(dest_ref, (i + 1) * n, yb_hbm, buf.at[1 - slot], sem.at[1 - slot], n)

    pltpu.make_async_copy(buf.at[slot], buf.at[slot], sem.at[slot]).wait()
    g = g_ref[...]
    o_ref[...] = (h_ref[...] + g[:, 0:1] * buf[slot, 0:tc, :] + g[:, 1:2] * buf[slot, tc:2 * tc, :])


def _moe_combine(h, yb, dest_tiles, gates, *, tc):
    t, d = h.shape
    ntile = t // tc
    gs = pltpu.PrefetchScalarGridSpec(
        num_scalar_prefetch=1, grid=(ntile,),
        in_specs=[
            pl.BlockSpec((tc, d), lambda i, dst: (i, 0)),
            pl.BlockSpec((tc, LANES), lambda i, dst: (i, 0)),
            pl.BlockSpec(memory_space=pl.ANY),
        ],
        out_specs=pl.BlockSpec((tc, d), lambda i, dst: (i, 0)),
        scratch_shapes=[pltpu.VMEM((2, TOP_K * tc, d), F32), pltpu.SemaphoreType.DMA((2,))],
    )
    return pl.pallas_call(
        functools.partial(_combine_body, tc=tc, ntile=ntile), grid_spec=gs,
        out_shape=jax.ShapeDtypeStruct((t, d), F32),
        compiler_params=_cparams(("arbitrary",)), name="moe_combine",
    )(dest_tiles, h, gates, yb)


def _moe_plan(idx, bm, tc):
    t = idx.shape[0]
    a = t * TOP_K
    e_ = N_EXPERTS
    flat_e = idx.reshape(a)
    onehot = (flat_e[:, None] == jnp.arange(e_, dtype=jnp.int32)[None, :]).astype(jnp.int32)
    csum = jnp.cumsum(onehot, axis=0)
    rank = jnp.sum(onehot * csum, axis=1) - 1
    counts = csum[-1]
    padded = (counts + bm - 1) // bm * bm
    pad_end = jnp.cumsum(padded)
    pad_start = pad_end - padded
    dest = jnp.sum(onehot * pad_start[None, :], axis=1) + rank
    nblk = -(-a // bm) + e_
    rows = nblk * bm
    row_token = jnp.zeros((rows,), jnp.int32).at[dest].set(jnp.arange(a, dtype=jnp.int32) // TOP_K)
    blk_row0 = jnp.arange(nblk, dtype=jnp.int32) * bm
    blk_e = jnp.minimum(jnp.sum((pad_end[None, :] <= blk_row0[:, None]).astype(jnp.int32), axis=1), e_ - 1)
    ids = jnp.arange(e_, dtype=jnp.int32)
    later_used = jnp.logical_and(ids[None, :] > ids[:, None], (counts > 0)[None, :])
    nxt_of_e = jnp.min(jnp.where(later_used, ids[None, :], e_), axis=1)
    nxt_of_e = jnp.where(nxt_of_e == e_, -1, nxt_of_e)
    nxt_e = jnp.sum(jnp.where(blk_e[:, None] == ids[None, :], nxt_of_e[None, :], 0), axis=1).astype(jnp.int32)
    nused = (pad_end[-1] // bm).astype(jnp.int32).reshape(1)
    dest_tiles = dest.reshape(t // tc, tc, TOP_K).transpose(0, 2, 1).reshape(a).astype(jnp.int32)
    return blk_e.astype(jnp.int32), nxt_e, row_token, nused, dest_tiles


def _flash_body(qi_ref, kj_ref, q_ref, k_ref, v_ref, o_ref, m_scr, acc_scr, *, hb):
    p_id = pl.program_id(1)
    qi = qi_ref[p_id]
    kj = kj_ref[p_id]
    tk = k_ref.shape[1]

    @pl.when(kj == 0)
    def _():
        m_scr[...] = jnp.full(m_scr.shape, -jnp.inf, F32)
        acc_scr[...] = jnp.zeros(acc_scr.shape, F32)

    def step(diag):
        for hh in range(hb):
            s = lax.dot_general(q_ref[hh], k_ref[hh], (((1,), (1,)), ((), ())), preferred_element_type=F32)
            if diag:
                r = lax.broadcasted_iota(jnp.int32, s.shape, 0)
                c = lax.broadcasted_iota(jnp.int32, s.shape, 1)
                s = jnp.where((c // CHUNK) <= (r // CHUNK), s, -jnp.inf)
            m_prev = m_scr[hh]
            m_next = jnp.maximum(m_prev, jnp.max(s, axis=1, keepdims=True))
            p = jnp.exp2(s - jnp.concatenate([m_next] * (tk // LANES), axis=1))
            alpha = jnp.exp2(m_prev - m_next)
            acc_scr[hh] = (jnp.concatenate([alpha, alpha], axis=1) * acc_scr[hh]
                           + jnp.dot(p.astype(BF16), v_ref[hh], preferred_element_type=F32))
            m_scr[hh] = m_next

    @pl.when(kj < qi)
    def _():
        step(False)

    @pl.when(kj == qi)
    def _():
        step(True)
        for hh in range(hb):
            a = acc_scr[hh]
            o_ref[:, hh * V_DIM:(hh + 1) * V_DIM] = (a[:, :V_DIM] / a[:, V_DIM:]).astype(o_ref.dtype)


def _flash_prompt(q, k, v, lp, *, tq, hb):
    assert V_DIM == LANES
    nh = q.shape[0]
    nq = lp // tq
    pairs = [(a, b) for a in range(nq) for b in range(a + 1)]
    qi_tab = jnp.asarray(np.array([p[0] for p in pairs], np.int32))
    kj_tab = jnp.asarray(np.array([p[1] for p in pairs], np.int32))
    gs = pltpu.PrefetchScalarGridSpec(
        num_scalar_prefetch=2, grid=(nh // hb, len(pairs)),
        in_specs=[
            pl.BlockSpec((hb, tq, QK_DIM), lambda h, p, qi, kj: (h, qi[p], 0)),
            pl.BlockSpec((hb, tq, QK_DIM), lambda h, p, qi, kj: (h, kj[p], 0)),
            pl.BlockSpec((hb, tq, 2 * V_DIM), lambda h, p, qi, kj: (h, kj[p], 0)),
        ],
        out_specs=pl.BlockSpec((tq, hb * V_DIM), lambda h, p, qi, kj: (qi[p], h)),
        scratch_shapes=[pltpu.VMEM((hb, tq, LANES), F32), pltpu.VMEM((hb, tq, 2 * V_DIM), F32)],
    )
    return pl.pallas_call(
        functools.partial(_flash_body, hb=hb), grid_spec=gs,
        out_shape=jax.ShapeDtypeStruct((lp, nh * V_DIM), BF16),
        compiler_params=_cparams(("parallel", "arbitrary")), name="flash_prompt",
    )(qi_tab, kj_tab, q, k, v)


def _attn_sample_body(q_ref, ckv_ref, kr_ref, ckv_new_ref, kr_new_ref, o_ref, *, ls, past):
    nh = q_ref.shape[0]
    q = q_ref[...].reshape(nh * ls, q_ref.shape[2])
    q_lat, q_rot = q[:, :KV_RANK], q[:, KV_RANK:KV_RANK + QK_ROPE]
    nt = (((1,), (1,)), ((), ()))

    def scores(lat, rot, pos0):
        s = (lax.dot_general(q_lat, lat, nt, preferred_element_type=F32)
             + lax.dot_general(q_rot, rot, nt, preferred_element_type=F32))
        r = lax.broadcasted_iota(jnp.int32, s.shape, 0)
        c = lax.broadcasted_iota(jnp.int32, s.shape, 1)
        q_pos = past + r % ls
        return jnp.where(((pos0 + c) // CHUNK) <= (q_pos // CHUNK), s, -jnp.inf)

    lat_p = ckv_ref[...].astype(BF16)
    lat_n = ckv_new_ref[...].astype(BF16)
    s_p = scores(lat_p, kr_ref[...].astype(BF16), 0)
    s_n = scores(lat_n, kr_new_ref[...].astype(BF16), past)
    m = jnp.maximum(jnp.max(s_p, axis=-1, keepdims=True), jnp.max(s_n, axis=-1, keepdims=True))
    p_p = jnp.exp2(s_p - m)
    p_n = jnp.exp2(s_n - m)
    l = jnp.sum(p_p, axis=-1, keepdims=True) + jnp.sum(p_n, axis=-1, keepdims=True)
    o = (jnp.dot(p_p.astype(BF16), lat_p, preferred_element_type=F32)
         + jnp.dot(p_n.astype(BF16), lat_n, preferred_element_type=F32)) / l
    o_ref[...] = o.reshape(nh, ls, KV_RANK).astype(o_ref.dtype)


def _attn_sample(q_abs, cache_ckv, cache_kr, ckv_new, kr_new, *, ls):
    nh, ts, dk = q_abs.shape
    nb, past, _ = cache_ckv.shape
    return pl.pallas_call(
        functools.partial(_attn_sample_body, ls=ls, past=past), grid=(nb,),
        in_specs=[
            pl.BlockSpec((nh, ls, dk), lambda b: (0, b, 0)),
            pl.BlockSpec((None, past, KV_RANK), lambda b: (b, 0, 0)),
            pl.BlockSpec((None, past, QK_ROPE), lambda b: (b, 0, 0)),
            pl.BlockSpec((None, ls, KV_RANK), lambda b: (b, 0, 0)),
            pl.BlockSpec((None, ls, QK_ROPE), lambda b: (b, 0, 0)),
        ],
        out_specs=pl.BlockSpec((nh, ls, KV_RANK), lambda b: (0, b, 0)),
        out_shape=jax.ShapeDtypeStruct((nh, ts, KV_RANK), BF16),
        compiler_params=_cparams(("parallel",)), name="attn_sample",
    )(q_abs, cache_ckv, cache_kr, ckv_new, kr_new)


def _norm_body(x_ref, g_ref, o_ref):
    o_ref[...] = _rms_rows(x_ref[...]) * g_ref[...]


def _final_norm(h, g, row0, nrows, *, tm):
    d = h.shape[1]
    blk0 = row0 // tm
    return pl.pallas_call(
        _norm_body, grid=(nrows // tm,),
        in_specs=[pl.BlockSpec((tm, d), lambda i: (blk0 + i, 0)), pl.BlockSpec((1, d), lambda i: (0, 0))],
        out_specs=pl.BlockSpec((tm, d), lambda i: (i, 0)),
        out_shape=jax.ShapeDtypeStruct((nrows, d), F32),
        compiler_params=_cparams(("parallel",)), name="final_norm",
    )(h, g)


def _rope_tables(pos):
    half = QK_ROPE // 2
    inv = ROPE_THETA ** (-np.arange(half, dtype=np.float64) / half)
    ang = np.asarray(pos, np.float64)[:, None] * inv[None, :]
    cos = np.concatenate([np.cos(ang), np.cos(ang)], axis=1)
    sin = np.concatenate([-np.sin(ang), np.sin(ang)], axis=1)
    return cos.astype(np.float32), sin.astype(np.float32)


def _swap_rope_halves(w):
    half = QK_ROPE // 2
    return jnp.concatenate([w[..., half:], w[..., :half]], axis=-1)


def _moe_layer(h, g_ffn, w_rg, b_rg, w_re, b_re, w_gate, w_up, w_down, *, tm, layer):
    t, d = h.shape
    npad = LANES - MOE_GROUPS - N_EXPERTS
    w_r = jnp.concatenate([w_rg, w_re, jnp.zeros((d, npad), F32)], axis=1)
    b_r = jnp.concatenate([b_rg, b_re, jnp.zeros((npad,), F32)]).reshape(1, LANES)
    grid = (t // tm, 1)
    x_args = [(h, pl.BlockSpec((tm, d), lambda i, j: (i, 0))),
              (g_ffn.reshape(1, d), pl.BlockSpec((1, d), lambda i, j: (0, 0)))]
    w_hi = w_r.astype(BF16)
    w_lo = (w_r - w_hi.astype(F32)).astype(BF16)
    w_args = [(jnp.concatenate([w_hi, w_hi, w_lo], axis=0), pl.BlockSpec((3 * d, LANES), lambda i, j: (0, 0)))]
    e_args = [(b_r, pl.BlockSpec((1, LANES), lambda i, j: (0, 0)))]
    tile = pl.BlockSpec((tm, LANES), lambda i, j: (i, 0))
    out_defs = [(jax.ShapeDtypeStruct((t, LANES), jnp.int32), tile),
                (jax.ShapeDtypeStruct((t, LANES), F32), tile)]
    xn_out = (jax.ShapeDtypeStruct((t, d // 2), jnp.uint32), pl.BlockSpec((tm, d // 2), lambda i, j: (i, 0)))
    idx_t, gates_t, xt = _fused_mm(x_args, w_args, e_args, out_defs, grid=grid, prologue=_pro_rms,
                                   epilogue=_epi_route, xn_shape=(tm, 3 * d), xn_dtype=BF16, xn_out=xn_out,
                                   xn_emit=_pack_bf16_pairs, xn_store=_split_hi_lo_hi, name="router")
    tc = min(tm, 128)
    blk_e, nxt_e, row_token, nused, dest_tiles = _moe_plan(idx_t[:, :TOP_K], MOE_BM, tc)
    yb = _moe_experts(xt, blk_e, nxt_e, row_token, nused, w_gate, w_up, w_down, bm=MOE_BM, layer=layer)
    return _moe_combine(h, yb, dest_tiles, gates_t, tc=tc)


def _ple_layer(h, p_rows, g_ple, w_gate, w_proj, *, tm, tn):
    t, d = h.shape
    return _mm_resident([h], [g_ple.reshape(1, d)], [p_rows.astype(BF16)], [w_gate.astype(BF16), w_proj.astype(BF16)],
                        [0, 1], [], prologue=_pro_rms, epilogue=_epi_ple, out_dtype=F32, tm=tm, name="ple_gate",
                        res_from_x=True)


def kernel(x_prompt, x_sample, state_conv, state_ssm, cache_kv_latent, cache_k_rope, p_prompt, p_sample,
           g_mix, w_ssm_in, w_conv, b_conv, dt_bias, a_log, d_skip, g_ssm_norm, w_ssm_out,
           g_kv_in, w_dkv, g_kv, w_uk, w_uv, w_dq, g_q, w_uq, w_o,
           g_ffn, w_router_grp, b_router_grp, w_router_exp, b_router_exp, w_exp_gate, w_exp_up, w_exp_down,
           g_ple, w_ple_gate, w_ple_proj, g_final):
    bp, lp_each, d = x_prompt.shape
    bs, ls, _ = x_sample.shape
    past = cache_kv_latent.shape[1]
    assert bp == 1 and w_ssm_in.shape[0] == 1 and w_dq.shape[0] == 1 and d == D_MODEL
    assert ls >= CONV_W - 1 and ls % SUBLANES == 0 and past % CHUNK == 0
    lp = bp * lp_each
    ts = bs * ls
    t = lp + ts
    tm = min(512, math.gcd(lp, ts))
    tn = 512
    g_, r_, n_, hd = SSM_GROUPS, HEADS_PER_GROUP, SSM_STATE, SSM_HEAD_DIM

    h = jnp.concatenate([x_prompt.reshape(lp, d), x_sample.reshape(ts, d)], axis=0)

    w_in = w_ssm_in[0]
    w_z = w_in[:, :D_INNER].astype(BF16)
    xw = SSM_IN_XW
    w_x = jnp.pad(w_in[:, D_INNER:], ((0, 0), (0, xw - CONV_DIM - SSM_HEADS))).astype(BF16)
    g0 = g_mix[0].reshape(1, d)
    tm_in = max(c for c in range(16, 1153, 16) if t % c == 0)
    mm_in = functools.partial(_mm_rows, [h], [g0], prologue=_pro_rms, epilogue=_epi_id, tm=tm_in)
    z = mm_in(w_z, [], out_dtype=BF16, tn=1024, name="ssm_in_z")[0]
    xbc = mm_in(w_x, [], out_dtype=F32, tn=xw // 5, name="ssm_in_xbc")[0]
    dt_raw = xbc[:, CONV_DIM:CONV_DIM + SSM_HEADS]

    wc = w_conv[0]
    bc = b_conv[0].reshape(1, CONV_DIM)
    st_p = jnp.zeros((1, SUBLANES, CONV_DIM), F32)
    st_s = jnp.pad(state_conv[0].astype(F32), ((0, 0), (SUBLANES - (CONV_W - 1), 0), (0, 0)))
    conv_p = xbc[lp - (CONV_W - 1):lp, :CONV_DIM].reshape(1, 1, CONV_W - 1, CONV_DIM)
    conv_s = xbc[lp:, :CONV_DIM].reshape(bs, ls, CONV_DIM)[:, ls - (CONV_W - 1):].reshape(
        1, bs, CONV_W - 1, CONV_DIM)

    bias = dt_bias[0].astype(F32).reshape(g_, 1, r_)
    a_neg = (-jnp.exp(a_log[0].astype(F32)) * math.log2(math.e)).reshape(g_, 1, r_)
    dsk = jnp.repeat(d_skip[0].astype(F32), hd).reshape(1, D_INNER)

    def dt_views(rows, nb, L):
        v = rows.reshape(nb, L, g_, r_).transpose(0, 2, 1, 3)
        return v, v.transpose(0, 1, 3, 2)

    def scan(xbc3, batch0, nb, L, st, rows, s0, q):
        dtg, dtg_t = dt_views(rows, nb, L)
        return _ssd_scan(xbc3, batch0, nb, L, st, wc, bc, dtg, dtg_t, bias, bias.transpose(0, 2, 1),
                         a_neg, a_neg.transpose(0, 2, 1), dsk, s0, q=q)

    def state_in(s):
        nb = s.shape[0]
        return s.astype(F32).reshape(nb, g_, r_ * hd, n_).transpose(0, 1, 3, 2)

    def state_out(s):
        nb = s.shape[0]
        return s.transpose(0, 1, 3, 2).reshape(1, nb, SSM_HEADS, hd, n_)

    y_p, s_p = scan(xbc.reshape(1, t, xw), 0, 1, lp, st_p, dt_raw[:lp],
                    jnp.zeros((1, g_, n_, r_ * hd), F32), min(SSD_Q, lp))
    y_s, s_s = scan(xbc.reshape(t // ls, ls, xw), lp // ls, bs, ls, st_s, dt_raw[lp:],
                    state_in(state_ssm[0]), ls)
    ssm_p, ssm_s = state_out(s_p), state_out(s_s)
    y_all = (y_p.reshape(lp, D_INNER), y_s.reshape(ts, D_INNER))

    h = _mm_resident([y_all, z], [g_ssm_norm[0].reshape(1, D_INNER)], [], [w_ssm_out[0].astype(BF16)], [0], [h],
                     prologue=_pro_gated, epilogue=_epi_res, out_dtype=F32, tm=tm // 2, name="ssm_out")

    def ffn_and_ple(h, i):
        h = _moe_layer(h, g_ffn[i], w_router_grp[i], b_router_grp[i], w_router_exp[i], b_router_exp[i],
                       w_exp_gate, w_exp_up, w_exp_down, tm=tm, layer=i)
        p_rows = jnp.concatenate([p_prompt[i].reshape(lp, -1), p_sample[i].reshape(ts, -1)], axis=0)
        return _ple_layer(h, p_rows, g_ple[i], w_ple_gate[i], w_ple_proj[i], tm=tm, tn=tn)

    h = ffn_and_ple(h, 0)

    pos = np.concatenate([np.arange(lp), np.tile(past + np.arange(ls), bs)])
    cos64, sin64 = _rope_tables(pos)
    gk = g_kv_in.reshape(1, d)
    ckv = _mm_rows([h], [gk], w_dkv[:, :KV_RANK].astype(BF16), [], prologue=_pro_rms, epilogue=_epi_rms_out,
                   out_dtype=F32, tm=tm, tn=KV_RANK, name="kv_latent", e_consts=[g_kv.reshape(1, KV_RANK)])[0]
    w_r = w_dkv[:, KV_RANK:]
    row64 = pl.BlockSpec((tm, QK_ROPE), lambda i, j: (i, 0))
    w64 = pl.BlockSpec((d, QK_ROPE), lambda i, j: (0, 0))
    kr = _fused_mm(
        [(h, pl.BlockSpec((tm, d), lambda i, j: (i, 0))), (gk, pl.BlockSpec((1, d), lambda i, j: (0, 0)))],
        [(w_r.astype(BF16), w64), (_swap_rope_halves(w_r).astype(BF16), w64)],
        [(jnp.asarray(cos64), row64), (jnp.asarray(sin64), row64)],
        [(jax.ShapeDtypeStruct((t, QK_ROPE), F32), row64)],
        grid=(t // tm, 1), prologue=_pro_rms, epilogue=_epi_rot, xn_shape=(tm, d), name="k_rope")[0]

    ql = _mm_rows([h], [g_mix[1].reshape(1, d)], w_dq[0].astype(BF16), [], prologue=_pro_rms,
                  epilogue=_epi_rms_out, out_dtype=BF16, tm=tm, tn=Q_RANK, name="q_latent",
                  e_consts=[g_q[0].reshape(1, Q_RANK)])[0]
    nh = MLA_HEADS
    w_q = w_uq[0].reshape(Q_RANK, nh, QK_DIM).transpose(1, 0, 2)
    w_q_rot = jnp.concatenate([jnp.zeros((nh, Q_RANK, QK_NOPE), F32), _swap_rope_halves(w_q[..., QK_NOPE:])], -1)
    cos_q = np.concatenate([np.ones((t, QK_NOPE), np.float32), cos64], axis=1) * np.float32(QK_PRESCALE)
    sin_q = np.concatenate([np.zeros((t, QK_NOPE), np.float32), sin64], axis=1) * np.float32(QK_PRESCALE)
    q_cat = _heads_mm(ql, [w_q.astype(BF16), w_q_rot.astype(BF16)], [jnp.asarray(cos_q), jnp.asarray(sin_q)],
                      [QK_DIM], tm=tm, hpb=8, epilogue=_epi_rot, name="q_heads")[0]

    lat_pad = LAT_PAD - KV_RANK - QK_ROPE
    ckr_p = jnp.concatenate([ckv[:lp], kr[:lp], jnp.zeros((lp, lat_pad), F32)], axis=1).astype(BF16)
    w_uk_h = w_uk.transpose(1, 0, 2)
    w_uv_h = w_uv.transpose(1, 0, 2)
    eye_r = jnp.broadcast_to(jnp.eye(QK_ROPE, dtype=F32), (nh, QK_ROPE, QK_ROPE))
    w_k_ext = jnp.concatenate([
        jnp.concatenate([w_uk_h, jnp.zeros((nh, KV_RANK, QK_ROPE), F32)], axis=2),
        jnp.concatenate([jnp.zeros((nh, QK_ROPE, QK_NOPE), F32), eye_r], axis=2),
        jnp.zeros((nh, lat_pad, QK_DIM), F32)], axis=1).astype(BF16)
    w_v_ext = jnp.concatenate([w_uv_h, jnp.zeros((nh, LAT_PAD - KV_RANK, V_DIM), F32)], axis=1).astype(BF16)
    k_cat, v_h = _heads_mm(ckr_p, [w_k_ext, w_v_ext], [], [QK_DIM, 2 * V_DIM], tm=tm, hpb=8,
                           epilogue=_epi_kv_ones, name="kv_heads")
    o_p = _flash_prompt(q_cat, k_cat, v_h, lp, tq=min(512, lp), hb=4)

    w_abs = jnp.concatenate([
        jnp.concatenate([w_uk_h.transpose(0, 2, 1), jnp.zeros((nh, QK_NOPE, LAT_PAD - KV_RANK), F32)], axis=2),
        jnp.concatenate([jnp.zeros((nh, QK_ROPE, KV_RANK), F32), eye_r,
                         jnp.zeros((nh, QK_ROPE, lat_pad), F32)], axis=2)], axis=1).astype(BF16)
    blk_s = lp // ts if lp % ts == 0 else None
    assert blk_s is not None
    q_abs = _fused_mm(
        [(q_cat, pl.BlockSpec((None, ts, QK_DIM), lambda i, j: (j, blk_s + i, 0)))],
        [(w_abs, pl.BlockSpec((None, QK_DIM, LAT_PAD), lambda i, j: (j, 0, 0)))],
        [],
        [(jax.ShapeDtypeStruct((nh, ts, LAT_PAD), BF16), pl.BlockSpec((None, ts, LAT_PAD), lambda i, j: (j, i, 0)))],
        grid=(1, nh), prologue=_pro_cast, epilogue=_epi_id, x_per_j=True, name="q_absorb")[0]
    o_lat = _attn_sample(q_abs, cache_kv_latent, cache_k_rope, ckv[lp:].reshape(bs, ls, KV_RANK),
                         kr[lp:].reshape(bs, ls, QK_ROPE), ls=ls)
    o_s = _fused_mm(
        [(o_lat, pl.BlockSpec((None, ts, KV_RANK), lambda i, j: (j, i, 0)))],
        [(w_uv_h.astype(BF16), pl.BlockSpec((None, KV_RANK, V_DIM), lambda i, j: (j, 0, 0)))],
        [],
        [(jax.ShapeDtypeStruct((ts, nh * V_DIM), BF16), pl.BlockSpec((ts, V_DIM), lambda i, j: (i, j)))],
        grid=(1, nh), prologue=_pro_cast, epilogue=_epi_id, x_per_j=True, name="v_absorb")[0]
    h = _mm_resident([(o_p, o_s)], [], [], [w_o[0].astype(BF16)], [0], [h], prologue=_pro_cast, epilogue=_epi_res,
                     out_dtype=F32, tm=tm, name="attn_out")
    h = ffn_and_ple(h, 1)

    gf = g_final.reshape(1, d)
    y_p_out = _final_norm(h, gf, 0, lp, tm=tm).reshape(bp, lp_each, d)
    y_s_out = _final_norm(h, gf, lp, ts, tm=tm).reshape(bs, ls, d)
    return (y_p_out, y_s_out, conv_p, ssm_p, ckv[:lp].reshape(bp, lp_each, KV_RANK),
            kr[:lp].reshape(bp, lp_each, QK_ROPE), conv_s, ssm_s, ckv[lp:].reshape(bs, ls, KV_RANK),
            kr[lp:].reshape(bs, ls, QK_ROPE))
```

```python
import functools
import math

import numpy as np
import jax
import jax.numpy as jnp
from jax import lax
from jax.experimental import pallas as pl
from jax.experimental.pallas import tpu as pltpu

F32 = jnp.float32
BF16 = jnp.bfloat16

EPS = 1e-6
CHUNK = 64
D_MODEL = 2048
D_INNER = 2 * D_MODEL
SSM_HEAD_DIM = 64
SSM_HEADS = D_INNER // SSM_HEAD_DIM
SSM_STATE = 128
SSM_GROUPS = 8
HEADS_PER_GROUP = SSM_HEADS // SSM_GROUPS
GROUP_WIDTH = D_INNER // SSM_GROUPS
CONV_W = 4
CONV_DIM = D_INNER + 2 * SSM_GROUPS * SSM_STATE
MLA_HEADS = 16
Q_RANK = 512
KV_RANK = 512
QK_NOPE = 128
QK_ROPE = 64
QK_DIM = QK_NOPE + QK_ROPE
V_DIM = 128
ROPE_THETA = 10000.0
ATTN_SCALE = QK_DIM ** -0.5
QK_PRESCALE = ATTN_SCALE * math.log2(math.e)
MOE_GROUPS = 4
EXPERTS_PER_GROUP = 8
N_EXPERTS = MOE_GROUPS * EXPERTS_PER_GROUP
TOP_K = 2
D_EXPERT = 512

LANES = 128
SUBLANES = 8
VMEM_LIMIT = 56 * 1024 * 1024
SSD_Q = 128
MOE_BM = 256
LAT_PAD = 640
SSM_IN_XW = 6400


def _cparams(sem):
    return pltpu.CompilerParams(dimension_semantics=sem, vmem_limit_bytes=VMEM_LIMIT)


def _sigmoid(v):
    return 1.0 / (1.0 + jnp.exp(-v))


def _silu(v):
    return v * _sigmoid(v)


def _softplus(v):
    return jnp.maximum(v, 0.0) + jnp.log1p(jnp.exp(-jnp.abs(v)))


def _rms_rows(x):
    return x * lax.rsqrt(jnp.mean(x * x, axis=-1, keepdims=True) + EPS)


def _mm_body(*refs, nx, nw, ne, no, prologue, epilogue, x_per_j, emit_xn, xn_emit, xn_store, precision):
    x_refs = refs[:nx]
    w_refs = refs[nx:nx + nw]
    e_refs = refs[nx + nw:nx + nw + ne]
    o_refs = refs[nx + nw + ne:nx + nw + ne + no]
    rest = refs[nx + nw + ne + no:]
    j = pl.program_id(1)
    if x_per_j:
        xn = prologue(*[r[...] for r in x_refs])
    else:
        xn_ref = rest[-1]

        @pl.when(j == 0)
        def _():
            v = prologue(*[r[...] for r in x_refs])
            xn_ref[...] = xn_store(v).astype(xn_ref.dtype)
            if emit_xn:
                xn_emit(v, rest[0])

        xn = xn_ref[...]
    accs = [jnp.dot(xn, w[...], preferred_element_type=F32, precision=precision) for w in w_refs]
    outs = epilogue(accs, [e[...] for e in e_refs])
    for o_ref, o in zip(o_refs, outs):
        o_ref[...] = o.astype(o_ref.dtype)


def _fused_mm(x_args, w_args, e_args, out_defs, *, grid, prologue, epilogue, xn_shape=None,
              xn_dtype=BF16, x_per_j=False, xn_out=None, xn_emit=None, xn_store=None, precision=None, name=None):
    arrays = [a for a, _ in x_args + w_args + e_args]
    in_specs = [s for _, s in x_args + w_args + e_args]
    out_shape = [d for d, _ in out_defs]
    out_specs = [s for _, s in out_defs]
    emit_xn = xn_out is not None
    if emit_xn:
        out_shape.append(xn_out[0])
        out_specs.append(xn_out[1])
    scratch = [] if x_per_j else [pltpu.VMEM(xn_shape, xn_dtype)]
    body = functools.partial(
        _mm_body, nx=len(x_args), nw=len(w_args), ne=len(e_args), no=len(out_defs),
        prologue=prologue, epilogue=epilogue, x_per_j=x_per_j, emit_xn=emit_xn,
        xn_emit=xn_emit or _emit_cast, xn_store=xn_store or _pro_cast, precision=precision)
    return pl.pallas_call(
        body, grid=grid, in_specs=in_specs, out_specs=out_specs, out_shape=out_shape,
        scratch_shapes=scratch, compiler_params=_cparams(("parallel", "arbitrary")), name=name,
    )(*arrays)


def _pro_rms(x, g):
    return _rms_rows(x.astype(F32)) * g


def _pro_cast(x):
    return x


def _emit_cast(v, o_ref):
    o_ref[...] = v.astype(o_ref.dtype)


def _emit_token_tiles(v, o_ref):
    w = _pack_bf16_pairs(v)
    m = w.shape[0]
    for s in range(SUBLANES):
        o_ref[pl.ds(s, m, stride=SUBLANES), :] = w[:, s * LANES:(s + 1) * LANES]


def _split3_bf16(v):
    hi = v.astype(BF16)
    r1 = v - hi.astype(F32)
    mid = r1.astype(BF16)
    return hi, mid, (r1 - mid.astype(F32)).astype(BF16)


def _split_hi_lo_hi(v):
    hi = v.astype(BF16)
    lo = (v - hi.astype(F32)).astype(BF16)
    return jnp.concatenate([hi, lo, hi], axis=1)


def _pack_bf16_pairs(v):
    k = v.shape[1] // 2
    lo = lax.bitcast_convert_type(v[:, :k].astype(BF16).astype(F32), jnp.uint32) >> 16
    hi = lax.bitcast_convert_type(v[:, k:].astype(BF16).astype(F32), jnp.uint32) & jnp.uint32(0xFFFF0000)
    return hi | lo


def _pro_gated(y, z, g):
    v = y.astype(F32) * _silu(z.astype(F32))
    parts = [_rms_rows(v[:, k * GROUP_WIDTH:(k + 1) * GROUP_WIDTH]) for k in range(SSM_GROUPS)]
    return jnp.concatenate(parts, axis=-1) * g


def _epi_id(accs, es):
    return accs


def _epi_kv_ones(accs, es):
    return [accs[0], jnp.concatenate([accs[1], jnp.ones_like(accs[1])], axis=1)]


def _epi_res(accs, es):
    return [es[0] + accs[0]]


def _epi_ple(accs, es):
    return [es[0] + _sigmoid(accs[0]) * accs[1]]


def _epi_rms_out(accs, es):
    return [_rms_rows(accs[0]) * es[0]]


def _epi_rot(accs, es):
    cos, sin = es
    return [accs[0] * cos + accs[1] * sin]


def _epi_route(accs, es):
    lg = accs[0] + es[0]
    lane = lax.broadcasted_iota(jnp.int32, lg.shape, 1).astype(F32)
    neg = -jnp.inf
    big = 1.0e4
    is_grp = lane < MOE_GROUPS
    gl = jnp.where(is_grp, lg, neg)
    mg = jnp.max(gl, axis=-1, keepdims=True)
    g_sel = jnp.min(jnp.where(gl == mg, lane, big), axis=-1, keepdims=True)
    p_sel = 1.0 / jnp.sum(jnp.where(is_grp, jnp.exp(gl - mg), 0.0), axis=-1, keepdims=True)
    lo = MOE_GROUPS + g_sel * EXPERTS_PER_GROUP
    in_grp = jnp.where(lane >= lo, jnp.where(lane < lo + EXPERTS_PER_GROUP, 1.0, 0.0), 0.0) > 0.5
    el = jnp.where(in_grp, lg, neg)
    v1 = jnp.max(el, axis=-1, keepdims=True)
    i1 = jnp.min(jnp.where(el == v1, lane, big), axis=-1, keepdims=True)
    el2 = jnp.where(lane == i1, neg, el)
    v2 = jnp.max(el2, axis=-1, keepdims=True)
    i2 = jnp.min(jnp.where(el2 == v2, jnp.where(lane == i1, big, lane), big), axis=-1, keepdims=True)
    e21 = jnp.exp(v2 - v1)
    g1 = p_sel / (1.0 + e21)
    g2 = p_sel * e21 / (1.0 + e21)
    idx = jnp.where(lane == 0.0, i1 - MOE_GROUPS, jnp.where(lane == 1.0, i2 - MOE_GROUPS, 0.0))
    gates = jnp.where(lane == 0.0, g1, jnp.where(lane == 1.0, g2, 0.0))
    return [idx.astype(jnp.int32), gates]


def _mm_rows(x_list, consts, w, extras, *, prologue, epilogue, out_dtype, tm, tn, name,
             xn_dtype=BF16, precision=None, emit_xn_dtype=None, e_consts=()):
    m = x_list[0].shape[0]
    k, n = w.shape
    grid = (m // tm, n // tn)
    x_args = [(x, pl.BlockSpec((tm, x.shape[1]), lambda i, j: (i, 0))) for x in x_list]
    x_args += [(c, pl.BlockSpec((1, c.shape[1]), lambda i, j: (0, 0))) for c in consts]
    w_args = [(w, pl.BlockSpec((k, tn), lambda i, j: (0, j)))]
    e_args = [(c, pl.BlockSpec((1, tn), lambda i, j: (0, j))) for c in e_consts]
    e_args += [(e, pl.BlockSpec((tm, tn), lambda i, j: (i, j))) for e in extras]
    out_defs = [(jax.ShapeDtypeStruct((m, n), out_dtype), pl.BlockSpec((tm, tn), lambda i, j: (i, j)))]
    xn_out = None
    if emit_xn_dtype is not None:
        xn_out = (jax.ShapeDtypeStruct((m, k), emit_xn_dtype), pl.BlockSpec((tm, k), lambda i, j: (i, 0)))
    return _fused_mm(x_args, w_args, e_args, out_defs, grid=grid, prologue=prologue, epilogue=epilogue,
                     xn_shape=(tm, k), xn_dtype=xn_dtype, xn_out=xn_out, precision=precision, name=name)


def _mm_res_body(*refs, x_split, nc, nl, nw, ne, lhs_of_w, res_from_x, prologue, epilogue, cw):
    i = pl.program_id(0)
    pos, xs = 0, []
    for first_tiles in x_split:
        if first_tiles is None:
            xs.append(refs[pos][...])
            pos += 1
        else:
            xs.append(jnp.where(i < first_tiles, refs[pos][...], refs[pos + 1][...]))
            pos += 2
    x0_ref = refs[0]
    c_refs = refs[pos:pos + nc]
    l_refs = refs[pos + nc:pos + nc + nl]
    w_refs = refs[pos + nc + nl:pos + nc + nl + nw]
    e_refs = refs[pos + nc + nl + nw:pos + nc + nl + nw + ne]
    o_ref = refs[pos + nc + nl + nw + ne]
    xn_ref = refs[-1]
    xn_ref[...] = prologue(*xs, *[r[...] for r in c_refs]).astype(xn_ref.dtype)
    lhs = [xn_ref] + list(l_refs)
    for c in range(o_ref.shape[1] // cw):
        sl = slice(c * cw, (c + 1) * cw)
        accs = [jnp.dot(lhs[li][...], w[:, sl], preferred_element_type=F32) for w, li in zip(w_refs, lhs_of_w)]
        es = ([x0_ref[:, sl]] if res_from_x else []) + [e[:, sl] for e in e_refs]
        o_ref[:, sl] = epilogue(accs, es)[0].astype(o_ref.dtype)


def _mm_resident(x_list, consts, lhs_list, ws, lhs_of_w, extras, *, prologue, epilogue, out_dtype, tm, name,
                 res_from_x=False, cw=512):
    rows = lambda x: x[0].shape[0] + x[1].shape[0] if isinstance(x, tuple) else x.shape[0]
    m = rows(x_list[0])
    k = ws[lhs_of_w.index(0)].shape[0]
    n = ws[0].shape[1]
    row = lambda a: pl.BlockSpec((tm, a.shape[1]), lambda i: (i, 0))
    whole = lambda a: pl.BlockSpec(a.shape, lambda i: (0, 0), pipeline_mode=pl.Buffered(1))
    arrays, in_specs, x_split = [], [], []
    for x in x_list:
        if isinstance(x, tuple):
            a, b = x
            na = a.shape[0] // tm
            assert a.shape[0] % tm == 0 and b.shape[0] % tm == 0
            arrays += [a, b]
            in_specs += [pl.BlockSpec((tm, a.shape[1]), lambda i, na=na: (jnp.minimum(i, na - 1), 0)),
                         pl.BlockSpec((tm, b.shape[1]), lambda i, na=na: (jnp.maximum(i - na, 0), 0))]
            x_split.append(na)
        else:
            arrays.append(x)
            in_specs.append(row(x))
            x_split.append(None)
    arrays += list(consts) + list(lhs_list) + list(ws) + list(extras)
    in_specs += ([whole(a) for a in consts] + [row(a) for a in lhs_list] + [whole(a) for a in ws]
                 + [row(a) for a in extras])
    body = functools.partial(
        _mm_res_body, x_split=tuple(x_split), nc=len(consts), nl=len(lhs_list), nw=len(ws), ne=len(extras),
        lhs_of_w=tuple(lhs_of_w), res_from_x=res_from_x, prologue=prologue, epilogue=epilogue, cw=cw)
    return pl.pallas_call(
        body, grid=(m // tm,), in_specs=in_specs,
        out_specs=pl.BlockSpec((tm, n), lambda i: (i, 0)),
        out_shape=jax.ShapeDtypeStruct((m, n), out_dtype),
        scratch_shapes=[pltpu.VMEM((tm, k), BF16)],
        compiler_params=_cparams(("parallel",)), name=name,
    )(*arrays)


def _heads_body(*refs, nw, ne, hpb, epilogue):
    x_ref = refs[0]
    w_refs = refs[1:1 + nw]
    e_refs = refs[1 + nw:1 + nw + ne]
    o_refs = refs[1 + nw + ne:]
    x = x_ref[...]
    es = [e[...] for e in e_refs]
    for hh in range(hpb):
        outs = epilogue([jnp.dot(x, w[hh], preferred_element_type=F32) for w in w_refs], es)
        for o_ref, o in zip(o_refs, outs):
            o_ref[hh] = o.astype(o_ref.dtype)


def _heads_mm(x, ws, es, out_dims, *, tm, hpb, epilogue, name):
    m, k = x.shape
    nh = ws[0].shape[0]
    in_specs = [pl.BlockSpec((tm, k), lambda i, j: (i, 0))]
    in_specs += [pl.BlockSpec((hpb, k, w.shape[2]), lambda i, j: (j, 0, 0)) for w in ws]
    in_specs += [pl.BlockSpec((tm, e.shape[1]), lambda i, j: (i, 0)) for e in es]
    return pl.pallas_call(
        functools.partial(_heads_body, nw=len(ws), ne=len(es), hpb=hpb, epilogue=epilogue),
        grid=(m // tm, nh // hpb), in_specs=in_specs,
        out_specs=[pl.BlockSpec((hpb, tm, n), lambda i, j: (j, i, 0)) for n in out_dims],
        out_shape=[jax.ShapeDtypeStruct((nh, m, n), BF16) for n in out_dims],
        compiler_params=_cparams(("parallel", "arbitrary")), name=name,
    )(x, *ws, *es)


def _conv_silu_chunk(ext, in_ref, st_ref, w_ref, b_ref, first, q):
    @pl.when(first)
    def _():
        ext[0:SUBLANES, :] = st_ref[...]

    ext[SUBLANES:SUBLANES + q, :] = in_ref[...]
    acc = b_ref[...]
    for k in range(CONV_W):
        off = SUBLANES - (CONV_W - 1) + k
        acc = acc + ext[off:off + q, :] * w_ref[k:k + 1, :]
    ext[0:SUBLANES, :] = ext[q:q + SUBLANES, :]
    return _silu(acc)


def _ssd_body(x_ref, b_ref, c_ref, stx_ref, stb_ref, stc_ref, wx_ref, wb_ref, wc_ref, bx_ref, bb_ref, bc_ref,
              dt_ref, dtT_ref, bias_ref, biasT_ref, a_ref, aT_ref, dsk_ref, s0_ref,
              y_ref, so_ref, s_scr, ext_x, ext_b, ext_c, *, q, gps):
    c = pl.program_id(2)
    first = c == 0

    @pl.when(first)
    def _():
        s_scr[...] = s0_ref[...]

    xs_all = _conv_silu_chunk(ext_x, x_ref, stx_ref, wx_ref, bx_ref, first, q)
    bm_all = _conv_silu_chunk(ext_b, b_ref, stb_ref, wb_ref, bb_ref, first, q).astype(BF16)
    cm_all = _conv_silu_chunk(ext_c, c_ref, stc_ref, wc_ref, bc_ref, first, q).astype(BF16)

    row = lax.broadcasted_iota(jnp.int32, (q, q), 0)
    col = lax.broadcasted_iota(jnp.int32, (q, q), 1)
    causal = row >= col
    tri = jnp.where(causal, 1.0, 0.0).astype(BF16)
    tri_t = jnp.where(row <= col, 1.0, 0.0).astype(BF16)
    left = lax.broadcasted_iota(jnp.int32, (q, LANES), 1) < SSM_HEAD_DIM
    gw = GROUP_WIDTH
    for gi in range(gps):
        dt = _softplus(dt_ref[gi] + bias_ref[gi])
        dta = dt * a_ref[gi]
        dta_t = _softplus(dtT_ref[gi] + biasT_ref[gi]) * aT_ref[gi]
        acum = sum(jnp.dot(tri, piece, preferred_element_type=F32) for piece in _split3_bf16(dta))
        acum_t = sum(jnp.dot(piece, tri_t, preferred_element_type=F32) for piece in _split3_bf16(dta_t))
        bm = bm_all[:, gi * SSM_STATE:(gi + 1) * SSM_STATE]
        cm = cm_all[:, gi * SSM_STATE:(gi + 1) * SSM_STATE]
        cb = lax.dot_general(cm, bm, (((1,), (1,)), ((), ())), preferred_element_type=F32)
        s_prev = s_scr[gi]
        y_off = jnp.dot(cm, s_prev.astype(BF16), preferred_element_type=F32)
        xw_parts, dec_parts = [], []
        for j in range(HEADS_PER_GROUP // 2):
            h0, h1 = 2 * j, 2 * j + 1
            sl = slice(j * LANES, (j + 1) * LANES)
            sg = slice(gi * gw + j * LANES, gi * gw + (j + 1) * LANES)
            col0, col1 = acum[:, h0:h0 + 1], acum[:, h1:h1 + 1]
            pa = jnp.where(left, col0, col1)
            dtp = jnp.where(left, dt[:, h0:h0 + 1], dt[:, h1:h1 + 1])
            x = xs_all[:, sg]
            xdt = x * dtp
            m0 = (jnp.exp2(jnp.where(causal, col0 - acum_t[h0:h0 + 1, :], -jnp.inf)) * cb).astype(BF16)
            m1 = (jnp.exp2(jnp.where(causal, col1 - acum_t[h1:h1 + 1, :], -jnp.inf)) * cb).astype(BF16)
            x_l = jnp.where(left, xdt, 0.0).astype(BF16)
            x_r = jnp.where(left, 0.0, xdt).astype(BF16)
            y_diag = (jnp.dot(m0, x_l, preferred_element_type=F32)
                      + jnp.dot(m1, x_r, preferred_element_type=F32))
            y = y_diag + y_off[:, sl] * jnp.exp2(pa) + x * dsk_ref[:, sg]
            y_ref[:, sg] = y.astype(y_ref.dtype)
            last = pa[q - 1:q, :]
            xw_parts.append((xdt * jnp.exp2(last - pa)).astype(BF16))
            dec_parts.append(jnp.exp2(last))
        xw = jnp.concatenate(xw_parts, axis=1)
        dec = jnp.concatenate(dec_parts, axis=1)
        s_new = s_prev * dec + lax.dot_general(bm, xw, (((0,), (0,)), ((), ())), preferred_element_type=F32)
        s_scr[gi] = s_new
        so_ref[gi] = s_new


def _ssd_scan(xbc3, batch0, nb, L, st, w_conv, b_conv, dtg, dtg_t, bias, bias_t, a, a_t, dsk, s0, *, q, gps=2):
    g_, r_, n_ = SSM_GROUPS, HEADS_PER_GROUP, SSM_STATE
    gw = GROUP_WIDTH
    xw_, bw_ = gps * gw, gps * n_
    b_off = D_INNER // bw_
    c_off = b_off + g_ // gps
    grid = (nb, g_ // gps, L // q)
    col_x = lambda b, g, c: (0, g)
    col_b = lambda b, g, c: (0, b_off + g)
    col_c = lambda b, g, c: (0, c_off + g)
    return pl.pallas_call(
        functools.partial(_ssd_body, q=q, gps=gps), grid=grid,
        in_specs=[
            pl.BlockSpec((None, q, xw_), lambda b, g, c: (batch0 + b, c, g)),
            pl.BlockSpec((None, q, bw_), lambda b, g, c: (batch0 + b, c, b_off + g)),
            pl.BlockSpec((None, q, bw_), lambda b, g, c: (batch0 + b, c, c_off + g)),
            pl.BlockSpec((None, SUBLANES, xw_), lambda b, g, c: (b, 0, g)),
            pl.BlockSpec((None, SUBLANES, bw_), lambda b, g, c: (b, 0, b_off + g)),
            pl.BlockSpec((None, SUBLANES, bw_), lambda b, g, c: (b, 0, c_off + g)),
            pl.BlockSpec((CONV_W, xw_), col_x),
            pl.BlockSpec((CONV_W, bw_), col_b),
            pl.BlockSpec((CONV_W, bw_), col_c),
            pl.BlockSpec((1, xw_), col_x),
            pl.BlockSpec((1, bw_), col_b),
            pl.BlockSpec((1, bw_), col_c),
            pl.BlockSpec((None, gps, q, r_), lambda b, g, c: (b, g, c, 0)),
            pl.BlockSpec((None, gps, r_, q), lambda b, g, c: (b, g, 0, c)),
            pl.BlockSpec((gps, 1, r_), lambda b, g, c: (g, 0, 0)),
            pl.BlockSpec((gps, r_, 1), lambda b, g, c: (g, 0, 0)),
            pl.BlockSpec((gps, 1, r_), lambda b, g, c: (g, 0, 0)),
            pl.BlockSpec((gps, r_, 1), lambda b, g, c: (g, 0, 0)),
            pl.BlockSpec((1, xw_), col_x),
            pl.BlockSpec((None, gps, n_, gw), lambda b, g, c: (b, g, 0, 0)),
        ],
        out_specs=[
            pl.BlockSpec((None, q, xw_), lambda b, g, c: (b, c, g)),
            pl.BlockSpec((None, gps, n_, gw), lambda b, g, c: (b, g, 0, 0)),
        ],
        out_shape=[jax.ShapeDtypeStruct((nb, L, D_INNER), BF16),
                   jax.ShapeDtypeStruct((nb, g_, n_, gw), F32)],
        scratch_shapes=[pltpu.VMEM((gps, n_, gw), F32), pltpu.VMEM((q + SUBLANES, xw_), F32),
                        pltpu.VMEM((q + SUBLANES, bw_), F32), pltpu.VMEM((q + SUBLANES, bw_), F32)],
        compiler_params=_cparams(("parallel", "parallel", "arbitrary")), name="ssd_scan",
    )(xbc3, xbc3, xbc3, st, st, st, w_conv, w_conv, w_conv, b_conv, b_conv, b_conv,
      dtg, dtg_t, bias, bias_t, a, a_t, dsk, s0)


def _gather_rows(tok_ref, base, src_hbm, dst, sem, n, rp=1):
    def body(r, carry):
        t = pl.multiple_of(tok_ref[base + r] * rp, rp)
        pltpu.make_async_copy(src_hbm.at[pl.ds(t, rp)], dst.at[pl.ds(pl.multiple_of(r * rp, rp), rp)], sem).start()
        return carry

    lax.fori_loop(0, n, body, 0)


def _gather_rows_unrolled(tok_ref, base, src_hbm, dst, sem, n):
    for r in range(n):
        t = tok_ref[base + r]
        pltpu.make_async_copy(src_hbm.at[pl.ds(t, 1)], dst.at[pl.ds(r, 1)], sem).start(priority=r % 2)


def _moe_body(blk_e_ref, nxt_e_ref, tok_ref, nused_ref, xt_hbm, wg_hbm, wu_hbm, wd_hbm, yb_ref,
              xbuf, xsem, wg32, wu32, wd32, wsem, wg_bf, wu_bf, wd_bf, xb, *, bm, layer):
    b = pl.program_id(0)
    slot = b % 2
    nused = nused_ref[0]
    e_cur = blk_e_ref[b]

    def weight_copies(e):
        return (pltpu.make_async_copy(wg_hbm.at[layer, e], wg32, wsem.at[0]),
                pltpu.make_async_copy(wu_hbm.at[layer, e], wu32, wsem.at[1]),
                pltpu.make_async_copy(wd_hbm.at[layer, e], wd32, wsem.at[2]))

    def gather_unrolled(base, slot_):
        for r in range(bm):
            t = pl.multiple_of(tok_ref[base + r] * SUBLANES, SUBLANES)
            pltpu.make_async_copy(xt_hbm.at[pl.ds(t, SUBLANES)], xbuf.at[slot_, pl.ds(r * SUBLANES, SUBLANES)],
                                  xsem.at[slot_]).start()

    @pl.when(jnp.logical_and(b == 0, nused > 0))
    def _():
        for cp in weight_copies(e_cur):
            cp.start(priority=1)
        _gather_rows(tok_ref, 0, xt_hbm, xbuf.at[0], xsem.at[0], bm, rp=SUBLANES)

    prev_e = blk_e_ref[jnp.maximum(b - 1, 0)]
    changed = jnp.logical_or(b == 0, e_cur != prev_e)

    @pl.when(jnp.logical_and(changed, b < nused))
    def _():
        for cp in weight_copies(e_cur):
            cp.wait()
        wg_bf[...] = wg32[...].astype(BF16)
        wu_bf[...] = wu32[...].astype(BF16)
        wd_bf[...] = wd32[...].astype(BF16)
        nxt = nxt_e_ref[b]

        @pl.when(nxt >= 0)
        def _():
            for cp in weight_copies(nxt):
                cp.start(priority=1)

    def compute(prefetch):
        pltpu.make_async_copy(xbuf.at[slot], xbuf.at[slot], xsem.at[slot]).wait()
        half = xb.shape[1] // 2
        for s in range(SUBLANES):
            w = xbuf[slot, pl.ds(s, bm, stride=SUBLANES), :]
            cols = slice(s * LANES, (s + 1) * LANES)
            xb[:, cols] = lax.bitcast_convert_type(w << 16, F32).astype(BF16)
            xb[:, half + s * LANES:half + (s + 1) * LANES] = lax.bitcast_convert_type(
                w & jnp.uint32(0xFFFF0000), F32).astype(BF16)
        if prefetch:
            gather_unrolled((b + 1) * bm, 1 - slot)
        x = xb[...]
        g = jnp.dot(x, wg_bf[...], preferred_element_type=F32)
        u = jnp.dot(x, wu_bf[...], preferred_element_type=F32)
        hid = (_silu(g) * u).astype(BF16)
        yb_ref[...] = jnp.dot(hid, wd_bf[...], preferred_element_type=F32)

    @pl.when(b + 1 < nused)
    def _():
        compute(True)

    @pl.when(b + 1 == nused)
    def _():
        compute(False)

    @pl.when(b >= nused)
    def _():
        yb_ref[...] = jnp.zeros(yb_ref.shape, yb_ref.dtype)


def _moe_experts(xt, blk_e, nxt_e, row_token, nused, w_gate, w_up, w_down, *, bm, layer):
    nblk = blk_e.shape[0]
    d = w_gate.shape[2]
    de = w_gate.shape[3]
    assert d == 2 * SUBLANES * LANES and xt.shape[1] == LANES and xt.dtype == jnp.uint32
    anyspec = pl.BlockSpec(memory_space=pl.ANY)
    gs = pltpu.PrefetchScalarGridSpec(
        num_scalar_prefetch=4, grid=(nblk,),
        in_specs=[anyspec, anyspec, anyspec, anyspec],
        out_specs=pl.BlockSpec((bm, d), lambda b, be, nx, tok, nu: (b, 0)),
        scratch_shapes=[
            pltpu.VMEM((2, bm * SUBLANES, LANES), jnp.uint32),
            pltpu.SemaphoreType.DMA((2,)),
            pltpu.VMEM((d, de), F32),
            pltpu.VMEM((d, de), F32),
            pltpu.VMEM((de, d), F32),
            pltpu.SemaphoreType.DMA((3,)),
            pltpu.VMEM((d, de), BF16),
            pltpu.VMEM((d, de), BF16),
            pltpu.VMEM((de, d), BF16),
            pltpu.VMEM((bm, d), BF16),
        ],
    )
    return pl.pallas_call(
        functools.partial(_moe_body, bm=bm, layer=layer), grid_spec=gs,
        out_shape=jax.ShapeDtypeStruct((nblk * bm, d), F32),
        compiler_params=_cparams(("arbitrary",)), name="moe_experts",
    )(blk_e, nxt_e, row_token, nused, xt, w_gate, w_up, w_down)


def _combine_body(dest_ref, h_ref, g_ref, yb_hbm, o_ref, buf, sem, *, tc, ntile):
    i = pl.program_id(0)
    slot = i % 2
    n = TOP_K * tc

    @pl.when(i == 0)
    def _():
        _gather_rows(dest_ref, 0, yb_hbm, buf.at[0], sem.at[0], n)

    @pl.when(i + 1 < ntile)
    def _():
        _gather_rows_unrolled(dest_ref, (i + 1) * n, yb_hbm, buf.at[1 - slot], sem.at[1 - slot], n)

    pltpu.make_async_copy(buf.at[slot], buf.at[slot], sem.at[slot]).wait()
    g = g_ref[...]
    o_ref[...] = (h_ref[...] + g[:, 0:1] * buf[slot, 0:tc, :] + g[:, 1:2] * buf[slot, tc:2 * tc, :])


def _moe_combine(h, yb, dest_tiles, gates, *, tc):
    t, d = h.shape
    ntile = t // tc
    gs = pltpu.PrefetchScalarGridSpec(
        num_scalar_prefetch=1, grid=(ntile,),
        in_specs=[
            pl.BlockSpec((tc, d), lambda i, dst: (i, 0)),
            pl.BlockSpec((tc, LANES), lambda i, dst: (i, 0)),
            pl.BlockSpec(memory_space=pl.ANY),
        ],
        out_specs=pl.BlockSpec((tc, d), lambda i, dst: (i, 0)),
        scratch_shapes=[pltpu.VMEM((2, TOP_K * tc, d), F32), pltpu.SemaphoreType.DMA((2,))],
    )
    return pl.pallas_call(
        functools.partial(_combine_body, tc=tc, ntile=ntile), grid_spec=gs,
        out_shape=jax.ShapeDtypeStruct((t, d), F32),
        compiler_params=_cparams(("arbitrary",)), name="moe_combine",
    )(dest_tiles, h, gates, yb)


def _moe_plan(idx, bm, tc):
    t = idx.shape[0]
    a = t * TOP_K
    e_ = N_EXPERTS
    flat_e = idx.reshape(a)
    onehot = (flat_e[:, None] == jnp.arange(e_, dtype=jnp.int32)[None, :]).astype(jnp.int32)
    csum = jnp.cumsum(onehot, axis=0)
    rank = jnp.sum(onehot * csum, axis=1) - 1
    counts = csum[-1]
    padded = (counts + bm - 1) // bm * bm
    pad_end = jnp.cumsum(padded)
    pad_start = pad_end - padded
    dest = jnp.sum(onehot * pad_start[None, :], axis=1) + rank
    nblk = -(-a // bm) + e_
    rows = nblk * bm
    row_token = jnp.zeros((rows,), jnp.int32).at[dest].set(jnp.arange(a, dtype=jnp.int32) // TOP_K)
    blk_row0 = jnp.arange(nblk, dtype=jnp.int32) * bm
    blk_e = jnp.minimum(jnp.sum((pad_end[None, :] <= blk_row0[:, None]).astype(jnp.int32), axis=1), e_ - 1)
    ids = jnp.arange(e_, dtype=jnp.int32)
    later_used = jnp.logical_and(ids[None, :] > ids[:, None], (counts > 0)[None, :])
    nxt_of_e = jnp.min(jnp.where(later_used, ids[None, :], e_), axis=1)
    nxt_of_e = jnp.where(nxt_of_e == e_, -1, nxt_of_e)
    nxt_e = jnp.sum(jnp.where(blk_e[:, None] == ids[None, :], nxt_of_e[None, :], 0), axis=1).astype(jnp.int32)
    nused = (pad_end[-1] // bm).astype(jnp.int32).reshape(1)
    dest_tiles = dest.reshape(t // tc, tc, TOP_K).transpose(0, 2, 1).reshape(a).astype(jnp.int32)
    return blk_e.astype(jnp.int32), nxt_e, row_token, nused, dest_tiles


def _flash_body(qi_ref, kj_ref, q_ref, k_ref, v_ref, o_ref, m_scr, acc_scr, *, hb):
    p_id = pl.program_id(1)
    qi = qi_ref[p_id]
    kj = kj_ref[p_id]
    tk = k_ref.shape[1]

    @pl.when(kj == 0)
    def _():
        m_scr[...] = jnp.full(m_scr.shape, -jnp.inf, F32)
        acc_scr[...] = jnp.zeros(acc_scr.shape, F32)

    def step(diag):
        for hh in range(hb):
            s = lax.dot_general(q_ref[hh], k_ref[hh], (((1,), (1,)), ((), ())), preferred_element_type=F32)
            if diag:
                r = lax.broadcasted_iota(jnp.int32, s.shape, 0)
                c = lax.broadcasted_iota(jnp.int32, s.shape, 1)
                s = jnp.where((c // CHUNK) <= (r // CHUNK), s, -jnp.inf)
            m_prev = m_scr[hh]
            m_next = jnp.maximum(m_prev, jnp.max(s, axis=1, keepdims=True))
            p = jnp.exp2(s - jnp.concatenate([m_next] * (tk // LANES), axis=1))
            alpha = jnp.exp2(m_prev - m_next)
            acc_scr[hh] = (jnp.concatenate([alpha, alpha], axis=1) * acc_scr[hh]
                           + jnp.dot(p.astype(BF16), v_ref[hh], preferred_element_type=F32))
            m_scr[hh] = m_next

    @pl.when(kj < qi)
    def _():
        step(False)

    @pl.when(kj == qi)
    def _():
        step(True)
        for hh in range(hb):
            a = acc_scr[hh]
            o_ref[:, hh * V_DIM:(hh + 1) * V_DIM] = (a[:, :V_DIM] / a[:, V_DIM:]).astype(o_ref.dtype)


def _flash_prompt(q, k, v, lp, *, tq, hb):
    assert V_DIM == LANES
    nh = q.shape[0]
    nq = lp // tq
    pairs = [(a, b) for a in range(nq) for b in range(a + 1)]
    qi_tab = jnp.asarray(np.array([p[0] for p in pairs], np.int32))
    kj_tab = jnp.asarray(np.array([p[1] for p in pairs], np.int32))
    gs = pltpu.PrefetchScalarGridSpec(
        num_scalar_prefetch=2, grid=(nh // hb, len(pairs)),
        in_specs=[
            pl.BlockSpec((hb, tq, QK_DIM), lambda h, p, qi, kj: (h, qi[p], 0)),
            pl.BlockSpec((hb, tq, QK_DIM), lambda h, p, qi, kj: (h, kj[p], 0)),
            pl.BlockSpec((hb, tq, 2 * V_DIM), lambda h, p, qi, kj: (h, kj[p], 0)),
        ],
        out_specs=pl.BlockSpec((tq, hb * V_DIM), lambda h, p, qi, kj: (qi[p], h)),
        scratch_shapes=[pltpu.VMEM((hb, tq, LANES), F32), pltpu.VMEM((hb, tq, 2 * V_DIM), F32)],
    )
    return pl.pallas_call(
        functools.partial(_flash_body, hb=hb), grid_spec=gs,
        out_shape=jax.ShapeDtypeStruct((lp, nh * V_DIM), BF16),
        compiler_params=_cparams(("parallel", "arbitrary")), name="flash_prompt",
    )(qi_tab, kj_tab, q, k, v)


def _attn_sample_body(q_ref, ckv_ref, kr_ref, ckv_new_ref, kr_new_ref, o_ref, *, ls, past):
    nh = q_ref.shape[0]
    q = q_ref[...].reshape(nh * ls, q_ref.shape[2])
    q_lat, q_rot = q[:, :KV_RANK], q[:, KV_RANK:KV_RANK + QK_ROPE]
    nt = (((1,), (1,)), ((), ()))

    def scores(lat, rot, pos0):
        s = (lax.dot_general(q_lat, lat, nt, preferred_element_type=F32)
             + lax.dot_general(q_rot, rot, nt, preferred_element_type=F32))
        r = lax.broadcasted_iota(jnp.int32, s.shape, 0)
        c = lax.broadcasted_iota(jnp.int32, s.shape, 1)
        q_pos = past + r % ls
        return jnp.where(((pos0 + c) // CHUNK) <= (q_pos // CHUNK), s, -jnp.inf)

    lat_p = ckv_ref[...].astype(BF16)
    lat_n = ckv_new_ref[...].astype(BF16)
    s_p = scores(lat_p, kr_ref[...].astype(BF16), 0)
    s_n = scores(lat_n, kr_new_ref[...].astype(BF16), past)
    m = jnp.maximum(jnp.max(s_p, axis=-1, keepdims=True), jnp.max(s_n, axis=-1, keepdims=True))
    p_p = jnp.exp2(s_p - m)
    p_n = jnp.exp2(s_n - m)
    l = jnp.sum(p_p, axis=-1, keepdims=True) + jnp.sum(p_n, axis=-1, keepdims=True)
    o = (jnp.dot(p_p.astype(BF16), lat_p, preferred_element_type=F32)
         + jnp.dot(p_n.astype(BF16), lat_n, preferred_element_type=F32)) / l
    o_ref[...] = o.reshape(nh, ls, KV_RANK).astype(o_ref.dtype)


def _attn_sample(q_abs, cache_ckv, cache_kr, ckv_new, kr_new, *, ls):
    nh, ts, dk = q_abs.shape
    nb, past, _ = cache_ckv.shape
    return pl.pallas_call(
        functools.partial(_attn_sample_body, ls=ls, past=past), grid=(nb,),
        in_specs=[
            pl.BlockSpec((nh, ls, dk), lambda b: (0, b, 0)),
            pl.BlockSpec((None, past, KV_RANK), lambda b: (b, 0, 0)),
            pl.BlockSpec((None, past, QK_ROPE), lambda b: (b, 0, 0)),
            pl.BlockSpec((None, ls, KV_RANK), lambda b: (b, 0, 0)),
            pl.BlockSpec((None, ls, QK_ROPE), lambda b: (b, 0, 0)),
        ],
        out_specs=pl.BlockSpec((nh, ls, KV_RANK), lambda b: (0, b, 0)),
        out_shape=jax.ShapeDtypeStruct((nh, ts, KV_RANK), BF16),
        compiler_params=_cparams(("parallel",)), name="attn_sample",
    )(q_abs, cache_ckv, cache_kr, ckv_new, kr_new)


def _norm_body(x_ref, g_ref, o_ref):
    o_ref[...] = _rms_rows(x_ref[...]) * g_ref[...]


def _final_norm(h, g, row0, nrows, *, tm):
    d = h.shape[1]
    blk0 = row0 // tm
    return pl.pallas_call(
        _norm_body, grid=(nrows // tm,),
        in_specs=[pl.BlockSpec((tm, d), lambda i: (blk0 + i, 0)), pl.BlockSpec((1, d), lambda i: (0, 0))],
        out_specs=pl.BlockSpec((tm, d), lambda i: (i, 0)),
        out_shape=jax.ShapeDtypeStruct((nrows, d), F32),
        compiler_params=_cparams(("parallel",)), name="final_norm",
    )(h, g)


def _rope_tables(pos):
    half = QK_ROPE // 2
    inv = ROPE_THETA ** (-np.arange(half, dtype=np.float64) / half)
    ang = np.asarray(pos, np.float64)[:, None] * inv[None, :]
    cos = np.concatenate([np.cos(ang), np.cos(ang)], axis=1)
    sin = np.concatenate([-np.sin(ang), np.sin(ang)], axis=1)
    return cos.astype(np.float32), sin.astype(np.float32)


def _swap_rope_halves(w):
    half = QK_ROPE // 2
    return jnp.concatenate([w[..., half:], w[..., :half]], axis=-1)


def _moe_layer(h, g_ffn, w_rg, b_rg, w_re, b_re, w_gate, w_up, w_down, *, tm, layer):
    t, d = h.shape
    npad = LANES - MOE_GROUPS - N_EXPERTS
    w_r = jnp.concatenate([w_rg, w_re, jnp.zeros((d, npad), F32)], axis=1)
    b_r = jnp.concatenate([b_rg, b_re, jnp.zeros((npad,), F32)]).reshape(1, LANES)
    grid = (t // tm, 1)
    x_args = [(h, pl.BlockSpec((tm, d), lambda i, j: (i, 0))),
              (g_ffn.reshape(1, d), pl.BlockSpec((1, d), lambda i, j: (0, 0)))]
    w_hi = w_r.astype(BF16)
    w_lo = (w_r - w_hi.astype(F32)).astype(BF16)
    w_args = [(jnp.concatenate([w_hi, w_hi, w_lo], axis=0), pl.BlockSpec((3 * d, LANES), lambda i, j: (0, 0)))]
    e_args = [(b_r, pl.BlockSpec((1, LANES), lambda i, j: (0, 0)))]
    tile = pl.BlockSpec((tm, LANES), lambda i, j: (i, 0))
    out_defs = [(jax.ShapeDtypeStruct((t, LANES), jnp.int32), tile),
                (jax.ShapeDtypeStruct((t, LANES), F32), tile)]
    assert d == 2 * SUBLANES * LANES
    xn_out = (jax.ShapeDtypeStruct((t * SUBLANES, LANES), jnp.uint32),
              pl.BlockSpec((tm * SUBLANES, LANES), lambda i, j: (i, 0)))
    idx_t, gates_t, xt = _fused_mm(x_args, w_args, e_args, out_defs, grid=grid, prologue=_pro_rms,
                                   epilogue=_epi_route, xn_shape=(tm, 3 * d), xn_dtype=BF16, xn_out=xn_out,
                                   xn_emit=_emit_token_tiles, xn_store=_split_hi_lo_hi, name="router")
    tc = min(tm, 128)
    blk_e, nxt_e, row_token, nused, dest_tiles = _moe_plan(idx_t[:, :TOP_K], MOE_BM, tc)
    yb = _moe_experts(xt, blk_e, nxt_e, row_token, nused, w_gate, w_up, w_down, bm=MOE_BM, layer=layer)
    return _moe_combine(h, yb, dest_tiles, gates_t, tc=tc)


def _ple_layer(h, p_rows, g_ple, w_gate, w_proj, *, tm, tn):
    t, d = h.shape
    return _mm_resident([h], [g_ple.reshape(1, d)], [p_rows.astype(BF16)], [w_gate.astype(BF16), w_proj.astype(BF16)],
                        [0, 1], [], prologue=_pro_rms, epilogue=_epi_ple, out_dtype=F32, tm=tm, name="ple_gate",
                        res_from_x=True)


def kernel(x_prompt, x_sample, state_conv, state_ssm, cache_kv_latent, cache_k_rope, p_prompt, p_sample,
           g_mix, w_ssm_in, w_conv, b_conv, dt_bias, a_log, d_skip, g_ssm_norm, w_ssm_out,
           g_kv_in, w_dkv, g_kv, w_uk, w_uv, w_dq, g_q, w_uq, w_o,
           g_ffn, w_router_grp, b_router_grp, w_router_exp, b_router_exp, w_exp_gate, w_exp_up, w_exp_down,
           g_ple, w_ple_gate, w_ple_proj, g_final):
    bp, lp_each, d = x_prompt.shape
    bs, ls, _ = x_sample.shape
    past = cache_kv_latent.shape[1]
    assert bp == 1 and w_ssm_in.shape[0] == 1 and w_dq.shape[0] == 1 and d == D_MODEL
    assert ls >= CONV_W - 1 and ls % SUBLANES == 0 and past % CHUNK == 0
    lp = bp * lp_each
    ts = bs * ls
    t = lp + ts
    tm = min(512, math.gcd(lp, ts))
    tn = 512
    g_, r_, n_, hd = SSM_GROUPS, HEADS_PER_GROUP, SSM_STATE, SSM_HEAD_DIM

    h = jnp.concatenate([x_prompt.reshape(lp, d), x_sample.reshape(ts, d)], axis=0)

    w_in = w_ssm_in[0]
    w_z = w_in[:, :D_INNER].astype(BF16)
    xw = SSM_IN_XW
    w_x = jnp.pad(w_in[:, D_INNER:], ((0, 0), (0, xw - CONV_DIM - SSM_HEADS))).astype(BF16)
    g0 = g_mix[0].reshape(1, d)
    tm_in = max(c for c in range(16, 1153, 16) if t % c == 0)
    mm_in = functools.partial(_mm_rows, [h], [g0], prologue=_pro_rms, epilogue=_epi_id, tm=tm_in)
    z = mm_in(w_z, [], out_dtype=BF16, tn=1024, name="ssm_in_z")[0]
    xbc = mm_in(w_x, [], out_dtype=F32, tn=xw // 5, name="ssm_in_xbc")[0]
    dt_raw = xbc[:, CONV_DIM:CONV_DIM + SSM_HEADS]

    wc = w_conv[0]
    bc = b_conv[0].reshape(1, CONV_DIM)
    st_p = jnp.zeros((1, SUBLANES, CONV_DIM), F32)
    st_s = jnp.pad(state_conv[0].astype(F32), ((0, 0), (SUBLANES - (CONV_W - 1), 0), (0, 0)))
    conv_p = xbc[lp - (CONV_W - 1):lp, :CONV_DIM].reshape(1, 1, CONV_W - 1, CONV_DIM)
    conv_s = xbc[lp:, :CONV_DIM].reshape(bs, ls, CONV_DIM)[:, ls - (CONV_W - 1):].reshape(
        1, bs, CONV_W - 1, CONV_DIM)

    bias = dt_bias[0].astype(F32).reshape(g_, 1, r_)
    a_neg = (-jnp.exp(a_log[0].astype(F32)) * math.log2(math.e)).reshape(g_, 1, r_)
    dsk = jnp.repeat(d_skip[0].astype(F32), hd).reshape(1, D_INNER)

    def dt_views(rows, nb, L):
        v = rows.reshape(nb, L, g_, r_).transpose(0, 2, 1, 3)
        return v, v.transpose(0, 1, 3, 2)

    def scan(xbc3, batch0, nb, L, st, rows, s0, q):
        dtg, dtg_t = dt_views(rows, nb, L)
        return _ssd_scan(xbc3, batch0, nb, L, st, wc, bc, dtg, dtg_t, bias, bias.transpose(0, 2, 1),
                         a_neg, a_neg.transpose(0, 2, 1), dsk, s0, q=q)

    def state_in(s):
        nb = s.shape[0]
        return s.astype(F32).reshape(nb, g_, r_ * hd, n_).transpose(0, 1, 3, 2)

    def state_out(s):
        nb = s.shape[0]
        return s.transpose(0, 1, 3, 2).reshape(1, nb, SSM_HEADS, hd, n_)

    y_p, s_p = scan(xbc.reshape(1, t, xw), 0, 1, lp, st_p, dt_raw[:lp],
                    jnp.zeros((1, g_, n_, r_ * hd), F32), min(SSD_Q, lp))
    y_s, s_s = scan(xbc.reshape(t // ls, ls, xw), lp // ls, bs, ls, st_s, dt_raw[lp:],
                    state_in(state_ssm[0]), ls)
    ssm_p, ssm_s = state_out(s_p), state_out(s_s)
    y_all = (y_p.reshape(lp, D_INNER), y_s.reshape(ts, D_INNER))

    h = _mm_resident([y_all, z], [g_ssm_norm[0].reshape(1, D_INNER)], [], [w_ssm_out[0].astype(BF16)], [0], [h],
                     prologue=_pro_gated, epilogue=_epi_res, out_dtype=F32, tm=tm // 2, name="ssm_out")

    def ffn_and_ple(h, i):
        h = _moe_layer(h, g_ffn[i], w_router_grp[i], b_router_grp[i], w_router_exp[i], b_router_exp[i],
                       w_exp_gate, w_exp_up, w_exp_down, tm=tm, layer=i)
        p_rows = jnp.concatenate([p_prompt[i].reshape(lp, -1), p_sample[i].reshape(ts, -1)], axis=0)
        return _ple_layer(h, p_rows, g_ple[i], w_ple_gate[i], w_ple_proj[i], tm=tm, tn=tn)

    h = ffn_and_ple(h, 0)

    pos = np.concatenate([np.arange(lp), np.tile(past + np.arange(ls), bs)])
    cos64, sin64 = _rope_tables(pos)
    gk = g_kv_in.reshape(1, d)
    ckv = _mm_rows([h], [gk], w_dkv[:, :KV_RANK].astype(BF16), [], prologue=_pro_rms, epilogue=_epi_rms_out,
                   out_dtype=F32, tm=tm, tn=KV_RANK, name="kv_latent", e_consts=[g_kv.reshape(1, KV_RANK)])[0]
    w_r = w_dkv[:, KV_RANK:]
    row64 = pl.BlockSpec((tm, QK_ROPE), lambda i, j: (i, 0))
    w64 = pl.BlockSpec((d, QK_ROPE), lambda i, j: (0, 0))
    kr = _fused_mm(
        [(h, pl.BlockSpec((tm, d), lambda i, j: (i, 0))), (gk, pl.BlockSpec((1, d), lambda i, j: (0, 0)))],
        [(w_r.astype(BF16), w64), (_swap_rope_halves(w_r).astype(BF16), w64)],
        [(jnp.asarray(cos64), row64), (jnp.asarray(sin64), row64)],
        [(jax.ShapeDtypeStruct((t, QK_ROPE), F32), row64)],
        grid=(t // tm, 1), prologue=_pro_rms, epilogue=_epi_rot, xn_shape=(tm, d), name="k_rope")[0]

    ql = _mm_rows([h], [g_mix[1].reshape(1, d)], w_dq[0].astype(BF16), [], prologue=_pro_rms,
                  epilogue=_epi_rms_out, out_dtype=BF16, tm=tm, tn=Q_RANK, name="q_latent",
                  e_consts=[g_q[0].reshape(1, Q_RANK)])[0]
    nh = MLA_HEADS
    w_q = w_uq[0].reshape(Q_RANK, nh, QK_DIM).transpose(1, 0, 2)
    w_q_rot = jnp.concatenate([jnp.zeros((nh, Q_RANK, QK_NOPE), F32), _swap_rope_halves(w_q[..., QK_NOPE:])], -1)
    cos_q = np.concatenate([np.ones((t, QK_NOPE), np.float32), cos64], axis=1) * np.float32(QK_PRESCALE)
    sin_q = np.concatenate([np.zeros((t, QK_NOPE), np.float32), sin64], axis=1) * np.float32(QK_PRESCALE)
    q_cat = _heads_mm(ql, [w_q.astype(BF16), w_q_rot.astype(BF16)], [jnp.asarray(cos_q), jnp.asarray(sin_q)],
                      [QK_DIM], tm=tm, hpb=8, epilogue=_epi_rot, name="q_heads")[0]

    lat_pad = LAT_PAD - KV_RANK - QK_ROPE
    ckr_p = jnp.concatenate([ckv[:lp], kr[:lp], jnp.zeros((lp, lat_pad), F32)], axis=1).astype(BF16)
    w_uk_h = w_uk.transpose(1, 0, 2)
    w_uv_h = w_uv.transpose(1, 0, 2)
    eye_r = jnp.broadcast_to(jnp.eye(QK_ROPE, dtype=F32), (nh, QK_ROPE, QK_ROPE))
    w_k_ext = jnp.concatenate([
        jnp.concatenate([w_uk_h, jnp.zeros((nh, KV_RANK, QK_ROPE), F32)], axis=2),
        jnp.concatenate([jnp.zeros((nh, QK_ROPE, QK_NOPE), F32), eye_r], axis=2),
        jnp.zeros((nh, lat_pad, QK_DIM), F32)], axis=1).astype(BF16)
    w_v_ext = jnp.concatenate([w_uv_h, jnp.zeros((nh, LAT_PAD - KV_RANK, V_DIM), F32)], axis=1).astype(BF16)
    k_cat, v_h = _heads_mm(ckr_p, [w_k_ext, w_v_ext], [], [QK_DIM, 2 * V_DIM], tm=tm, hpb=8,
                           epilogue=_epi_kv_ones, name="kv_heads")
    o_p = _flash_prompt(q_cat, k_cat, v_h, lp, tq=min(512, lp), hb=4)

    w_abs = jnp.concatenate([
        jnp.concatenate([w_uk_h.transpose(0, 2, 1), jnp.zeros((nh, QK_NOPE, LAT_PAD - KV_RANK), F32)], axis=2),
        jnp.concatenate([jnp.zeros((nh, QK_ROPE, KV_RANK), F32), eye_r,
                         jnp.zeros((nh, QK_ROPE, lat_pad), F32)], axis=2)], axis=1).astype(BF16)
    blk_s = lp // ts if lp % ts == 0 else None
    assert blk_s is not None
    q_abs = _fused_mm(
        [(q_cat, pl.BlockSpec((None, ts, QK_DIM), lambda i, j: (j, blk_s + i, 0)))],
        [(w_abs, pl.BlockSpec((None, QK_DIM, LAT_PAD), lambda i, j: (j, 0, 0)))],
        [],
        [(jax.ShapeDtypeStruct((nh, ts, LAT_PAD), BF16), pl.BlockSpec((None, ts, LAT_PAD), lambda i, j: (j, i, 0)))],
        grid=(1, nh), prologue=_pro_cast, epilogue=_epi_id, x_per_j=True, name="q_absorb")[0]
    o_lat = _attn_sample(q_abs, cache_kv_latent, cache_k_rope, ckv[lp:].reshape(bs, ls, KV_RANK),
                         kr[lp:].reshape(bs, ls, QK_ROPE), ls=ls)
    o_s = _fused_mm(
        [(o_lat, pl.BlockSpec((None, ts, KV_RANK), lambda i, j: (j, i, 0)))],
        [(w_uv_h.astype(BF16), pl.BlockSpec((None, KV_RANK, V_DIM), lambda i, j: (j, 0, 0)))],
        [],
        [(jax.ShapeDtypeStruct((ts, nh * V_DIM), BF16), pl.BlockSpec((ts, V_DIM), lambda i, j: (i, j)))],
        grid=(1, nh), prologue=_pro_cast, epilogue=_epi_id, x_per_j=True, name="v_absorb")[0]
    h = _mm_resident([(o_p, o_s)], [], [], [w_o[0].astype(BF16)], [0], [h], prologue=_pro_cast, epilogue=_epi_res,
                     out_dtype=F32, tm=tm, name="attn_out")
    h = ffn_and_ple(h, 1)

    gf = g_final.reshape(1, d)
    y_p_out = _final_norm(h, gf, 0, lp, tm=tm).reshape(bp, lp_each, d)
    y_s_out = _final_norm(h, gf, lp, ts, tm=tm).reshape(bs, ls, d)
    return (y_p_out, y_s_out, conv_p, ssm_p, ckv[:lp].reshape(bp, lp_each, KV_RANK),
            kr[:lp].reshape(bp, lp_each, QK_ROPE), conv_s, ssm_s, ckv[lp:].reshape(bs, ls, KV_RANK),
            kr[lp:].reshape(bs, ls, QK_ROPE))
```

```python
import functools
import math

import numpy as np
import jax
import jax.numpy as jnp
from jax import lax
from jax.experimental import pallas as pl
from jax.experimental.pallas import tpu as pltpu

F32 = jnp.float32
BF16 = jnp.bfloat16

EPS = 1e-6
CHUNK = 64
D_MODEL = 2048
D_INNER = 2 * D_MODEL
SSM_HEAD_DIM = 64
SSM_HEADS = D_INNER // SSM_HEAD_DIM
SSM_STATE = 128
SSM_GROUPS = 8
HEADS_PER_GROUP = SSM_HEADS // SSM_GROUPS
GROUP_WIDTH = D_INNER // SSM_GROUPS
CONV_W = 4
CONV_DIM = D_INNER + 2 * SSM_GROUPS * SSM_STATE
MLA_HEADS = 16
Q_RANK = 512
KV_RANK = 512
QK_NOPE = 128
QK_ROPE = 64
QK_DIM = QK_NOPE + QK_ROPE
V_DIM = 128
ROPE_THETA = 10000.0
ATTN_SCALE = QK_DIM ** -0.5
QK_PRESCALE = ATTN_SCALE * math.log2(math.e)
MOE_GROUPS = 4
EXPERTS_PER_GROUP = 8
N_EXPERTS = MOE_GROUPS * EXPERTS_PER_GROUP
TOP_K = 2
D_EXPERT = 512

LANES = 128
SUBLANES = 8
VMEM_LIMIT = 56 * 1024 * 1024
SSD_Q = 128
MOE_BM = 256
LAT_PAD = 640
SSM_IN_XW = 6400


def _cparams(sem):
    return pltpu.CompilerParams(dimension_semantics=sem, vmem_limit_bytes=VMEM_LIMIT)


def _sigmoid(v):
    return 1.0 / (1.0 + jnp.exp(-v))


def _silu(v):
    return v * _sigmoid(v)


def _softplus(v):
    return jnp.maximum(v, 0.0) + jnp.log1p(jnp.exp(-jnp.abs(v)))


def _rms_rows(x):
    return x * lax.rsqrt(jnp.mean(x * x, axis=-1, keepdims=True) + EPS)


def _mm_body(*refs, nx, nw, ne, no, prologue, epilogue, x_per_j, emit_xn, xn_emit, xn_store, precision):
    x_refs = refs[:nx]
    w_refs = refs[nx:nx + nw]
    e_refs = refs[nx + nw:nx + nw + ne]
    o_refs = refs[nx + nw + ne:nx + nw + ne + no]
    rest = refs[nx + nw + ne + no:]
    j = pl.program_id(1)
    if x_per_j:
        xn = prologue(*[r[...] for r in x_refs])
    else:
        xn_ref = rest[-1]

        @pl.when(j == 0)
        def _():
            v = prologue(*[r[...] for r in x_refs])
            xn_ref[...] = xn_store(v).astype(xn_ref.dtype)
            if emit_xn:
                xn_emit(v, rest[0])

        xn = xn_ref[...]
    accs = [jnp.dot(xn, w[...], preferred_element_type=F32, precision=precision) for w in w_refs]
    outs = epilogue(accs, [e[...] for e in e_refs])
    for o_ref, o in zip(o_refs, outs):
        o_ref[...] = o.astype(o_ref.dtype)


def _fused_mm(x_args, w_args, e_args, out_defs, *, grid, prologue, epilogue, xn_shape=None,
              xn_dtype=BF16, x_per_j=False, xn_out=None, xn_emit=None, xn_store=None, precision=None, name=None):
    arrays = [a for a, _ in x_args + w_args + e_args]
    in_specs = [s for _, s in x_args + w_args + e_args]
    out_shape = [d for d, _ in out_defs]
    out_specs = [s for _, s in out_defs]
    emit_xn = xn_out is not None
    if emit_xn:
        out_shape.append(xn_out[0])
        out_specs.append(xn_out[1])
    scratch = [] if x_per_j else [pltpu.VMEM(xn_shape, xn_dtype)]
    body = functools.partial(
        _mm_body, nx=len(x_args), nw=len(w_args), ne=len(e_args), no=len(out_defs),
        prologue=prologue, epilogue=epilogue, x_per_j=x_per_j, emit_xn=emit_xn,
        xn_emit=xn_emit or _emit_cast, xn_store=xn_store or _pro_cast, precision=precision)
    return pl.pallas_call(
        body, grid=grid, in_specs=in_specs, out_specs=out_specs, out_shape=out_shape,
        scratch_shapes=scratch, compiler_params=_cparams(("parallel", "arbitrary")), name=name,
    )(*arrays)


def _pro_rms(x, g):
    return _rms_rows(x.astype(F32)) * g


def _pro_cast(x):
    return x


def _emit_cast(v, o_ref):
    o_ref[...] = v.astype(o_ref.dtype)


def _emit_token_tiles(v, o_ref):
    w = _pack_bf16_pairs(v)
    m = w.shape[0]
    for s in range(SUBLANES):
        o_ref[pl.ds(s, m, stride=SUBLANES), :] = w[:, s * LANES:(s + 1) * LANES]


def _split3_bf16(v):
    hi = v.astype(BF16)
    r1 = v - hi.astype(F32)
    mid = r1.astype(BF16)
    return hi, mid, (r1 - mid.astype(F32)).astype(BF16)


def _split_hi_lo_hi(v):
    hi = v.astype(BF16)
    lo = (v - hi.astype(F32)).astype(BF16)
    return jnp.concatenate([hi, lo, hi], axis=1)


def _pack_bf16_pairs(v):
    k = v.shape[1] // 2
    lo = lax.bitcast_convert_type(v[:, :k].astype(BF16).astype(F32), jnp.uint32) >> 16
    hi = lax.bitcast_convert_type(v[:, k:].astype(BF16).astype(F32), jnp.uint32) & jnp.uint32(0xFFFF0000)
    return hi | lo


def _epi_silu(accs, es):
    return [_silu(accs[0])]


def _pro_gated(y, zs, g):
    v = y.astype(F32) * zs.astype(F32)
    parts = [_rms_rows(v[:, k * GROUP_WIDTH:(k + 1) * GROUP_WIDTH]) for k in range(SSM_GROUPS)]
    return jnp.concatenate(parts, axis=-1) * g


def _epi_id(accs, es):
    return accs


def _epi_kv_ones(accs, es):
    return [accs[0], jnp.concatenate([accs[1], jnp.ones_like(accs[1])], axis=1)]


def _epi_res(accs, es):
    return [es[0] + accs[0]]


def _epi_ple(accs, es):
    return [es[0] + _sigmoid(accs[0]) * accs[1]]


def _epi_rms_out(accs, es):
    return [_rms_rows(accs[0]) * es[0]]


def _epi_latent_kv(accs, es):
    g_kv, cos, sin = es
    return [_rms_rows(accs[0]) * g_kv, accs[1] * cos + accs[2] * sin]


def _epi_rot(accs, es):
    cos, sin = es
    return [accs[0] * cos + accs[1] * sin]


def _epi_route(accs, es):
    lg = accs[0] + es[0]
    lane = lax.broadcasted_iota(jnp.int32, lg.shape, 1).astype(F32)
    neg = -jnp.inf
    big = 1.0e4
    is_grp = lane < MOE_GROUPS
    gl = jnp.where(is_grp, lg, neg)
    mg = jnp.max(gl, axis=-1, keepdims=True)
    g_sel = jnp.min(jnp.where(gl == mg, lane, big), axis=-1, keepdims=True)
    p_sel = 1.0 / jnp.sum(jnp.where(is_grp, jnp.exp(gl - mg), 0.0), axis=-1, keepdims=True)
    lo = MOE_GROUPS + g_sel * EXPERTS_PER_GROUP
    in_grp = jnp.where(lane >= lo, jnp.where(lane < lo + EXPERTS_PER_GROUP, 1.0, 0.0), 0.0) > 0.5
    el = jnp.where(in_grp, lg, neg)
    v1 = jnp.max(el, axis=-1, keepdims=True)
    i1 = jnp.min(jnp.where(el == v1, lane, big), axis=-1, keepdims=True)
    el2 = jnp.where(lane == i1, neg, el)
    v2 = jnp.max(el2, axis=-1, keepdims=True)
    i2 = jnp.min(jnp.where(el2 == v2, jnp.where(lane == i1, big, lane), big), axis=-1, keepdims=True)
    e21 = jnp.exp(v2 - v1)
    g1 = p_sel / (1.0 + e21)
    g2 = p_sel * e21 / (1.0 + e21)
    idx = jnp.where(lane == 0.0, i1 - MOE_GROUPS, jnp.where(lane == 1.0, i2 - MOE_GROUPS, 0.0))
    gates = jnp.where(lane == 0.0, g1, jnp.where(lane == 1.0, g2, 0.0))
    return [idx.astype(jnp.int32), gates]


def _mm_rows(x_list, consts, w, extras, *, prologue, epilogue, out_dtype, tm, tn, name,
             xn_dtype=BF16, precision=None, emit_xn_dtype=None, e_consts=()):
    m = x_list[0].shape[0]
    k, n = w.shape
    grid = (m // tm, n // tn)
    x_args = [(x, pl.BlockSpec((tm, x.shape[1]), lambda i, j: (i, 0))) for x in x_list]
    x_args += [(c, pl.BlockSpec((1, c.shape[1]), lambda i, j: (0, 0))) for c in consts]
    w_args = [(w, pl.BlockSpec((k, tn), lambda i, j: (0, j)))]
    e_args = [(c, pl.BlockSpec((1, tn), lambda i, j: (0, j))) for c in e_consts]
    e_args += [(e, pl.BlockSpec((tm, tn), lambda i, j: (i, j))) for e in extras]
    out_defs = [(jax.ShapeDtypeStruct((m, n), out_dtype), pl.BlockSpec((tm, tn), lambda i, j: (i, j)))]
    xn_out = None
    if emit_xn_dtype is not None:
        xn_out = (jax.ShapeDtypeStruct((m, k), emit_xn_dtype), pl.BlockSpec((tm, k), lambda i, j: (i, 0)))
    return _fused_mm(x_args, w_args, e_args, out_defs, grid=grid, prologue=prologue, epilogue=epilogue,
                     xn_shape=(tm, k), xn_dtype=xn_dtype, xn_out=xn_out, precision=precision, name=name)


def _mm_res_body(*refs, x_split, nc, nl, nw, ne, npc, lhs_of_w, res_from_x, prologue, epilogue, post, out_split,
                 cw):
    i = pl.program_id(0)
    pos, xs = 0, []
    for first_tiles in x_split:
        if first_tiles is None:
            xs.append(refs[pos][...])
            pos += 1
        else:
            xs.append(jnp.where(i < first_tiles, refs[pos][...], refs[pos + 1][...]))
            pos += 2
    x0_ref = refs[0]
    c_refs = refs[pos:pos + nc]
    l_refs = refs[pos + nc:pos + nc + nl]
    w_refs = refs[pos + nc + nl:pos + nc + nl + nw]
    e_refs = refs[pos + nc + nl + nw:pos + nc + nl + nw + ne]
    p_refs = refs[pos + nc + nl + nw + ne:pos + nc + nl + nw + ne + npc]
    o_refs = refs[pos + nc + nl + nw + ne + npc:-1] if post is None else refs[pos + nc + nl + nw + ne + npc:-2]
    xn_ref = refs[-1]
    rows_ref = o_refs[0] if post is None else refs[-2]
    xn_ref[...] = prologue(*xs, *[r[...] for r in c_refs]).astype(xn_ref.dtype)
    lhs = [xn_ref] + list(l_refs)
    for c in range(rows_ref.shape[1] // cw):
        sl = slice(c * cw, (c + 1) * cw)
        accs = [jnp.dot(lhs[li][...], w[:, sl], preferred_element_type=F32) for w, li in zip(w_refs, lhs_of_w)]
        es = ([x0_ref[:, sl]] if res_from_x else []) + [e[:, sl] for e in e_refs]
        rows_ref[:, sl] = epilogue(accs, es)[0].astype(rows_ref.dtype)
    if post is not None:
        val = post(rows_ref[...], *[p[...] for p in p_refs])
        if out_split is None:
            o_refs[0][...] = val.astype(o_refs[0].dtype)
        else:
            @pl.when(i < out_split)
            def _():
                o_refs[0][...] = val.astype(o_refs[0].dtype)

            @pl.when(i >= out_split)
            def _():
                o_refs[1][...] = val.astype(o_refs[1].dtype)


def _mm_resident(x_list, consts, lhs_list, ws, lhs_of_w, extras, *, prologue, epilogue, out_dtype, tm, name,
                 res_from_x=False, cw=512, post=None, post_consts=(), out_rows=None):
    rows = lambda x: x[0].shape[0] + x[1].shape[0] if isinstance(x, tuple) else x.shape[0]
    m = rows(x_list[0])
    k = ws[lhs_of_w.index(0)].shape[0]
    n = ws[0].shape[1]
    row = lambda a: pl.BlockSpec((tm, a.shape[1]), lambda i: (i, 0))
    whole = lambda a: pl.BlockSpec(a.shape, lambda i: (0, 0), pipeline_mode=pl.Buffered(1))
    arrays, in_specs, x_split = [], [], []
    for x in x_list:
        if isinstance(x, tuple):
            a, b = x
            na = a.shape[0] // tm
            assert a.shape[0] % tm == 0 and b.shape[0] % tm == 0
            arrays += [a, b]
            in_specs += [pl.BlockSpec((tm, a.shape[1]), lambda i, na=na: (jnp.minimum(i, na - 1), 0)),
                         pl.BlockSpec((tm, b.shape[1]), lambda i, na=na: (jnp.maximum(i - na, 0), 0))]
            x_split.append(na)
        else:
            arrays.append(x)
            in_specs.append(row(x))
            x_split.append(None)
    arrays += list(consts) + list(lhs_list) + list(ws) + list(extras) + list(post_consts)
    in_specs += ([whole(a) for a in consts] + [row(a) for a in lhs_list] + [whole(a) for a in ws]
                 + [row(a) for a in extras] + [whole(a) for a in post_consts])
    out_split = None
    out_specs = pl.BlockSpec((tm, n), lambda i: (i, 0))
    out_shape = jax.ShapeDtypeStruct((m, n), out_dtype)
    if out_rows is not None:
        ra, rb = out_rows
        assert post is not None and ra % tm == 0 and rb % tm == 0 and ra + rb == m
        out_split = ra // tm
        out_specs = [pl.BlockSpec((tm, n), lambda i: (jnp.minimum(i, out_split - 1), 0)),
                     pl.BlockSpec((tm, n), lambda i: (jnp.maximum(i - out_split, 0), 0))]
        out_shape = [jax.ShapeDtypeStruct((ra, n), out_dtype), jax.ShapeDtypeStruct((rb, n), out_dtype)]
    scratch = ([pltpu.VMEM((tm, n), F32)] if post is not None else []) + [pltpu.VMEM((tm, k), BF16)]
    body = functools.partial(
        _mm_res_body, x_split=tuple(x_split), nc=len(consts), nl=len(lhs_list), nw=len(ws), ne=len(extras),
        npc=len(post_consts), lhs_of_w=tuple(lhs_of_w), res_from_x=res_from_x, prologue=prologue,
        epilogue=epilogue, post=post, out_split=out_split, cw=cw)
    return pl.pallas_call(
        body, grid=(m // tm,), in_specs=in_specs, out_specs=out_specs, out_shape=out_shape,
        scratch_shapes=scratch,
        compiler_params=_cparams(("arbitrary",) if out_rows is not None else ("parallel",)), name=name,
    )(*arrays)


def _heads_body(*refs, nw, ne, hpb, epilogue):
    x_ref = refs[0]
    w_refs = refs[1:1 + nw]
    e_refs = refs[1 + nw:1 + nw + ne]
    o_refs = refs[1 + nw + ne:]
    x = x_ref[...]
    es = [e[...] for e in e_refs]
    for hh in range(hpb):
        outs = epilogue([jnp.dot(x, w[hh], preferred_element_type=F32) for w in w_refs], es)
        for o_ref, o in zip(o_refs, outs):
            o_ref[hh] = o.astype(o_ref.dtype)


def _heads_mm(x, ws, es, out_dims, *, tm, hpb, epilogue, name):
    m, k = x.shape
    nh = ws[0].shape[0]
    in_specs = [pl.BlockSpec((tm, k), lambda i, j: (i, 0))]
    in_specs += [pl.BlockSpec((hpb, k, w.shape[2]), lambda i, j: (j, 0, 0)) for w in ws]
    in_specs += [pl.BlockSpec((tm, e.shape[1]), lambda i, j: (i, 0)) for e in es]
    return pl.pallas_call(
        functools.partial(_heads_body, nw=len(ws), ne=len(es), hpb=hpb, epilogue=epilogue),
        grid=(m // tm, nh // hpb), in_specs=in_specs,
        out_specs=[pl.BlockSpec((hpb, tm, n), lambda i, j: (j, i, 0)) for n in out_dims],
        out_shape=[jax.ShapeDtypeStruct((nh, m, n), BF16) for n in out_dims],
        compiler_params=_cparams(("parallel", "arbitrary")), name=name,
    )(x, *ws, *es)


def _conv_silu_chunk(ext, in_ref, st_ref, w_ref, b_ref, first, q):
    @pl.when(first)
    def _():
        ext[0:SUBLANES, :] = st_ref[...]

    ext[SUBLANES:SUBLANES + q, :] = in_ref[...]
    acc = b_ref[...]
    for k in range(CONV_W):
        off = SUBLANES - (CONV_W - 1) + k
        acc = acc + ext[off:off + q, :] * w_ref[k:k + 1, :]
    ext[0:SUBLANES, :] = ext[q:q + SUBLANES, :]
    return _silu(acc)


def _ssd_body(x_ref, b_ref, c_ref, stx_ref, stb_ref, stc_ref, wx_ref, wb_ref, wc_ref, bx_ref, bb_ref, bc_ref,
              dt_ref, dtT_ref, bias_ref, biasT_ref, a_ref, aT_ref, dsk_ref, s0_ref,
              y_ref, so_ref, s_scr, ext_x, ext_b, ext_c, *, q, gps):
    c = pl.program_id(2)
    first = c == 0

    @pl.when(first)
    def _():
        s_scr[...] = s0_ref[...]

    xs_all = _conv_silu_chunk(ext_x, x_ref, stx_ref, wx_ref, bx_ref, first, q)
    bm_all = _conv_silu_chunk(ext_b, b_ref, stb_ref, wb_ref, bb_ref, first, q).astype(BF16)
    cm_all = _conv_silu_chunk(ext_c, c_ref, stc_ref, wc_ref, bc_ref, first, q).astype(BF16)

    row = lax.broadcasted_iota(jnp.int32, (q, q), 0)
    col = lax.broadcasted_iota(jnp.int32, (q, q), 1)
    causal = row >= col
    tri = jnp.where(causal, 1.0, 0.0).astype(BF16)
    tri_t = jnp.where(row <= col, 1.0, 0.0).astype(BF16)
    left = lax.broadcasted_iota(jnp.int32, (q, LANES), 1) < SSM_HEAD_DIM
    gw = GROUP_WIDTH
    for gi in range(gps):
        dt = _softplus(dt_ref[gi] + bias_ref[gi])
        dta = dt * a_ref[gi]
        dta_t = _softplus(dtT_ref[gi] + biasT_ref[gi]) * aT_ref[gi]
        acum = sum(jnp.dot(tri, piece, preferred_element_type=F32) for piece in _split3_bf16(dta))
        acum_t = sum(jnp.dot(piece, tri_t, preferred_element_type=F32) for piece in _split3_bf16(dta_t))
        bm = bm_all[:, gi * SSM_STATE:(gi + 1) * SSM_STATE]
        cm = cm_all[:, gi * SSM_STATE:(gi + 1) * SSM_STATE]
        cb = lax.dot_general(cm, bm, (((1,), (1,)), ((), ())), preferred_element_type=F32)
        s_prev = s_scr[gi]
        y_off = jnp.dot(cm, s_prev.astype(BF16), preferred_element_type=F32)
        xw_parts, dec_parts = [], []
        for j in range(HEADS_PER_GROUP // 2):
            h0, h1 = 2 * j, 2 * j + 1
            sl = slice(j * LANES, (j + 1) * LANES)
            sg = slice(gi * gw + j * LANES, gi * gw + (j + 1) * LANES)
            col0, col1 = acum[:, h0:h0 + 1], acum[:, h1:h1 + 1]
            pa = jnp.where(left, col0, col1)
            dtp = jnp.where(left, dt[:, h0:h0 + 1], dt[:, h1:h1 + 1])
            x = xs_all[:, sg]
            xdt = x * dtp
            m0 = (jnp.exp2(jnp.where(causal, col0 - acum_t[h0:h0 + 1, :], -jnp.inf)) * cb).astype(BF16)
            m1 = (jnp.exp2(jnp.where(causal, col1 - acum_t[h1:h1 + 1, :], -jnp.inf)) * cb).astype(BF16)
            x_l = jnp.where(left, xdt, 0.0).astype(BF16)
            x_r = jnp.where(left, 0.0, xdt).astype(BF16)
            y_diag = (jnp.dot(m0, x_l, preferred_element_type=F32)
                      + jnp.dot(m1, x_r, preferred_element_type=F32))
            y = y_diag + y_off[:, sl] * jnp.exp2(pa) + x * dsk_ref[:, sg]
            y_ref[:, sg] = y.astype(y_ref.dtype)
            last = pa[q - 1:q, :]
            xw_parts.append((xdt * jnp.exp2(last - pa)).astype(BF16))
            dec_parts.append(jnp.exp2(last))
        xw = jnp.concatenate(xw_parts, axis=1)
        dec = jnp.concatenate(dec_parts, axis=1)
        s_new = s_prev * dec + lax.dot_general(bm, xw, (((0,), (0,)), ((), ())), preferred_element_type=F32)
        s_scr[gi] = s_new
        so_ref[gi] = s_new


def _ssd_scan(xbc3, batch0, nb, L, st, w_conv, b_conv, dtg, dtg_t, bias, bias_t, a, a_t, dsk, s0, *, q, gps=2):
    g_, r_, n_ = SSM_GROUPS, HEADS_PER_GROUP, SSM_STATE
    gw = GROUP_WIDTH
    xw_, bw_ = gps * gw, gps * n_
    b_off = D_INNER // bw_
    c_off = b_off + g_ // gps
    grid = (nb, g_ // gps, L // q)
    col_x = lambda b, g, c: (0, g)
    col_b = lambda b, g, c: (0, b_off + g)
    col_c = lambda b, g, c: (0, c_off + g)
    return pl.pallas_call(
        functools.partial(_ssd_body, q=q, gps=gps), grid=grid,
        in_specs=[
            pl.BlockSpec((None, q, xw_), lambda b, g, c: (batch0 + b, c, g)),
            pl.BlockSpec((None, q, bw_), lambda b, g, c: (batch0 + b, c, b_off + g)),
            pl.BlockSpec((None, q, bw_), lambda b, g, c: (batch0 + b, c, c_off + g)),
            pl.BlockSpec((None, SUBLANES, xw_), lambda b, g, c: (b, 0, g)),
            pl.BlockSpec((None, SUBLANES, bw_), lambda b, g, c: (b, 0, b_off + g)),
            pl.BlockSpec((None, SUBLANES, bw_), lambda b, g, c: (b, 0, c_off + g)),
            pl.BlockSpec((CONV_W, xw_), col_x),
            pl.BlockSpec((CONV_W, bw_), col_b),
            pl.BlockSpec((CONV_W, bw_), col_c),
            pl.BlockSpec((1, xw_), col_x),
            pl.BlockSpec((1, bw_), col_b),
            pl.BlockSpec((1, bw_), col_c),
            pl.BlockSpec((None, gps, q, r_), lambda b, g, c: (b, g, c, 0)),
            pl.BlockSpec((None, gps, r_, q), lambda b, g, c: (b, g, 0, c)),
            pl.BlockSpec((gps, 1, r_), lambda b, g, c: (g, 0, 0)),
            pl.BlockSpec((gps, r_, 1), lambda b, g, c: (g, 0, 0)),
            pl.BlockSpec((gps, 1, r_), lambda b, g, c: (g, 0, 0)),
            pl.BlockSpec((gps, r_, 1), lambda b, g, c: (g, 0, 0)),
            pl.BlockSpec((1, xw_), col_x),
            pl.BlockSpec((None, gps, n_, gw), lambda b, g, c: (b, g, 0, 0)),
        ],
        out_specs=[
            pl.BlockSpec((None, q, xw_), lambda b, g, c: (b, c, g)),
            pl.BlockSpec((None, gps, n_, gw), lambda b, g, c: (b, g, 0, 0)),
        ],
        out_shape=[jax.ShapeDtypeStruct((nb, L, D_INNER), BF16),
                   jax.ShapeDtypeStruct((nb, g_, n_, gw), F32)],
        scratch_shapes=[pltpu.VMEM((gps, n_, gw), F32), pltpu.VMEM((q + SUBLANES, xw_), F32),
                        pltpu.VMEM((q + SUBLANES, bw_), F32), pltpu.VMEM((q + SUBLANES, bw_), F32)],
        compiler_params=_cparams(("parallel", "parallel", "arbitrary")), name="ssd_scan",
    )(xbc3, xbc3, xbc3, st, st, st, w_conv, w_conv, w_conv, b_conv, b_conv, b_conv,
      dtg, dtg_t, bias, bias_t, a, a_t, dsk, s0)


def _gather_rows(tok_ref, base, src_hbm, dst, sem, n, rp=1):
    def body(r, carry):
        t = pl.multiple_of(tok_ref[base + r] * rp, rp)
        pltpu.make_async_copy(src_hbm.at[pl.ds(t, rp)], dst.at[pl.ds(pl.multiple_of(r * rp, rp), rp)], sem).start()
        return carry

    lax.fori_loop(0, n, body, 0)


def _gather_rows_unrolled(tok_ref, base, src_hbm, dst, sem, n):
    for r in range(n):
        t = tok_ref[base + r]
        pltpu.make_async_copy(src_hbm.at[pl.ds(t, 1)], dst.at[pl.ds(r, 1)], sem).start(priority=r % 2)


def _moe_body(blk_e_ref, nxt_e_ref, tok_ref, nused_ref, xt_hbm, wg_hbm, wu_hbm, wd_hbm, yb_ref,
              xbuf, xsem, wg32, wu32, wd32, wsem, wg_bf, wu_bf, wd_bf, xb, *, bm, layer):
    b = pl.program_id(0)
    slot = b % 2
    nused = nused_ref[0]
    e_cur = blk_e_ref[b]

    def weight_copies(e):
        return (pltpu.make_async_copy(wg_hbm.at[layer, e], wg32, wsem.at[0]),
                pltpu.make_async_copy(wu_hbm.at[layer, e], wu32, wsem.at[1]),
                pltpu.make_async_copy(wd_hbm.at[layer, e], wd32, wsem.at[2]))

    def gather_unrolled(base, slot_):
        for r in range(bm):
            t = pl.multiple_of(tok_ref[base + r] * SUBLANES, SUBLANES)
            pltpu.make_async_copy(xt_hbm.at[pl.ds(t, SUBLANES)], xbuf.at[slot_, pl.ds(r * SUBLANES, SUBLANES)],
                                  xsem.at[slot_]).start()

    @pl.when(jnp.logical_and(b == 0, nused > 0))
    def _():
        for cp in weight_copies(e_cur):
            cp.start(priority=1)
        _gather_rows(tok_ref, 0, xt_hbm, xbuf.at[0], xsem.at[0], bm, rp=SUBLANES)

    prev_e = blk_e_ref[jnp.maximum(b - 1, 0)]
    changed = jnp.logical_or(b == 0, e_cur != prev_e)

    @pl.when(jnp.logical_and(changed, b < nused))
    def _():
        for cp in weight_copies(e_cur):
            cp.wait()
        wg_bf[...] = wg32[...].astype(BF16)
        wu_bf[...] = wu32[...].astype(BF16)
        wd_bf[...] = wd32[...].astype(BF16)
        nxt = nxt_e_ref[b]

        @pl.when(nxt >= 0)
        def _():
            for cp in weight_copies(nxt):
                cp.start(priority=1)

    def compute(prefetch):
        pltpu.make_async_copy(xbuf.at[slot], xbuf.at[slot], xsem.at[slot]).wait()
        half = xb.shape[1] // 2
        for s in range(SUBLANES):
            w = xbuf[slot, pl.ds(s, bm, stride=SUBLANES), :]
            cols = slice(s * LANES, (s + 1) * LANES)
            xb[:, cols] = lax.bitcast_convert_type(w << 16, F32).astype(BF16)
            xb[:, half + s * LANES:half + (s + 1) * LANES] = lax.bitcast_convert_type(
                w & jnp.uint32(0xFFFF0000), F32).astype(BF16)
        if prefetch:
            gather_unrolled((b + 1) * bm, 1 - slot)
        x = xb[...]
        g = jnp.dot(x, wg_bf[...], preferred_element_type=F32)
        u = jnp.dot(x, wu_bf[...], preferred_element_type=F32)
        hid = (_silu(g) * u).astype(BF16)
        yb_ref[...] = jnp.dot(hid, wd_bf[...], preferred_element_type=F32)

    @pl.when(b + 1 < nused)
    def _():
        compute(True)

    @pl.when(b + 1 == nused)
    def _():
        compute(False)

    @pl.when(b >= nused)
    def _():
        yb_ref[...] = jnp.zeros(yb_ref.shape, yb_ref.dtype)


def _moe_experts(xt, blk_e, nxt_e, row_token, nused, w_gate, w_up, w_down, *, bm, layer):
    nblk = blk_e.shape[0]
    d = w_gate.shape[2]
    de = w_gate.shape[3]
    assert d == 2 * SUBLANES * LANES and xt.shape[1] == LANES and xt.dtype == jnp.uint32
    anyspec = pl.BlockSpec(memory_space=pl.ANY)
    gs = pltpu.PrefetchScalarGridSpec(
        num_scalar_prefetch=4, grid=(nblk,),
        in_specs=[anyspec, anyspec, anyspec, anyspec],
        out_specs=pl.BlockSpec((bm, d), lambda b, be, nx, tok, nu: (b, 0)),
        scratch_shapes=[
            pltpu.VMEM((2, bm * SUBLANES, LANES), jnp.uint32),
            pltpu.SemaphoreType.DMA((2,)),
            pltpu.VMEM((d, de), F32),
            pltpu.VMEM((d, de), F32),
            pltpu.VMEM((de, d), F32),
            pltpu.SemaphoreType.DMA((3,)),
            pltpu.VMEM((d, de), BF16),
            pltpu.VMEM((d, de), BF16),
            pltpu.VMEM((de, d), BF16),
            pltpu.VMEM((bm, d), BF16),
        ],
    )
    return pl.pallas_call(
        functools.partial(_moe_body, bm=bm, layer=layer), grid_spec=gs,
        out_shape=jax.ShapeDtypeStruct((nblk * bm, d), F32),
        compiler_params=_cparams(("arbitrary",)), name="moe_experts",
    )(blk_e, nxt_e, row_token, nused, xt, w_gate, w_up, w_down)


def _combine_body(dest_ref, h_ref, g_ref, yb_hbm, o_ref, buf, sem, *, tc, ntile):
    i = pl.program_id(0)
    slot = i % 2
    n = TOP_K * tc

    @pl.when(i == 0)
    def _():
        _gather_rows(dest_ref, 0, yb_hbm, buf.at[0], sem.at[0], n)

    @pl.when(i + 1 < ntile)
    def _():
        _gather_rows_unrolled(dest_ref, (i + 1) * n, yb_hbm, buf.at[1 - slot], sem.at[1 - slot], n)

    pltpu.make_async_copy(buf.at[slot], buf.at[slot], sem.at[slot]).wait()
    g = g_ref[...]
    o_ref[...] = (h_ref[...] + g[:, 0:1] * buf[slot, 0:tc, :] + g[:, 1:2] * buf[slot, tc:2 * tc, :])


def _moe_combine(h, yb, dest_tiles, gates, *, tc):
    t, d = h.shape
    ntile = t // tc
    gs = pltpu.PrefetchScalarGridSpec(
        num_scalar_prefetch=1, grid=(ntile,),
        in_specs=[
            pl.BlockSpec((tc, d), lambda i, dst: (i, 0)),
            pl.BlockSpec((tc, LANES), lambda i, dst: (i, 0)),
            pl.BlockSpec(memory_space=pl.ANY),
        ],
        out_specs=pl.BlockSpec((tc, d), lambda i, dst: (i, 0)),
        scratch_shapes=[pltpu.VMEM((2, TOP_K * tc, d), F32), pltpu.SemaphoreType.DMA((2,))],
    )
    return pl.pallas_call(
        functools.partial(_combine_body, tc=tc, ntile=ntile), grid_spec=gs,
        out_shape=jax.ShapeDtypeStruct((t, d), F32),
        compiler_params=_cparams(("arbitrary",)), name="moe_combine",
    )(dest_tiles, h, gates, yb)


def _moe_plan(idx, bm, tc):
    t = idx.shape[0]
    a = t * TOP_K
    e_ = N_EXPERTS
    flat_e = idx.reshape(a)
    onehot = (flat_e[:, None] == jnp.arange(e_, dtype=jnp.int32)[None, :]).astype(jnp.int32)
    csum = jnp.cumsum(onehot, axis=0)
    rank = jnp.sum(onehot * csum, axis=1) - 1
    counts = csum[-1]
    padded = (counts + bm - 1) // bm * bm
    pad_end = jnp.cumsum(padded)
    pad_start = pad_end - padded
    dest = jnp.sum(onehot * pad_start[None, :], axis=1) + rank
    nblk = -(-a // bm) + e_
    rows = nblk * bm
    row_token = jnp.zeros((rows,), jnp.int32).at[dest].set(jnp.arange(a, dtype=jnp.int32) // TOP_K)
    blk_row0 = jnp.arange(nblk, dtype=jnp.int32) * bm
    blk_e = jnp.minimum(jnp.sum((pad_end[None, :] <= blk_row0[:, None]).astype(jnp.int32), axis=1), e_ - 1)
    ids = jnp.arange(e_, dtype=jnp.int32)
    later_used = jnp.logical_and(ids[None, :] > ids[:, None], (counts > 0)[None, :])
    nxt_of_e = jnp.min(jnp.where(later_used, ids[None, :], e_), axis=1)
    nxt_of_e = jnp.where(nxt_of_e == e_, -1, nxt_of_e)
    nxt_e = jnp.sum(jnp.where(blk_e[:, None] == ids[None, :], nxt_of_e[None, :], 0), axis=1).astype(jnp.int32)
    nused = (pad_end[-1] // bm).astype(jnp.int32).reshape(1)
    dest_tiles = dest.reshape(t // tc, tc, TOP_K).transpose(0, 2, 1).reshape(a).astype(jnp.int32)
    return blk_e.astype(jnp.int32), nxt_e, row_token, nused, dest_tiles


def _flash_body(qi_ref, kj_ref, q_ref, k_ref, v_ref, o_ref, m_scr, acc_scr, *, hb):
    p_id = pl.program_id(1)
    qi = qi_ref[p_id]
    kj = kj_ref[p_id]
    tk = k_ref.shape[1]

    @pl.when(kj == 0)
    def _():
        m_scr[...] = jnp.full(m_scr.shape, -jnp.inf, F32)
        acc_scr[...] = jnp.zeros(acc_scr.shape, F32)

    def step(diag):
        for hh in range(hb):
            s = lax.dot_general(q_ref[hh], k_ref[hh], (((1,), (1,)), ((), ())), preferred_element_type=F32)
            if diag:
                r = lax.broadcasted_iota(jnp.int32, s.shape, 0)
                c = lax.broadcasted_iota(jnp.int32, s.shape, 1)
                s = jnp.where((c // CHUNK) <= (r // CHUNK), s, -jnp.inf)
            m_prev = m_scr[hh]
            m_next = jnp.maximum(m_prev, jnp.max(s, axis=1, keepdims=True))
            p = jnp.exp2(s - jnp.concatenate([m_next] * (tk // LANES), axis=1))
            alpha = jnp.exp2(m_prev - m_next)
            acc_scr[hh] = (jnp.concatenate([alpha, alpha], axis=1) * acc_scr[hh]
                           + jnp.dot(p.astype(BF16), v_ref[hh], preferred_element_type=F32))
            m_scr[hh] = m_next

    @pl.when(kj < qi)
    def _():
        step(False)

    @pl.when(kj == qi)
    def _():
        step(True)
        for hh in range(hb):
            a = acc_scr[hh]
            o_ref[:, hh * V_DIM:(hh + 1) * V_DIM] = (a[:, :V_DIM] / a[:, V_DIM:]).astype(o_ref.dtype)


def _flash_prompt(q, k, v, lp, *, tq, hb):
    assert V_DIM == LANES
    nh = q.shape[0]
    nq = lp // tq
    pairs = [(a, b) for a in range(nq) for b in range(a + 1)]
    qi_tab = jnp.asarray(np.array([p[0] for p in pairs], np.int32))
    kj_tab = jnp.asarray(np.array([p[1] for p in pairs], np.int32))
    gs = pltpu.PrefetchScalarGridSpec(
        num_scalar_prefetch=2, grid=(nh // hb, len(pairs)),
        in_specs=[
            pl.BlockSpec((hb, tq, QK_DIM), lambda h, p, qi, kj: (h, qi[p], 0)),
            pl.BlockSpec((hb, tq, QK_DIM), lambda h, p, qi, kj: (h, kj[p], 0)),
            pl.BlockSpec((hb, tq, 2 * V_DIM), lambda h, p, qi, kj: (h, kj[p], 0)),
        ],
        out_specs=pl.BlockSpec((tq, hb * V_DIM), lambda h, p, qi, kj: (qi[p], h)),
        scratch_shapes=[pltpu.VMEM((hb, tq, LANES), F32), pltpu.VMEM((hb, tq, 2 * V_DIM), F32)],
    )
    return pl.pallas_call(
        functools.partial(_flash_body, hb=hb), grid_spec=gs,
        out_shape=jax.ShapeDtypeStruct((lp, nh * V_DIM), BF16),
        compiler_params=_cparams(("parallel", "arbitrary")), name="flash_prompt",
    )(qi_tab, kj_tab, q, k, v)


def _attn_sample_body(q_ref, ckv_ref, kr_ref, ckv_new_ref, kr_new_ref, o_ref, *, ls, past):
    nh = q_ref.shape[0]
    q = q_ref[...].reshape(nh * ls, q_ref.shape[2])
    q_lat, q_rot = q[:, :KV_RANK], q[:, KV_RANK:KV_RANK + QK_ROPE]
    nt = (((1,), (1,)), ((), ()))

    def scores(lat, rot, pos0):
        s = (lax.dot_general(q_lat, lat, nt, preferred_element_type=F32)
             + lax.dot_general(q_rot, rot, nt, preferred_element_type=F32))
        r = lax.broadcasted_iota(jnp.int32, s.shape, 0)
        c = lax.broadcasted_iota(jnp.int32, s.shape, 1)
        q_pos = past + r % ls
        return jnp.where(((pos0 + c) // CHUNK) <= (q_pos // CHUNK), s, -jnp.inf)

    lat_p = ckv_ref[...].astype(BF16)
    lat_n = ckv_new_ref[...].astype(BF16)
    s_p = scores(lat_p, kr_ref[...].astype(BF16), 0)
    s_n = scores(lat_n, kr_new_ref[...].astype(BF16), past)
    m = jnp.maximum(jnp.max(s_p, axis=-1, keepdims=True), jnp.max(s_n, axis=-1, keepdims=True))
    p_p = jnp.exp2(s_p - m)
    p_n = jnp.exp2(s_n - m)
    l = jnp.sum(p_p, axis=-1, keepdims=True) + jnp.sum(p_n, axis=-1, keepdims=True)
    o = (jnp.dot(p_p.astype(BF16), lat_p, preferred_element_type=F32)
         + jnp.dot(p_n.astype(BF16), lat_n, preferred_element_type=F32)) / l
    o_ref[...] = o.reshape(nh, ls, KV_RANK).astype(o_ref.dtype)


def _attn_sample(q_abs, cache_ckv, cache_kr, ckv_new, kr_new, *, ls):
    nh, ts, dk = q_abs.shape
    nb, past, _ = cache_ckv.shape
    return pl.pallas_call(
        functools.partial(_attn_sample_body, ls=ls, past=past), grid=(nb,),
        in_specs=[
            pl.BlockSpec((nh, ls, dk), lambda b: (0, b, 0)),
            pl.BlockSpec((None, past, KV_RANK), lambda b: (b, 0, 0)),
            pl.BlockSpec((None, past, QK_ROPE), lambda b: (b, 0, 0)),
            pl.BlockSpec((None, ls, KV_RANK), lambda b: (b, 0, 0)),
            pl.BlockSpec((None, ls, QK_ROPE), lambda b: (b, 0, 0)),
        ],
        out_specs=pl.BlockSpec((nh, ls, KV_RANK), lambda b: (0, b, 0)),
        out_shape=jax.ShapeDtypeStruct((nh, ts, KV_RANK), BF16),
        compiler_params=_cparams(("parallel",)), name="attn_sample",
    )(q_abs, cache_ckv, cache_kr, ckv_new, kr_new)


def _rope_tables(pos):
    half = QK_ROPE // 2
    inv = ROPE_THETA ** (-np.arange(half, dtype=np.float64) / half)
    ang = np.asarray(pos, np.float64)[:, None] * inv[None, :]
    cos = np.concatenate([np.cos(ang), np.cos(ang)], axis=1)
    sin = np.concatenate([-np.sin(ang), np.sin(ang)], axis=1)
    return cos.astype(np.float32), sin.astype(np.float32)


def _swap_rope_halves(w):
    half = QK_ROPE // 2
    return jnp.concatenate([w[..., half:], w[..., :half]], axis=-1)


def _moe_layer(h, g_ffn, w_rg, b_rg, w_re, b_re, w_gate, w_up, w_down, *, tm, layer):
    t, d = h.shape
    npad = LANES - MOE_GROUPS - N_EXPERTS
    w_r = jnp.concatenate([w_rg, w_re, jnp.zeros((d, npad), F32)], axis=1)
    b_r = jnp.concatenate([b_rg, b_re, jnp.zeros((npad,), F32)]).reshape(1, LANES)
    grid = (t // tm, 1)
    x_args = [(h, pl.BlockSpec((tm, d), lambda i, j: (i, 0))),
              (g_ffn.reshape(1, d), pl.BlockSpec((1, d), lambda i, j: (0, 0)))]
    w_hi = w_r.astype(BF16)
    w_lo = (w_r - w_hi.astype(F32)).astype(BF16)
    w_args = [(jnp.concatenate([w_hi, w_hi, w_lo], axis=0), pl.BlockSpec((3 * d, LANES), lambda i, j: (0, 0)))]
    e_args = [(b_r, pl.BlockSpec((1, LANES), lambda i, j: (0, 0)))]
    tile = pl.BlockSpec((tm, LANES), lambda i, j: (i, 0))
    out_defs = [(jax.ShapeDtypeStruct((t, LANES), jnp.int32), tile),
                (jax.ShapeDtypeStruct((t, LANES), F32), tile)]
    assert d == 2 * SUBLANES * LANES
    xn_out = (jax.ShapeDtypeStruct((t * SUBLANES, LANES), jnp.uint32),
              pl.BlockSpec((tm * SUBLANES, LANES), lambda i, j: (i, 0)))
    idx_t, gates_t, xt = _fused_mm(x_args, w_args, e_args, out_defs, grid=grid, prologue=_pro_rms,
                                   epilogue=_epi_route, xn_shape=(tm, 3 * d), xn_dtype=BF16, xn_out=xn_out,
                                   xn_emit=_emit_token_tiles, xn_store=_split_hi_lo_hi, name="router")
    tc = min(tm, 128)
    blk_e, nxt_e, row_token, nused, dest_tiles = _moe_plan(idx_t[:, :TOP_K], MOE_BM, tc)
    yb = _moe_experts(xt, blk_e, nxt_e, row_token, nused, w_gate, w_up, w_down, bm=MOE_BM, layer=layer)
    return _moe_combine(h, yb, dest_tiles, gates_t, tc=tc)


def _post_rms(rows, g):
    return _rms_rows(rows) * g


def _ple_layer(h, p_rows, g_ple, w_gate, w_proj, *, tm, final=None):
    t, d = h.shape
    extra = {}
    if final is not None:
        extra = dict(post=_post_rms, post_consts=[final[0].reshape(1, d)], out_rows=final[1:])
    return _mm_resident([h], [g_ple.reshape(1, d)], [p_rows.astype(BF16)], [w_gate.astype(BF16), w_proj.astype(BF16)],
                        [0, 1], [], prologue=_pro_rms, epilogue=_epi_ple, out_dtype=F32, tm=tm, name="ple_gate",
                        res_from_x=True, **extra)


def kernel(x_prompt, x_sample, state_conv, state_ssm, cache_kv_latent, cache_k_rope, p_prompt, p_sample,
           g_mix, w_ssm_in, w_conv, b_conv, dt_bias, a_log, d_skip, g_ssm_norm, w_ssm_out,
           g_kv_in, w_dkv, g_kv, w_uk, w_uv, w_dq, g_q, w_uq, w_o,
           g_ffn, w_router_grp, b_router_grp, w_router_exp, b_router_exp, w_exp_gate, w_exp_up, w_exp_down,
           g_ple, w_ple_gate, w_ple_proj, g_final):
    bp, lp_each, d = x_prompt.shape
    bs, ls, _ = x_sample.shape
    past = cache_kv_latent.shape[1]
    assert bp == 1 and w_ssm_in.shape[0] == 1 and w_dq.shape[0] == 1 and d == D_MODEL
    assert ls >= CONV_W - 1 and ls % SUBLANES == 0 and past % CHUNK == 0
    lp = bp * lp_each
    ts = bs * ls
    t = lp + ts
    tm = min(512, math.gcd(lp, ts))
    g_, r_, n_, hd = SSM_GROUPS, HEADS_PER_GROUP, SSM_STATE, SSM_HEAD_DIM

    h = jnp.concatenate([x_prompt.reshape(lp, d), x_sample.reshape(ts, d)], axis=0)

    w_in = w_ssm_in[0]
    w_z = w_in[:, :D_INNER].astype(BF16)
    xw = SSM_IN_XW
    w_x = jnp.pad(w_in[:, D_INNER:], ((0, 0), (0, xw - CONV_DIM - SSM_HEADS))).astype(BF16)
    g0 = g_mix[0].reshape(1, d)
    tm_in = max(c for c in range(16, 1153, 16) if t % c == 0)
    mm_in = functools.partial(_mm_rows, [h], [g0], prologue=_pro_rms, tm=tm_in)
    z = mm_in(w_z, [], epilogue=_epi_silu, out_dtype=BF16, tn=1024, name="ssm_in_z")[0]
    xbc = mm_in(w_x, [], epilogue=_epi_id, out_dtype=F32, tn=xw // 5, name="ssm_in_xbc")[0]
    dt_raw = xbc[:, CONV_DIM:CONV_DIM + SSM_HEADS]

    wc = w_conv[0]
    bc = b_conv[0].reshape(1, CONV_DIM)
    st_p = jnp.zeros((1, SUBLANES, CONV_DIM), F32)
    st_s = jnp.pad(state_conv[0].astype(F32), ((0, 0), (SUBLANES - (CONV_W - 1), 0), (0, 0)))
    conv_p = xbc[lp - (CONV_W - 1):lp, :CONV_DIM].reshape(1, 1, CONV_W - 1, CONV_DIM)
    conv_s = xbc[lp:, :CONV_DIM].reshape(bs, ls, CONV_DIM)[:, ls - (CONV_W - 1):].reshape(
        1, bs, CONV_W - 1, CONV_DIM)

    bias = dt_bias[0].astype(F32).reshape(g_, 1, r_)
    a_neg = (-jnp.exp(a_log[0].astype(F32)) * math.log2(math.e)).reshape(g_, 1, r_)
    dsk = jnp.repeat(d_skip[0].astype(F32), hd).reshape(1, D_INNER)

    def dt_views(rows, nb, L):
        v = rows.reshape(nb, L, g_, r_).transpose(0, 2, 1, 3)
        return v, v.transpose(0, 1, 3, 2)

    def scan(xbc3, batch0, nb, L, st, rows, s0, q):
        dtg, dtg_t = dt_views(rows, nb, L)
        return _ssd_scan(xbc3, batch0, nb, L, st, wc, bc, dtg, dtg_t, bias, bias.transpose(0, 2, 1),
                         a_neg, a_neg.transpose(0, 2, 1), dsk, s0, q=q)

    def state_in(s):
        nb = s.shape[0]
        return s.astype(F32).reshape(nb, g_, r_ * hd, n_).transpose(0, 1, 3, 2)

    def state_out(s):
        nb = s.shape[0]
        return s.transpose(0, 1, 3, 2).reshape(1, nb, SSM_HEADS, hd, n_)

    y_p, s_p = scan(xbc.reshape(1, t, xw), 0, 1, lp, st_p, dt_raw[:lp],
                    jnp.zeros((1, g_, n_, r_ * hd), F32), min(SSD_Q, lp))
    y_s, s_s = scan(xbc.reshape(t // ls, ls, xw), lp // ls, bs, ls, st_s, dt_raw[lp:],
                    state_in(state_ssm[0]), ls)
    ssm_p, ssm_s = state_out(s_p), state_out(s_s)
    y_all = (y_p.reshape(lp, D_INNER), y_s.reshape(ts, D_INNER))

    h = _mm_resident([y_all, z], [g_ssm_norm[0].reshape(1, D_INNER)], [], [w_ssm_out[0].astype(BF16)], [0], [h],
                     prologue=_pro_gated, epilogue=_epi_res, out_dtype=F32, tm=tm // 2, name="ssm_out")

    def ffn_and_ple(h, i, final=None):
        h = _moe_layer(h, g_ffn[i], w_router_grp[i], b_router_grp[i], w_router_exp[i], b_router_exp[i],
                       w_exp_gate, w_exp_up, w_exp_down, tm=tm, layer=i)
        p_rows = jnp.concatenate([p_prompt[i].reshape(lp, -1), p_sample[i].reshape(ts, -1)], axis=0)
        return _ple_layer(h, p_rows, g_ple[i], w_ple_gate[i], w_ple_proj[i], tm=tm, final=final)

    h = ffn_and_ple(h, 0)

    pos = np.concatenate([np.arange(lp), np.tile(past + np.arange(ls), bs)])
    cos64, sin64 = _rope_tables(pos)
    gk = g_kv_in.reshape(1, d)
    w_r = w_dkv[:, KV_RANK:]
    row64 = pl.BlockSpec((tm, QK_ROPE), lambda i, j: (i, 0))
    w64 = pl.BlockSpec((d, QK_ROPE), lambda i, j: (0, 0))
    whole = lambda a: pl.BlockSpec(a.shape, lambda i, j: (0, 0))
    w_lat = w_dkv[:, :KV_RANK].astype(BF16)
    g_kv_row = g_kv.reshape(1, KV_RANK)
    ckv, kr = _fused_mm(
        [(h, pl.BlockSpec((tm, d), lambda i, j: (i, 0))), (gk, whole(gk))],
        [(w_lat, whole(w_lat)), (w_r.astype(BF16), w64), (_swap_rope_halves(w_r).astype(BF16), w64)],
        [(g_kv_row, whole(g_kv_row)), (jnp.asarray(cos64), row64), (jnp.asarray(sin64), row64)],
        [(jax.ShapeDtypeStruct((t, KV_RANK), F32), pl.BlockSpec((tm, KV_RANK), lambda i, j: (i, 0))),
         (jax.ShapeDtypeStruct((t, QK_ROPE), F32), row64)],
        grid=(t // tm, 1), prologue=_pro_rms, epilogue=_epi_latent_kv, xn_shape=(tm, d), name="kv_latent")

    ql = _mm_rows([h], [g_mix[1].reshape(1, d)], w_dq[0].astype(BF16), [], prologue=_pro_rms,
                  epilogue=_epi_rms_out, out_dtype=BF16, tm=tm, tn=Q_RANK, name="q_latent",
                  e_consts=[g_q[0].reshape(1, Q_RANK)])[0]
    nh = MLA_HEADS
    w_q = w_uq[0].reshape(Q_RANK, nh, QK_DIM).transpose(1, 0, 2)
    w_q_rot = jnp.concatenate([jnp.zeros((nh, Q_RANK, QK_NOPE), F32), _swap_rope_halves(w_q[..., QK_NOPE:])], -1)
    cos_q = np.concatenate([np.ones((t, QK_NOPE), np.float32), cos64], axis=1) * np.float32(QK_PRESCALE)
    sin_q = np.concatenate([np.zeros((t, QK_NOPE), np.float32), sin64], axis=1) * np.float32(QK_PRESCALE)
    q_cat = _heads_mm(ql, [w_q.astype(BF16), w_q_rot.astype(BF16)], [jnp.asarray(cos_q), jnp.asarray(sin_q)],
                      [QK_DIM], tm=tm, hpb=8, epilogue=_epi_rot, name="q_heads")[0]

    lat_pad = LAT_PAD - KV_RANK - QK_ROPE
    ckr_p = jnp.concatenate([ckv[:lp], kr[:lp], jnp.zeros((lp, lat_pad), F32)], axis=1).astype(BF16)
    w_uk_h = w_uk.transpose(1, 0, 2)
    w_uv_h = w_uv.transpose(1, 0, 2)
    eye_r = jnp.broadcast_to(jnp.eye(QK_ROPE, dtype=F32), (nh, QK_ROPE, QK_ROPE))
    w_k_ext = jnp.concatenate([
        jnp.concatenate([w_uk_h, jnp.zeros((nh, KV_RANK, QK_ROPE), F32)], axis=2),
        jnp.concatenate([jnp.zeros((nh, QK_ROPE, QK_NOPE), F32), eye_r], axis=2),
        jnp.zeros((nh, lat_pad, QK_DIM), F32)], axis=1).astype(BF16)
    w_v_ext = jnp.concatenate([w_uv_h, jnp.zeros((nh, LAT_PAD - KV_RANK, V_DIM), F32)], axis=1).astype(BF16)
    k_cat, v_h = _heads_mm(ckr_p, [w_k_ext, w_v_ext], [], [QK_DIM, 2 * V_DIM], tm=tm, hpb=8,
                           epilogue=_epi_kv_ones, name="kv_heads")
    o_p = _flash_prompt(q_cat, k_cat, v_h, lp, tq=min(512, lp), hb=4)

    w_abs = jnp.concatenate([
        jnp.concatenate([w_uk_h.transpose(0, 2, 1), jnp.zeros((nh, QK_NOPE, LAT_PAD - KV_RANK), F32)], axis=2),
        jnp.concatenate([jnp.zeros((nh, QK_ROPE, KV_RANK), F32), eye_r,
                         jnp.zeros((nh, QK_ROPE, lat_pad), F32)], axis=2)], axis=1).astype(BF16)
    blk_s = lp // ts if lp % ts == 0 else None
    assert blk_s is not None
    q_abs = _fused_mm(
        [(q_cat, pl.BlockSpec((None, ts, QK_DIM), lambda i, j: (j, blk_s + i, 0)))],
        [(w_abs, pl.BlockSpec((None, QK_DIM, LAT_PAD), lambda i, j: (j, 0, 0)))],
        [],
        [(jax.ShapeDtypeStruct((nh, ts, LAT_PAD), BF16), pl.BlockSpec((None, ts, LAT_PAD), lambda i, j: (j, i, 0)))],
        grid=(1, nh), prologue=_pro_cast, epilogue=_epi_id, x_per_j=True, name="q_absorb")[0]
    o_lat = _attn_sample(q_abs, cache_kv_latent, cache_k_rope, ckv[lp:].reshape(bs, ls, KV_RANK),
                         kr[lp:].reshape(bs, ls, QK_ROPE), ls=ls)
    o_s = _fused_mm(
        [(o_lat, pl.BlockSpec((None, ts, KV_RANK), lambda i, j: (j, i, 0)))],
        [(w_uv_h.astype(BF16), pl.BlockSpec((None, KV_RANK, V_DIM), lambda i, j: (j, 0, 0)))],
        [],
        [(jax.ShapeDtypeStruct((ts, nh * V_DIM), BF16), pl.BlockSpec((ts, V_DIM), lambda i, j: (i, j)))],
        grid=(1, nh), prologue=_pro_cast, epilogue=_epi_id, x_per_j=True, name="v_absorb")[0]
    h = _mm_resident([(o_p, o_s)], [], [], [w_o[0].astype(BF16)], [0], [h], prologue=_pro_cast, epilogue=_epi_res,
                     out_dtype=F32, tm=tm, name="attn_out")
    y_p_out, y_s_out = ffn_and_ple(h, 1, final=(g_final, lp, ts))
    y_p_out = y_p_out.reshape(bp, lp_each, d)
    y_s_out = y_s_out.reshape(bs, ls, d)
    return (y_p_out, y_s_out, conv_p, ssm_p, ckv[:lp].reshape(bp, lp_each, KV_RANK),
            kr[:lp].reshape(bp, lp_each, QK_ROPE), conv_s, ssm_s, ckv[lp:].reshape(bs, ls, KV_RANK),
            kr[lp:].reshape(bs, ls, QK_ROPE))
```

```python
import functools
import math

import numpy as np
import jax
import jax.numpy as jnp
from jax import lax
from jax.experimental import pallas as pl
from jax.experimental.pallas import tpu as pltpu

F32 = jnp.float32
BF16 = jnp.bfloat16

EPS = 1e-6
CHUNK = 64
D_MODEL = 2048
D_INNER = 2 * D_MODEL
SSM_HEAD_DIM = 64
SSM_HEADS = D_INNER // SSM_HEAD_DIM
SSM_STATE = 128
SSM_GROUPS = 8
HEADS_PER_GROUP = SSM_HEADS // SSM_GROUPS
GROUP_WIDTH = D_INNER // SSM_GROUPS
CONV_W = 4
CONV_DIM = D_INNER + 2 * SSM_GROUPS * SSM_STATE
MLA_HEADS = 16
Q_RANK = 512
KV_RANK = 512
QK_NOPE = 128
QK_ROPE = 64
QK_DIM = QK_NOPE + QK_ROPE
V_DIM = 128
ROPE_THETA = 10000.0
ATTN_SCALE = QK_DIM ** -0.5
QK_PRESCALE = ATTN_SCALE * math.log2(math.e)
MOE_GROUPS = 4
EXPERTS_PER_GROUP = 8
N_EXPERTS = MOE_GROUPS * EXPERTS_PER_GROUP
TOP_K = 2
D_EXPERT = 512

LANES = 128
SUBLANES = 8
VMEM_LIMIT = 56 * 1024 * 1024
SSD_Q = 128
MOE_BM = 256
LAT_PAD = 640
SSM_IN_XW = 6400


def _cparams(sem):
    return pltpu.CompilerParams(dimension_semantics=sem, vmem_limit_bytes=VMEM_LIMIT)


def _sigmoid(v):
    return 1.0 / (1.0 + jnp.exp(-v))


def _silu(v):
    return v * _sigmoid(v)


def _softplus(v):
    return jnp.maximum(v, 0.0) + jnp.log1p(jnp.exp(-jnp.abs(v)))


def _rms_rows(x):
    return x * lax.rsqrt(jnp.mean(x * x, axis=-1, keepdims=True) + EPS)


def _mm_body(*refs, nx, nw, ne, no, prologue, epilogue, x_per_j, emit_xn, xn_emit, xn_store, precision):
    x_refs = refs[:nx]
    w_refs = refs[nx:nx + nw]
    e_refs = refs[nx + nw:nx + nw + ne]
    o_refs = refs[nx + nw + ne:nx + nw + ne + no]
    rest = refs[nx + nw + ne + no:]
    j = pl.program_id(1)
    if x_per_j:
        xn = prologue(*[r[...] for r in x_refs])
    else:
        xn_ref = rest[-1]

        @pl.when(j == 0)
        def _():
            v = prologue(*[r[...] for r in x_refs])
            xn_ref[...] = xn_store(v).astype(xn_ref.dtype)
            if emit_xn:
                xn_emit(v, rest[0])

        xn = xn_ref[...]
    accs = [jnp.dot(xn, w[...], preferred_element_type=F32, precision=precision) for w in w_refs]
    outs = epilogue(accs, [e[...] for e in e_refs])
    for o_ref, o in zip(o_refs, outs):
        o_ref[...] = o.astype(o_ref.dtype)


def _fused_mm(x_args, w_args, e_args, out_defs, *, grid, prologue, epilogue, xn_shape=None,
              xn_dtype=BF16, x_per_j=False, xn_out=None, xn_emit=None, xn_store=None, precision=None, name=None):
    arrays = [a for a, _ in x_args + w_args + e_args]
    in_specs = [s for _, s in x_args + w_args + e_args]
    out_shape = [d for d, _ in out_defs]
    out_specs = [s for _, s in out_defs]
    emit_xn = xn_out is not None
    if emit_xn:
        out_shape.append(xn_out[0])
        out_specs.append(xn_out[1])
    scratch = [] if x_per_j else [pltpu.VMEM(xn_shape, xn_dtype)]
    body = functools.partial(
        _mm_body, nx=len(x_args), nw=len(w_args), ne=len(e_args), no=len(out_defs),
        prologue=prologue, epilogue=epilogue, x_per_j=x_per_j, emit_xn=emit_xn,
        xn_emit=xn_emit or _emit_cast, xn_store=xn_store or _pro_cast, precision=precision)
    return pl.pallas_call(
        body, grid=grid, in_specs=in_specs, out_specs=out_specs, out_shape=out_shape,
        scratch_shapes=scratch, compiler_params=_cparams(("parallel", "arbitrary")), name=name,
    )(*arrays)


def _pro_rms(x, g):
    return _rms_rows(x.astype(F32)) * g


def _pro_cast(x):
    return x


def _emit_cast(v, o_ref):
    o_ref[...] = v.astype(o_ref.dtype)


def _emit_token_tiles(v, o_ref):
    w = _pack_bf16_pairs(v)
    m = w.shape[0]
    for s in range(SUBLANES):
        o_ref[pl.ds(s, m, stride=SUBLANES), :] = w[:, s * LANES:(s + 1) * LANES]


def _split3_bf16(v):
    hi = v.astype(BF16)
    r1 = v - hi.astype(F32)
    mid = r1.astype(BF16)
    return hi, mid, (r1 - mid.astype(F32)).astype(BF16)


def _split_hi_lo_hi(v):
    hi = v.astype(BF16)
    lo = (v - hi.astype(F32)).astype(BF16)
    return jnp.concatenate([hi, lo, hi], axis=1)


def _pack_bf16_pairs(v):
    k = v.shape[1] // 2
    lo = lax.bitcast_convert_type(v[:, :k].astype(BF16).astype(F32), jnp.uint32) >> 16
    hi = lax.bitcast_convert_type(v[:, k:].astype(BF16).astype(F32), jnp.uint32) & jnp.uint32(0xFFFF0000)
    return hi | lo


def _epi_silu(accs, es):
    return [_silu(accs[0])]


def _pro_gated(y, zs, g):
    v = y.astype(F32) * zs.astype(F32)
    parts = [_rms_rows(v[:, k * GROUP_WIDTH:(k + 1) * GROUP_WIDTH]) for k in range(SSM_GROUPS)]
    return jnp.concatenate(parts, axis=-1) * g


def _epi_id(accs, es):
    return accs


def _epi_res(accs, es):
    return [es[0] + accs[0]]


def _epi_ple(accs, es):
    return [es[0] + _sigmoid(accs[0]) * accs[1]]


def _epi_rms_out(accs, es):
    return [_rms_rows(accs[0]) * es[0]]


def _epi_latent_kv(accs, es):
    g_kv, cos, sin = es
    return [_rms_rows(accs[0]) * g_kv, accs[1] * cos + accs[2] * sin]


def _epi_route(accs, es):
    lg = accs[0] + es[0]
    lane = lax.broadcasted_iota(jnp.int32, lg.shape, 1).astype(F32)
    neg = -jnp.inf
    big = 1.0e4
    is_grp = lane < MOE_GROUPS
    gl = jnp.where(is_grp, lg, neg)
    mg = jnp.max(gl, axis=-1, keepdims=True)
    g_sel = jnp.min(jnp.where(gl == mg, lane, big), axis=-1, keepdims=True)
    p_sel = 1.0 / jnp.sum(jnp.where(is_grp, jnp.exp(gl - mg), 0.0), axis=-1, keepdims=True)
    lo = MOE_GROUPS + g_sel * EXPERTS_PER_GROUP
    in_grp = jnp.where(lane >= lo, jnp.where(lane < lo + EXPERTS_PER_GROUP, 1.0, 0.0), 0.0) > 0.5
    el = jnp.where(in_grp, lg, neg)
    v1 = jnp.max(el, axis=-1, keepdims=True)
    i1 = jnp.min(jnp.where(el == v1, lane, big), axis=-1, keepdims=True)
    el2 = jnp.where(lane == i1, neg, el)
    v2 = jnp.max(el2, axis=-1, keepdims=True)
    i2 = jnp.min(jnp.where(el2 == v2, jnp.where(lane == i1, big, lane), big), axis=-1, keepdims=True)
    e21 = jnp.exp(v2 - v1)
    g1 = p_sel / (1.0 + e21)
    g2 = p_sel * e21 / (1.0 + e21)
    idx = jnp.where(lane == 0.0, i1 - MOE_GROUPS, jnp.where(lane == 1.0, i2 - MOE_GROUPS, 0.0))
    gates = jnp.where(lane == 0.0, g1, jnp.where(lane == 1.0, g2, 0.0))
    return [idx.astype(jnp.int32), gates]


def _mm_rows(x_list, consts, w, extras, *, prologue, epilogue, out_dtype, tm, tn, name,
             xn_dtype=BF16, precision=None, emit_xn_dtype=None, e_consts=()):
    m = x_list[0].shape[0]
    k, n = w.shape
    grid = (m // tm, n // tn)
    x_args = [(x, pl.BlockSpec((tm, x.shape[1]), lambda i, j: (i, 0))) for x in x_list]
    x_args += [(c, pl.BlockSpec((1, c.shape[1]), lambda i, j: (0, 0))) for c in consts]
    w_args = [(w, pl.BlockSpec((k, tn), lambda i, j: (0, j)))]
    e_args = [(c, pl.BlockSpec((1, tn), lambda i, j: (0, j))) for c in e_consts]
    e_args += [(e, pl.BlockSpec((tm, tn), lambda i, j: (i, j))) for e in extras]
    out_defs = [(jax.ShapeDtypeStruct((m, n), out_dtype), pl.BlockSpec((tm, tn), lambda i, j: (i, j)))]
    xn_out = None
    if emit_xn_dtype is not None:
        xn_out = (jax.ShapeDtypeStruct((m, k), emit_xn_dtype), pl.BlockSpec((tm, k), lambda i, j: (i, 0)))
    return _fused_mm(x_args, w_args, e_args, out_defs, grid=grid, prologue=prologue, epilogue=epilogue,
                     xn_shape=(tm, k), xn_dtype=xn_dtype, xn_out=xn_out, precision=precision, name=name)


def _mm_res_body(*refs, x_split, nc, nl, nw, ne, npc, lhs_of_w, res_from_x, prologue, epilogue, post, out_split,
                 cw):
    i = pl.program_id(0)
    pos, xs = 0, []
    for first_tiles in x_split:
        if first_tiles is None:
            xs.append(refs[pos][...])
            pos += 1
        else:
            xs.append(jnp.where(i < first_tiles, refs[pos][...], refs[pos + 1][...]))
            pos += 2
    x0_ref = refs[0]
    c_refs = refs[pos:pos + nc]
    l_refs = refs[pos + nc:pos + nc + nl]
    w_refs = refs[pos + nc + nl:pos + nc + nl + nw]
    e_refs = refs[pos + nc + nl + nw:pos + nc + nl + nw + ne]
    p_refs = refs[pos + nc + nl + nw + ne:pos + nc + nl + nw + ne + npc]
    o_refs = refs[pos + nc + nl + nw + ne + npc:-1] if post is None else refs[pos + nc + nl + nw + ne + npc:-2]
    xn_ref = refs[-1]
    rows_ref = o_refs[0] if post is None else refs[-2]
    xn_ref[...] = prologue(*xs, *[r[...] for r in c_refs]).astype(xn_ref.dtype)
    lhs = [xn_ref] + list(l_refs)
    for c in range(rows_ref.shape[1] // cw):
        sl = slice(c * cw, (c + 1) * cw)
        accs = [jnp.dot(lhs[li][...], w[:, sl], preferred_element_type=F32) for w, li in zip(w_refs, lhs_of_w)]
        es = ([x0_ref[:, sl]] if res_from_x else []) + [e[:, sl] for e in e_refs]
        rows_ref[:, sl] = epilogue(accs, es)[0].astype(rows_ref.dtype)
    if post is not None:
        val = post(rows_ref[...], *[p[...] for p in p_refs])
        if out_split is None:
            o_refs[0][...] = val.astype(o_refs[0].dtype)
        else:
            @pl.when(i < out_split)
            def _():
                o_refs[0][...] = val.astype(o_refs[0].dtype)

            @pl.when(i >= out_split)
            def _():
                o_refs[1][...] = val.astype(o_refs[1].dtype)


def _mm_resident(x_list, consts, lhs_list, ws, lhs_of_w, extras, *, prologue, epilogue, out_dtype, tm, name,
                 res_from_x=False, cw=512, post=None, post_consts=(), out_rows=None):
    rows = lambda x: x[0].shape[0] + x[1].shape[0] if isinstance(x, tuple) else x.shape[0]
    m = rows(x_list[0])
    k = ws[lhs_of_w.index(0)].shape[0]
    n = ws[0].shape[1]
    row = lambda a: pl.BlockSpec((tm, a.shape[1]), lambda i: (i, 0))
    whole = lambda a: pl.BlockSpec(a.shape, lambda i: (0, 0), pipeline_mode=pl.Buffered(1))
    arrays, in_specs, x_split = [], [], []
    for x in x_list:
        if isinstance(x, tuple):
            a, b = x
            na = a.shape[0] // tm
            assert a.shape[0] % tm == 0 and b.shape[0] % tm == 0
            arrays += [a, b]
            in_specs += [pl.BlockSpec((tm, a.shape[1]), lambda i, na=na: (jnp.minimum(i, na - 1), 0)),
                         pl.BlockSpec((tm, b.shape[1]), lambda i, na=na: (jnp.maximum(i - na, 0), 0))]
            x_split.append(na)
        else:
            arrays.append(x)
            in_specs.append(row(x))
            x_split.append(None)
    arrays += list(consts) + list(lhs_list) + list(ws) + list(extras) + list(post_consts)
    in_specs += ([whole(a) for a in consts] + [row(a) for a in lhs_list] + [whole(a) for a in ws]
                 + [row(a) for a in extras] + [whole(a) for a in post_consts])
    out_split = None
    out_specs = pl.BlockSpec((tm, n), lambda i: (i, 0))
    out_shape = jax.ShapeDtypeStruct((m, n), out_dtype)
    if out_rows is not None:
        ra, rb = out_rows
        assert post is not None and ra % tm == 0 and rb % tm == 0 and ra + rb == m
        out_split = ra // tm
        out_specs = [pl.BlockSpec((tm, n), lambda i: (jnp.minimum(i, out_split - 1), 0)),
                     pl.BlockSpec((tm, n), lambda i: (jnp.maximum(i - out_split, 0), 0))]
        out_shape = [jax.ShapeDtypeStruct((ra, n), out_dtype), jax.ShapeDtypeStruct((rb, n), out_dtype)]
    scratch = ([pltpu.VMEM((tm, n), F32)] if post is not None else []) + [pltpu.VMEM((tm, k), BF16)]
    body = functools.partial(
        _mm_res_body, x_split=tuple(x_split), nc=len(consts), nl=len(lhs_list), nw=len(ws), ne=len(extras),
        npc=len(post_consts), lhs_of_w=tuple(lhs_of_w), res_from_x=res_from_x, prologue=prologue,
        epilogue=epilogue, post=post, out_split=out_split, cw=cw)
    return pl.pallas_call(
        body, grid=(m // tm,), in_specs=in_specs, out_specs=out_specs, out_shape=out_shape,
        scratch_shapes=scratch,
        compiler_params=_cparams(("arbitrary",) if out_rows is not None else ("parallel",)), name=name,
    )(*arrays)


def _kv_expand_body(ckv_ref, kr_ref, wk_ref, wv_ref, k_ref, v_ref, *, nh):
    c = ckv_ref[...].astype(BF16)
    kn = jnp.dot(c, wk_ref[...], preferred_element_type=F32)
    vv = jnp.dot(c, wv_ref[...], preferred_element_type=F32)
    kr = kr_ref[...].astype(k_ref.dtype)
    ones = jnp.ones((c.shape[0], V_DIM), v_ref.dtype)
    for hh in range(nh):
        k_ref[hh, :, :QK_NOPE] = kn[:, hh * QK_NOPE:(hh + 1) * QK_NOPE].astype(k_ref.dtype)
        k_ref[hh, :, QK_NOPE:] = kr
        v_ref[hh, :, :V_DIM] = vv[:, hh * V_DIM:(hh + 1) * V_DIM].astype(v_ref.dtype)
        v_ref[hh, :, V_DIM:] = ones


def _kv_expand(ckv, kr, w_uk_flat, w_uv_flat, rows, *, tm):
    nh = w_uk_flat.shape[1] // QK_NOPE
    whole = lambda a: pl.BlockSpec(a.shape, lambda i: (0, 0), pipeline_mode=pl.Buffered(1))
    return pl.pallas_call(
        functools.partial(_kv_expand_body, nh=nh), grid=(rows // tm,),
        in_specs=[pl.BlockSpec((tm, KV_RANK), lambda i: (i, 0)), pl.BlockSpec((tm, QK_ROPE), lambda i: (i, 0)),
                  whole(w_uk_flat), whole(w_uv_flat)],
        out_specs=[pl.BlockSpec((nh, tm, QK_DIM), lambda i: (0, i, 0)),
                   pl.BlockSpec((nh, tm, 2 * V_DIM), lambda i: (0, i, 0))],
        out_shape=[jax.ShapeDtypeStruct((nh, rows, QK_DIM), BF16),
                   jax.ShapeDtypeStruct((nh, rows, 2 * V_DIM), BF16)],
        compiler_params=_cparams(("parallel",)), name="kv_heads",
    )(ckv, kr, w_uk_flat, w_uv_flat)


def _q_expand_body(ql_ref, wn_ref, wr_ref, wrr_ref, cos_ref, sin_ref, q_ref, *, nh):
    x = ql_ref[...]
    qn = jnp.dot(x, wn_ref[...], preferred_element_type=F32) * QK_PRESCALE
    reps = nh * QK_ROPE // cos_ref.shape[1]
    cos = jnp.concatenate([cos_ref[...]] * reps, axis=1)
    sin = jnp.concatenate([sin_ref[...]] * reps, axis=1)
    qr = (jnp.dot(x, wr_ref[...], preferred_element_type=F32) * cos
          + jnp.dot(x, wrr_ref[...], preferred_element_type=F32) * sin)
    for hh in range(nh):
        q_ref[hh, :, :QK_NOPE] = qn[:, hh * QK_NOPE:(hh + 1) * QK_NOPE].astype(q_ref.dtype)
        q_ref[hh, :, QK_NOPE:] = qr[:, hh * QK_ROPE:(hh + 1) * QK_ROPE].astype(q_ref.dtype)


def _q_expand(ql, w_nope, w_rope, w_rope_rot, cos2, sin2, *, tm):
    m = ql.shape[0]
    nh = w_nope.shape[1] // QK_NOPE
    whole = lambda a: pl.BlockSpec(a.shape, lambda i: (0, 0), pipeline_mode=pl.Buffered(1))
    row = lambda a: pl.BlockSpec((tm, a.shape[1]), lambda i: (i, 0))
    return pl.pallas_call(
        functools.partial(_q_expand_body, nh=nh), grid=(m // tm,),
        in_specs=[row(ql), whole(w_nope), whole(w_rope), whole(w_rope_rot), row(cos2), row(sin2)],
        out_specs=pl.BlockSpec((nh, tm, QK_DIM), lambda i: (0, i, 0)),
        out_shape=jax.ShapeDtypeStruct((nh, m, QK_DIM), BF16),
        compiler_params=_cparams(("parallel",)), name="q_heads",
    )(ql, w_nope, w_rope, w_rope_rot, cos2, sin2)


def _conv_silu_chunk(ext, in_ref, st_ref, w_ref, b_ref, first, q):
    @pl.when(first)
    def _():
        ext[0:SUBLANES, :] = st_ref[...]

    ext[SUBLANES:SUBLANES + q, :] = in_ref[...]
    acc = b_ref[...]
    for k in range(CONV_W):
        off = SUBLANES - (CONV_W - 1) + k
        acc = acc + ext[off:off + q, :] * w_ref[k:k + 1, :]
    ext[0:SUBLANES, :] = ext[q:q + SUBLANES, :]
    return _silu(acc)


def _ssd_body(x_ref, b_ref, c_ref, stx_ref, stb_ref, stc_ref, wx_ref, wb_ref, wc_ref, bx_ref, bb_ref, bc_ref,
              dt_ref, dtT_ref, bias_ref, biasT_ref, a_ref, aT_ref, dsk_ref, s0_ref,
              y_ref, so_ref, s_scr, ext_x, ext_b, ext_c, *, q, gps):
    c = pl.program_id(2)
    first = c == 0

    @pl.when(first)
    def _():
        s_scr[...] = s0_ref[...]

    xs_all = _conv_silu_chunk(ext_x, x_ref, stx_ref, wx_ref, bx_ref, first, q)
    bm_all = _conv_silu_chunk(ext_b, b_ref, stb_ref, wb_ref, bb_ref, first, q).astype(BF16)
    cm_all = _conv_silu_chunk(ext_c, c_ref, stc_ref, wc_ref, bc_ref, first, q).astype(BF16)

    row = lax.broadcasted_iota(jnp.int32, (q, q), 0)
    col = lax.broadcasted_iota(jnp.int32, (q, q), 1)
    causal = row >= col
    tri = jnp.where(causal, 1.0, 0.0).astype(BF16)
    tri_t = jnp.where(row <= col, 1.0, 0.0).astype(BF16)
    left = lax.broadcasted_iota(jnp.int32, (q, LANES), 1) < SSM_HEAD_DIM
    gw = GROUP_WIDTH
    for gi in range(gps):
        dt = _softplus(dt_ref[gi] + bias_ref[gi])
        dta = dt * a_ref[gi]
        dta_t = _softplus(dtT_ref[gi] + biasT_ref[gi]) * aT_ref[gi]
        acum = sum(jnp.dot(tri, piece, preferred_element_type=F32) for piece in _split3_bf16(dta))
        acum_t = sum(jnp.dot(piece, tri_t, preferred_element_type=F32) for piece in _split3_bf16(dta_t))
        bm = bm_all[:, gi * SSM_STATE:(gi + 1) * SSM_STATE]
        cm = cm_all[:, gi * SSM_STATE:(gi + 1) * SSM_STATE]
        cb = lax.dot_general(cm, bm, (((1,), (1,)), ((), ())), preferred_element_type=F32)
        s_prev = s_scr[gi]
        y_off = jnp.dot(cm, s_prev.astype(BF16), preferred_element_type=F32)
        xw_parts, dec_parts = [], []
        for j in range(HEADS_PER_GROUP // 2):
            h0, h1 = 2 * j, 2 * j + 1
            sl = slice(j * LANES, (j + 1) * LANES)
            sg = slice(gi * gw + j * LANES, gi * gw + (j + 1) * LANES)
            col0, col1 = acum[:, h0:h0 + 1], acum[:, h1:h1 + 1]
            pa = jnp.where(left, col0, col1)
            dtp = jnp.where(left, dt[:, h0:h0 + 1], dt[:, h1:h1 + 1])
            x = xs_all[:, sg]
            xdt = x * dtp
            m0 = (jnp.exp2(jnp.where(causal, col0 - acum_t[h0:h0 + 1, :], -jnp.inf)) * cb).astype(BF16)
            m1 = (jnp.exp2(jnp.where(causal, col1 - acum_t[h1:h1 + 1, :], -jnp.inf)) * cb).astype(BF16)
            x_l = jnp.where(left, xdt, 0.0).astype(BF16)
            x_r = jnp.where(left, 0.0, xdt).astype(BF16)
            y_diag = (jnp.dot(m0, x_l, preferred_element_type=F32)
                      + jnp.dot(m1, x_r, preferred_element_type=F32))
            y = y_diag + y_off[:, sl] * jnp.exp2(pa) + x * dsk_ref[:, sg]
            y_ref[:, sg] = y.astype(y_ref.dtype)
            last = pa[q - 1:q, :]
            xw_parts.append((xdt * jnp.exp2(last - pa)).astype(BF16))
            dec_parts.append(jnp.exp2(last))
        xw = jnp.concatenate(xw_parts, axis=1)
        dec = jnp.concatenate(dec_parts, axis=1)
        s_new = s_prev * dec + lax.dot_general(bm, xw, (((0,), (0,)), ((), ())), preferred_element_type=F32)
        s_scr[gi] = s_new
        so_ref[gi] = s_new


def _ssd_scan(xbc3, batch0, nb, L, st, w_conv, b_conv, dtg, dtg_t, bias, bias_t, a, a_t, dsk, s0, *, q, gps=2):
    g_, r_, n_ = SSM_GROUPS, HEADS_PER_GROUP, SSM_STATE
    gw = GROUP_WIDTH
    xw_, bw_ = gps * gw, gps * n_
    b_off = D_INNER // bw_
    c_off = b_off + g_ // gps
    grid = (nb, g_ // gps, L // q)
    col_x = lambda b, g, c: (0, g)
    col_b = lambda b, g, c: (0, b_off + g)
    col_c = lambda b, g, c: (0, c_off + g)
    return pl.pallas_call(
        functools.partial(_ssd_body, q=q, gps=gps), grid=grid,
        in_specs=[
            pl.BlockSpec((None, q, xw_), lambda b, g, c: (batch0 + b, c, g)),
            pl.BlockSpec((None, q, bw_), lambda b, g, c: (batch0 + b, c, b_off + g)),
            pl.BlockSpec((None, q, bw_), lambda b, g, c: (batch0 + b, c, c_off + g)),
            pl.BlockSpec((None, SUBLANES, xw_), lambda b, g, c: (b, 0, g)),
            pl.BlockSpec((None, SUBLANES, bw_), lambda b, g, c: (b, 0, b_off + g)),
            pl.BlockSpec((None, SUBLANES, bw_), lambda b, g, c: (b, 0, c_off + g)),
            pl.BlockSpec((CONV_W, xw_), col_x),
            pl.BlockSpec((CONV_W, bw_), col_b),
            pl.BlockSpec((CONV_W, bw_), col_c),
            pl.BlockSpec((1, xw_), col_x),
            pl.BlockSpec((1, bw_), col_b),
            pl.BlockSpec((1, bw_), col_c),
            pl.BlockSpec((None, gps, q, r_), lambda b, g, c: (b, g, c, 0)),
            pl.BlockSpec((None, gps, r_, q), lambda b, g, c: (b, g, 0, c)),
            pl.BlockSpec((gps, 1, r_), lambda b, g, c: (g, 0, 0)),
            pl.BlockSpec((gps, r_, 1), lambda b, g, c: (g, 0, 0)),
            pl.BlockSpec((gps, 1, r_), lambda b, g, c: (g, 0, 0)),
            pl.BlockSpec((gps, r_, 1), lambda b, g, c: (g, 0, 0)),
            pl.BlockSpec((1, xw_), col_x),
            pl.BlockSpec((None, gps, n_, gw), lambda b, g, c: (b, g, 0, 0)),
        ],
        out_specs=[
            pl.BlockSpec((None, q, xw_), lambda b, g, c: (b, c, g)),
            pl.BlockSpec((None, gps, n_, gw), lambda b, g, c: (b, g, 0, 0)),
        ],
        out_shape=[jax.ShapeDtypeStruct((nb, L, D_INNER), BF16),
                   jax.ShapeDtypeStruct((nb, g_, n_, gw), F32)],
        scratch_shapes=[pltpu.VMEM((gps, n_, gw), F32), pltpu.VMEM((q + SUBLANES, xw_), F32),
                        pltpu.VMEM((q + SUBLANES, bw_), F32), pltpu.VMEM((q + SUBLANES, bw_), F32)],
        compiler_params=_cparams(("parallel", "parallel", "arbitrary")), name="ssd_scan",
    )(xbc3, xbc3, xbc3, st, st, st, w_conv, w_conv, w_conv, b_conv, b_conv, b_conv,
      dtg, dtg_t, bias, bias_t, a, a_t, dsk, s0)


def _gather_rows(tok_ref, base, src_hbm, dst, sem, n, rp=1):
    def body(r, carry):
        t = pl.multiple_of(tok_ref[base + r] * rp, rp)
        pltpu.make_async_copy(src_hbm.at[pl.ds(t, rp)], dst.at[pl.ds(pl.multiple_of(r * rp, rp), rp)], sem).start()
        return carry

    lax.fori_loop(0, n, body, 0)


def _gather_rows_unrolled(tok_ref, base, src_hbm, dst, sem, n):
    for r in range(n):
        t = tok_ref[base + r]
        pltpu.make_async_copy(src_hbm.at[pl.ds(t, 1)], dst.at[pl.ds(r, 1)], sem).start(priority=r % 2)


def _moe_body(blk_e_ref, nxt_e_ref, tok_ref, nused_ref, xt_hbm, wg_hbm, wu_hbm, wd_hbm, yb_ref,
              xbuf, xsem, wg32, wu32, wd32, wsem, wg_bf, wu_bf, wd_bf, xb, *, bm, layer):
    b = pl.program_id(0)
    slot = b % 2
    nused = nused_ref[0]
    e_cur = blk_e_ref[b]

    def weight_copies(e):
        return (pltpu.make_async_copy(wg_hbm.at[layer, e], wg32, wsem.at[0]),
                pltpu.make_async_copy(wu_hbm.at[layer, e], wu32, wsem.at[1]),
                pltpu.make_async_copy(wd_hbm.at[layer, e], wd32, wsem.at[2]))

    def gather_unrolled(base, slot_):
        for r in range(bm):
            t = pl.multiple_of(tok_ref[base + r] * SUBLANES, SUBLANES)
            pltpu.make_async_copy(xt_hbm.at[pl.ds(t, SUBLANES)], xbuf.at[slot_, pl.ds(r * SUBLANES, SUBLANES)],
                                  xsem.at[slot_]).start()

    @pl.when(jnp.logical_and(b == 0, nused > 0))
    def _():
        for cp in weight_copies(e_cur):
            cp.start(priority=1)
        _gather_rows(tok_ref, 0, xt_hbm, xbuf.at[0], xsem.at[0], bm, rp=SUBLANES)

    prev_e = blk_e_ref[jnp.maximum(b - 1, 0)]
    changed = jnp.logical_or(b == 0, e_cur != prev_e)

    @pl.when(jnp.logical_and(changed, b < nused))
    def _():
        for cp in weight_copies(e_cur):
            cp.wait()
        wg_bf[...] = wg32[...].astype(BF16)
        wu_bf[...] = wu32[...].astype(BF16)
        wd_bf[...] = wd32[...].astype(BF16)
        nxt = nxt_e_ref[b]

        @pl.when(nxt >= 0)
        def _():
            for cp in weight_copies(nxt):
                cp.start(priority=1)

    def compute(prefetch):
        pltpu.make_async_copy(xbuf.at[slot], xbuf.at[slot], xsem.at[slot]).wait()
        half = xb.shape[1] // 2
        for s in range(SUBLANES):
            w = xbuf[slot, pl.ds(s, bm, stride=SUBLANES), :]
            cols = slice(s * LANES, (s + 1) * LANES)
            xb[:, cols] = lax.bitcast_convert_type(w << 16, F32).astype(BF16)
            xb[:, half + s * LANES:half + (s + 1) * LANES] = lax.bitcast_convert_type(
                w & jnp.uint32(0xFFFF0000), F32).astype(BF16)
        if prefetch:
            gather_unrolled((b + 1) * bm, 1 - slot)
        x = xb[...]
        g = jnp.dot(x, wg_bf[...], preferred_element_type=F32)
        u = jnp.dot(x, wu_bf[...], preferred_element_type=F32)
        hid = (_silu(g) * u).astype(BF16)
        yb_ref[...] = jnp.dot(hid, wd_bf[...], preferred_element_type=F32)

    @pl.when(b + 1 < nused)
    def _():
        compute(True)

    @pl.when(b + 1 == nused)
    def _():
        compute(False)

    @pl.when(b >= nused)
    def _():
        yb_ref[...] = jnp.zeros(yb_ref.shape, yb_ref.dtype)


def _moe_experts(xt, blk_e, nxt_e, row_token, nused, w_gate, w_up, w_down, *, bm, layer):
    nblk = blk_e.shape[0]
    d = w_gate.shape[2]
    de = w_gate.shape[3]
    assert d == 2 * SUBLANES * LANES and xt.shape[1] == LANES and xt.dtype == jnp.uint32
    anyspec = pl.BlockSpec(memory_space=pl.ANY)
    gs = pltpu.PrefetchScalarGridSpec(
        num_scalar_prefetch=4, grid=(nblk,),
        in_specs=[anyspec, anyspec, anyspec, anyspec],
        out_specs=pl.BlockSpec((bm, d), lambda b, be, nx, tok, nu: (b, 0)),
        scratch_shapes=[
            pltpu.VMEM((2, bm * SUBLANES, LANES), jnp.uint32),
            pltpu.SemaphoreType.DMA((2,)),
            pltpu.VMEM((d, de), F32),
            pltpu.VMEM((d, de), F32),
            pltpu.VMEM((de, d), F32),
            pltpu.SemaphoreType.DMA((3,)),
            pltpu.VMEM((d, de), BF16),
            pltpu.VMEM((d, de), BF16),
            pltpu.VMEM((de, d), BF16),
            pltpu.VMEM((bm, d), BF16),
        ],
    )
    return pl.pallas_call(
        functools.partial(_moe_body, bm=bm, layer=layer), grid_spec=gs,
        out_shape=jax.ShapeDtypeStruct((nblk * bm, d), F32),
        compiler_params=_cparams(("arbitrary",)), name="moe_experts",
    )(blk_e, nxt_e, row_token, nused, xt, w_gate, w_up, w_down)


def _combine_body(dest_ref, h_ref, g_ref, yb_hbm, o_ref, buf, sem, *, tc, ntile):
    i = pl.program_id(0)
    slot = i % 2
    n = TOP_K * tc

    @pl.when(i == 0)
    def _():
        _gather_rows(dest_ref, 0, yb_hbm, buf.at[0], sem.at[0], n)

    @pl.when(i + 1 < ntile)
    def _():
        _gather_rows_unrolled(dest_ref, (i + 1) * n, yb_hbm, buf.at[1 - slot], sem.at[1 - slot], n)

    pltpu.make_async_copy(buf.at[slot], buf.at[slot], sem.at[slot]).wait()
    g = g_ref[...]
    o_ref[...] = (h_ref[...] + g[:, 0:1] * buf[slot, 0:tc, :] + g[:, 1:2] * buf[slot, tc:2 * tc, :])


def _moe_combine(h, yb, dest_tiles, gates, *, tc):
    t, d = h.shape
    ntile = t // tc
    gs = pltpu.PrefetchScalarGridSpec(
        num_scalar_prefetch=1, grid=(ntile,),
        in_specs=[
            pl.BlockSpec((tc, d), lambda i, dst: (i, 0)),
            pl.BlockSpec((tc, LANES), lambda i, dst: (i, 0)),
            pl.BlockSpec(memory_space=pl.ANY),
        ],
        out_specs=pl.BlockSpec((tc, d), lambda i, dst: (i, 0)),
        scratch_shapes=[pltpu.VMEM((2, TOP_K * tc, d), F32), pltpu.SemaphoreType.DMA((2,))],
    )
    return pl.pallas_call(
        functools.partial(_combine_body, tc=tc, ntile=ntile), grid_spec=gs,
        out_shape=jax.ShapeDtypeStruct((t, d), F32),
        compiler_params=_cparams(("arbitrary",)), name="moe_combine",
    )(dest_tiles, h, gates, yb)


def _moe_plan(idx, bm, tc):
    t = idx.shape[0]
    a = t * TOP_K
    e_ = N_EXPERTS
    flat_e = idx.reshape(a)
    onehot = (flat_e[:, None] == jnp.arange(e_, dtype=jnp.int32)[None, :]).astype(jnp.int32)
    csum = jnp.cumsum(onehot, axis=0)
    rank = jnp.sum(onehot * csum, axis=1) - 1
    counts = csum[-1]
    padded = (counts + bm - 1) // bm * bm
    pad_end = jnp.cumsum(padded)
    pad_start = pad_end - padded
    dest = jnp.sum(onehot * pad_start[None, :], axis=1) + rank
    nblk = -(-a // bm) + e_
    rows = nblk * bm
    row_token = jnp.zeros((rows,), jnp.int32).at[dest].set(jnp.arange(a, dtype=jnp.int32) // TOP_K)
    blk_row0 = jnp.arange(nblk, dtype=jnp.int32) * bm
    blk_e = jnp.minimum(jnp.sum((pad_end[None, :] <= blk_row0[:, None]).astype(jnp.int32), axis=1), e_ - 1)
    ids = jnp.arange(e_, dtype=jnp.int32)
    later_used = jnp.logical_and(ids[None, :] > ids[:, None], (counts > 0)[None, :])
    nxt_of_e = jnp.min(jnp.where(later_used, ids[None, :], e_), axis=1)
    nxt_of_e = jnp.where(nxt_of_e == e_, -1, nxt_of_e)
    nxt_e = jnp.sum(jnp.where(blk_e[:, None] == ids[None, :], nxt_of_e[None, :], 0), axis=1).astype(jnp.int32)
    nused = (pad_end[-1] // bm).astype(jnp.int32).reshape(1)
    dest_tiles = dest.reshape(t // tc, tc, TOP_K).transpose(0, 2, 1).reshape(a).astype(jnp.int32)
    return blk_e.astype(jnp.int32), nxt_e, row_token, nused, dest_tiles


def _flash_body(qi_ref, kj_ref, q_ref, k_ref, v_ref, o_ref, m_scr, acc_scr, *, hb):
    p_id = pl.program_id(1)
    qi = qi_ref[p_id]
    kj = kj_ref[p_id]
    tk = k_ref.shape[1]

    @pl.when(kj == 0)
    def _():
        m_scr[...] = jnp.full(m_scr.shape, -jnp.inf, F32)
        acc_scr[...] = jnp.zeros(acc_scr.shape, F32)

    def step(diag):
        for hh in range(hb):
            s = lax.dot_general(q_ref[hh], k_ref[hh], (((1,), (1,)), ((), ())), preferred_element_type=F32)
            if diag:
                r = lax.broadcasted_iota(jnp.int32, s.shape, 0)
                c = lax.broadcasted_iota(jnp.int32, s.shape, 1)
                s = jnp.where((c // CHUNK) <= (r // CHUNK), s, -jnp.inf)
            m_prev = m_scr[hh]
            m_next = jnp.maximum(m_prev, jnp.max(s, axis=1, keepdims=True))
            p = jnp.exp2(s - jnp.concatenate([m_next] * (tk // LANES), axis=1))
            alpha = jnp.exp2(m_prev - m_next)
            acc_scr[hh] = (jnp.concatenate([alpha, alpha], axis=1) * acc_scr[hh]
                           + jnp.dot(p.astype(BF16), v_ref[hh], preferred_element_type=F32))
            m_scr[hh] = m_next

    @pl.when(kj < qi)
    def _():
        step(False)

    @pl.when(kj == qi)
    def _():
        step(True)
        for hh in range(hb):
            a = acc_scr[hh]
            o_ref[:, hh * V_DIM:(hh + 1) * V_DIM] = (a[:, :V_DIM] / a[:, V_DIM:]).astype(o_ref.dtype)


def _flash_prompt(q, k, v, lp, *, tq, hb):
    assert V_DIM == LANES
    nh = q.shape[0]
    nq = lp // tq
    pairs = [(a, b) for a in range(nq) for b in range(a + 1)]
    qi_tab = jnp.asarray(np.array([p[0] for p in pairs], np.int32))
    kj_tab = jnp.asarray(np.array([p[1] for p in pairs], np.int32))
    gs = pltpu.PrefetchScalarGridSpec(
        num_scalar_prefetch=2, grid=(nh // hb, len(pairs)),
        in_specs=[
            pl.BlockSpec((hb, tq, QK_DIM), lambda h, p, qi, kj: (h, qi[p], 0)),
            pl.BlockSpec((hb, tq, QK_DIM), lambda h, p, qi, kj: (h, kj[p], 0)),
            pl.BlockSpec((hb, tq, 2 * V_DIM), lambda h, p, qi, kj: (h, kj[p], 0)),
        ],
        out_specs=pl.BlockSpec((tq, hb * V_DIM), lambda h, p, qi, kj: (qi[p], h)),
        scratch_shapes=[pltpu.VMEM((hb, tq, LANES), F32), pltpu.VMEM((hb, tq, 2 * V_DIM), F32)],
    )
    return pl.pallas_call(
        functools.partial(_flash_body, hb=hb), grid_spec=gs,
        out_shape=jax.ShapeDtypeStruct((lp, nh * V_DIM), BF16),
        compiler_params=_cparams(("parallel", "arbitrary")), name="flash_prompt",
    )(qi_tab, kj_tab, q, k, v)


def _attn_sample_body(q_ref, ckv_ref, kr_ref, ckv_new_ref, kr_new_ref, o_ref, *, ls, past):
    nh = q_ref.shape[0]
    q = q_ref[...].reshape(nh * ls, q_ref.shape[2])
    q_lat, q_rot = q[:, :KV_RANK], q[:, KV_RANK:KV_RANK + QK_ROPE]
    nt = (((1,), (1,)), ((), ()))

    def scores(lat, rot, pos0):
        s = (lax.dot_general(q_lat, lat, nt, preferred_element_type=F32)
             + lax.dot_general(q_rot, rot, nt, preferred_element_type=F32))
        r = lax.broadcasted_iota(jnp.int32, s.shape, 0)
        c = lax.broadcasted_iota(jnp.int32, s.shape, 1)
        q_pos = past + r % ls
        return jnp.where(((pos0 + c) // CHUNK) <= (q_pos // CHUNK), s, -jnp.inf)

    lat_p = ckv_ref[...].astype(BF16)
    lat_n = ckv_new_ref[...].astype(BF16)
    s_p = scores(lat_p, kr_ref[...].astype(BF16), 0)
    s_n = scores(lat_n, kr_new_ref[...].astype(BF16), past)
    m = jnp.maximum(jnp.max(s_p, axis=-1, keepdims=True), jnp.max(s_n, axis=-1, keepdims=True))
    p_p = jnp.exp2(s_p - m)
    p_n = jnp.exp2(s_n - m)
    l = jnp.sum(p_p, axis=-1, keepdims=True) + jnp.sum(p_n, axis=-1, keepdims=True)
    o = (jnp.dot(p_p.astype(BF16), lat_p, preferred_element_type=F32)
         + jnp.dot(p_n.astype(BF16), lat_n, preferred_element_type=F32)) / l
    o_ref[...] = o.reshape(nh, ls, KV_RANK).astype(o_ref.dtype)


def _attn_sample(q_abs, cache_ckv, cache_kr, ckv_new, kr_new, *, ls):
    nh, ts, dk = q_abs.shape
    nb, past, _ = cache_ckv.shape
    return pl.pallas_call(
        functools.partial(_attn_sample_body, ls=ls, past=past), grid=(nb,),
        in_specs=[
            pl.BlockSpec((nh, ls, dk), lambda b: (0, b, 0)),
            pl.BlockSpec((None, past, KV_RANK), lambda b: (b, 0, 0)),
            pl.BlockSpec((None, past, QK_ROPE), lambda b: (b, 0, 0)),
            pl.BlockSpec((None, ls, KV_RANK), lambda b: (b, 0, 0)),
            pl.BlockSpec((None, ls, QK_ROPE), lambda b: (b, 0, 0)),
        ],
        out_specs=pl.BlockSpec((nh, ls, KV_RANK), lambda b: (0, b, 0)),
        out_shape=jax.ShapeDtypeStruct((nh, ts, KV_RANK), BF16),
        compiler_params=_cparams(("parallel",)), name="attn_sample",
    )(q_abs, cache_ckv, cache_kr, ckv_new, kr_new)


def _rope_tables(pos):
    half = QK_ROPE // 2
    inv = ROPE_THETA ** (-np.arange(half, dtype=np.float64) / half)
    ang = np.asarray(pos, np.float64)[:, None] * inv[None, :]
    cos = np.concatenate([np.cos(ang), np.cos(ang)], axis=1)
    sin = np.concatenate([-np.sin(ang), np.sin(ang)], axis=1)
    return cos.astype(np.float32), sin.astype(np.float32)


def _swap_rope_halves(w):
    half = QK_ROPE // 2
    return jnp.concatenate([w[..., half:], w[..., :half]], axis=-1)


def _moe_layer(h, g_ffn, w_rg, b_rg, w_re, b_re, w_gate, w_up, w_down, *, tm, layer):
    t, d = h.shape
    npad = LANES - MOE_GROUPS - N_EXPERTS
    w_r = jnp.concatenate([w_rg, w_re, jnp.zeros((d, npad), F32)], axis=1)
    b_r = jnp.concatenate([b_rg, b_re, jnp.zeros((npad,), F32)]).reshape(1, LANES)
    grid = (t // tm, 1)
    x_args = [(h, pl.BlockSpec((tm, d), lambda i, j: (i, 0))),
              (g_ffn.reshape(1, d), pl.BlockSpec((1, d), lambda i, j: (0, 0)))]
    w_hi = w_r.astype(BF16)
    w_lo = (w_r - w_hi.astype(F32)).astype(BF16)
    w_args = [(jnp.concatenate([w_hi, w_hi, w_lo], axis=0), pl.BlockSpec((3 * d, LANES), lambda i, j: (0, 0)))]
    e_args = [(b_r, pl.BlockSpec((1, LANES), lambda i, j: (0, 0)))]
    tile = pl.BlockSpec((tm, LANES), lambda i, j: (i, 0))
    out_defs = [(jax.ShapeDtypeStruct((t, LANES), jnp.int32), tile),
                (jax.ShapeDtypeStruct((t, LANES), F32), tile)]
    assert d == 2 * SUBLANES * LANES
    xn_out = (jax.ShapeDtypeStruct((t * SUBLANES, LANES), jnp.uint32),
              pl.BlockSpec((tm * SUBLANES, LANES), lambda i, j: (i, 0)))
    idx_t, gates_t, xt = _fused_mm(x_args, w_args, e_args, out_defs, grid=grid, prologue=_pro_rms,
                                   epilogue=_epi_route, xn_shape=(tm, 3 * d), xn_dtype=BF16, xn_out=xn_out,
                                   xn_emit=_emit_token_tiles, xn_store=_split_hi_lo_hi, name="router")
    tc = min(tm, 128)
    blk_e, nxt_e, row_token, nused, dest_tiles = _moe_plan(idx_t[:, :TOP_K], MOE_BM, tc)
    yb = _moe_experts(xt, blk_e, nxt_e, row_token, nused, w_gate, w_up, w_down, bm=MOE_BM, layer=layer)
    return _moe_combine(h, yb, dest_tiles, gates_t, tc=tc)


def _post_rms(rows, g):
    return _rms_rows(rows) * g


def _ple_layer(h, p_rows, g_ple, w_gate, w_proj, *, tm, final=None):
    t, d = h.shape
    extra = {}
    if final is not None:
        extra = dict(post=_post_rms, post_consts=[final[0].reshape(1, d)], out_rows=final[1:])
    return _mm_resident([h], [g_ple.reshape(1, d)], [p_rows.astype(BF16)], [w_gate.astype(BF16), w_proj.astype(BF16)],
                        [0, 1], [], prologue=_pro_rms, epilogue=_epi_ple, out_dtype=F32, tm=tm, name="ple_gate",
                        res_from_x=True, **extra)


def kernel(x_prompt, x_sample, state_conv, state_ssm, cache_kv_latent, cache_k_rope, p_prompt, p_sample,
           g_mix, w_ssm_in, w_conv, b_conv, dt_bias, a_log, d_skip, g_ssm_norm, w_ssm_out,
           g_kv_in, w_dkv, g_kv, w_uk, w_uv, w_dq, g_q, w_uq, w_o,
           g_ffn, w_router_grp, b_router_grp, w_router_exp, b_router_exp, w_exp_gate, w_exp_up, w_exp_down,
           g_ple, w_ple_gate, w_ple_proj, g_final):
    bp, lp_each, d = x_prompt.shape
    bs, ls, _ = x_sample.shape
    past = cache_kv_latent.shape[1]
    assert bp == 1 and w_ssm_in.shape[0] == 1 and w_dq.shape[0] == 1 and d == D_MODEL
    assert ls >= CONV_W - 1 and ls % SUBLANES == 0 and past % CHUNK == 0
    lp = bp * lp_each
    ts = bs * ls
    t = lp + ts
    tm = min(512, math.gcd(lp, ts))
    g_, r_, n_, hd = SSM_GROUPS, HEADS_PER_GROUP, SSM_STATE, SSM_HEAD_DIM

    h = jnp.concatenate([x_prompt.reshape(lp, d), x_sample.reshape(ts, d)], axis=0)

    w_in = w_ssm_in[0]
    w_z = w_in[:, :D_INNER].astype(BF16)
    xw = SSM_IN_XW
    w_x = jnp.pad(w_in[:, D_INNER:], ((0, 0), (0, xw - CONV_DIM - SSM_HEADS))).astype(BF16)
    g0 = g_mix[0].reshape(1, d)
    tm_in = max(c for c in range(16, 1153, 16) if t % c == 0)
    mm_in = functools.partial(_mm_rows, [h], [g0], prologue=_pro_rms, tm=tm_in)
    z = mm_in(w_z, [], epilogue=_epi_silu, out_dtype=BF16, tn=1024, name="ssm_in_z")[0]
    xbc = mm_in(w_x, [], epilogue=_epi_id, out_dtype=F32, tn=xw // 5, name="ssm_in_xbc")[0]
    dt_raw = xbc[:, CONV_DIM:CONV_DIM + SSM_HEADS]

    wc = w_conv[0]
    bc = b_conv[0].reshape(1, CONV_DIM)
    st_p = jnp.zeros((1, SUBLANES, CONV_DIM), F32)
    st_s = jnp.pad(state_conv[0].astype(F32), ((0, 0), (SUBLANES - (CONV_W - 1), 0), (0, 0)))
    conv_p = xbc[lp - (CONV_W - 1):lp, :CONV_DIM].reshape(1, 1, CONV_W - 1, CONV_DIM)
    conv_s = xbc[lp:, :CONV_DIM].reshape(bs, ls, CONV_DIM)[:, ls - (CONV_W - 1):].reshape(
        1, bs, CONV_W - 1, CONV_DIM)

    bias = dt_bias[0].astype(F32).reshape(g_, 1, r_)
    a_neg = (-jnp.exp(a_log[0].astype(F32)) * math.log2(math.e)).reshape(g_, 1, r_)
    dsk = jnp.repeat(d_skip[0].astype(F32), hd).reshape(1, D_INNER)

    def dt_views(rows, nb, L):
        v = rows.reshape(nb, L, g_, r_).transpose(0, 2, 1, 3)
        return v, v.transpose(0, 1, 3, 2)

    def scan(xbc3, batch0, nb, L, st, rows, s0, q):
        dtg, dtg_t = dt_views(rows, nb, L)
        return _ssd_scan(xbc3, batch0, nb, L, st, wc, bc, dtg, dtg_t, bias, bias.transpose(0, 2, 1),
                         a_neg, a_neg.transpose(0, 2, 1), dsk, s0, q=q)

    def state_in(s):
        nb = s.shape[0]
        return s.astype(F32).reshape(nb, g_, r_ * hd, n_).transpose(0, 1, 3, 2)

    def state_out(s):
        nb = s.shape[0]
        return s.transpose(0, 1, 3, 2).reshape(1, nb, SSM_HEADS, hd, n_)

    y_p, s_p = scan(xbc.reshape(1, t, xw), 0, 1, lp, st_p, dt_raw[:lp],
                    jnp.zeros((1, g_, n_, r_ * hd), F32), min(SSD_Q, lp))
    y_s, s_s = scan(xbc.reshape(t // ls, ls, xw), lp // ls, bs, ls, st_s, dt_raw[lp:],
                    state_in(state_ssm[0]), ls)
    ssm_p, ssm_s = state_out(s_p), state_out(s_s)
    y_all = (y_p.reshape(lp, D_INNER), y_s.reshape(ts, D_INNER))

    h = _mm_resident([y_all, z], [g_ssm_norm[0].reshape(1, D_INNER)], [], [w_ssm_out[0].astype(BF16)], [0], [h],
                     prologue=_pro_gated, epilogue=_epi_res, out_dtype=F32, tm=tm // 2, name="ssm_out")

    def ffn_and_ple(h, i, final=None):
        h = _moe_layer(h, g_ffn[i], w_router_grp[i], b_router_grp[i], w_router_exp[i], b_router_exp[i],
                       w_exp_gate, w_exp_up, w_exp_down, tm=tm, layer=i)
        p_rows = jnp.concatenate([p_prompt[i].reshape(lp, -1), p_sample[i].reshape(ts, -1)], axis=0)
        return _ple_layer(h, p_rows, g_ple[i], w_ple_gate[i], w_ple_proj[i], tm=tm, final=final)

    h = ffn_and_ple(h, 0)

    pos = np.concatenate([np.arange(lp), np.tile(past + np.arange(ls), bs)])
    cos64, sin64 = _rope_tables(pos)
    gk = g_kv_in.reshape(1, d)
    w_r = w_dkv[:, KV_RANK:]
    row64 = pl.BlockSpec((tm, QK_ROPE), lambda i, j: (i, 0))
    w64 = pl.BlockSpec((d, QK_ROPE), lambda i, j: (0, 0))
    whole = lambda a: pl.BlockSpec(a.shape, lambda i, j: (0, 0))
    w_lat = w_dkv[:, :KV_RANK].astype(BF16)
    g_kv_row = g_kv.reshape(1, KV_RANK)
    ckv, kr = _fused_mm(
        [(h, pl.BlockSpec((tm, d), lambda i, j: (i, 0))), (gk, whole(gk))],
        [(w_lat, whole(w_lat)), (w_r.astype(BF16), w64), (_swap_rope_halves(w_r).astype(BF16), w64)],
        [(g_kv_row, whole(g_kv_row)), (jnp.asarray(cos64), row64), (jnp.asarray(sin64), row64)],
        [(jax.ShapeDtypeStruct((t, KV_RANK), F32), pl.BlockSpec((tm, KV_RANK), lambda i, j: (i, 0))),
         (jax.ShapeDtypeStruct((t, QK_ROPE), F32), row64)],
        grid=(t // tm, 1), prologue=_pro_rms, epilogue=_epi_latent_kv, xn_shape=(tm, d), name="kv_latent")

    ql = _mm_rows([h], [g_mix[1].reshape(1, d)], w_dq[0].astype(BF16), [], prologue=_pro_rms,
                  epilogue=_epi_rms_out, out_dtype=BF16, tm=tm, tn=Q_RANK, name="q_latent",
                  e_consts=[g_q[0].reshape(1, Q_RANK)])[0]
    nh = MLA_HEADS
    w_q = w_uq[0].reshape(Q_RANK, nh, QK_DIM)
    w_q_rope = w_q[..., QK_NOPE:]
    cos_q = np.concatenate([cos64, cos64], axis=1) * np.float32(QK_PRESCALE)
    sin_q = np.concatenate([sin64, sin64], axis=1) * np.float32(QK_PRESCALE)
    q_cat = _q_expand(ql, w_q[..., :QK_NOPE].reshape(Q_RANK, nh * QK_NOPE).astype(BF16),
                      w_q_rope.reshape(Q_RANK, nh * QK_ROPE).astype(BF16),
                      _swap_rope_halves(w_q_rope).reshape(Q_RANK, nh * QK_ROPE).astype(BF16),
                      jnp.asarray(cos_q), jnp.asarray(sin_q), tm=tm)

    lat_pad = LAT_PAD - KV_RANK - QK_ROPE
    w_uk_h = w_uk.transpose(1, 0, 2)
    w_uv_h = w_uv.transpose(1, 0, 2)
    eye_r = jnp.broadcast_to(jnp.eye(QK_ROPE, dtype=F32), (nh, QK_ROPE, QK_ROPE))
    k_cat, v_h = _kv_expand(ckv, kr, w_uk.reshape(KV_RANK, nh * QK_NOPE).astype(BF16),
                            w_uv.reshape(KV_RANK, nh * V_DIM).astype(BF16), lp, tm=tm)
    o_p = _flash_prompt(q_cat, k_cat, v_h, lp, tq=min(512, lp), hb=4)

    w_abs = jnp.concatenate([
        jnp.concatenate([w_uk_h.transpose(0, 2, 1), jnp.zeros((nh, QK_NOPE, LAT_PAD - KV_RANK), F32)], axis=2),
        jnp.concatenate([jnp.zeros((nh, QK_ROPE, KV_RANK), F32), eye_r,
                         jnp.zeros((nh, QK_ROPE, lat_pad), F32)], axis=2)], axis=1).astype(BF16)
    blk_s = lp // ts if lp % ts == 0 else None
    assert blk_s is not None
    q_abs = _fused_mm(
        [(q_cat, pl.BlockSpec((None, ts, QK_DIM), lambda i, j: (j, blk_s + i, 0)))],
        [(w_abs, pl.BlockSpec((None, QK_DIM, LAT_PAD), lambda i, j: (j, 0, 0)))],
        [],
        [(jax.ShapeDtypeStruct((nh, ts, LAT_PAD), BF16), pl.BlockSpec((None, ts, LAT_PAD), lambda i, j: (j, i, 0)))],
        grid=(1, nh), prologue=_pro_cast, epilogue=_epi_id, x_per_j=True, name="q_absorb")[0]
    o_lat = _attn_sample(q_abs, cache_kv_latent, cache_k_rope, ckv[lp:].reshape(bs, ls, KV_RANK),
                         kr[lp:].reshape(bs, ls, QK_ROPE), ls=ls)
    o_s = _fused_mm(
        [(o_lat, pl.BlockSpec((None, ts, KV_RANK), lambda i, j: (j, i, 0)))],
        [(w_uv_h.astype(BF16), pl.BlockSpec((None, KV_RANK, V_DIM), lambda i, j: (j, 0, 0)))],
        [],
        [(jax.ShapeDtypeStruct((ts, nh * V_DIM), BF16), pl.BlockSpec((ts, V_DIM), lambda i, j: (i, j)))],
        grid=(1, nh), prologue=_pro_cast, epilogue=_epi_id, x_per_j=True, name="v_absorb")[0]
    h = _mm_resident([(o_p, o_s)], [], [], [w_o[0].astype(BF16)], [0], [h], prologue=_pro_cast, epilogue=_epi_res,
                     out_dtype=F32, tm=tm, name="attn_out")
    y_p_out, y_s_out = ffn_and_ple(h, 1, final=(g_final, lp, ts))
    y_p_out = y_p_out.reshape(bp, lp_each, d)
    y_s_out = y_s_out.reshape(bs, ls, d)
    return (y_p_out, y_s_out, conv_p, ssm_p, ckv[:lp].reshape(bp, lp_each, KV_RANK),
            kr[:lp].reshape(bp, lp_each, QK_ROPE), conv_s, ssm_s, ckv[lp:].reshape(bs, ls, KV_RANK),
            kr[lp:].reshape(bs, ls, QK_ROPE))
```

```python
import functools
import math

import numpy as np
import jax
import jax.numpy as jnp
from jax import lax
from jax.experimental import pallas as pl
from jax.experimental.pallas import tpu as pltpu

F32 = jnp.float32
BF16 = jnp.bfloat16

EPS = 1e-6
CHUNK = 64
D_MODEL = 2048
D_INNER = 2 * D_MODEL
SSM_HEAD_DIM = 64
SSM_HEADS = D_INNER // SSM_HEAD_DIM
SSM_STATE = 128
SSM_GROUPS = 8
HEADS_PER_GROUP = SSM_HEADS // SSM_GROUPS
GROUP_WIDTH = D_INNER // SSM_GROUPS
CONV_W = 4
CONV_DIM = D_INNER + 2 * SSM_GROUPS * SSM_STATE
MLA_HEADS = 16
Q_RANK = 512
KV_RANK = 512
QK_NOPE = 128
QK_ROPE = 64
QK_DIM = QK_NOPE + QK_ROPE
V_DIM = 128
ROPE_THETA = 10000.0
ATTN_SCALE = QK_DIM ** -0.5
QK_PRESCALE = ATTN_SCALE * math.log2(math.e)
MOE_GROUPS = 4
EXPERTS_PER_GROUP = 8
N_EXPERTS = MOE_GROUPS * EXPERTS_PER_GROUP
TOP_K = 2
D_EXPERT = 512

LANES = 128
SUBLANES = 8
VMEM_LIMIT = 56 * 1024 * 1024
SSD_Q = 128
MOE_BM = 128
LAT_PAD = 640
SSM_IN_XW = 6400


def _cparams(sem):
    return pltpu.CompilerParams(dimension_semantics=sem, vmem_limit_bytes=VMEM_LIMIT)


def _sigmoid(v):
    return 1.0 / (1.0 + jnp.exp(-v))


def _silu(v):
    return v * _sigmoid(v)


def _softplus(v):
    return jnp.maximum(v, 0.0) + jnp.log1p(jnp.exp(-jnp.abs(v)))


def _rms_rows(x):
    return x * lax.rsqrt(jnp.mean(x * x, axis=-1, keepdims=True) + EPS)


def _mm_body(*refs, nx, nw, ne, no, prologue, epilogue, x_per_j, emit_xn, xn_emit, xn_store, precision):
    x_refs = refs[:nx]
    w_refs = refs[nx:nx + nw]
    e_refs = refs[nx + nw:nx + nw + ne]
    o_refs = refs[nx + nw + ne:nx + nw + ne + no]
    rest = refs[nx + nw + ne + no:]
    j = pl.program_id(1)
    if x_per_j:
        xn = prologue(*[r[...] for r in x_refs])
    else:
        xn_ref = rest[-1]

        @pl.when(j == 0)
        def _():
            v = prologue(*[r[...] for r in x_refs])
            xn_ref[...] = xn_store(v).astype(xn_ref.dtype)
            if emit_xn:
                xn_emit(v, rest[0])

        xn = xn_ref[...]
    accs = [jnp.dot(xn, w[...], preferred_element_type=F32, precision=precision) for w in w_refs]
    outs = epilogue(accs, [e[...] for e in e_refs])
    for o_ref, o in zip(o_refs, outs):
        o_ref[...] = o.astype(o_ref.dtype)


def _fused_mm(x_args, w_args, e_args, out_defs, *, grid, prologue, epilogue, xn_shape=None,
              xn_dtype=BF16, x_per_j=False, xn_out=None, xn_emit=None, xn_store=None, precision=None, name=None):
    arrays = [a for a, _ in x_args + w_args + e_args]
    in_specs = [s for _, s in x_args + w_args + e_args]
    out_shape = [d for d, _ in out_defs]
    out_specs = [s for _, s in out_defs]
    emit_xn = xn_out is not None
    if emit_xn:
        out_shape.append(xn_out[0])
        out_specs.append(xn_out[1])
    scratch = [] if x_per_j else [pltpu.VMEM(xn_shape, xn_dtype)]
    body = functools.partial(
        _mm_body, nx=len(x_args), nw=len(w_args), ne=len(e_args), no=len(out_defs),
        prologue=prologue, epilogue=epilogue, x_per_j=x_per_j, emit_xn=emit_xn,
        xn_emit=xn_emit or _emit_cast, xn_store=xn_store or _pro_cast, precision=precision)
    return pl.pallas_call(
        body, grid=grid, in_specs=in_specs, out_specs=out_specs, out_shape=out_shape,
        scratch_shapes=scratch, compiler_params=_cparams(("parallel", "arbitrary")), name=name,
    )(*arrays)


def _pro_rms(x, g):
    return _rms_rows(x.astype(F32)) * g


def _pro_cast(x):
    return x


def _emit_cast(v, o_ref):
    o_ref[...] = v.astype(o_ref.dtype)


def _emit_token_tiles(v, o_ref):
    w = _pack_bf16_pairs(v)
    m = w.shape[0]
    for s in range(SUBLANES):
        o_ref[pl.ds(s, m, stride=SUBLANES), :] = w[:, s * LANES:(s + 1) * LANES]


def _split3_bf16(v):
    hi = v.astype(BF16)
    r1 = v - hi.astype(F32)
    mid = r1.astype(BF16)
    return hi, mid, (r1 - mid.astype(F32)).astype(BF16)


def _split_hi_lo_hi(v):
    hi = v.astype(BF16)
    lo = (v - hi.astype(F32)).astype(BF16)
    return jnp.concatenate([hi, lo, hi], axis=1)


def _pack_bf16_pairs(v):
    k = v.shape[1] // 2
    lo = lax.bitcast_convert_type(v[:, :k].astype(BF16).astype(F32), jnp.uint32) >> 16
    hi = lax.bitcast_convert_type(v[:, k:].astype(BF16).astype(F32), jnp.uint32) & jnp.uint32(0xFFFF0000)
    return hi | lo


def _epi_silu(accs, es):
    return [_silu(accs[0])]


def _pro_gated(y, zs, g):
    v = y.astype(F32) * zs.astype(F32)
    parts = [_rms_rows(v[:, k * GROUP_WIDTH:(k + 1) * GROUP_WIDTH]) for k in range(SSM_GROUPS)]
    return jnp.concatenate(parts, axis=-1) * g


def _epi_id(accs, es):
    return accs


def _epi_res(accs, es):
    return [es[0] + accs[0]]


def _epi_ple(accs, es):
    return [es[0] + _sigmoid(accs[0]) * accs[1]]


def _epi_rms_out(accs, es):
    return [_rms_rows(accs[0]) * es[0]]


def _epi_latent_kv(accs, es):
    g_kv, cos, sin = es
    return [_rms_rows(accs[0]) * g_kv, accs[1] * cos + accs[2] * sin]


def _epi_route(accs, es):
    lg = accs[0] + es[0]
    lane = lax.broadcasted_iota(jnp.int32, lg.shape, 1).astype(F32)
    neg = -jnp.inf
    big = 1.0e4
    is_grp = lane < MOE_GROUPS
    gl = jnp.where(is_grp, lg, neg)
    mg = jnp.max(gl, axis=-1, keepdims=True)
    g_sel = jnp.min(jnp.where(gl == mg, lane, big), axis=-1, keepdims=True)
    p_sel = 1.0 / jnp.sum(jnp.where(is_grp, jnp.exp(gl - mg), 0.0), axis=-1, keepdims=True)
    lo = MOE_GROUPS + g_sel * EXPERTS_PER_GROUP
    in_grp = jnp.where(lane >= lo, jnp.where(lane < lo + EXPERTS_PER_GROUP, 1.0, 0.0), 0.0) > 0.5
    el = jnp.where(in_grp, lg, neg)
    v1 = jnp.max(el, axis=-1, keepdims=True)
    i1 = jnp.min(jnp.where(el == v1, lane, big), axis=-1, keepdims=True)
    el2 = jnp.where(lane == i1, neg, el)
    v2 = jnp.max(el2, axis=-1, keepdims=True)
    i2 = jnp.min(jnp.where(el2 == v2, jnp.where(lane == i1, big, lane), big), axis=-1, keepdims=True)
    e21 = jnp.exp(v2 - v1)
    g1 = p_sel / (1.0 + e21)
    g2 = p_sel * e21 / (1.0 + e21)
    idx = jnp.where(lane == 0.0, i1 - MOE_GROUPS, jnp.where(lane == 1.0, i2 - MOE_GROUPS, 0.0))
    gates = jnp.where(lane == 0.0, g1, jnp.where(lane == 1.0, g2, 0.0))
    return [idx.astype(jnp.int32), gates]


def _mm_rows(x_list, consts, w, extras, *, prologue, epilogue, out_dtype, tm, tn, name,
             xn_dtype=BF16, precision=None, emit_xn_dtype=None, e_consts=()):
    m = x_list[0].shape[0]
    k, n = w.shape
    grid = (m // tm, n // tn)
    x_args = [(x, pl.BlockSpec((tm, x.shape[1]), lambda i, j: (i, 0))) for x in x_list]
    x_args += [(c, pl.BlockSpec((1, c.shape[1]), lambda i, j: (0, 0))) for c in consts]
    w_args = [(w, pl.BlockSpec((k, tn), lambda i, j: (0, j)))]
    e_args = [(c, pl.BlockSpec((1, tn), lambda i, j: (0, j))) for c in e_consts]
    e_args += [(e, pl.BlockSpec((tm, tn), lambda i, j: (i, j))) for e in extras]
    out_defs = [(jax.ShapeDtypeStruct((m, n), out_dtype), pl.BlockSpec((tm, tn), lambda i, j: (i, j)))]
    xn_out = None
    if emit_xn_dtype is not None:
        xn_out = (jax.ShapeDtypeStruct((m, k), emit_xn_dtype), pl.BlockSpec((tm, k), lambda i, j: (i, 0)))
    return _fused_mm(x_args, w_args, e_args, out_defs, grid=grid, prologue=prologue, epilogue=epilogue,
                     xn_shape=(tm, k), xn_dtype=xn_dtype, xn_out=xn_out, precision=precision, name=name)


def _mm_res_body(*refs, x_split, nc, nl, nw, ne, npc, lhs_of_w, res_from_x, prologue, epilogue, post, out_split,
                 cw):
    i = pl.program_id(0)
    pos, xs = 0, []
    for first_tiles in x_split:
        if first_tiles is None:
            xs.append(refs[pos][...])
            pos += 1
        else:
            xs.append(jnp.where(i < first_tiles, refs[pos][...], refs[pos + 1][...]))
            pos += 2
    x0_ref = refs[0]
    c_refs = refs[pos:pos + nc]
    l_refs = refs[pos + nc:pos + nc + nl]
    w_refs = refs[pos + nc + nl:pos + nc + nl + nw]
    e_refs = refs[pos + nc + nl + nw:pos + nc + nl + nw + ne]
    p_refs = refs[pos + nc + nl + nw + ne:pos + nc + nl + nw + ne + npc]
    o_refs = refs[pos + nc + nl + nw + ne + npc:-1] if post is None else refs[pos + nc + nl + nw + ne + npc:-2]
    xn_ref = refs[-1]
    rows_ref = o_refs[0] if post is None else refs[-2]
    xn_ref[...] = prologue(*xs, *[r[...] for r in c_refs]).astype(xn_ref.dtype)
    lhs = [xn_ref] + list(l_refs)
    for c in range(rows_ref.shape[1] // cw):
        sl = slice(c * cw, (c + 1) * cw)
        accs = [jnp.dot(lhs[li][...], w[:, sl], preferred_element_type=F32) for w, li in zip(w_refs, lhs_of_w)]
        es = ([x0_ref[:, sl]] if res_from_x else []) + [e[:, sl] for e in e_refs]
        rows_ref[:, sl] = epilogue(accs, es)[0].astype(rows_ref.dtype)
    if post is not None:
        val = post(rows_ref[...], *[p[...] for p in p_refs])
        if out_split is None:
            o_refs[0][...] = val.astype(o_refs[0].dtype)
        else:
            @pl.when(i < out_split)
            def _():
                o_refs[0][...] = val.astype(o_refs[0].dtype)

            @pl.when(i >= out_split)
            def _():
                o_refs[1][...] = val.astype(o_refs[1].dtype)


def _mm_resident(x_list, consts, lhs_list, ws, lhs_of_w, extras, *, prologue, epilogue, out_dtype, tm, name,
                 res_from_x=False, cw=512, post=None, post_consts=(), out_rows=None):
    rows = lambda x: x[0].shape[0] + x[1].shape[0] if isinstance(x, tuple) else x.shape[0]
    m = rows(x_list[0])
    k = ws[lhs_of_w.index(0)].shape[0]
    n = ws[0].shape[1]
    row = lambda a: pl.BlockSpec((tm, a.shape[1]), lambda i: (i, 0))
    whole = lambda a: pl.BlockSpec(a.shape, lambda i: (0, 0), pipeline_mode=pl.Buffered(1))
    arrays, in_specs, x_split = [], [], []
    for x in x_list:
        if isinstance(x, tuple):
            a, b = x
            na = a.shape[0] // tm
            assert a.shape[0] % tm == 0 and b.shape[0] % tm == 0
            arrays += [a, b]
            in_specs += [pl.BlockSpec((tm, a.shape[1]), lambda i, na=na: (jnp.minimum(i, na - 1), 0)),
                         pl.BlockSpec((tm, b.shape[1]), lambda i, na=na: (jnp.maximum(i - na, 0), 0))]
            x_split.append(na)
        else:
            arrays.append(x)
            in_specs.append(row(x))
            x_split.append(None)
    arrays += list(consts) + list(lhs_list) + list(ws) + list(extras) + list(post_consts)
    in_specs += ([whole(a) for a in consts] + [row(a) for a in lhs_list] + [whole(a) for a in ws]
                 + [row(a) for a in extras] + [whole(a) for a in post_consts])
    out_split = None
    out_specs = pl.BlockSpec((tm, n), lambda i: (i, 0))
    out_shape = jax.ShapeDtypeStruct((m, n), out_dtype)
    if out_rows is not None:
        ra, rb = out_rows
        assert post is not None and ra % tm == 0 and rb % tm == 0 and ra + rb == m
        out_split = ra // tm
        out_specs = [pl.BlockSpec((tm, n), lambda i: (jnp.minimum(i, out_split - 1), 0)),
                     pl.BlockSpec((tm, n), lambda i: (jnp.maximum(i - out_split, 0), 0))]
        out_shape = [jax.ShapeDtypeStruct((ra, n), out_dtype), jax.ShapeDtypeStruct((rb, n), out_dtype)]
    scratch = ([pltpu.VMEM((tm, n), F32)] if post is not None else []) + [pltpu.VMEM((tm, k), BF16)]
    body = functools.partial(
        _mm_res_body, x_split=tuple(x_split), nc=len(consts), nl=len(lhs_list), nw=len(ws), ne=len(extras),
        npc=len(post_consts), lhs_of_w=tuple(lhs_of_w), res_from_x=res_from_x, prologue=prologue,
        epilogue=epilogue, post=post, out_split=out_split, cw=cw)
    return pl.pallas_call(
        body, grid=(m // tm,), in_specs=in_specs, out_specs=out_specs, out_shape=out_shape,
        scratch_shapes=scratch,
        compiler_params=_cparams(("arbitrary",) if out_rows is not None else ("parallel",)), name=name,
    )(*arrays)


def _kv_expand_body(ckv_ref, kr_ref, wk_ref, wv_ref, k_ref, v_ref, *, nh):
    c = ckv_ref[...].astype(BF16)
    kn = jnp.dot(c, wk_ref[...], preferred_element_type=F32)
    vv = jnp.dot(c, wv_ref[...], preferred_element_type=F32)
    kr = kr_ref[...].astype(k_ref.dtype)
    ones = jnp.ones((c.shape[0], V_DIM), v_ref.dtype)
    for hh in range(nh):
        k_ref[hh, :, :QK_NOPE] = kn[:, hh * QK_NOPE:(hh + 1) * QK_NOPE].astype(k_ref.dtype)
        k_ref[hh, :, QK_NOPE:] = kr
        v_ref[hh, :, :V_DIM] = vv[:, hh * V_DIM:(hh + 1) * V_DIM].astype(v_ref.dtype)
        v_ref[hh, :, V_DIM:] = ones


def _kv_expand(ckv, kr, w_uk_flat, w_uv_flat, rows, *, tm):
    nh = w_uk_flat.shape[1] // QK_NOPE
    whole = lambda a: pl.BlockSpec(a.shape, lambda i: (0, 0), pipeline_mode=pl.Buffered(1))
    return pl.pallas_call(
        functools.partial(_kv_expand_body, nh=nh), grid=(rows // tm,),
        in_specs=[pl.BlockSpec((tm, KV_RANK), lambda i: (i, 0)), pl.BlockSpec((tm, QK_ROPE), lambda i: (i, 0)),
                  whole(w_uk_flat), whole(w_uv_flat)],
        out_specs=[pl.BlockSpec((nh, tm, QK_DIM), lambda i: (0, i, 0)),
                   pl.BlockSpec((nh, tm, 2 * V_DIM), lambda i: (0, i, 0))],
        out_shape=[jax.ShapeDtypeStruct((nh, rows, QK_DIM), BF16),
                   jax.ShapeDtypeStruct((nh, rows, 2 * V_DIM), BF16)],
        compiler_params=_cparams(("parallel",)), name="kv_heads",
    )(ckv, kr, w_uk_flat, w_uv_flat)


def _q_expand_body(ql_ref, wn_ref, wr_ref, wrr_ref, cos_ref, sin_ref, q_ref, *, nh):
    x = ql_ref[...]
    qn = jnp.dot(x, wn_ref[...], preferred_element_type=F32) * QK_PRESCALE
    reps = nh * QK_ROPE // cos_ref.shape[1]
    cos = jnp.concatenate([cos_ref[...]] * reps, axis=1)
    sin = jnp.concatenate([sin_ref[...]] * reps, axis=1)
    qr = (jnp.dot(x, wr_ref[...], preferred_element_type=F32) * cos
          + jnp.dot(x, wrr_ref[...], preferred_element_type=F32) * sin)
    for hh in range(nh):
        q_ref[hh, :, :QK_NOPE] = qn[:, hh * QK_NOPE:(hh + 1) * QK_NOPE].astype(q_ref.dtype)
        q_ref[hh, :, QK_NOPE:] = qr[:, hh * QK_ROPE:(hh + 1) * QK_ROPE].astype(q_ref.dtype)


def _q_expand(ql, w_nope, w_rope, w_rope_rot, cos2, sin2, *, tm):
    m = ql.shape[0]
    nh = w_nope.shape[1] // QK_NOPE
    whole = lambda a: pl.BlockSpec(a.shape, lambda i: (0, 0), pipeline_mode=pl.Buffered(1))
    row = lambda a: pl.BlockSpec((tm, a.shape[1]), lambda i: (i, 0))
    return pl.pallas_call(
        functools.partial(_q_expand_body, nh=nh), grid=(m // tm,),
        in_specs=[row(ql), whole(w_nope), whole(w_rope), whole(w_rope_rot), row(cos2), row(sin2)],
        out_specs=pl.BlockSpec((nh, tm, QK_DIM), lambda i: (0, i, 0)),
        out_shape=jax.ShapeDtypeStruct((nh, m, QK_DIM), BF16),
        compiler_params=_cparams(("parallel",)), name="q_heads",
    )(ql, w_nope, w_rope, w_rope_rot, cos2, sin2)


def _conv_silu_chunk(ext, in_ref, st_ref, w_ref, b_ref, first, q):
    @pl.when(first)
    def _():
        ext[0:SUBLANES, :] = st_ref[...]

    ext[SUBLANES:SUBLANES + q, :] = in_ref[...]
    acc = b_ref[...]
    for k in range(CONV_W):
        off = SUBLANES - (CONV_W - 1) + k
        acc = acc + ext[off:off + q, :] * w_ref[k:k + 1, :]
    ext[0:SUBLANES, :] = ext[q:q + SUBLANES, :]
    return _silu(acc)


def _ssd_body(x_ref, b_ref, c_ref, stx_ref, stb_ref, stc_ref, wx_ref, wb_ref, wc_ref, bx_ref, bb_ref, bc_ref,
              dt_ref, dtT_ref, bias_ref, biasT_ref, a_ref, aT_ref, dsk_ref, s0_ref,
              y_ref, so_ref, s_scr, ext_x, ext_b, ext_c, *, q, gps):
    c = pl.program_id(2)
    first = c == 0

    @pl.when(first)
    def _():
        s_scr[...] = s0_ref[...]

    xs_all = _conv_silu_chunk(ext_x, x_ref, stx_ref, wx_ref, bx_ref, first, q)
    bm_all = _conv_silu_chunk(ext_b, b_ref, stb_ref, wb_ref, bb_ref, first, q).astype(BF16)
    cm_all = _conv_silu_chunk(ext_c, c_ref, stc_ref, wc_ref, bc_ref, first, q).astype(BF16)

    row = lax.broadcasted_iota(jnp.int32, (q, q), 0)
    col = lax.broadcasted_iota(jnp.int32, (q, q), 1)
    causal = row >= col
    tri = jnp.where(causal, 1.0, 0.0).astype(BF16)
    tri_t = jnp.where(row <= col, 1.0, 0.0).astype(BF16)
    left = lax.broadcasted_iota(jnp.int32, (q, LANES), 1) < SSM_HEAD_DIM
    gw = GROUP_WIDTH
    for gi in range(gps):
        dt = _softplus(dt_ref[gi] + bias_ref[gi])
        dta = dt * a_ref[gi]
        dta_t = _softplus(dtT_ref[gi] + biasT_ref[gi]) * aT_ref[gi]
        acum = sum(jnp.dot(tri, piece, preferred_element_type=F32) for piece in _split3_bf16(dta))
        acum_t = sum(jnp.dot(piece, tri_t, preferred_element_type=F32) for piece in _split3_bf16(dta_t))
        bm = bm_all[:, gi * SSM_STATE:(gi + 1) * SSM_STATE]
        cm = cm_all[:, gi * SSM_STATE:(gi + 1) * SSM_STATE]
        cb = lax.dot_general(cm, bm, (((1,), (1,)), ((), ())), preferred_element_type=F32)
        s_prev = s_scr[gi]
        y_off = jnp.dot(cm, s_prev.astype(BF16), preferred_element_type=F32)
        xw_parts, dec_parts = [], []
        for j in range(HEADS_PER_GROUP // 2):
            h0, h1 = 2 * j, 2 * j + 1
            sl = slice(j * LANES, (j + 1) * LANES)
            sg = slice(gi * gw + j * LANES, gi * gw + (j + 1) * LANES)
            col0, col1 = acum[:, h0:h0 + 1], acum[:, h1:h1 + 1]
            pa = jnp.where(left, col0, col1)
            dtp = jnp.where(left, dt[:, h0:h0 + 1], dt[:, h1:h1 + 1])
            x = xs_all[:, sg]
            xdt = x * dtp
            m0 = (jnp.exp2(jnp.where(causal, col0 - acum_t[h0:h0 + 1, :], -jnp.inf)) * cb).astype(BF16)
            m1 = (jnp.exp2(jnp.where(causal, col1 - acum_t[h1:h1 + 1, :], -jnp.inf)) * cb).astype(BF16)
            x_l = jnp.where(left, xdt, 0.0).astype(BF16)
            x_r = jnp.where(left, 0.0, xdt).astype(BF16)
            y_diag = (jnp.dot(m0, x_l, preferred_element_type=F32)
                      + jnp.dot(m1, x_r, preferred_element_type=F32))
            y = y_diag + y_off[:, sl] * jnp.exp2(pa) + x * dsk_ref[:, sg]
            y_ref[:, sg] = y.astype(y_ref.dtype)
            last = pa[q - 1:q, :]
            xw_parts.append((xdt * jnp.exp2(last - pa)).astype(BF16))
            dec_parts.append(jnp.exp2(last))
        xw = jnp.concatenate(xw_parts, axis=1)
        dec = jnp.concatenate(dec_parts, axis=1)
        s_new = s_prev * dec + lax.dot_general(bm, xw, (((0,), (0,)), ((), ())), preferred_element_type=F32)
        s_scr[gi] = s_new
        so_ref[gi] = s_new


def _ssd_scan(xbc3, batch0, nb, L, st, w_conv, b_conv, dtg, dtg_t, bias, bias_t, a, a_t, dsk, s0, *, q, gps=2):
    g_, r_, n_ = SSM_GROUPS, HEADS_PER_GROUP, SSM_STATE
    gw = GROUP_WIDTH
    xw_, bw_ = gps * gw, gps * n_
    b_off = D_INNER // bw_
    c_off = b_off + g_ // gps
    grid = (nb, g_ // gps, L // q)
    col_x = lambda b, g, c: (0, g)
    col_b = lambda b, g, c: (0, b_off + g)
    col_c = lambda b, g, c: (0, c_off + g)
    return pl.pallas_call(
        functools.partial(_ssd_body, q=q, gps=gps), grid=grid,
        in_specs=[
            pl.BlockSpec((None, q, xw_), lambda b, g, c: (batch0 + b, c, g)),
            pl.BlockSpec((None, q, bw_), lambda b, g, c: (batch0 + b, c, b_off + g)),
            pl.BlockSpec((None, q, bw_), lambda b, g, c: (batch0 + b, c, c_off + g)),
            pl.BlockSpec((None, SUBLANES, xw_), lambda b, g, c: (b, 0, g)),
            pl.BlockSpec((None, SUBLANES, bw_), lambda b, g, c: (b, 0, b_off + g)),
            pl.BlockSpec((None, SUBLANES, bw_), lambda b, g, c: (b, 0, c_off + g)),
            pl.BlockSpec((CONV_W, xw_), col_x),
            pl.BlockSpec((CONV_W, bw_), col_b),
            pl.BlockSpec((CONV_W, bw_), col_c),
            pl.BlockSpec((1, xw_), col_x),
            pl.BlockSpec((1, bw_), col_b),
            pl.BlockSpec((1, bw_), col_c),
            pl.BlockSpec((None, gps, q, r_), lambda b, g, c: (b, g, c, 0)),
            pl.BlockSpec((None, gps, r_, q), lambda b, g, c: (b, g, 0, c)),
            pl.BlockSpec((gps, 1, r_), lambda b, g, c: (g, 0, 0)),
            pl.BlockSpec((gps, r_, 1), lambda b, g, c: (g, 0, 0)),
            pl.BlockSpec((gps, 1, r_), lambda b, g, c: (g, 0, 0)),
            pl.BlockSpec((gps, r_, 1), lambda b, g, c: (g, 0, 0)),
            pl.BlockSpec((1, xw_), col_x),
            pl.BlockSpec((None, gps, n_, gw), lambda b, g, c: (b, g, 0, 0)),
        ],
        out_specs=[
            pl.BlockSpec((None, q, xw_), lambda b, g, c: (b, c, g)),
            pl.BlockSpec((None, gps, n_, gw), lambda b, g, c: (b, g, 0, 0)),
        ],
        out_shape=[jax.ShapeDtypeStruct((nb, L, D_INNER), BF16),
                   jax.ShapeDtypeStruct((nb, g_, n_, gw), F32)],
        scratch_shapes=[pltpu.VMEM((gps, n_, gw), F32), pltpu.VMEM((q + SUBLANES, xw_), F32),
                        pltpu.VMEM((q + SUBLANES, bw_), F32), pltpu.VMEM((q + SUBLANES, bw_), F32)],
        compiler_params=_cparams(("parallel", "parallel", "arbitrary")), name="ssd_scan",
    )(xbc3, xbc3, xbc3, st, st, st, w_conv, w_conv, w_conv, b_conv, b_conv, b_conv,
      dtg, dtg_t, bias, bias_t, a, a_t, dsk, s0)


def _gather_rows(tok_ref, base, src_hbm, dst, sem, n, rp=1):
    def body(r, carry):
        t = pl.multiple_of(tok_ref[base + r] * rp, rp)
        pltpu.make_async_copy(src_hbm.at[pl.ds(t, rp)], dst.at[pl.ds(pl.multiple_of(r * rp, rp), rp)], sem).start()
        return carry

    lax.fori_loop(0, n, body, 0)


def _gather_rows_unrolled(tok_ref, base, src_hbm, dst, sem, n):
    for r in range(n):
        t = tok_ref[base + r]
        pltpu.make_async_copy(src_hbm.at[pl.ds(t, 1)], dst.at[pl.ds(r, 1)], sem).start(priority=r % 2)


def _moe_body(blk_e_ref, nxt_e_ref, tok_ref, nused_ref, xt_hbm, wg_hbm, wu_hbm, wd_hbm, yb_ref,
              xbuf, xsem, wg32, wu32, wd32, wsem, wg_bf, wu_bf, wd_bf, xb, *, bm, layer):
    b = pl.program_id(0)
    slot = b % 2
    nused = nused_ref[0]
    e_cur = blk_e_ref[b]

    def weight_copies(e):
        return (pltpu.make_async_copy(wg_hbm.at[layer, e], wg32, wsem.at[0]),
                pltpu.make_async_copy(wu_hbm.at[layer, e], wu32, wsem.at[1]),
                pltpu.make_async_copy(wd_hbm.at[layer, e], wd32, wsem.at[2]))

    def gather_unrolled(base, slot_):
        for r in range(bm):
            t = pl.multiple_of(tok_ref[base + r] * SUBLANES, SUBLANES)
            pltpu.make_async_copy(xt_hbm.at[pl.ds(t, SUBLANES)], xbuf.at[slot_, pl.ds(r * SUBLANES, SUBLANES)],
                                  xsem.at[slot_]).start()

    @pl.when(jnp.logical_and(b == 0, nused > 0))
    def _():
        for cp in weight_copies(e_cur):
            cp.start(priority=1)
        _gather_rows(tok_ref, 0, xt_hbm, xbuf.at[0], xsem.at[0], bm, rp=SUBLANES)

    prev_e = blk_e_ref[jnp.maximum(b - 1, 0)]
    changed = jnp.logical_or(b == 0, e_cur != prev_e)

    @pl.when(jnp.logical_and(changed, b < nused))
    def _():
        for cp in weight_copies(e_cur):
            cp.wait()
        wg_bf[...] = wg32[...].astype(BF16)
        wu_bf[...] = wu32[...].astype(BF16)
        wd_bf[...] = wd32[...].astype(BF16)
        nxt = nxt_e_ref[b]

        @pl.when(nxt >= 0)
        def _():
            for cp in weight_copies(nxt):
                cp.start(priority=1)

    def compute(prefetch):
        pltpu.make_async_copy(xbuf.at[slot], xbuf.at[slot], xsem.at[slot]).wait()
        half = xb.shape[1] // 2
        for s in range(SUBLANES):
            w = xbuf[slot, pl.ds(s, bm, stride=SUBLANES), :]
            cols = slice(s * LANES, (s + 1) * LANES)
            xb[:, cols] = lax.bitcast_convert_type(w << 16, F32).astype(BF16)
            xb[:, half + s * LANES:half + (s + 1) * LANES] = lax.bitcast_convert_type(
                w & jnp.uint32(0xFFFF0000), F32).astype(BF16)
        if prefetch:
            gather_unrolled((b + 1) * bm, 1 - slot)
        x = xb[...]
        g = jnp.dot(x, wg_bf[...], preferred_element_type=F32)
        u = jnp.dot(x, wu_bf[...], preferred_element_type=F32)
        hid = (_silu(g) * u).astype(BF16)
        yb_ref[...] = jnp.dot(hid, wd_bf[...], preferred_element_type=F32)

    @pl.when(b + 1 < nused)
    def _():
        compute(True)

    @pl.when(b + 1 == nused)
    def _():
        compute(False)

    @pl.when(b >= nused)
    def _():
        yb_ref[...] = jnp.zeros(yb_ref.shape, yb_ref.dtype)


def _moe_experts(xt, blk_e, nxt_e, row_token, nused, w_gate, w_up, w_down, *, bm, layer):
    nblk = blk_e.shape[0]
    d = w_gate.shape[2]
    de = w_gate.shape[3]
    assert d == 2 * SUBLANES * LANES and xt.shape[1] == LANES and xt.dtype == jnp.uint32
    anyspec = pl.BlockSpec(memory_space=pl.ANY)
    gs = pltpu.PrefetchScalarGridSpec(
        num_scalar_prefetch=4, grid=(nblk,),
        in_specs=[anyspec, anyspec, anyspec, anyspec],
        out_specs=pl.BlockSpec((bm, d), lambda b, be, nx, tok, nu: (b, 0)),
        scratch_shapes=[
            pltpu.VMEM((2, bm * SUBLANES, LANES), jnp.uint32),
            pltpu.SemaphoreType.DMA((2,)),
            pltpu.VMEM((d, de), F32),
            pltpu.VMEM((d, de), F32),
            pltpu.VMEM((de, d), F32),
            pltpu.SemaphoreType.DMA((3,)),
            pltpu.VMEM((d, de), BF16),
            pltpu.VMEM((d, de), BF16),
            pltpu.VMEM((de, d), BF16),
            pltpu.VMEM((bm, d), BF16),
        ],
    )
    return pl.pallas_call(
        functools.partial(_moe_body, bm=bm, layer=layer), grid_spec=gs,
        out_shape=jax.ShapeDtypeStruct((nblk * bm, d), F32),
        compiler_params=_cparams(("arbitrary",)), name="moe_experts",
    )(blk_e, nxt_e, row_token, nused, xt, w_gate, w_up, w_down)


def _combine_body(dest_ref, h_ref, g_ref, yb_hbm, o_ref, buf, sem, *, tc, ntile):
    i = pl.program_id(0)
    slot = i % 2
    n = TOP_K * tc

    @pl.when(i == 0)
    def _():
        _gather_rows(dest_ref, 0, yb_hbm, buf.at[0], sem.at[0], n)

    @pl.when(i + 1 < ntile)
    def _():
        _gather_rows_unrolled(dest_ref, (i + 1) * n, yb_hbm, buf.at[1 - slot], sem.at[1 - slot], n)

    pltpu.make_async_copy(buf.at[slot], buf.at[slot], sem.at[slot]).wait()
    g = g_ref[...]
    o_ref[...] = (h_ref[...] + g[:, 0:1] * buf[slot, 0:tc, :] + g[:, 1:2] * buf[slot, tc:2 * tc, :])


def _moe_combine(h, yb, dest_tiles, gates, *, tc):
    t, d = h.shape
    ntile = t // tc
    gs = pltpu.PrefetchScalarGridSpec(
        num_scalar_prefetch=1, grid=(ntile,),
        in_specs=[
            pl.BlockSpec((tc, d), lambda i, dst: (i, 0)),
            pl.BlockSpec((tc, LANES), lambda i, dst: (i, 0)),
            pl.BlockSpec(memory_space=pl.ANY),
        ],
        out_specs=pl.BlockSpec((tc, d), lambda i, dst: (i, 0)),
        scratch_shapes=[pltpu.VMEM((2, TOP_K * tc, d), F32), pltpu.SemaphoreType.DMA((2,))],
    )
    return pl.pallas_call(
        functools.partial(_combine_body, tc=tc, ntile=ntile), grid_spec=gs,
        out_shape=jax.ShapeDtypeStruct((t, d), F32),
        compiler_params=_cparams(("arbitrary",)), name="moe_combine",
    )(dest_tiles, h, gates, yb)


def _moe_plan(idx, bm, tc):
    t = idx.shape[0]
    a = t * TOP_K
    e_ = N_EXPERTS
    flat_e = idx.reshape(a)
    onehot = (flat_e[:, None] == jnp.arange(e_, dtype=jnp.int32)[None, :]).astype(jnp.int32)
    csum = jnp.cumsum(onehot, axis=0)
    rank = jnp.sum(onehot * csum, axis=1) - 1
    counts = csum[-1]
    padded = (counts + bm - 1) // bm * bm
    pad_end = jnp.cumsum(padded)
    pad_start = pad_end - padded
    dest = jnp.sum(onehot * pad_start[None, :], axis=1) + rank
    nblk = -(-a // bm) + e_
    rows = nblk * bm
    row_token = jnp.zeros((rows,), jnp.int32).at[dest].set(jnp.arange(a, dtype=jnp.int32) // TOP_K)
    blk_row0 = jnp.arange(nblk, dtype=jnp.int32) * bm
    blk_e = jnp.minimum(jnp.sum((pad_end[None, :] <= blk_row0[:, None]).astype(jnp.int32), axis=1), e_ - 1)
    ids = jnp.arange(e_, dtype=jnp.int32)
    later_used = jnp.logical_and(ids[None, :] > ids[:, None], (counts > 0)[None, :])
    nxt_of_e = jnp.min(jnp.where(later_used, ids[None, :], e_), axis=1)
    nxt_of_e = jnp.where(nxt_of_e == e_, -1, nxt_of_e)
    nxt_e = jnp.sum(jnp.where(blk_e[:, None] == ids[None, :], nxt_of_e[None, :], 0), axis=1).astype(jnp.int32)
    nused = (pad_end[-1] // bm).astype(jnp.int32).reshape(1)
    dest_tiles = dest.reshape(t // tc, tc, TOP_K).transpose(0, 2, 1).reshape(a).astype(jnp.int32)
    return blk_e.astype(jnp.int32), nxt_e, row_token, nused, dest_tiles


def _flash_body(qi_ref, kj_ref, q_ref, k_ref, v_ref, o_ref, m_scr, acc_scr, *, hb):
    p_id = pl.program_id(1)
    qi = qi_ref[p_id]
    kj = kj_ref[p_id]
    tk = k_ref.shape[1]

    @pl.when(kj == 0)
    def _():
        m_scr[...] = jnp.full(m_scr.shape, -jnp.inf, F32)
        acc_scr[...] = jnp.zeros(acc_scr.shape, F32)

    def step(diag):
        for hh in range(hb):
            s = lax.dot_general(q_ref[hh], k_ref[hh], (((1,), (1,)), ((), ())), preferred_element_type=F32)
            if diag:
                r = lax.broadcasted_iota(jnp.int32, s.shape, 0)
                c = lax.broadcasted_iota(jnp.int32, s.shape, 1)
                s = jnp.where((c // CHUNK) <= (r // CHUNK), s, -jnp.inf)
            m_prev = m_scr[hh]
            m_next = jnp.maximum(m_prev, jnp.max(s, axis=1, keepdims=True))
            p = jnp.exp2(s - jnp.concatenate([m_next] * (tk // LANES), axis=1))
            alpha = jnp.exp2(m_prev - m_next)
            acc_scr[hh] = (jnp.concatenate([alpha, alpha], axis=1) * acc_scr[hh]
                           + jnp.dot(p.astype(BF16), v_ref[hh], preferred_element_type=F32))
            m_scr[hh] = m_next

    @pl.when(kj < qi)
    def _():
        step(False)

    @pl.when(kj == qi)
    def _():
        step(True)
        for hh in range(hb):
            a = acc_scr[hh]
            o_ref[:, hh * V_DIM:(hh + 1) * V_DIM] = (a[:, :V_DIM] / a[:, V_DIM:]).astype(o_ref.dtype)


def _flash_prompt(q, k, v, lp, *, tq, hb):
    assert V_DIM == LANES
    nh = q.shape[0]
    nq = lp // tq
    pairs = [(a, b) for a in range(nq) for b in range(a + 1)]
    qi_tab = jnp.asarray(np.array([p[0] for p in pairs], np.int32))
    kj_tab = jnp.asarray(np.array([p[1] for p in pairs], np.int32))
    gs = pltpu.PrefetchScalarGridSpec(
        num_scalar_prefetch=2, grid=(nh // hb, len(pairs)),
        in_specs=[
            pl.BlockSpec((hb, tq, QK_DIM), lambda h, p, qi, kj: (h, qi[p], 0)),
            pl.BlockSpec((hb, tq, QK_DIM), lambda h, p, qi, kj: (h, kj[p], 0)),
            pl.BlockSpec((hb, tq, 2 * V_DIM), lambda h, p, qi, kj: (h, kj[p], 0)),
        ],
        out_specs=pl.BlockSpec((tq, hb * V_DIM), lambda h, p, qi, kj: (qi[p], h)),
        scratch_shapes=[pltpu.VMEM((hb, tq, LANES), F32), pltpu.VMEM((hb, tq, 2 * V_DIM), F32)],
    )
    return pl.pallas_call(
        functools.partial(_flash_body, hb=hb), grid_spec=gs,
        out_shape=jax.ShapeDtypeStruct((lp, nh * V_DIM), BF16),
        compiler_params=_cparams(("parallel", "arbitrary")), name="flash_prompt",
    )(qi_tab, kj_tab, q, k, v)


def _attn_sample_body(q_ref, ckv_ref, kr_ref, ckv_new_ref, kr_new_ref, o_ref, *, ls, past):
    nh = q_ref.shape[0]
    q = q_ref[...].reshape(nh * ls, q_ref.shape[2])
    q_lat, q_rot = q[:, :KV_RANK], q[:, KV_RANK:KV_RANK + QK_ROPE]
    nt = (((1,), (1,)), ((), ()))

    def scores(lat, rot, pos0):
        s = (lax.dot_general(q_lat, lat, nt, preferred_element_type=F32)
             + lax.dot_general(q_rot, rot, nt, preferred_element_type=F32))
        r = lax.broadcasted_iota(jnp.int32, s.shape, 0)
        c = lax.broadcasted_iota(jnp.int32, s.shape, 1)
        q_pos = past + r % ls
        return jnp.where(((pos0 + c) // CHUNK) <= (q_pos // CHUNK), s, -jnp.inf)

    lat_p = ckv_ref[...].astype(BF16)
    lat_n = ckv_new_ref[...].astype(BF16)
    s_p = scores(lat_p, kr_ref[...].astype(BF16), 0)
    s_n = scores(lat_n, kr_new_ref[...].astype(BF16), past)
    m = jnp.maximum(jnp.max(s_p, axis=-1, keepdims=True), jnp.max(s_n, axis=-1, keepdims=True))
    p_p = jnp.exp2(s_p - m)
    p_n = jnp.exp2(s_n - m)
    l = jnp.sum(p_p, axis=-1, keepdims=True) + jnp.sum(p_n, axis=-1, keepdims=True)
    o = (jnp.dot(p_p.astype(BF16), lat_p, preferred_element_type=F32)
         + jnp.dot(p_n.astype(BF16), lat_n, preferred_element_type=F32)) / l
    o_ref[...] = o.reshape(nh, ls, KV_RANK).astype(o_ref.dtype)


def _attn_sample(q_abs, cache_ckv, cache_kr, ckv_new, kr_new, *, ls):
    nh, ts, dk = q_abs.shape
    nb, past, _ = cache_ckv.shape
    return pl.pallas_call(
        functools.partial(_attn_sample_body, ls=ls, past=past), grid=(nb,),
        in_specs=[
            pl.BlockSpec((nh, ls, dk), lambda b: (0, b, 0)),
            pl.BlockSpec((None, past, KV_RANK), lambda b: (b, 0, 0)),
            pl.BlockSpec((None, past, QK_ROPE), lambda b: (b, 0, 0)),
            pl.BlockSpec((None, ls, KV_RANK), lambda b: (b, 0, 0)),
            pl.BlockSpec((None, ls, QK_ROPE), lambda b: (b, 0, 0)),
        ],
        out_specs=pl.BlockSpec((nh, ls, KV_RANK), lambda b: (0, b, 0)),
        out_shape=jax.ShapeDtypeStruct((nh, ts, KV_RANK), BF16),
        compiler_params=_cparams(("parallel",)), name="attn_sample",
    )(q_abs, cache_ckv, cache_kr, ckv_new, kr_new)


def _rope_tables(pos):
    half = QK_ROPE // 2
    inv = ROPE_THETA ** (-np.arange(half, dtype=np.float64) / half)
    ang = np.asarray(pos, np.float64)[:, None] * inv[None, :]
    cos = np.concatenate([np.cos(ang), np.cos(ang)], axis=1)
    sin = np.concatenate([-np.sin(ang), np.sin(ang)], axis=1)
    return cos.astype(np.float32), sin.astype(np.float32)


def _swap_rope_halves(w):
    half = QK_ROPE // 2
    return jnp.concatenate([w[..., half:], w[..., :half]], axis=-1)


def _moe_layer(h, g_ffn, w_rg, b_rg, w_re, b_re, w_gate, w_up, w_down, *, tm, layer):
    t, d = h.shape
    npad = LANES - MOE_GROUPS - N_EXPERTS
    w_r = jnp.concatenate([w_rg, w_re, jnp.zeros((d, npad), F32)], axis=1)
    b_r = jnp.concatenate([b_rg, b_re, jnp.zeros((npad,), F32)]).reshape(1, LANES)
    grid = (t // tm, 1)
    x_args = [(h, pl.BlockSpec((tm, d), lambda i, j: (i, 0))),
              (g_ffn.reshape(1, d), pl.BlockSpec((1, d), lambda i, j: (0, 0)))]
    w_hi = w_r.astype(BF16)
    w_lo = (w_r - w_hi.astype(F32)).astype(BF16)
    w_args = [(jnp.concatenate([w_hi, w_hi, w_lo], axis=0), pl.BlockSpec((3 * d, LANES), lambda i, j: (0, 0)))]
    e_args = [(b_r, pl.BlockSpec((1, LANES), lambda i, j: (0, 0)))]
    tile = pl.BlockSpec((tm, LANES), lambda i, j: (i, 0))
    out_defs = [(jax.ShapeDtypeStruct((t, LANES), jnp.int32), tile),
                (jax.ShapeDtypeStruct((t, LANES), F32), tile)]
    assert d == 2 * SUBLANES * LANES
    xn_out = (jax.ShapeDtypeStruct((t * SUBLANES, LANES), jnp.uint32),
              pl.BlockSpec((tm * SUBLANES, LANES), lambda i, j: (i, 0)))
    idx_t, gates_t, xt = _fused_mm(x_args, w_args, e_args, out_defs, grid=grid, prologue=_pro_rms,
                                   epilogue=_epi_route, xn_shape=(tm, 3 * d), xn_dtype=BF16, xn_out=xn_out,
                                   xn_emit=_emit_token_tiles, xn_store=_split_hi_lo_hi, name="router")
    tc = min(tm, 128)
    blk_e, nxt_e, row_token, nused, dest_tiles = _moe_plan(idx_t[:, :TOP_K], MOE_BM, tc)
    yb = _moe_experts(xt, blk_e, nxt_e, row_token, nused, w_gate, w_up, w_down, bm=MOE_BM, layer=layer)
    return _moe_combine(h, yb, dest_tiles, gates_t, tc=tc)


def _post_rms(rows, g):
    return _rms_rows(rows) * g


def _ple_layer(h, p_rows, g_ple, w_gate, w_proj, *, tm, final=None):
    t, d = h.shape
    extra = {}
    if final is not None:
        extra = dict(post=_post_rms, post_consts=[final[0].reshape(1, d)], out_rows=final[1:])
    return _mm_resident([h], [g_ple.reshape(1, d)], [p_rows.astype(BF16)], [w_gate.astype(BF16), w_proj.astype(BF16)],
                        [0, 1], [], prologue=_pro_rms, epilogue=_epi_ple, out_dtype=F32, tm=tm, name="ple_gate",
                        res_from_x=True, **extra)


def kernel(x_prompt, x_sample, state_conv, state_ssm, cache_kv_latent, cache_k_rope, p_prompt, p_sample,
           g_mix, w_ssm_in, w_conv, b_conv, dt_bias, a_log, d_skip, g_ssm_norm, w_ssm_out,
           g_kv_in, w_dkv, g_kv, w_uk, w_uv, w_dq, g_q, w_uq, w_o,
           g_ffn, w_router_grp, b_router_grp, w_router_exp, b_router_exp, w_exp_gate, w_exp_up, w_exp_down,
           g_ple, w_ple_gate, w_ple_proj, g_final):
    bp, lp_each, d = x_prompt.shape
    bs, ls, _ = x_sample.shape
    past = cache_kv_latent.shape[1]
    assert bp == 1 and w_ssm_in.shape[0] == 1 and w_dq.shape[0] == 1 and d == D_MODEL
    assert ls >= CONV_W - 1 and ls % SUBLANES == 0 and past % CHUNK == 0
    lp = bp * lp_each
    ts = bs * ls
    t = lp + ts
    tm = min(512, math.gcd(lp, ts))
    g_, r_, n_, hd = SSM_GROUPS, HEADS_PER_GROUP, SSM_STATE, SSM_HEAD_DIM

    h = jnp.concatenate([x_prompt.reshape(lp, d), x_sample.reshape(ts, d)], axis=0)

    w_in = w_ssm_in[0]
    w_z = w_in[:, :D_INNER].astype(BF16)
    xw = SSM_IN_XW
    w_x = jnp.pad(w_in[:, D_INNER:], ((0, 0), (0, xw - CONV_DIM - SSM_HEADS))).astype(BF16)
    g0 = g_mix[0].reshape(1, d)
    tm_in = max(c for c in range(16, 1153, 16) if t % c == 0)
    mm_in = functools.partial(_mm_rows, [h], [g0], prologue=_pro_rms, tm=tm_in)
    z = mm_in(w_z, [], epilogue=_epi_silu, out_dtype=BF16, tn=1024, name="ssm_in_z")[0]
    xbc = mm_in(w_x, [], epilogue=_epi_id, out_dtype=F32, tn=xw // 5, name="ssm_in_xbc")[0]
    dt_raw = xbc[:, CONV_DIM:CONV_DIM + SSM_HEADS]

    wc = w_conv[0]
    bc = b_conv[0].reshape(1, CONV_DIM)
    st_p = jnp.zeros((1, SUBLANES, CONV_DIM), F32)
    st_s = jnp.pad(state_conv[0].astype(F32), ((0, 0), (SUBLANES - (CONV_W - 1), 0), (0, 0)))
    conv_p = xbc[lp - (CONV_W - 1):lp, :CONV_DIM].reshape(1, 1, CONV_W - 1, CONV_DIM)
    conv_s = xbc[lp:, :CONV_DIM].reshape(bs, ls, CONV_DIM)[:, ls - (CONV_W - 1):].reshape(
        1, bs, CONV_W - 1, CONV_DIM)

    bias = dt_bias[0].astype(F32).reshape(g_, 1, r_)
    a_neg = (-jnp.exp(a_log[0].astype(F32)) * math.log2(math.e)).reshape(g_, 1, r_)
    dsk = jnp.repeat(d_skip[0].astype(F32), hd).reshape(1, D_INNER)

    def dt_views(rows, nb, L):
        v = rows.reshape(nb, L, g_, r_).transpose(0, 2, 1, 3)
        return v, v.transpose(0, 1, 3, 2)

    def scan(xbc3, batch0, nb, L, st, rows, s0, q):
        dtg, dtg_t = dt_views(rows, nb, L)
        return _ssd_scan(xbc3, batch0, nb, L, st, wc, bc, dtg, dtg_t, bias, bias.transpose(0, 2, 1),
                         a_neg, a_neg.transpose(0, 2, 1), dsk, s0, q=q)

    def state_in(s):
        nb = s.shape[0]
        return s.astype(F32).reshape(nb, g_, r_ * hd, n_).transpose(0, 1, 3, 2)

    def state_out(s):
        nb = s.shape[0]
        return s.transpose(0, 1, 3, 2).reshape(1, nb, SSM_HEADS, hd, n_)

    y_p, s_p = scan(xbc.reshape(1, t, xw), 0, 1, lp, st_p, dt_raw[:lp],
                    jnp.zeros((1, g_, n_, r_ * hd), F32), min(SSD_Q, lp))
    y_s, s_s = scan(xbc.reshape(t // ls, ls, xw), lp // ls, bs, ls, st_s, dt_raw[lp:],
                    state_in(state_ssm[0]), ls)
    ssm_p, ssm_s = state_out(s_p), state_out(s_s)
    y_all = (y_p.reshape(lp, D_INNER), y_s.reshape(ts, D_INNER))

    h = _mm_resident([y_all, z], [g_ssm_norm[0].reshape(1, D_INNER)], [], [w_ssm_out[0].astype(BF16)], [0], [h],
                     prologue=_pro_gated, epilogue=_epi_res, out_dtype=F32, tm=tm // 2, name="ssm_out")

    def ffn_and_ple(h, i, final=None):
        h = _moe_layer(h, g_ffn[i], w_router_grp[i], b_router_grp[i], w_router_exp[i], b_router_exp[i],
                       w_exp_gate, w_exp_up, w_exp_down, tm=tm, layer=i)
        p_rows = jnp.concatenate([p_prompt[i].reshape(lp, -1), p_sample[i].reshape(ts, -1)], axis=0)
        return _ple_layer(h, p_rows, g_ple[i], w_ple_gate[i], w_ple_proj[i], tm=tm, final=final)

    h = ffn_and_ple(h, 0)

    pos = np.concatenate([np.arange(lp), np.tile(past + np.arange(ls), bs)])
    cos64, sin64 = _rope_tables(pos)
    gk = g_kv_in.reshape(1, d)
    w_r = w_dkv[:, KV_RANK:]
    row64 = pl.BlockSpec((tm, QK_ROPE), lambda i, j: (i, 0))
    w64 = pl.BlockSpec((d, QK_ROPE), lambda i, j: (0, 0))
    whole = lambda a: pl.BlockSpec(a.shape, lambda i, j: (0, 0))
    w_lat = w_dkv[:, :KV_RANK].astype(BF16)
    g_kv_row = g_kv.reshape(1, KV_RANK)
    ckv, kr = _fused_mm(
        [(h, pl.BlockSpec((tm, d), lambda i, j: (i, 0))), (gk, whole(gk))],
        [(w_lat, whole(w_lat)), (w_r.astype(BF16), w64), (_swap_rope_halves(w_r).astype(BF16), w64)],
        [(g_kv_row, whole(g_kv_row)), (jnp.asarray(cos64), row64), (jnp.asarray(sin64), row64)],
        [(jax.ShapeDtypeStruct((t, KV_RANK), F32), pl.BlockSpec((tm, KV_RANK), lambda i, j: (i, 0))),
         (jax.ShapeDtypeStruct((t, QK_ROPE), F32), row64)],
        grid=(t // tm, 1), prologue=_pro_rms, epilogue=_epi_latent_kv, xn_shape=(tm, d), name="kv_latent")

    ql = _mm_rows([h], [g_mix[1].reshape(1, d)], w_dq[0].astype(BF16), [], prologue=_pro_rms,
                  epilogue=_epi_rms_out, out_dtype=BF16, tm=tm, tn=Q_RANK, name="q_latent",
                  e_consts=[g_q[0].reshape(1, Q_RANK)])[0]
    nh = MLA_HEADS
    w_q = w_uq[0].reshape(Q_RANK, nh, QK_DIM)
    w_q_rope = w_q[..., QK_NOPE:]
    cos_q = np.concatenate([cos64, cos64], axis=1) * np.float32(QK_PRESCALE)
    sin_q = np.concatenate([sin64, sin64], axis=1) * np.float32(QK_PRESCALE)
    q_cat = _q_expand(ql, w_q[..., :QK_NOPE].reshape(Q_RANK, nh * QK_NOPE).astype(BF16),
                      w_q_rope.reshape(Q_RANK, nh * QK_ROPE).astype(BF16),
                      _swap_rope_halves(w_q_rope).reshape(Q_RANK, nh * QK_ROPE).astype(BF16),
                      jnp.asarray(cos_q), jnp.asarray(sin_q), tm=tm)

    lat_pad = LAT_PAD - KV_RANK - QK_ROPE
    w_uk_h = w_uk.transpose(1, 0, 2)
    w_uv_h = w_uv.transpose(1, 0, 2)
    eye_r = jnp.broadcast_to(jnp.eye(QK_ROPE, dtype=F32), (nh, QK_ROPE, QK_ROPE))
    k_cat, v_h = _kv_expand(ckv, kr, w_uk.reshape(KV_RANK, nh * QK_NOPE).astype(BF16),
                            w_uv.reshape(KV_RANK, nh * V_DIM).astype(BF16), lp, tm=tm)
    o_p = _flash_prompt(q_cat, k_cat, v_h, lp, tq=min(512, lp), hb=8)

    w_abs = jnp.concatenate([
        jnp.concatenate([w_uk_h.transpose(0, 2, 1), jnp.zeros((nh, QK_NOPE, LAT_PAD - KV_RANK), F32)], axis=2),
        jnp.concatenate([jnp.zeros((nh, QK_ROPE, KV_RANK), F32), eye_r,
                         jnp.zeros((nh, QK_ROPE, lat_pad), F32)], axis=2)], axis=1).astype(BF16)
    blk_s = lp // ts if lp % ts == 0 else None
    assert blk_s is not None
    q_abs = _fused_mm(
        [(q_cat, pl.BlockSpec((None, ts, QK_DIM), lambda i, j: (j, blk_s + i, 0)))],
        [(w_abs, pl.BlockSpec((None, QK_DIM, LAT_PAD), lambda i, j: (j, 0, 0)))],
        [],
        [(jax.ShapeDtypeStruct((nh, ts, LAT_PAD), BF16), pl.BlockSpec((None, ts, LAT_PAD), lambda i, j: (j, i, 0)))],
        grid=(1, nh), prologue=_pro_cast, epilogue=_epi_id, x_per_j=True, name="q_absorb")[0]
    o_lat = _attn_sample(q_abs, cache_kv_latent, cache_k_rope, ckv[lp:].reshape(bs, ls, KV_RANK),
                         kr[lp:].reshape(bs, ls, QK_ROPE), ls=ls)
    o_s = _fused_mm(
        [(o_lat, pl.BlockSpec((None, ts, KV_RANK), lambda i, j: (j, i, 0)))],
        [(w_uv_h.astype(BF16), pl.BlockSpec((None, KV_RANK, V_DIM), lambda i, j: (j, 0, 0)))],
        [],
        [(jax.ShapeDtypeStruct((ts, nh * V_DIM), BF16), pl.BlockSpec((ts, V_DIM), lambda i, j: (i, j)))],
        grid=(1, nh), prologue=_pro_cast, epilogue=_epi_id, x_per_j=True, name="v_absorb")[0]
    h = _mm_resident([(o_p, o_s)], [], [], [w_o[0].astype(BF16)], [0], [h], prologue=_pro_cast, epilogue=_epi_res,
                     out_dtype=F32, tm=tm, name="attn_out")
    y_p_out, y_s_out = ffn_and_ple(h, 1, final=(g_final, lp, ts))
    y_p_out = y_p_out.reshape(bp, lp_each, d)
    y_s_out = y_s_out.reshape(bs, ls, d)
    return (y_p_out, y_s_out, conv_p, ssm_p, ckv[:lp].reshape(bp, lp_each, KV_RANK),
            kr[:lp].reshape(bp, lp_each, QK_ROPE), conv_s, ssm_s, ckv[lp:].reshape(bs, ls, KV_RANK),
            kr[lp:].reshape(bs, ls, QK_ROPE))
```

```python
import functools
import math

import numpy as np
import jax
import jax.numpy as jnp
from jax import lax
from jax.experimental import pallas as pl
from jax.experimental.pallas import tpu as pltpu

F32 = jnp.float32
BF16 = jnp.bfloat16

EPS = 1e-6
CHUNK = 64
D_MODEL = 2048
D_INNER = 2 * D_MODEL
SSM_HEAD_DIM = 64
SSM_HEADS = D_INNER // SSM_HEAD_DIM
SSM_STATE = 128
SSM_GROUPS = 8
HEADS_PER_GROUP = SSM_HEADS // SSM_GROUPS
GROUP_WIDTH = D_INNER // SSM_GROUPS
CONV_W = 4
CONV_DIM = D_INNER + 2 * SSM_GROUPS * SSM_STATE
MLA_HEADS = 16
Q_RANK = 512
KV_RANK = 512
QK_NOPE = 128
QK_ROPE = 64
QK_DIM = QK_NOPE + QK_ROPE
V_DIM = 128
ROPE_THETA = 10000.0
ATTN_SCALE = QK_DIM ** -0.5
QK_PRESCALE = ATTN_SCALE * math.log2(math.e)
MOE_GROUPS = 4
EXPERTS_PER_GROUP = 8
N_EXPERTS = MOE_GROUPS * EXPERTS_PER_GROUP
TOP_K = 2
D_EXPERT = 512

LANES = 128
SUBLANES = 8
VMEM_LIMIT = 56 * 1024 * 1024
SSD_Q = 128
MOE_BM = 128
LAT_PAD = 640
SSM_IN_XW = 6400


def _cparams(sem):
    return pltpu.CompilerParams(dimension_semantics=sem, vmem_limit_bytes=VMEM_LIMIT)


def _sigmoid(v):
    return 1.0 / (1.0 + jnp.exp(-v))


def _silu(v):
    return v * _sigmoid(v)


def _softplus(v):
    return jnp.maximum(v, 0.0) + jnp.log1p(jnp.exp(-jnp.abs(v)))


def _rms_rows(x):
    return x * lax.rsqrt(jnp.mean(x * x, axis=-1, keepdims=True) + EPS)


def _mm_body(*refs, nx, nw, ne, no, prologue, epilogue, x_per_j, emit_xn, xn_emit, xn_store, precision):
    x_refs = refs[:nx]
    w_refs = refs[nx:nx + nw]
    e_refs = refs[nx + nw:nx + nw + ne]
    o_refs = refs[nx + nw + ne:nx + nw + ne + no]
    rest = refs[nx + nw + ne + no:]
    j = pl.program_id(1)
    if x_per_j:
        xn = prologue(*[r[...] for r in x_refs])
    else:
        xn_ref = rest[-1]

        @pl.when(j == 0)
        def _():
            v = prologue(*[r[...] for r in x_refs])
            xn_ref[...] = xn_store(v).astype(xn_ref.dtype)
            if emit_xn:
                xn_emit(v, rest[0])

        xn = xn_ref[...]
    accs = [jnp.dot(xn, w[...], preferred_element_type=F32, precision=precision) for w in w_refs]
    outs = epilogue(accs, [e[...] for e in e_refs])
    for o_ref, o in zip(o_refs, outs):
        o_ref[...] = o.astype(o_ref.dtype)


def _fused_mm(x_args, w_args, e_args, out_defs, *, grid, prologue, epilogue, xn_shape=None,
              xn_dtype=BF16, x_per_j=False, xn_out=None, xn_emit=None, xn_store=None, precision=None, name=None):
    arrays = [a for a, _ in x_args + w_args + e_args]
    in_specs = [s for _, s in x_args + w_args + e_args]
    out_shape = [d for d, _ in out_defs]
    out_specs = [s for _, s in out_defs]
    emit_xn = xn_out is not None
    if emit_xn:
        out_shape.append(xn_out[0])
        out_specs.append(xn_out[1])
    scratch = [] if x_per_j else [pltpu.VMEM(xn_shape, xn_dtype)]
    body = functools.partial(
        _mm_body, nx=len(x_args), nw=len(w_args), ne=len(e_args), no=len(out_defs),
        prologue=prologue, epilogue=epilogue, x_per_j=x_per_j, emit_xn=emit_xn,
        xn_emit=xn_emit or _emit_cast, xn_store=xn_store or _pro_cast, precision=precision)
    return pl.pallas_call(
        body, grid=grid, in_specs=in_specs, out_specs=out_specs, out_shape=out_shape,
        scratch_shapes=scratch, compiler_params=_cparams(("parallel", "arbitrary")), name=name,
    )(*arrays)


def _pro_rms(x, g):
    return _rms_rows(x.astype(F32)) * g


def _pro_cast(x):
    return x


def _emit_cast(v, o_ref):
    o_ref[...] = v.astype(o_ref.dtype)


def _emit_token_tiles(v, o_ref):
    w = _pack_bf16_pairs(v)
    m = w.shape[0]
    for s in range(SUBLANES):
        o_ref[pl.ds(s, m, stride=SUBLANES), :] = w[:, s * LANES:(s + 1) * LANES]


def _split3_bf16(v):
    hi = v.astype(BF16)
    r1 = v - hi.astype(F32)
    mid = r1.astype(BF16)
    return hi, mid, (r1 - mid.astype(F32)).astype(BF16)


def _split_hi_lo_hi(v):
    hi = v.astype(BF16)
    lo = (v - hi.astype(F32)).astype(BF16)
    return jnp.concatenate([hi, lo, hi], axis=1)


def _pack_bf16_pairs(v):
    k = v.shape[1] // 2
    lo = lax.bitcast_convert_type(v[:, :k].astype(BF16).astype(F32), jnp.uint32) >> 16
    hi = lax.bitcast_convert_type(v[:, k:].astype(BF16).astype(F32), jnp.uint32) & jnp.uint32(0xFFFF0000)
    return hi | lo


def _epi_silu(accs, es):
    return [_silu(accs[0])]


def _pro_gated(y, zs, g):
    v = y.astype(F32) * zs.astype(F32)
    parts = [_rms_rows(v[:, k * GROUP_WIDTH:(k + 1) * GROUP_WIDTH]) for k in range(SSM_GROUPS)]
    return jnp.concatenate(parts, axis=-1) * g


def _epi_id(accs, es):
    return accs


def _epi_res(accs, es):
    return [es[0] + accs[0]]


def _epi_ple(accs, es):
    return [es[0] + _sigmoid(accs[0]) * accs[1]]


def _epi_rms_out(accs, es):
    return [_rms_rows(accs[0]) * es[0]]


def _epi_latent_kv(accs, es):
    g_kv, cos, sin = es
    return [_rms_rows(accs[0]) * g_kv, accs[1] * cos + accs[2] * sin]


def _epi_route(accs, es):
    lg = accs[0] + es[0]
    lane = lax.broadcasted_iota(jnp.int32, lg.shape, 1).astype(F32)
    neg = -jnp.inf
    big = 1.0e4
    is_grp = lane < MOE_GROUPS
    gl = jnp.where(is_grp, lg, neg)
    mg = jnp.max(gl, axis=-1, keepdims=True)
    g_sel = jnp.min(jnp.where(gl == mg, lane, big), axis=-1, keepdims=True)
    p_sel = 1.0 / jnp.sum(jnp.where(is_grp, jnp.exp(gl - mg), 0.0), axis=-1, keepdims=True)
    lo = MOE_GROUPS + g_sel * EXPERTS_PER_GROUP
    in_grp = jnp.where(lane >= lo, jnp.where(lane < lo + EXPERTS_PER_GROUP, 1.0, 0.0), 0.0) > 0.5
    el = jnp.where(in_grp, lg, neg)
    v1 = jnp.max(el, axis=-1, keepdims=True)
    i1 = jnp.min(jnp.where(el == v1, lane, big), axis=-1, keepdims=True)
    el2 = jnp.where(lane == i1, neg, el)
    v2 = jnp.max(el2, axis=-1, keepdims=True)
    i2 = jnp.min(jnp.where(el2 == v2, jnp.where(lane == i1, big, lane), big), axis=-1, keepdims=True)
    e21 = jnp.exp(v2 - v1)
    g1 = p_sel / (1.0 + e21)
    g2 = p_sel * e21 / (1.0 + e21)
    idx = jnp.where(lane == 0.0, i1 - MOE_GROUPS, jnp.where(lane == 1.0, i2 - MOE_GROUPS, 0.0))
    gates = jnp.where(lane == 0.0, g1, jnp.where(lane == 1.0, g2, 0.0))
    return [idx.astype(jnp.int32), gates]


def _mm_rows(x_list, consts, w, extras, *, prologue, epilogue, out_dtype, tm, tn, name,
             xn_dtype=BF16, precision=None, emit_xn_dtype=None, e_consts=()):
    m = x_list[0].shape[0]
    k, n = w.shape
    grid = (m // tm, n // tn)
    x_args = [(x, pl.BlockSpec((tm, x.shape[1]), lambda i, j: (i, 0))) for x in x_list]
    x_args += [(c, pl.BlockSpec((1, c.shape[1]), lambda i, j: (0, 0))) for c in consts]
    w_args = [(w, pl.BlockSpec((k, tn), lambda i, j: (0, j)))]
    e_args = [(c, pl.BlockSpec((1, tn), lambda i, j: (0, j))) for c in e_consts]
    e_args += [(e, pl.BlockSpec((tm, tn), lambda i, j: (i, j))) for e in extras]
    out_defs = [(jax.ShapeDtypeStruct((m, n), out_dtype), pl.BlockSpec((tm, tn), lambda i, j: (i, j)))]
    xn_out = None
    if emit_xn_dtype is not None:
        xn_out = (jax.ShapeDtypeStruct((m, k), emit_xn_dtype), pl.BlockSpec((tm, k), lambda i, j: (i, 0)))
    return _fused_mm(x_args, w_args, e_args, out_defs, grid=grid, prologue=prologue, epilogue=epilogue,
                     xn_shape=(tm, k), xn_dtype=xn_dtype, xn_out=xn_out, precision=precision, name=name)


def _mm_res_body(*refs, x_split, nc, nl, nw, ne, npc, lhs_of_w, res_from_x, prologue, epilogue, post, out_split,
                 cw):
    i = pl.program_id(0)
    pos, xs = 0, []
    for first_tiles in x_split:
        if first_tiles is None:
            xs.append(refs[pos][...])
            pos += 1
        else:
            xs.append(jnp.where(i < first_tiles, refs[pos][...], refs[pos + 1][...]))
            pos += 2
    x0_ref = refs[0]
    c_refs = refs[pos:pos + nc]
    l_refs = refs[pos + nc:pos + nc + nl]
    w_refs = refs[pos + nc + nl:pos + nc + nl + nw]
    e_refs = refs[pos + nc + nl + nw:pos + nc + nl + nw + ne]
    p_refs = refs[pos + nc + nl + nw + ne:pos + nc + nl + nw + ne + npc]
    o_refs = refs[pos + nc + nl + nw + ne + npc:-1] if post is None else refs[pos + nc + nl + nw + ne + npc:-2]
    xn_ref = refs[-1]
    rows_ref = o_refs[0] if post is None else refs[-2]
    xn_ref[...] = prologue(*xs, *[r[...] for r in c_refs]).astype(xn_ref.dtype)
    lhs = [xn_ref] + list(l_refs)
    for c in range(rows_ref.shape[1] // cw):
        sl = slice(c * cw, (c + 1) * cw)
        accs = [jnp.dot(lhs[li][...], w[:, sl], preferred_element_type=F32) for w, li in zip(w_refs, lhs_of_w)]
        es = ([x0_ref[:, sl]] if res_from_x else []) + [e[:, sl] for e in e_refs]
        rows_ref[:, sl] = epilogue(accs, es)[0].astype(rows_ref.dtype)
    if post is not None:
        val = post(rows_ref[...], *[p[...] for p in p_refs])
        if out_split is None:
            o_refs[0][...] = val.astype(o_refs[0].dtype)
        else:
            @pl.when(i < out_split)
            def _():
                o_refs[0][...] = val.astype(o_refs[0].dtype)

            @pl.when(i >= out_split)
            def _():
                o_refs[1][...] = val.astype(o_refs[1].dtype)


def _mm_resident(x_list, consts, lhs_list, ws, lhs_of_w, extras, *, prologue, epilogue, out_dtype, tm, name,
                 res_from_x=False, cw=512, post=None, post_consts=(), out_rows=None):
    rows = lambda x: x[0].shape[0] + x[1].shape[0] if isinstance(x, tuple) else x.shape[0]
    m = rows(x_list[0])
    k = ws[lhs_of_w.index(0)].shape[0]
    n = ws[0].shape[1]
    row = lambda a: pl.BlockSpec((tm, a.shape[1]), lambda i: (i, 0))
    whole = lambda a: pl.BlockSpec(a.shape, lambda i: (0, 0), pipeline_mode=pl.Buffered(1))
    arrays, in_specs, x_split = [], [], []
    for x in x_list:
        if isinstance(x, tuple):
            a, b = x
            na = a.shape[0] // tm
            assert a.shape[0] % tm == 0 and b.shape[0] % tm == 0
            arrays += [a, b]
            in_specs += [pl.BlockSpec((tm, a.shape[1]), lambda i, na=na: (jnp.minimum(i, na - 1), 0)),
                         pl.BlockSpec((tm, b.shape[1]), lambda i, na=na: (jnp.maximum(i - na, 0), 0))]
            x_split.append(na)
        else:
            arrays.append(x)
            in_specs.append(row(x))
            x_split.append(None)
    arrays += list(consts) + list(lhs_list) + list(ws) + list(extras) + list(post_consts)
    in_specs += ([whole(a) for a in consts] + [row(a) for a in lhs_list] + [whole(a) for a in ws]
                 + [row(a) for a in extras] + [whole(a) for a in post_consts])
    out_split = None
    out_specs = pl.BlockSpec((tm, n), lambda i: (i, 0))
    out_shape = jax.ShapeDtypeStruct((m, n), out_dtype)
    if out_rows is not None:
        ra, rb = out_rows
        assert post is not None and ra % tm == 0 and rb % tm == 0 and ra + rb == m
        out_split = ra // tm
        out_specs = [pl.BlockSpec((tm, n), lambda i: (jnp.minimum(i, out_split - 1), 0)),
                     pl.BlockSpec((tm, n), lambda i: (jnp.maximum(i - out_split, 0), 0))]
        out_shape = [jax.ShapeDtypeStruct((ra, n), out_dtype), jax.ShapeDtypeStruct((rb, n), out_dtype)]
    scratch = ([pltpu.VMEM((tm, n), F32)] if post is not None else []) + [pltpu.VMEM((tm, k), BF16)]
    body = functools.partial(
        _mm_res_body, x_split=tuple(x_split), nc=len(consts), nl=len(lhs_list), nw=len(ws), ne=len(extras),
        npc=len(post_consts), lhs_of_w=tuple(lhs_of_w), res_from_x=res_from_x, prologue=prologue,
        epilogue=epilogue, post=post, out_split=out_split, cw=cw)
    return pl.pallas_call(
        body, grid=(m // tm,), in_specs=in_specs, out_specs=out_specs, out_shape=out_shape,
        scratch_shapes=scratch,
        compiler_params=_cparams(("arbitrary",) if out_rows is not None else ("parallel",)), name=name,
    )(*arrays)


def _kv_expand_body(ckv_ref, kr_ref, wk_ref, wv_ref, k_ref, v_ref, *, nh):
    c = ckv_ref[...].astype(BF16)
    kn = jnp.dot(c, wk_ref[...], preferred_element_type=F32)
    vv = jnp.dot(c, wv_ref[...], preferred_element_type=F32)
    kr = kr_ref[...].astype(k_ref.dtype)
    ones = jnp.ones((c.shape[0], V_DIM), v_ref.dtype)
    for hh in range(nh):
        k_ref[hh, :, :QK_NOPE] = kn[:, hh * QK_NOPE:(hh + 1) * QK_NOPE].astype(k_ref.dtype)
        k_ref[hh, :, QK_NOPE:] = kr
        v_ref[hh, :, :V_DIM] = vv[:, hh * V_DIM:(hh + 1) * V_DIM].astype(v_ref.dtype)
        v_ref[hh, :, V_DIM:] = ones


def _kv_expand(ckv, kr, w_uk_flat, w_uv_flat, rows, *, tm):
    nh = w_uk_flat.shape[1] // QK_NOPE
    whole = lambda a: pl.BlockSpec(a.shape, lambda i: (0, 0), pipeline_mode=pl.Buffered(1))
    return pl.pallas_call(
        functools.partial(_kv_expand_body, nh=nh), grid=(rows // tm,),
        in_specs=[pl.BlockSpec((tm, KV_RANK), lambda i: (i, 0)), pl.BlockSpec((tm, QK_ROPE), lambda i: (i, 0)),
                  whole(w_uk_flat), whole(w_uv_flat)],
        out_specs=[pl.BlockSpec((nh, tm, QK_DIM), lambda i: (0, i, 0)),
                   pl.BlockSpec((nh, tm, 2 * V_DIM), lambda i: (0, i, 0))],
        out_shape=[jax.ShapeDtypeStruct((nh, rows, QK_DIM), BF16),
                   jax.ShapeDtypeStruct((nh, rows, 2 * V_DIM), BF16)],
        compiler_params=_cparams(("parallel",)), name="kv_heads",
    )(ckv, kr, w_uk_flat, w_uv_flat)


def _q_expand_body(ql_ref, wn_ref, wr_ref, wrr_ref, cos_ref, sin_ref, q_ref, *, nh):
    x = ql_ref[...]
    qn = jnp.dot(x, wn_ref[...], preferred_element_type=F32) * QK_PRESCALE
    reps = nh * QK_ROPE // cos_ref.shape[1]
    cos = jnp.concatenate([cos_ref[...]] * reps, axis=1)
    sin = jnp.concatenate([sin_ref[...]] * reps, axis=1)
    qr = (jnp.dot(x, wr_ref[...], preferred_element_type=F32) * cos
          + jnp.dot(x, wrr_ref[...], preferred_element_type=F32) * sin)
    for hh in range(nh):
        q_ref[hh, :, :QK_NOPE] = qn[:, hh * QK_NOPE:(hh + 1) * QK_NOPE].astype(q_ref.dtype)
        q_ref[hh, :, QK_NOPE:] = qr[:, hh * QK_ROPE:(hh + 1) * QK_ROPE].astype(q_ref.dtype)


def _q_expand(ql, w_nope, w_rope, w_rope_rot, cos2, sin2, *, tm):
    m = ql.shape[0]
    nh = w_nope.shape[1] // QK_NOPE
    whole = lambda a: pl.BlockSpec(a.shape, lambda i: (0, 0), pipeline_mode=pl.Buffered(1))
    row = lambda a: pl.BlockSpec((tm, a.shape[1]), lambda i: (i, 0))
    return pl.pallas_call(
        functools.partial(_q_expand_body, nh=nh), grid=(m // tm,),
        in_specs=[row(ql), whole(w_nope), whole(w_rope), whole(w_rope_rot), row(cos2), row(sin2)],
        out_specs=pl.BlockSpec((nh, tm, QK_DIM), lambda i: (0, i, 0)),
        out_shape=jax.ShapeDtypeStruct((nh, m, QK_DIM), BF16),
        compiler_params=_cparams(("parallel",)), name="q_heads",
    )(ql, w_nope, w_rope, w_rope_rot, cos2, sin2)


def _conv_silu_chunk(ext, in_ref, st_ref, w_ref, b_ref, first, q):
    @pl.when(first)
    def _():
        ext[0:SUBLANES, :] = st_ref[...]

    ext[SUBLANES:SUBLANES + q, :] = in_ref[...]
    acc = b_ref[...]
    for k in range(CONV_W):
        off = SUBLANES - (CONV_W - 1) + k
        acc = acc + ext[off:off + q, :] * w_ref[k:k + 1, :]
    ext[0:SUBLANES, :] = ext[q:q + SUBLANES, :]
    return _silu(acc)


def _ssd_body(x_ref, b_ref, c_ref, stx_ref, stb_ref, stc_ref, wx_ref, wb_ref, wc_ref, bx_ref, bb_ref, bc_ref,
              dt_ref, dtT_ref, bias_ref, biasT_ref, a_ref, aT_ref, dsk_ref, s0_ref,
              y_ref, so_ref, s_scr, ext_x, ext_b, ext_c, *, q, gps):
    c = pl.program_id(2)
    first = c == 0

    @pl.when(first)
    def _():
        s_scr[...] = s0_ref[...]

    xs_all = _conv_silu_chunk(ext_x, x_ref, stx_ref, wx_ref, bx_ref, first, q)
    bm_all = _conv_silu_chunk(ext_b, b_ref, stb_ref, wb_ref, bb_ref, first, q).astype(BF16)
    cm_all = _conv_silu_chunk(ext_c, c_ref, stc_ref, wc_ref, bc_ref, first, q).astype(BF16)

    row = lax.broadcasted_iota(jnp.int32, (q, q), 0)
    col = lax.broadcasted_iota(jnp.int32, (q, q), 1)
    causal = row >= col
    tri = jnp.where(causal, 1.0, 0.0).astype(BF16)
    tri_t = jnp.where(row <= col, 1.0, 0.0).astype(BF16)
    left = lax.broadcasted_iota(jnp.int32, (q, LANES), 1) < SSM_HEAD_DIM
    gw = GROUP_WIDTH
    for gi in range(gps):
        dt = _softplus(dt_ref[gi] + bias_ref[gi])
        dta = dt * a_ref[gi]
        dta_t = _softplus(dtT_ref[gi] + biasT_ref[gi]) * aT_ref[gi]
        acum = sum(jnp.dot(tri, piece, preferred_element_type=F32) for piece in _split3_bf16(dta))
        acum_t = sum(jnp.dot(piece, tri_t, preferred_element_type=F32) for piece in _split3_bf16(dta_t))
        bm = bm_all[:, gi * SSM_STATE:(gi + 1) * SSM_STATE]
        cm = cm_all[:, gi * SSM_STATE:(gi + 1) * SSM_STATE]
        cb = lax.dot_general(cm, bm, (((1,), (1,)), ((), ())), preferred_element_type=F32)
        s_prev = s_scr[gi]
        y_off = jnp.dot(cm, s_prev.astype(BF16), preferred_element_type=F32)
        xw_parts, dec_parts = [], []
        for j in range(HEADS_PER_GROUP // 2):
            h0, h1 = 2 * j, 2 * j + 1
            sl = slice(j * LANES, (j + 1) * LANES)
            sg = slice(gi * gw + j * LANES, gi * gw + (j + 1) * LANES)
            col0, col1 = acum[:, h0:h0 + 1], acum[:, h1:h1 + 1]
            pa = jnp.where(left, col0, col1)
            dtp = jnp.where(left, dt[:, h0:h0 + 1], dt[:, h1:h1 + 1])
            x = xs_all[:, sg]
            xdt = x * dtp
            m0 = (jnp.exp2(jnp.where(causal, col0 - acum_t[h0:h0 + 1, :], -jnp.inf)) * cb).astype(BF16)
            m1 = (jnp.exp2(jnp.where(causal, col1 - acum_t[h1:h1 + 1, :], -jnp.inf)) * cb).astype(BF16)
            x_l = jnp.where(left, xdt, 0.0).astype(BF16)
            x_r = jnp.where(left, 0.0, xdt).astype(BF16)
            y_diag = (jnp.dot(m0, x_l, preferred_element_type=F32)
                      + jnp.dot(m1, x_r, preferred_element_type=F32))
            y = y_diag + y_off[:, sl] * jnp.exp2(pa) + x * dsk_ref[:, sg]
            y_ref[:, sg] = y.astype(y_ref.dtype)
            last = pa[q - 1:q, :]
            xw_parts.append((xdt * jnp.exp2(last - pa)).astype(BF16))
            dec_parts.append(jnp.exp2(last))
        xw = jnp.concatenate(xw_parts, axis=1)
        dec = jnp.concatenate(dec_parts, axis=1)
        s_new = s_prev * dec + lax.dot_general(bm, xw, (((0,), (0,)), ((), ())), preferred_element_type=F32)
        s_scr[gi] = s_new
        so_ref[gi] = s_new


def _ssd_scan(xbc3, batch0, nb, L, st, w_conv, b_conv, dtg, dtg_t, bias, bias_t, a, a_t, dsk, s0, *, q, gps=4):
    g_, r_, n_ = SSM_GROUPS, HEADS_PER_GROUP, SSM_STATE
    gw = GROUP_WIDTH
    xw_, bw_ = gps * gw, gps * n_
    b_off = D_INNER // bw_
    c_off = b_off + g_ // gps
    grid = (nb, g_ // gps, L // q)
    col_x = lambda b, g, c: (0, g)
    col_b = lambda b, g, c: (0, b_off + g)
    col_c = lambda b, g, c: (0, c_off + g)
    return pl.pallas_call(
        functools.partial(_ssd_body, q=q, gps=gps), grid=grid,
        in_specs=[
            pl.BlockSpec((None, q, xw_), lambda b, g, c: (batch0 + b, c, g)),
            pl.BlockSpec((None, q, bw_), lambda b, g, c: (batch0 + b, c, b_off + g)),
            pl.BlockSpec((None, q, bw_), lambda b, g, c: (batch0 + b, c, c_off + g)),
            pl.BlockSpec((None, SUBLANES, xw_), lambda b, g, c: (b, 0, g)),
            pl.BlockSpec((None, SUBLANES, bw_), lambda b, g, c: (b, 0, b_off + g)),
            pl.BlockSpec((None, SUBLANES, bw_), lambda b, g, c: (b, 0, c_off + g)),
            pl.BlockSpec((CONV_W, xw_), col_x),
            pl.BlockSpec((CONV_W, bw_), col_b),
            pl.BlockSpec((CONV_W, bw_), col_c),
            pl.BlockSpec((1, xw_), col_x),
            pl.BlockSpec((1, bw_), col_b),
            pl.BlockSpec((1, bw_), col_c),
            pl.BlockSpec((None, gps, q, r_), lambda b, g, c: (b, g, c, 0)),
            pl.BlockSpec((None, gps, r_, q), lambda b, g, c: (b, g, 0, c)),
            pl.BlockSpec((gps, 1, r_), lambda b, g, c: (g, 0, 0)),
            pl.BlockSpec((gps, r_, 1), lambda b, g, c: (g, 0, 0)),
            pl.BlockSpec((gps, 1, r_), lambda b, g, c: (g, 0, 0)),
            pl.BlockSpec((gps, r_, 1), lambda b, g, c: (g, 0, 0)),
            pl.BlockSpec((1, xw_), col_x),
            pl.BlockSpec((None, gps, n_, gw), lambda b, g, c: (b, g, 0, 0)),
        ],
        out_specs=[
            pl.BlockSpec((None, q, xw_), lambda b, g, c: (b, c, g)),
            pl.BlockSpec((None, gps, n_, gw), lambda b, g, c: (b, g, 0, 0)),
        ],
        out_shape=[jax.ShapeDtypeStruct((nb, L, D_INNER), BF16),
                   jax.ShapeDtypeStruct((nb, g_, n_, gw), F32)],
        scratch_shapes=[pltpu.VMEM((gps, n_, gw), F32), pltpu.VMEM((q + SUBLANES, xw_), F32),
                        pltpu.VMEM((q + SUBLANES, bw_), F32), pltpu.VMEM((q + SUBLANES, bw_), F32)],
        compiler_params=_cparams(("parallel", "parallel", "arbitrary")), name="ssd_scan",
    )(xbc3, xbc3, xbc3, st, st, st, w_conv, w_conv, w_conv, b_conv, b_conv, b_conv,
      dtg, dtg_t, bias, bias_t, a, a_t, dsk, s0)


def _gather_rows(tok_ref, base, src_hbm, dst, sem, n, rp=1):
    def body(r, carry):
        t = pl.multiple_of(tok_ref[base + r] * rp, rp)
        pltpu.make_async_copy(src_hbm.at[pl.ds(t, rp)], dst.at[pl.ds(pl.multiple_of(r * rp, rp), rp)], sem).start()
        return carry

    lax.fori_loop(0, n, body, 0)


def _gather_rows_unrolled(tok_ref, base, src_hbm, dst, sem, n):
    for r in range(n):
        t = tok_ref[base + r]
        pltpu.make_async_copy(src_hbm.at[pl.ds(t, 1)], dst.at[pl.ds(r, 1)], sem).start(priority=r % 2)


def _moe_body(blk_e_ref, nxt_e_ref, tok_ref, nused_ref, xt_hbm, wg_hbm, wu_hbm, wd_hbm, yb_ref,
              xbuf, xsem, wg32, wu32, wd32, wsem, wg_bf, wu_bf, wd_bf, xb, *, bm, layer):
    b = pl.program_id(0)
    slot = b % 2
    nused = nused_ref[0]
    e_cur = blk_e_ref[b]

    def weight_copies(e):
        return (pltpu.make_async_copy(wg_hbm.at[layer, e], wg32, wsem.at[0]),
                pltpu.make_async_copy(wu_hbm.at[layer, e], wu32, wsem.at[1]),
                pltpu.make_async_copy(wd_hbm.at[layer, e], wd32, wsem.at[2]))

    def gather_unrolled(base, slot_):
        for r in range(bm):
            t = pl.multiple_of(tok_ref[base + r] * SUBLANES, SUBLANES)
            pltpu.make_async_copy(xt_hbm.at[pl.ds(t, SUBLANES)], xbuf.at[slot_, pl.ds(r * SUBLANES, SUBLANES)],
                                  xsem.at[slot_]).start()

    @pl.when(jnp.logical_and(b == 0, nused > 0))
    def _():
        for cp in weight_copies(e_cur):
            cp.start(priority=1)
        _gather_rows(tok_ref, 0, xt_hbm, xbuf.at[0], xsem.at[0], bm, rp=SUBLANES)

    prev_e = blk_e_ref[jnp.maximum(b - 1, 0)]
    changed = jnp.logical_or(b == 0, e_cur != prev_e)

    @pl.when(jnp.logical_and(changed, b < nused))
    def _():
        for cp in weight_copies(e_cur):
            cp.wait()
        wg_bf[...] = wg32[...].astype(BF16)
        wu_bf[...] = wu32[...].astype(BF16)
        wd_bf[...] = wd32[...].astype(BF16)
        nxt = nxt_e_ref[b]

        @pl.when(nxt >= 0)
        def _():
            for cp in weight_copies(nxt):
                cp.start(priority=1)

    def compute(prefetch):
        pltpu.make_async_copy(xbuf.at[slot], xbuf.at[slot], xsem.at[slot]).wait()
        half = xb.shape[1] // 2
        for s in range(SUBLANES):
            w = xbuf[slot, pl.ds(s, bm, stride=SUBLANES), :]
            cols = slice(s * LANES, (s + 1) * LANES)
            xb[:, cols] = lax.bitcast_convert_type(w << 16, F32).astype(BF16)
            xb[:, half + s * LANES:half + (s + 1) * LANES] = lax.bitcast_convert_type(
                w & jnp.uint32(0xFFFF0000), F32).astype(BF16)
        if prefetch:
            gather_unrolled((b + 1) * bm, 1 - slot)
        x = xb[...]
        g = jnp.dot(x, wg_bf[...], preferred_element_type=F32)
        u = jnp.dot(x, wu_bf[...], preferred_element_type=F32)
        hid = (_silu(g) * u).astype(BF16)
        yb_ref[...] = jnp.dot(hid, wd_bf[...], preferred_element_type=F32)

    @pl.when(b + 1 < nused)
    def _():
        compute(True)

    @pl.when(b + 1 == nused)
    def _():
        compute(False)

    @pl.when(b >= nused)
    def _():
        yb_ref[...] = jnp.zeros(yb_ref.shape, yb_ref.dtype)


def _moe_experts(xt, blk_e, nxt_e, row_token, nused, w_gate, w_up, w_down, *, bm, layer):
    nblk = blk_e.shape[0]
    d = w_gate.shape[2]
    de = w_gate.shape[3]
    assert d == 2 * SUBLANES * LANES and xt.shape[1] == LANES and xt.dtype == jnp.uint32
    anyspec = pl.BlockSpec(memory_space=pl.ANY)
    gs = pltpu.PrefetchScalarGridSpec(
        num_scalar_prefetch=4, grid=(nblk,),
        in_specs=[anyspec, anyspec, anyspec, anyspec],
        out_specs=pl.BlockSpec((bm, d), lambda b, be, nx, tok, nu: (b, 0)),
        scratch_shapes=[
            pltpu.VMEM((2, bm * SUBLANES, LANES), jnp.uint32),
            pltpu.SemaphoreType.DMA((2,)),
            pltpu.VMEM((d, de), F32),
            pltpu.VMEM((d, de), F32),
            pltpu.VMEM((de, d), F32),
            pltpu.SemaphoreType.DMA((3,)),
            pltpu.VMEM((d, de), BF16),
            pltpu.VMEM((d, de), BF16),
            pltpu.VMEM((de, d), BF16),
            pltpu.VMEM((bm, d), BF16),
        ],
    )
    return pl.pallas_call(
        functools.partial(_moe_body, bm=bm, layer=layer), grid_spec=gs,
        out_shape=jax.ShapeDtypeStruct((nblk * bm, d), F32),
        compiler_params=_cparams(("arbitrary",)), name="moe_experts",
    )(blk_e, nxt_e, row_token, nused, xt, w_gate, w_up, w_down)


def _combine_body(dest_ref, h_ref, g_ref, yb_hbm, o_ref, buf, sem, *, tc, ntile):
    i = pl.program_id(0)
    slot = i % 2
    n = TOP_K * tc

    @pl.when(i == 0)
    def _():
        _gather_rows(dest_ref, 0, yb_hbm, buf.at[0], sem.at[0], n)

    @pl.when(i + 1 < ntile)
    def _():
        _gather_rows_unrolled(dest_ref, (i + 1) * n, yb_hbm, buf.at[1 - slot], sem.at[1 - slot], n)

    pltpu.make_async_copy(buf.at[slot], buf.at[slot], sem.at[slot]).wait()
    g = g_ref[...]
    o_ref[...] = (h_ref[...] + g[:, 0:1] * buf[slot, 0:tc, :] + g[:, 1:2] * buf[slot, tc:2 * tc, :])


def _moe_combine(h, yb, dest_tiles, gates, *, tc):
    t, d = h.shape
    ntile = t // tc
    gs = pltpu.PrefetchScalarGridSpec(
        num_scalar_prefetch=1, grid=(ntile,),
        in_specs=[
            pl.BlockSpec((tc, d), lambda i, dst: (i, 0)),
            pl.BlockSpec((tc, LANES), lambda i, dst: (i, 0)),
            pl.BlockSpec(memory_space=pl.ANY),
        ],
        out_specs=pl.BlockSpec((tc, d), lambda i, dst: (i, 0)),
        scratch_shapes=[pltpu.VMEM((2, TOP_K * tc, d), F32), pltpu.SemaphoreType.DMA((2,))],
    )
    return pl.pallas_call(
        functools.partial(_combine_body, tc=tc, ntile=ntile), grid_spec=gs,
        out_shape=jax.ShapeDtypeStruct((t, d), F32),
        compiler_params=_cparams(("arbitrary",)), name="moe_combine",
    )(dest_tiles, h, gates, yb)


def _moe_plan(idx, bm, tc):
    t = idx.shape[0]
    a = t * TOP_K
    e_ = N_EXPERTS
    flat_e = idx.reshape(a)
    onehot = (flat_e[:, None] == jnp.arange(e_, dtype=jnp.int32)[None, :]).astype(jnp.int32)
    csum = jnp.cumsum(onehot, axis=0)
    rank = jnp.sum(onehot * csum, axis=1) - 1
    counts = csum[-1]
    padded = (counts + bm - 1) // bm * bm
    pad_end = jnp.cumsum(padded)
    pad_start = pad_end - padded
    dest = jnp.sum(onehot * pad_start[None, :], axis=1) + rank
    nblk = -(-a // bm) + e_
    rows = nblk * bm
    row_token = jnp.zeros((rows,), jnp.int32).at[dest].set(jnp.arange(a, dtype=jnp.int32) // TOP_K)
    blk_row0 = jnp.arange(nblk, dtype=jnp.int32) * bm
    blk_e = jnp.minimum(jnp.sum((pad_end[None, :] <= blk_row0[:, None]).astype(jnp.int32), axis=1), e_ - 1)
    ids = jnp.arange(e_, dtype=jnp.int32)
    later_used = jnp.logical_and(ids[None, :] > ids[:, None], (counts > 0)[None, :])
    nxt_of_e = jnp.min(jnp.where(later_used, ids[None, :], e_), axis=1)
    nxt_of_e = jnp.where(nxt_of_e == e_, -1, nxt_of_e)
    nxt_e = jnp.sum(jnp.where(blk_e[:, None] == ids[None, :], nxt_of_e[None, :], 0), axis=1).astype(jnp.int32)
    nused = (pad_end[-1] // bm).astype(jnp.int32).reshape(1)
    dest_tiles = dest.reshape(t // tc, tc, TOP_K).transpose(0, 2, 1).reshape(a).astype(jnp.int32)
    return blk_e.astype(jnp.int32), nxt_e, row_token, nused, dest_tiles


def _flash_body(qi_ref, kj_ref, q_ref, k_ref, v_ref, o_ref, m_scr, acc_scr, *, hb):
    p_id = pl.program_id(1)
    qi = qi_ref[p_id]
    kj = kj_ref[p_id]
    tk = k_ref.shape[1]

    @pl.when(kj == 0)
    def _():
        m_scr[...] = jnp.full(m_scr.shape, -jnp.inf, F32)
        acc_scr[...] = jnp.zeros(acc_scr.shape, F32)

    def step(diag):
        for hh in range(hb):
            s = lax.dot_general(q_ref[hh], k_ref[hh], (((1,), (1,)), ((), ())), preferred_element_type=F32)
            if diag:
                r = lax.broadcasted_iota(jnp.int32, s.shape, 0)
                c = lax.broadcasted_iota(jnp.int32, s.shape, 1)
                s = jnp.where((c // CHUNK) <= (r // CHUNK), s, -jnp.inf)
            m_prev = m_scr[hh]
            m_next = jnp.maximum(m_prev, jnp.max(s, axis=1, keepdims=True))
            p = jnp.exp2(s - jnp.concatenate([m_next] * (tk // LANES), axis=1))
            alpha = jnp.exp2(m_prev - m_next)
            acc_scr[hh] = (jnp.concatenate([alpha, alpha], axis=1) * acc_scr[hh]
                           + jnp.dot(p.astype(BF16), v_ref[hh], preferred_element_type=F32))
            m_scr[hh] = m_next

    @pl.when(kj < qi)
    def _():
        step(False)

    @pl.when(kj == qi)
    def _():
        step(True)
        for hh in range(hb):
            a = acc_scr[hh]
            o_ref[:, hh * V_DIM:(hh + 1) * V_DIM] = (a[:, :V_DIM] / a[:, V_DIM:]).astype(o_ref.dtype)


def _flash_prompt(q, k, v, lp, *, tq, hb):
    assert V_DIM == LANES
    nh = q.shape[0]
    nq = lp // tq
    pairs = [(a, b) for a in range(nq) for b in range(a + 1)]
    qi_tab = jnp.asarray(np.array([p[0] for p in pairs], np.int32))
    kj_tab = jnp.asarray(np.array([p[1] for p in pairs], np.int32))
    gs = pltpu.PrefetchScalarGridSpec(
        num_scalar_prefetch=2, grid=(nh // hb, len(pairs)),
        in_specs=[
            pl.BlockSpec((hb, tq, QK_DIM), lambda h, p, qi, kj: (h, qi[p], 0)),
            pl.BlockSpec((hb, tq, QK_DIM), lambda h, p, qi, kj: (h, kj[p], 0)),
            pl.BlockSpec((hb, tq, 2 * V_DIM), lambda h, p, qi, kj: (h, kj[p], 0)),
        ],
        out_specs=pl.BlockSpec((tq, hb * V_DIM), lambda h, p, qi, kj: (qi[p], h)),
        scratch_shapes=[pltpu.VMEM((hb, tq, LANES), F32), pltpu.VMEM((hb, tq, 2 * V_DIM), F32)],
    )
    return pl.pallas_call(
        functools.partial(_flash_body, hb=hb), grid_spec=gs,
        out_shape=jax.ShapeDtypeStruct((lp, nh * V_DIM), BF16),
        compiler_params=_cparams(("parallel", "arbitrary")), name="flash_prompt",
    )(qi_tab, kj_tab, q, k, v)


def _attn_sample_body(q_ref, ckv_ref, kr_ref, ckv_new_ref, kr_new_ref, o_ref, *, ls, past):
    nh = q_ref.shape[0]
    q = q_ref[...].reshape(nh * ls, q_ref.shape[2])
    q_lat, q_rot = q[:, :KV_RANK], q[:, KV_RANK:KV_RANK + QK_ROPE]
    nt = (((1,), (1,)), ((), ()))

    def scores(lat, rot, pos0):
        s = (lax.dot_general(q_lat, lat, nt, preferred_element_type=F32)
             + lax.dot_general(q_rot, rot, nt, preferred_element_type=F32))
        r = lax.broadcasted_iota(jnp.int32, s.shape, 0)
        c = lax.broadcasted_iota(jnp.int32, s.shape, 1)
        q_pos = past + r % ls
        return jnp.where(((pos0 + c) // CHUNK) <= (q_pos // CHUNK), s, -jnp.inf)

    lat_p = ckv_ref[...].astype(BF16)
    lat_n = ckv_new_ref[...].astype(BF16)
    s_p = scores(lat_p, kr_ref[...].astype(BF16), 0)
    s_n = scores(lat_n, kr_new_ref[...].astype(BF16), past)
    m = jnp.maximum(jnp.max(s_p, axis=-1, keepdims=True), jnp.max(s_n, axis=-1, keepdims=True))
    p_p = jnp.exp2(s_p - m)
    p_n = jnp.exp2(s_n - m)
    l = jnp.sum(p_p, axis=-1, keepdims=True) + jnp.sum(p_n, axis=-1, keepdims=True)
    o = (jnp.dot(p_p.astype(BF16), lat_p, preferred_element_type=F32)
         + jnp.dot(p_n.astype(BF16), lat_n, preferred_element_type=F32)) / l
    o_ref[...] = o.reshape(nh, ls, KV_RANK).astype(o_ref.dtype)


def _attn_sample(q_abs, cache_ckv, cache_kr, ckv_new, kr_new, *, ls):
    nh, ts, dk = q_abs.shape
    nb, past, _ = cache_ckv.shape
    return pl.pallas_call(
        functools.partial(_attn_sample_body, ls=ls, past=past), grid=(nb,),
        in_specs=[
            pl.BlockSpec((nh, ls, dk), lambda b: (0, b, 0)),
            pl.BlockSpec((None, past, KV_RANK), lambda b: (b, 0, 0)),
            pl.BlockSpec((None, past, QK_ROPE), lambda b: (b, 0, 0)),
            pl.BlockSpec((None, ls, KV_RANK), lambda b: (b, 0, 0)),
            pl.BlockSpec((None, ls, QK_ROPE), lambda b: (b, 0, 0)),
        ],
        out_specs=pl.BlockSpec((nh, ls, KV_RANK), lambda b: (0, b, 0)),
        out_shape=jax.ShapeDtypeStruct((nh, ts, KV_RANK), BF16),
        compiler_params=_cparams(("parallel",)), name="attn_sample",
    )(q_abs, cache_ckv, cache_kr, ckv_new, kr_new)


def _rope_tables(pos):
    half = QK_ROPE // 2
    inv = ROPE_THETA ** (-np.arange(half, dtype=np.float64) / half)
    ang = np.asarray(pos, np.float64)[:, None] * inv[None, :]
    cos = np.concatenate([np.cos(ang), np.cos(ang)], axis=1)
    sin = np.concatenate([-np.sin(ang), np.sin(ang)], axis=1)
    return cos.astype(np.float32), sin.astype(np.float32)


def _swap_rope_halves(w):
    half = QK_ROPE // 2
    return jnp.concatenate([w[..., half:], w[..., :half]], axis=-1)


def _moe_layer(h, g_ffn, w_rg, b_rg, w_re, b_re, w_gate, w_up, w_down, *, tm, layer):
    t, d = h.shape
    npad = LANES - MOE_GROUPS - N_EXPERTS
    w_r = jnp.concatenate([w_rg, w_re, jnp.zeros((d, npad), F32)], axis=1)
    b_r = jnp.concatenate([b_rg, b_re, jnp.zeros((npad,), F32)]).reshape(1, LANES)
    grid = (t // tm, 1)
    x_args = [(h, pl.BlockSpec((tm, d), lambda i, j: (i, 0))),
              (g_ffn.reshape(1, d), pl.BlockSpec((1, d), lambda i, j: (0, 0)))]
    w_hi = w_r.astype(BF16)
    w_lo = (w_r - w_hi.astype(F32)).astype(BF16)
    w_args = [(jnp.concatenate([w_hi, w_hi, w_lo], axis=0), pl.BlockSpec((3 * d, LANES), lambda i, j: (0, 0)))]
    e_args = [(b_r, pl.BlockSpec((1, LANES), lambda i, j: (0, 0)))]
    tile = pl.BlockSpec((tm, LANES), lambda i, j: (i, 0))
    out_defs = [(jax.ShapeDtypeStruct((t, LANES), jnp.int32), tile),
                (jax.ShapeDtypeStruct((t, LANES), F32), tile)]
    assert d == 2 * SUBLANES * LANES
    xn_out = (jax.ShapeDtypeStruct((t * SUBLANES, LANES), jnp.uint32),
              pl.BlockSpec((tm * SUBLANES, LANES), lambda i, j: (i, 0)))
    idx_t, gates_t, xt = _fused_mm(x_args, w_args, e_args, out_defs, grid=grid, prologue=_pro_rms,
                                   epilogue=_epi_route, xn_shape=(tm, 3 * d), xn_dtype=BF16, xn_out=xn_out,
                                   xn_emit=_emit_token_tiles, xn_store=_split_hi_lo_hi, name="router")
    tc = min(tm, 128)
    blk_e, nxt_e, row_token, nused, dest_tiles = _moe_plan(idx_t[:, :TOP_K], MOE_BM, tc)
    yb = _moe_experts(xt, blk_e, nxt_e, row_token, nused, w_gate, w_up, w_down, bm=MOE_BM, layer=layer)
    return _moe_combine(h, yb, dest_tiles, gates_t, tc=tc)


def _post_rms(rows, g):
    return _rms_rows(rows) * g


def _ple_layer(h, p_rows, g_ple, w_gate, w_proj, *, tm, final=None):
    t, d = h.shape
    extra = {}
    if final is not None:
        extra = dict(post=_post_rms, post_consts=[final[0].reshape(1, d)], out_rows=final[1:])
    return _mm_resident([h], [g_ple.reshape(1, d)], [p_rows.astype(BF16)], [w_gate.astype(BF16), w_proj.astype(BF16)],
                        [0, 1], [], prologue=_pro_rms, epilogue=_epi_ple, out_dtype=F32, tm=tm, name="ple_gate",
                        res_from_x=True, **extra)


def kernel(x_prompt, x_sample, state_conv, state_ssm, cache_kv_latent, cache_k_rope, p_prompt, p_sample,
           g_mix, w_ssm_in, w_conv, b_conv, dt_bias, a_log, d_skip, g_ssm_norm, w_ssm_out,
           g_kv_in, w_dkv, g_kv, w_uk, w_uv, w_dq, g_q, w_uq, w_o,
           g_ffn, w_router_grp, b_router_grp, w_router_exp, b_router_exp, w_exp_gate, w_exp_up, w_exp_down,
           g_ple, w_ple_gate, w_ple_proj, g_final):
    bp, lp_each, d = x_prompt.shape
    bs, ls, _ = x_sample.shape
    past = cache_kv_latent.shape[1]
    assert bp == 1 and w_ssm_in.shape[0] == 1 and w_dq.shape[0] == 1 and d == D_MODEL
    assert ls >= CONV_W - 1 and ls % SUBLANES == 0 and past % CHUNK == 0
    lp = bp * lp_each
    ts = bs * ls
    t = lp + ts
    tm = min(512, math.gcd(lp, ts))
    g_, r_, n_, hd = SSM_GROUPS, HEADS_PER_GROUP, SSM_STATE, SSM_HEAD_DIM

    h = jnp.concatenate([x_prompt.reshape(lp, d), x_sample.reshape(ts, d)], axis=0)

    w_in = w_ssm_in[0]
    w_z = w_in[:, :D_INNER].astype(BF16)
    xw = SSM_IN_XW
    w_x = jnp.pad(w_in[:, D_INNER:], ((0, 0), (0, xw - CONV_DIM - SSM_HEADS))).astype(BF16)
    g0 = g_mix[0].reshape(1, d)
    tm_in = max(c for c in range(16, 1153, 16) if t % c == 0)
    mm_in = functools.partial(_mm_rows, [h], [g0], prologue=_pro_rms, tm=tm_in)
    z = mm_in(w_z, [], epilogue=_epi_silu, out_dtype=BF16, tn=1024, name="ssm_in_z")[0]
    xbc = mm_in(w_x, [], epilogue=_epi_id, out_dtype=F32, tn=xw // 5, name="ssm_in_xbc")[0]
    dt_raw = xbc[:, CONV_DIM:CONV_DIM + SSM_HEADS]

    wc = w_conv[0]
    bc = b_conv[0].reshape(1, CONV_DIM)
    st_p = jnp.zeros((1, SUBLANES, CONV_DIM), F32)
    st_s = jnp.pad(state_conv[0].astype(F32), ((0, 0), (SUBLANES - (CONV_W - 1), 0), (0, 0)))
    conv_p = xbc[lp - (CONV_W - 1):lp, :CONV_DIM].reshape(1, 1, CONV_W - 1, CONV_DIM)
    conv_s = xbc[lp:, :CONV_DIM].reshape(bs, ls, CONV_DIM)[:, ls - (CONV_W - 1):].reshape(
        1, bs, CONV_W - 1, CONV_DIM)

    bias = dt_bias[0].astype(F32).reshape(g_, 1, r_)
    a_neg = (-jnp.exp(a_log[0].astype(F32)) * math.log2(math.e)).reshape(g_, 1, r_)
    dsk = jnp.repeat(d_skip[0].astype(F32), hd).reshape(1, D_INNER)

    def dt_views(rows, nb, L):
        v = rows.reshape(nb, L, g_, r_).transpose(0, 2, 1, 3)
        return v, v.transpose(0, 1, 3, 2)

    def scan(xbc3, batch0, nb, L, st, rows, s0, q):
        dtg, dtg_t = dt_views(rows, nb, L)
        return _ssd_scan(xbc3, batch0, nb, L, st, wc, bc, dtg, dtg_t, bias, bias.transpose(0, 2, 1),
                         a_neg, a_neg.transpose(0, 2, 1), dsk, s0, q=q)

    def state_in(s):
        nb = s.shape[0]
        return s.astype(F32).reshape(nb, g_, r_ * hd, n_).transpose(0, 1, 3, 2)

    def state_out(s):
        nb = s.shape[0]
        return s.transpose(0, 1, 3, 2).reshape(1, nb, SSM_HEADS, hd, n_)

    y_p, s_p = scan(xbc.reshape(1, t, xw), 0, 1, lp, st_p, dt_raw[:lp],
                    jnp.zeros((1, g_, n_, r_ * hd), F32), min(SSD_Q, lp))
    y_s, s_s = scan(xbc.reshape(t // ls, ls, xw), lp // ls, bs, ls, st_s, dt_raw[lp:],
                    state_in(state_ssm[0]), ls)
    ssm_p, ssm_s = state_out(s_p), state_out(s_s)
    y_all = (y_p.reshape(lp, D_INNER), y_s.reshape(ts, D_INNER))

    h = _mm_resident([y_all, z], [g_ssm_norm[0].reshape(1, D_INNER)], [], [w_ssm_out[0].astype(BF16)], [0], [h],
                     prologue=_pro_gated, epilogue=_epi_res, out_dtype=F32, tm=tm // 2, name="ssm_out")

    def ffn_and_ple(h, i, final=None):
        h = _moe_layer(h, g_ffn[i], w_router_grp[i], b_router_grp[i], w_router_exp[i], b_router_exp[i],
                       w_exp_gate, w_exp_up, w_exp_down, tm=tm, layer=i)
        p_rows = jnp.concatenate([p_prompt[i].reshape(lp, -1), p_sample[i].reshape(ts, -1)], axis=0)
        return _ple_layer(h, p_rows, g_ple[i], w_ple_gate[i], w_ple_proj[i], tm=tm, final=final)

    h = ffn_and_ple(h, 0)

    pos = np.concatenate([np.arange(lp), np.tile(past + np.arange(ls), bs)])
    cos64, sin64 = _rope_tables(pos)
    gk = g_kv_in.reshape(1, d)
    w_r = w_dkv[:, KV_RANK:]
    row64 = pl.BlockSpec((tm, QK_ROPE), lambda i, j: (i, 0))
    w64 = pl.BlockSpec((d, QK_ROPE), lambda i, j: (0, 0))
    whole = lambda a: pl.BlockSpec(a.shape, lambda i, j: (0, 0))
    w_lat = w_dkv[:, :KV_RANK].astype(BF16)
    g_kv_row = g_kv.reshape(1, KV_RANK)
    ckv, kr = _fused_mm(
        [(h, pl.BlockSpec((tm, d), lambda i, j: (i, 0))), (gk, whole(gk))],
        [(w_lat, whole(w_lat)), (w_r.astype(BF16), w64), (_swap_rope_halves(w_r).astype(BF16), w64)],
        [(g_kv_row, whole(g_kv_row)), (jnp.asarray(cos64), row64), (jnp.asarray(sin64), row64)],
        [(jax.ShapeDtypeStruct((t, KV_RANK), F32), pl.BlockSpec((tm, KV_RANK), lambda i, j: (i, 0))),
         (jax.ShapeDtypeStruct((t, QK_ROPE), F32), row64)],
        grid=(t // tm, 1), prologue=_pro_rms, epilogue=_epi_latent_kv, xn_shape=(tm, d), name="kv_latent")

    ql = _mm_rows([h], [g_mix[1].reshape(1, d)], w_dq[0].astype(BF16), [], prologue=_pro_rms,
                  epilogue=_epi_rms_out, out_dtype=BF16, tm=tm, tn=Q_RANK, name="q_latent",
                  e_consts=[g_q[0].reshape(1, Q_RANK)])[0]
    nh = MLA_HEADS
    w_q = w_uq[0].reshape(Q_RANK, nh, QK_DIM)
    w_q_rope = w_q[..., QK_NOPE:]
    cos_q = np.concatenate([cos64, cos64], axis=1) * np.float32(QK_PRESCALE)
    sin_q = np.concatenate([sin64, sin64], axis=1) * np.float32(QK_PRESCALE)
    q_cat = _q_expand(ql, w_q[..., :QK_NOPE].reshape(Q_RANK, nh * QK_NOPE).astype(BF16),
                      w_q_rope.reshape(Q_RANK, nh * QK_ROPE).astype(BF16),
                      _swap_rope_halves(w_q_rope).reshape(Q_RANK, nh * QK_ROPE).astype(BF16),
                      jnp.asarray(cos_q), jnp.asarray(sin_q), tm=tm)

    lat_pad = LAT_PAD - KV_RANK - QK_ROPE
    w_uk_h = w_uk.transpose(1, 0, 2)
    w_uv_h = w_uv.transpose(1, 0, 2)
    eye_r = jnp.broadcast_to(jnp.eye(QK_ROPE, dtype=F32), (nh, QK_ROPE, QK_ROPE))
    k_cat, v_h = _kv_expand(ckv, kr, w_uk.reshape(KV_RANK, nh * QK_NOPE).astype(BF16),
                            w_uv.reshape(KV_RANK, nh * V_DIM).astype(BF16), lp, tm=tm)
    o_p = _flash_prompt(q_cat, k_cat, v_h, lp, tq=min(512, lp), hb=16)

    w_abs = jnp.concatenate([
        jnp.concatenate([w_uk_h.transpose(0, 2, 1), jnp.zeros((nh, QK_NOPE, LAT_PAD - KV_RANK), F32)], axis=2),
        jnp.concatenate([jnp.zeros((nh, QK_ROPE, KV_RANK), F32), eye_r,
                         jnp.zeros((nh, QK_ROPE, lat_pad), F32)], axis=2)], axis=1).astype(BF16)
    blk_s = lp // ts if lp % ts == 0 else None
    assert blk_s is not None
    q_abs = _fused_mm(
        [(q_cat, pl.BlockSpec((None, ts, QK_DIM), lambda i, j: (j, blk_s + i, 0)))],
        [(w_abs, pl.BlockSpec((None, QK_DIM, LAT_PAD), lambda i, j: (j, 0, 0)))],
        [],
        [(jax.ShapeDtypeStruct((nh, ts, LAT_PAD), BF16), pl.BlockSpec((None, ts, LAT_PAD), lambda i, j: (j, i, 0)))],
        grid=(1, nh), prologue=_pro_cast, epilogue=_epi_id, x_per_j=True, name="q_absorb")[0]
    o_lat = _attn_sample(q_abs, cache_kv_latent, cache_k_rope, ckv[lp:].reshape(bs, ls, KV_RANK),
                         kr[lp:].reshape(bs, ls, QK_ROPE), ls=ls)
    o_s = _fused_mm(
        [(o_lat, pl.BlockSpec((None, ts, KV_RANK), lambda i, j: (j, i, 0)))],
        [(w_uv_h.astype(BF16), pl.BlockSpec((None, KV_RANK, V_DIM), lambda i, j: (j, 0, 0)))],
        [],
        [(jax.ShapeDtypeStruct((ts, nh * V_DIM), BF16), pl.BlockSpec((ts, V_DIM), lambda i, j: (i, j)))],
        grid=(1, nh), prologue=_pro_cast, epilogue=_epi_id, x_per_j=True, name="v_absorb")[0]
    h = _mm_resident([(o_p, o_s)], [], [], [w_o[0].astype(BF16)], [0], [h], prologue=_pro_cast, epilogue=_epi_res,
                     out_dtype=F32, tm=tm, name="attn_out")
    y_p_out, y_s_out = ffn_and_ple(h, 1, final=(g_final, lp, ts))
    y_p_out = y_p_out.reshape(bp, lp_each, d)
    y_s_out = y_s_out.reshape(bs, ls, d)
    return (y_p_out, y_s_out, conv_p, ssm_p, ckv[:lp].reshape(bp, lp_each, KV_RANK),
            kr[:lp].reshape(bp, lp_each, QK_ROPE), conv_s, ssm_s, ckv[lp:].reshape(bs, ls, KV_RANK),
            kr[lp:].reshape(bs, ls, QK_ROPE))
```

```python
import functools
import math

import numpy as np
import jax
import jax.numpy as jnp
from jax import lax
from jax.experimental import pallas as pl
from jax.experimental.pallas import tpu as pltpu

F32 = jnp.float32
BF16 = jnp.bfloat16

EPS = 1e-6
CHUNK = 64
D_MODEL = 2048
D_INNER = 2 * D_MODEL
SSM_HEAD_DIM = 64
SSM_HEADS = D_INNER // SSM_HEAD_DIM
SSM_STATE = 128
SSM_GROUPS = 8
HEADS_PER_GROUP = SSM_HEADS // SSM_GROUPS
GROUP_WIDTH = D_INNER // SSM_GROUPS
CONV_W = 4
CONV_DIM = D_INNER + 2 * SSM_GROUPS * SSM_STATE
MLA_HEADS = 16
Q_RANK = 512
KV_RANK = 512
QK_NOPE = 128
QK_ROPE = 64
QK_DIM = QK_NOPE + QK_ROPE
V_DIM = 128
ROPE_THETA = 10000.0
ATTN_SCALE = QK_DIM ** -0.5
QK_PRESCALE = ATTN_SCALE * math.log2(math.e)
MOE_GROUPS = 4
EXPERTS_PER_GROUP = 8
N_EXPERTS = MOE_GROUPS * EXPERTS_PER_GROUP
TOP_K = 2
D_EXPERT = 512

LANES = 128
SUBLANES = 8
VMEM_LIMIT = 56 * 1024 * 1024
SSD_Q = 128
MOE_BM = 128
LAT_PAD = 640
SSM_IN_XW = 6400


def _cparams(sem):
    return pltpu.CompilerParams(dimension_semantics=sem, vmem_limit_bytes=VMEM_LIMIT)


def _sigmoid(v):
    return 1.0 / (1.0 + jnp.exp(-v))


def _silu(v):
    return v * _sigmoid(v)


def _softplus(v):
    return jnp.maximum(v, 0.0) + jnp.log1p(jnp.exp(-jnp.abs(v)))


def _rms_rows(x):
    return x * lax.rsqrt(jnp.mean(x * x, axis=-1, keepdims=True) + EPS)


def _mm_body(*refs, nx, nw, ne, no, prologue, epilogue, x_per_j, emit_xn, xn_emit, xn_store, precision):
    x_refs = refs[:nx]
    w_refs = refs[nx:nx + nw]
    e_refs = refs[nx + nw:nx + nw + ne]
    o_refs = refs[nx + nw + ne:nx + nw + ne + no]
    rest = refs[nx + nw + ne + no:]
    j = pl.program_id(1)
    if x_per_j:
        xn = prologue(*[r[...] for r in x_refs])
    else:
        xn_ref = rest[-1]

        @pl.when(j == 0)
        def _():
            v = prologue(*[r[...] for r in x_refs])
            xn_ref[...] = xn_store(v).astype(xn_ref.dtype)
            if emit_xn:
                xn_emit(v, rest[0])

        xn = xn_ref[...]
    accs = [jnp.dot(xn, w[...], preferred_element_type=F32, precision=precision) for w in w_refs]
    outs = epilogue(accs, [e[...] for e in e_refs])
    for o_ref, o in zip(o_refs, outs):
        o_ref[...] = o.astype(o_ref.dtype)


def _fused_mm(x_args, w_args, e_args, out_defs, *, grid, prologue, epilogue, xn_shape=None,
              xn_dtype=BF16, x_per_j=False, xn_out=None, xn_emit=None, xn_store=None, precision=None, name=None):
    arrays = [a for a, _ in x_args + w_args + e_args]
    in_specs = [s for _, s in x_args + w_args + e_args]
    out_shape = [d for d, _ in out_defs]
    out_specs = [s for _, s in out_defs]
    emit_xn = xn_out is not None
    if emit_xn:
        out_shape.append(xn_out[0])
        out_specs.append(xn_out[1])
    scratch = [] if x_per_j else [pltpu.VMEM(xn_shape, xn_dtype)]
    body = functools.partial(
        _mm_body, nx=len(x_args), nw=len(w_args), ne=len(e_args), no=len(out_defs),
        prologue=prologue, epilogue=epilogue, x_per_j=x_per_j, emit_xn=emit_xn,
        xn_emit=xn_emit or _emit_cast, xn_store=xn_store or _pro_cast, precision=precision)
    return pl.pallas_call(
        body, grid=grid, in_specs=in_specs, out_specs=out_specs, out_shape=out_shape,
        scratch_shapes=scratch, compiler_params=_cparams(("parallel", "arbitrary")), name=name,
    )(*arrays)


def _pro_rms(x, g):
    return _rms_rows(x.astype(F32)) * g


def _pro_cast(x):
    return x


def _emit_cast(v, o_ref):
    o_ref[...] = v.astype(o_ref.dtype)


def _emit_token_tiles(v, o_ref):
    w = _pack_bf16_pairs(v)
    m = w.shape[0]
    for s in range(SUBLANES):
        o_ref[pl.ds(s, m, stride=SUBLANES), :] = w[:, s * LANES:(s + 1) * LANES]


def _split3_bf16(v):
    hi = v.astype(BF16)
    r1 = v - hi.astype(F32)
    mid = r1.astype(BF16)
    return hi, mid, (r1 - mid.astype(F32)).astype(BF16)


def _split_hi_lo_hi(v):
    hi = v.astype(BF16)
    lo = (v - hi.astype(F32)).astype(BF16)
    return jnp.concatenate([hi, lo, hi], axis=1)


def _pack_bf16_pairs(v):
    k = v.shape[1] // 2
    lo = lax.bitcast_convert_type(v[:, :k].astype(BF16).astype(F32), jnp.uint32) >> 16
    hi = lax.bitcast_convert_type(v[:, k:].astype(BF16).astype(F32), jnp.uint32) & jnp.uint32(0xFFFF0000)
    return hi | lo


def _epi_silu(accs, es):
    return [_silu(accs[0])]


def _pro_gated(y, zs, g):
    v = y.astype(F32) * zs.astype(F32)
    parts = [_rms_rows(v[:, k * GROUP_WIDTH:(k + 1) * GROUP_WIDTH]) for k in range(SSM_GROUPS)]
    return jnp.concatenate(parts, axis=-1) * g


def _epi_id(accs, es):
    return accs


def _epi_res(accs, es):
    return [es[0] + accs[0]]


def _epi_ple(accs, es):
    return [es[0] + _sigmoid(accs[0]) * accs[1]]


def _epi_rms_out(accs, es):
    return [_rms_rows(accs[0]) * es[0]]


def _epi_latent_kv(accs, es):
    g_kv, cos, sin = es
    return [_rms_rows(accs[0]) * g_kv, accs[1] * cos + accs[2] * sin]


def _epi_route(accs, es):
    lg = accs[0] + es[0]
    lane = lax.broadcasted_iota(jnp.int32, lg.shape, 1).astype(F32)
    neg = -jnp.inf
    big = 1.0e4
    is_grp = lane < MOE_GROUPS
    gl = jnp.where(is_grp, lg, neg)
    mg = jnp.max(gl, axis=-1, keepdims=True)
    g_sel = jnp.min(jnp.where(gl == mg, lane, big), axis=-1, keepdims=True)
    p_sel = 1.0 / jnp.sum(jnp.where(is_grp, jnp.exp(gl - mg), 0.0), axis=-1, keepdims=True)
    lo = MOE_GROUPS + g_sel * EXPERTS_PER_GROUP
    in_grp = jnp.where(lane >= lo, jnp.where(lane < lo + EXPERTS_PER_GROUP, 1.0, 0.0), 0.0) > 0.5
    el = jnp.where(in_grp, lg, neg)
    v1 = jnp.max(el, axis=-1, keepdims=True)
    i1 = jnp.min(jnp.where(el == v1, lane, big), axis=-1, keepdims=True)
    el2 = jnp.where(lane == i1, neg, el)
    v2 = jnp.max(el2, axis=-1, keepdims=True)
    i2 = jnp.min(jnp.where(el2 == v2, jnp.where(lane == i1, big, lane), big), axis=-1, keepdims=True)
    e21 = jnp.exp(v2 - v1)
    g1 = p_sel / (1.0 + e21)
    g2 = p_sel * e21 / (1.0 + e21)
    idx = jnp.where(lane == 0.0, i1 - MOE_GROUPS, jnp.where(lane == 1.0, i2 - MOE_GROUPS, 0.0))
    gates = jnp.where(lane == 0.0, g1, jnp.where(lane == 1.0, g2, 0.0))
    return [idx.astype(jnp.int32), gates]


def _mm_rows(x_list, consts, w, extras, *, prologue, epilogue, out_dtype, tm, tn, name,
             xn_dtype=BF16, precision=None, emit_xn_dtype=None, e_consts=()):
    m = x_list[0].shape[0]
    k, n = w.shape
    grid = (m // tm, n // tn)
    x_args = [(x, pl.BlockSpec((tm, x.shape[1]), lambda i, j: (i, 0))) for x in x_list]
    x_args += [(c, pl.BlockSpec((1, c.shape[1]), lambda i, j: (0, 0))) for c in consts]
    w_args = [(w, pl.BlockSpec((k, tn), lambda i, j: (0, j)))]
    e_args = [(c, pl.BlockSpec((1, tn), lambda i, j: (0, j))) for c in e_consts]
    e_args += [(e, pl.BlockSpec((tm, tn), lambda i, j: (i, j))) for e in extras]
    out_defs = [(jax.ShapeDtypeStruct((m, n), out_dtype), pl.BlockSpec((tm, tn), lambda i, j: (i, j)))]
    xn_out = None
    if emit_xn_dtype is not None:
        xn_out = (jax.ShapeDtypeStruct((m, k), emit_xn_dtype), pl.BlockSpec((tm, k), lambda i, j: (i, 0)))
    return _fused_mm(x_args, w_args, e_args, out_defs, grid=grid, prologue=prologue, epilogue=epilogue,
                     xn_shape=(tm, k), xn_dtype=xn_dtype, xn_out=xn_out, precision=precision, name=name)


def _mm_res_body(*refs, x_split, nc, nl, nw, ne, npc, lhs_of_w, res_from_x, prologue, epilogue, post, out_split,
                 cw):
    i = pl.program_id(0)
    pos, xs = 0, []
    for first_tiles in x_split:
        if first_tiles is None:
            xs.append(refs[pos][...])
            pos += 1
        else:
            xs.append(jnp.where(i < first_tiles, refs[pos][...], refs[pos + 1][...]))
            pos += 2
    x0_ref = refs[0]
    c_refs = refs[pos:pos + nc]
    l_refs = refs[pos + nc:pos + nc + nl]
    w_refs = refs[pos + nc + nl:pos + nc + nl + nw]
    e_refs = refs[pos + nc + nl + nw:pos + nc + nl + nw + ne]
    p_refs = refs[pos + nc + nl + nw + ne:pos + nc + nl + nw + ne + npc]
    o_refs = refs[pos + nc + nl + nw + ne + npc:-1] if post is None else refs[pos + nc + nl + nw + ne + npc:-2]
    xn_ref = refs[-1]
    rows_ref = o_refs[0] if post is None else refs[-2]
    xn_ref[...] = prologue(*xs, *[r[...] for r in c_refs]).astype(xn_ref.dtype)
    lhs = [xn_ref] + list(l_refs)
    for c in range(rows_ref.shape[1] // cw):
        sl = slice(c * cw, (c + 1) * cw)
        accs = [jnp.dot(lhs[li][...], w[:, sl], preferred_element_type=F32) for w, li in zip(w_refs, lhs_of_w)]
        es = ([x0_ref[:, sl]] if res_from_x else []) + [e[:, sl] for e in e_refs]
        rows_ref[:, sl] = epilogue(accs, es)[0].astype(rows_ref.dtype)
    if post is not None:
        val = post(rows_ref[...], *[p[...] for p in p_refs])
        if out_split is None:
            o_refs[0][...] = val.astype(o_refs[0].dtype)
        else:
            @pl.when(i < out_split)
            def _():
                o_refs[0][...] = val.astype(o_refs[0].dtype)

            @pl.when(i >= out_split)
            def _():
                o_refs[1][...] = val.astype(o_refs[1].dtype)


def _mm_resident(x_list, consts, lhs_list, ws, lhs_of_w, extras, *, prologue, epilogue, out_dtype, tm, name,
                 res_from_x=False, cw=512, post=None, post_consts=(), out_rows=None):
    rows = lambda x: x[0].shape[0] + x[1].shape[0] if isinstance(x, tuple) else x.shape[0]
    m = rows(x_list[0])
    k = ws[lhs_of_w.index(0)].shape[0]
    n = ws[0].shape[1]
    row = lambda a: pl.BlockSpec((tm, a.shape[1]), lambda i: (i, 0))
    whole = lambda a: pl.BlockSpec(a.shape, lambda i: (0, 0), pipeline_mode=pl.Buffered(1))
    arrays, in_specs, x_split = [], [], []
    for x in x_list:
        if isinstance(x, tuple):
            a, b = x
            na = a.shape[0] // tm
            assert a.shape[0] % tm == 0 and b.shape[0] % tm == 0
            arrays += [a, b]
            in_specs += [pl.BlockSpec((tm, a.shape[1]), lambda i, na=na: (jnp.minimum(i, na - 1), 0)),
                         pl.BlockSpec((tm, b.shape[1]), lambda i, na=na: (jnp.maximum(i - na, 0), 0))]
            x_split.append(na)
        else:
            arrays.append(x)
            in_specs.append(row(x))
            x_split.append(None)
    arrays += list(consts) + list(lhs_list) + list(ws) + list(extras) + list(post_consts)
    in_specs += ([whole(a) for a in consts] + [row(a) for a in lhs_list] + [whole(a) for a in ws]
                 + [row(a) for a in extras] + [whole(a) for a in post_consts])
    out_split = None
    out_specs = pl.BlockSpec((tm, n), lambda i: (i, 0))
    out_shape = jax.ShapeDtypeStruct((m, n), out_dtype)
    if out_rows is not None:
        ra, rb = out_rows
        assert post is not None and ra % tm == 0 and rb % tm == 0 and ra + rb == m
        out_split = ra // tm
        out_specs = [pl.BlockSpec((tm, n), lambda i: (jnp.minimum(i, out_split - 1), 0)),
                     pl.BlockSpec((tm, n), lambda i: (jnp.maximum(i - out_split, 0), 0))]
        out_shape = [jax.ShapeDtypeStruct((ra, n), out_dtype), jax.ShapeDtypeStruct((rb, n), out_dtype)]
    scratch = ([pltpu.VMEM((tm, n), F32)] if post is not None else []) + [pltpu.VMEM((tm, k), BF16)]
    body = functools.partial(
        _mm_res_body, x_split=tuple(x_split), nc=len(consts), nl=len(lhs_list), nw=len(ws), ne=len(extras),
        npc=len(post_consts), lhs_of_w=tuple(lhs_of_w), res_from_x=res_from_x, prologue=prologue,
        epilogue=epilogue, post=post, out_split=out_split, cw=cw)
    return pl.pallas_call(
        body, grid=(m // tm,), in_specs=in_specs, out_specs=out_specs, out_shape=out_shape,
        scratch_shapes=scratch,
        compiler_params=_cparams(("arbitrary",) if out_rows is not None else ("parallel",)), name=name,
    )(*arrays)


def _kv_expand_body(ckv_ref, kr_ref, wk_ref, wv_ref, k_ref, v_ref, *, nh):
    c = ckv_ref[...].astype(BF16)
    kn = jnp.dot(c, wk_ref[...], preferred_element_type=F32)
    vv = jnp.dot(c, wv_ref[...], preferred_element_type=F32)
    kr = kr_ref[...].astype(k_ref.dtype)
    ones = jnp.ones((c.shape[0], V_DIM), v_ref.dtype)
    for hh in range(nh):
        k_ref[hh, :, :QK_NOPE] = kn[:, hh * QK_NOPE:(hh + 1) * QK_NOPE].astype(k_ref.dtype)
        k_ref[hh, :, QK_NOPE:] = kr
        v_ref[hh, :, :V_DIM] = vv[:, hh * V_DIM:(hh + 1) * V_DIM].astype(v_ref.dtype)
        v_ref[hh, :, V_DIM:] = ones


def _kv_expand(ckv, kr, w_uk_flat, w_uv_flat, rows, *, tm):
    nh = w_uk_flat.shape[1] // QK_NOPE
    whole = lambda a: pl.BlockSpec(a.shape, lambda i: (0, 0), pipeline_mode=pl.Buffered(1))
    return pl.pallas_call(
        functools.partial(_kv_expand_body, nh=nh), grid=(rows // tm,),
        in_specs=[pl.BlockSpec((tm, KV_RANK), lambda i: (i, 0)), pl.BlockSpec((tm, QK_ROPE), lambda i: (i, 0)),
                  whole(w_uk_flat), whole(w_uv_flat)],
        out_specs=[pl.BlockSpec((nh, tm, QK_DIM), lambda i: (0, i, 0)),
                   pl.BlockSpec((nh, tm, 2 * V_DIM), lambda i: (0, i, 0))],
        out_shape=[jax.ShapeDtypeStruct((nh, rows, QK_DIM), BF16),
                   jax.ShapeDtypeStruct((nh, rows, 2 * V_DIM), BF16)],
        compiler_params=_cparams(("parallel",)), name="kv_heads",
    )(ckv, kr, w_uk_flat, w_uv_flat)


def _q_expand_body(ql_ref, wn_ref, wr_ref, wrr_ref, cos_ref, sin_ref, q_ref, *, nh):
    x = ql_ref[...]
    qn = jnp.dot(x, wn_ref[...], preferred_element_type=F32) * QK_PRESCALE
    reps = nh * QK_ROPE // cos_ref.shape[1]
    cos = jnp.concatenate([cos_ref[...]] * reps, axis=1)
    sin = jnp.concatenate([sin_ref[...]] * reps, axis=1)
    qr = (jnp.dot(x, wr_ref[...], preferred_element_type=F32) * cos
          + jnp.dot(x, wrr_ref[...], preferred_element_type=F32) * sin)
    for hh in range(nh):
        q_ref[hh, :, :QK_NOPE] = qn[:, hh * QK_NOPE:(hh + 1) * QK_NOPE].astype(q_ref.dtype)
        q_ref[hh, :, QK_NOPE:] = qr[:, hh * QK_ROPE:(hh + 1) * QK_ROPE].astype(q_ref.dtype)


def _q_expand(ql, w_nope, w_rope, w_rope_rot, cos2, sin2, *, tm):
    m = ql.shape[0]
    nh = w_nope.shape[1] // QK_NOPE
    whole = lambda a: pl.BlockSpec(a.shape, lambda i: (0, 0), pipeline_mode=pl.Buffered(1))
    row = lambda a: pl.BlockSpec((tm, a.shape[1]), lambda i: (i, 0))
    return pl.pallas_call(
        functools.partial(_q_expand_body, nh=nh), grid=(m // tm,),
        in_specs=[row(ql), whole(w_nope), whole(w_rope), whole(w_rope_rot), row(cos2), row(sin2)],
        out_specs=pl.BlockSpec((nh, tm, QK_DIM), lambda i: (0, i, 0)),
        out_shape=jax.ShapeDtypeStruct((nh, m, QK_DIM), BF16),
        compiler_params=_cparams(("parallel",)), name="q_heads",
    )(ql, w_nope, w_rope, w_rope_rot, cos2, sin2)


def _conv_silu_chunk(ext, in_ref, st_ref, w_ref, b_ref, first, q):
    @pl.when(first)
    def _():
        ext[0:SUBLANES, :] = st_ref[...]

    ext[SUBLANES:SUBLANES + q, :] = in_ref[...]
    acc = b_ref[...]
    for k in range(CONV_W):
        off = SUBLANES - (CONV_W - 1) + k
        acc = acc + ext[off:off + q, :] * w_ref[k:k + 1, :]
    ext[0:SUBLANES, :] = ext[q:q + SUBLANES, :]
    return _silu(acc)


def _ssd_body(x_ref, b_ref, c_ref, stx_ref, stb_ref, stc_ref, wx_ref, wb_ref, wc_ref, bx_ref, bb_ref, bc_ref,
              dt_ref, dtT_ref, bias_ref, biasT_ref, a_ref, aT_ref, dsk_ref, s0_ref,
              y_ref, so_ref, s_scr, ext_x, ext_b, ext_c, *, q, gps):
    c = pl.program_id(2)
    first = c == 0

    @pl.when(first)
    def _():
        s_scr[...] = s0_ref[...]

    xs_all = _conv_silu_chunk(ext_x, x_ref, stx_ref, wx_ref, bx_ref, first, q)
    bm_all = _conv_silu_chunk(ext_b, b_ref, stb_ref, wb_ref, bb_ref, first, q).astype(BF16)
    cm_all = _conv_silu_chunk(ext_c, c_ref, stc_ref, wc_ref, bc_ref, first, q).astype(BF16)

    row = lax.broadcasted_iota(jnp.int32, (q, q), 0)
    col = lax.broadcasted_iota(jnp.int32, (q, q), 1)
    causal = row >= col
    tri = jnp.where(causal, 1.0, 0.0).astype(BF16)
    tri_t = jnp.where(row <= col, 1.0, 0.0).astype(BF16)
    left = lax.broadcasted_iota(jnp.int32, (q, LANES), 1) < SSM_HEAD_DIM
    gw = GROUP_WIDTH
    for gi in range(gps):
        dt = _softplus(dt_ref[gi] + bias_ref[gi])
        dta = dt * a_ref[gi]
        dta_t = _softplus(dtT_ref[gi] + biasT_ref[gi]) * aT_ref[gi]
        acum = sum(jnp.dot(tri, piece, preferred_element_type=F32) for piece in _split3_bf16(dta))
        acum_t = sum(jnp.dot(piece, tri_t, preferred_element_type=F32) for piece in _split3_bf16(dta_t))
        bm = bm_all[:, gi * SSM_STATE:(gi + 1) * SSM_STATE]
        cm = cm_all[:, gi * SSM_STATE:(gi + 1) * SSM_STATE]
        cb = lax.dot_general(cm, bm, (((1,), (1,)), ((), ())), preferred_element_type=F32)
        s_prev = s_scr[gi]
        y_off = jnp.dot(cm, s_prev.astype(BF16), preferred_element_type=F32)
        xw_parts, dec_parts = [], []
        for j in range(HEADS_PER_GROUP // 2):
            h0, h1 = 2 * j, 2 * j + 1
            sl = slice(j * LANES, (j + 1) * LANES)
            sg = slice(gi * gw + j * LANES, gi * gw + (j + 1) * LANES)
            col0, col1 = acum[:, h0:h0 + 1], acum[:, h1:h1 + 1]
            pa = jnp.where(left, col0, col1)
            dtp = jnp.where(left, dt[:, h0:h0 + 1], dt[:, h1:h1 + 1])
            x = xs_all[:, sg]
            xdt = x * dtp
            m0 = (jnp.exp2(jnp.where(causal, col0 - acum_t[h0:h0 + 1, :], -jnp.inf)) * cb).astype(BF16)
            m1 = (jnp.exp2(jnp.where(causal, col1 - acum_t[h1:h1 + 1, :], -jnp.inf)) * cb).astype(BF16)
            x_l = jnp.where(left, xdt, 0.0).astype(BF16)
            x_r = jnp.where(left, 0.0, xdt).astype(BF16)
            y_diag = (jnp.dot(m0, x_l, preferred_element_type=F32)
                      + jnp.dot(m1, x_r, preferred_element_type=F32))
            y = y_diag + y_off[:, sl] * jnp.exp2(pa) + x * dsk_ref[:, sg]
            y_ref[:, sg] = y.astype(y_ref.dtype)
            last = pa[q - 1:q, :]
            xw_parts.append((xdt * jnp.exp2(last - pa)).astype(BF16))
            dec_parts.append(jnp.exp2(last))
        xw = jnp.concatenate(xw_parts, axis=1)
        dec = jnp.concatenate(dec_parts, axis=1)
        s_new = s_prev * dec + lax.dot_general(bm, xw, (((0,), (0,)), ((), ())), preferred_element_type=F32)
        s_scr[gi] = s_new
        so_ref[gi] = s_new


def _ssd_scan(xbc3, batch0, nb, L, st, w_conv, b_conv, dtg, dtg_t, bias, bias_t, a, a_t, dsk, s0, *, q, gps=8):
    g_, r_, n_ = SSM_GROUPS, HEADS_PER_GROUP, SSM_STATE
    gw = GROUP_WIDTH
    xw_, bw_ = gps * gw, gps * n_
    b_off = D_INNER // bw_
    c_off = b_off + g_ // gps
    grid = (nb, g_ // gps, L // q)
    col_x = lambda b, g, c: (0, g)
    col_b = lambda b, g, c: (0, b_off + g)
    col_c = lambda b, g, c: (0, c_off + g)
    return pl.pallas_call(
        functools.partial(_ssd_body, q=q, gps=gps), grid=grid,
        in_specs=[
            pl.BlockSpec((None, q, xw_), lambda b, g, c: (batch0 + b, c, g)),
            pl.BlockSpec((None, q, bw_), lambda b, g, c: (batch0 + b, c, b_off + g)),
            pl.BlockSpec((None, q, bw_), lambda b, g, c: (batch0 + b, c, c_off + g)),
            pl.BlockSpec((None, SUBLANES, xw_), lambda b, g, c: (b, 0, g)),
            pl.BlockSpec((None, SUBLANES, bw_), lambda b, g, c: (b, 0, b_off + g)),
            pl.BlockSpec((None, SUBLANES, bw_), lambda b, g, c: (b, 0, c_off + g)),
            pl.BlockSpec((CONV_W, xw_), col_x),
            pl.BlockSpec((CONV_W, bw_), col_b),
            pl.BlockSpec((CONV_W, bw_), col_c),
            pl.BlockSpec((1, xw_), col_x),
            pl.BlockSpec((1, bw_), col_b),
            pl.BlockSpec((1, bw_), col_c),
            pl.BlockSpec((None, gps, q, r_), lambda b, g, c: (b, g, c, 0)),
            pl.BlockSpec((None, gps, r_, q), lambda b, g, c: (b, g, 0, c)),
            pl.BlockSpec((gps, 1, r_), lambda b, g, c: (g, 0, 0)),
            pl.BlockSpec((gps, r_, 1), lambda b, g, c: (g, 0, 0)),
            pl.BlockSpec((gps, 1, r_), lambda b, g, c: (g, 0, 0)),
            pl.BlockSpec((gps, r_, 1), lambda b, g, c: (g, 0, 0)),
            pl.BlockSpec((1, xw_), col_x),
            pl.BlockSpec((None, gps, n_, gw), lambda b, g, c: (b, g, 0, 0)),
        ],
        out_specs=[
            pl.BlockSpec((None, q, xw_), lambda b, g, c: (b, c, g)),
            pl.BlockSpec((None, gps, n_, gw), lambda b, g, c: (b, g, 0, 0)),
        ],
        out_shape=[jax.ShapeDtypeStruct((nb, L, D_INNER), BF16),
                   jax.ShapeDtypeStruct((nb, g_, n_, gw), F32)],
        scratch_shapes=[pltpu.VMEM((gps, n_, gw), F32), pltpu.VMEM((q + SUBLANES, xw_), F32),
                        pltpu.VMEM((q + SUBLANES, bw_), F32), pltpu.VMEM((q + SUBLANES, bw_), F32)],
        compiler_params=_cparams(("parallel", "parallel", "arbitrary")), name="ssd_scan",
    )(xbc3, xbc3, xbc3, st, st, st, w_conv, w_conv, w_conv, b_conv, b_conv, b_conv,
      dtg, dtg_t, bias, bias_t, a, a_t, dsk, s0)


def _gather_rows(tok_ref, base, src_hbm, dst, sem, n, rp=1):
    def body(r, carry):
        t = pl.multiple_of(tok_ref[base + r] * rp, rp)
        pltpu.make_async_copy(src_hbm.at[pl.ds(t, rp)], dst.at[pl.ds(pl.multiple_of(r * rp, rp), rp)], sem).start()
        return carry

    lax.fori_loop(0, n, body, 0)


def _gather_rows_unrolled(tok_ref, base, src_hbm, dst, sem, n):
    for r in range(n):
        t = tok_ref[base + r]
        pltpu.make_async_copy(src_hbm.at[pl.ds(t, 1)], dst.at[pl.ds(r, 1)], sem).start(priority=r % 2)


def _moe_body(blk_e_ref, nxt_e_ref, tok_ref, nused_ref, xt_hbm, wg_hbm, wu_hbm, wd_hbm, yb_ref,
              xbuf, xsem, wg32, wu32, wd32, wsem, wg_bf, wu_bf, wd_bf, xb, *, bm, layer):
    b = pl.program_id(0)
    slot = b % 2
    nused = nused_ref[0]
    e_cur = blk_e_ref[b]

    def weight_copies(e):
        return (pltpu.make_async_copy(wg_hbm.at[layer, e], wg32, wsem.at[0]),
                pltpu.make_async_copy(wu_hbm.at[layer, e], wu32, wsem.at[1]),
                pltpu.make_async_copy(wd_hbm.at[layer, e], wd32, wsem.at[2]))

    def gather_unrolled(base, slot_):
        for r in range(bm):
            t = pl.multiple_of(tok_ref[base + r] * SUBLANES, SUBLANES)
            pltpu.make_async_copy(xt_hbm.at[pl.ds(t, SUBLANES)], xbuf.at[slot_, pl.ds(r * SUBLANES, SUBLANES)],
                                  xsem.at[slot_]).start()

    @pl.when(jnp.logical_and(b == 0, nused > 0))
    def _():
        for cp in weight_copies(e_cur):
            cp.start(priority=1)
        _gather_rows(tok_ref, 0, xt_hbm, xbuf.at[0], xsem.at[0], bm, rp=SUBLANES)

    prev_e = blk_e_ref[jnp.maximum(b - 1, 0)]
    changed = jnp.logical_or(b == 0, e_cur != prev_e)

    @pl.when(jnp.logical_and(changed, b < nused))
    def _():
        for cp in weight_copies(e_cur):
            cp.wait()
        wg_bf[...] = wg32[...].astype(BF16)
        wu_bf[...] = wu32[...].astype(BF16)
        wd_bf[...] = wd32[...].astype(BF16)
        nxt = nxt_e_ref[b]

        @pl.when(nxt >= 0)
        def _():
            for cp in weight_copies(nxt):
                cp.start(priority=1)

    def compute(prefetch):
        pltpu.make_async_copy(xbuf.at[slot], xbuf.at[slot], xsem.at[slot]).wait()
        half = xb.shape[1] // 2
        for s in range(SUBLANES):
            w = xbuf[slot, pl.ds(s, bm, stride=SUBLANES), :]
            cols = slice(s * LANES, (s + 1) * LANES)
            xb[:, cols] = lax.bitcast_convert_type(w << 16, F32).astype(BF16)
            xb[:, half + s * LANES:half + (s + 1) * LANES] = lax.bitcast_convert_type(
                w & jnp.uint32(0xFFFF0000), F32).astype(BF16)
        if prefetch:
            gather_unrolled((b + 1) * bm, 1 - slot)
        x = xb[...]
        g = jnp.dot(x, wg_bf[...], preferred_element_type=F32)
        u = jnp.dot(x, wu_bf[...], preferred_element_type=F32)
        hid = (_silu(g) * u).astype(BF16)
        yb_ref[...] = jnp.dot(hid, wd_bf[...], preferred_element_type=F32)

    @pl.when(b + 1 < nused)
    def _():
        compute(True)

    @pl.when(b + 1 == nused)
    def _():
        compute(False)

    @pl.when(b >= nused)
    def _():
        yb_ref[...] = jnp.zeros(yb_ref.shape, yb_ref.dtype)


def _moe_experts(xt, blk_e, nxt_e, row_token, nused, w_gate, w_up, w_down, *, bm, layer):
    nblk = blk_e.shape[0]
    d = w_gate.shape[2]
    de = w_gate.shape[3]
    assert d == 2 * SUBLANES * LANES and xt.shape[1] == LANES and xt.dtype == jnp.uint32
    anyspec = pl.BlockSpec(memory_space=pl.ANY)
    gs = pltpu.PrefetchScalarGridSpec(
        num_scalar_prefetch=4, grid=(nblk,),
        in_specs=[anyspec, anyspec, anyspec, anyspec],
        out_specs=pl.BlockSpec((bm, d), lambda b, be, nx, tok, nu: (b, 0)),
        scratch_shapes=[
            pltpu.VMEM((2, bm * SUBLANES, LANES), jnp.uint32),
            pltpu.SemaphoreType.DMA((2,)),
            pltpu.VMEM((d, de), F32),
            pltpu.VMEM((d, de), F32),
            pltpu.VMEM((de, d), F32),
            pltpu.SemaphoreType.DMA((3,)),
            pltpu.VMEM((d, de), BF16),
            pltpu.VMEM((d, de), BF16),
            pltpu.VMEM((de, d), BF16),
            pltpu.VMEM((bm, d), BF16),
        ],
    )
    return pl.pallas_call(
        functools.partial(_moe_body, bm=bm, layer=layer), grid_spec=gs,
        out_shape=jax.ShapeDtypeStruct((nblk * bm, d), F32),
        compiler_params=_cparams(("arbitrary",)), name="moe_experts",
    )(blk_e, nxt_e, row_token, nused, xt, w_gate, w_up, w_down)


def _combine_body(dest_ref, h_ref, g_ref, yb_hbm, o_ref, buf, sem, *, tc, ntile):
    i = pl.program_id(0)
    slot = i % 2
    n = TOP_K * tc

    @pl.when(i == 0)
    def _():
        _gather_rows(dest_ref, 0, yb_hbm, buf.at[0], sem.at[0], n)

    @pl.when(i + 1 < ntile)
    def _():
        _gather_rows_unrolled(dest_ref, (i + 1) * n, yb_hbm, buf.at[1 - slot], sem.at[1 - slot], n)

    pltpu.make_async_copy(buf.at[slot], buf.at[slot], sem.at[slot]).wait()
    g = g_ref[...]
    o_ref[...] = (h_ref[...] + g[:, 0:1] * buf[slot, 0:tc, :] + g[:, 1:2] * buf[slot, tc:2 * tc, :])


def _moe_combine(h, yb, dest_tiles, gates, *, tc):
    t, d = h.shape
    ntile = t // tc
    gs = pltpu.PrefetchScalarGridSpec(
        num_scalar_prefetch=1, grid=(ntile,),
        in_specs=[
            pl.BlockSpec((tc, d), lambda i, dst: (i, 0)),
            pl.BlockSpec((tc, LANES), lambda i, dst: (i, 0)),
            pl.BlockSpec(memory_space=pl.ANY),
        ],
        out_specs=pl.BlockSpec((tc, d), lambda i, dst: (i, 0)),
        scratch_shapes=[pltpu.VMEM((2, TOP_K * tc, d), F32), pltpu.SemaphoreType.DMA((2,))],
    )
    return pl.pallas_call(
        functools.partial(_combine_body, tc=tc, ntile=ntile), grid_spec=gs,
        out_shape=jax.ShapeDtypeStruct((t, d), F32),
        compiler_params=_cparams(("arbitrary",)), name="moe_combine",
    )(dest_tiles, h, gates, yb)


def _moe_plan(idx, bm, tc):
    t = idx.shape[0]
    a = t * TOP_K
    e_ = N_EXPERTS
    flat_e = idx.reshape(a)
    onehot = (flat_e[:, None] == jnp.arange(e_, dtype=jnp.int32)[None, :]).astype(jnp.int32)
    csum = jnp.cumsum(onehot, axis=0)
    rank = jnp.sum(onehot * csum, axis=1) - 1
    counts = csum[-1]
    padded = (counts + bm - 1) // bm * bm
    pad_end = jnp.cumsum(padded)
    pad_start = pad_end - padded
    dest = jnp.sum(onehot * pad_start[None, :], axis=1) + rank
    nblk = -(-a // bm) + e_
    rows = nblk * bm
    row_token = jnp.zeros((rows,), jnp.int32).at[dest].set(jnp.arange(a, dtype=jnp.int32) // TOP_K)
    blk_row0 = jnp.arange(nblk, dtype=jnp.int32) * bm
    blk_e = jnp.minimum(jnp.sum((pad_end[None, :] <= blk_row0[:, None]).astype(jnp.int32), axis=1), e_ - 1)
    ids = jnp.arange(e_, dtype=jnp.int32)
    later_used = jnp.logical_and(ids[None, :] > ids[:, None], (counts > 0)[None, :])
    nxt_of_e = jnp.min(jnp.where(later_used, ids[None, :], e_), axis=1)
    nxt_of_e = jnp.where(nxt_of_e == e_, -1, nxt_of_e)
    nxt_e = jnp.sum(jnp.where(blk_e[:, None] == ids[None, :], nxt_of_e[None, :], 0), axis=1).astype(jnp.int32)
    nused = (pad_end[-1] // bm).astype(jnp.int32).reshape(1)
    dest_tiles = dest.reshape(t // tc, tc, TOP_K).transpose(0, 2, 1).reshape(a).astype(jnp.int32)
    return blk_e.astype(jnp.int32), nxt_e, row_token, nused, dest_tiles


def _flash_body(qi_ref, kj_ref, q_ref, k_ref, v_ref, o_ref, m_scr, acc_scr, *, hb):
    p_id = pl.program_id(1)
    qi = qi_ref[p_id]
    kj = kj_ref[p_id]
    tk = k_ref.shape[1]

    @pl.when(kj == 0)
    def _():
        m_scr[...] = jnp.full(m_scr.shape, -jnp.inf, F32)
        acc_scr[...] = jnp.zeros(acc_scr.shape, F32)

    def step(diag):
        for hh in range(hb):
            s = lax.dot_general(q_ref[hh], k_ref[hh], (((1,), (1,)), ((), ())), preferred_element_type=F32)
            if diag:
                r = lax.broadcasted_iota(jnp.int32, s.shape, 0)
                c = lax.broadcasted_iota(jnp.int32, s.shape, 1)
                s = jnp.where((c // CHUNK) <= (r // CHUNK), s, -jnp.inf)
            m_prev = m_scr[hh]
            m_next = jnp.maximum(m_prev, jnp.max(s, axis=1, keepdims=True))
            p = jnp.exp2(s - jnp.concatenate([m_next] * (tk // LANES), axis=1))
            alpha = jnp.exp2(m_prev - m_next)
            acc_scr[hh] = (jnp.concatenate([alpha, alpha], axis=1) * acc_scr[hh]
                           + jnp.dot(p.astype(BF16), v_ref[hh], preferred_element_type=F32))
            m_scr[hh] = m_next

    @pl.when(kj < qi)
    def _():
        step(False)

    @pl.when(kj == qi)
    def _():
        step(True)
        for hh in range(hb):
            a = acc_scr[hh]
            o_ref[:, hh * V_DIM:(hh + 1) * V_DIM] = (a[:, :V_DIM] / a[:, V_DIM:]).astype(o_ref.dtype)


def _flash_prompt(q, k, v, lp, *, tq, hb):
    assert V_DIM == LANES
    nh = q.shape[0]
    nq = lp // tq
    pairs = [(a, b) for a in range(nq) for b in range(a + 1)]
    qi_tab = jnp.asarray(np.array([p[0] for p in pairs], np.int32))
    kj_tab = jnp.asarray(np.array([p[1] for p in pairs], np.int32))
    gs = pltpu.PrefetchScalarGridSpec(
        num_scalar_prefetch=2, grid=(nh // hb, len(pairs)),
        in_specs=[
            pl.BlockSpec((hb, tq, QK_DIM), lambda h, p, qi, kj: (h, qi[p], 0)),
            pl.BlockSpec((hb, tq, QK_DIM), lambda h, p, qi, kj: (h, kj[p], 0)),
            pl.BlockSpec((hb, tq, 2 * V_DIM), lambda h, p, qi, kj: (h, kj[p], 0)),
        ],
        out_specs=pl.BlockSpec((tq, hb * V_DIM), lambda h, p, qi, kj: (qi[p], h)),
        scratch_shapes=[pltpu.VMEM((hb, tq, LANES), F32), pltpu.VMEM((hb, tq, 2 * V_DIM), F32)],
    )
    return pl.pallas_call(
        functools.partial(_flash_body, hb=hb), grid_spec=gs,
        out_shape=jax.ShapeDtypeStruct((lp, nh * V_DIM), BF16),
        compiler_params=_cparams(("parallel", "arbitrary")), name="flash_prompt",
    )(qi_tab, kj_tab, q, k, v)


def _attn_sample_body(q_ref, ckv_ref, kr_ref, ckv_new_ref, kr_new_ref, o_ref, *, ls, past):
    nh = q_ref.shape[0]
    q = q_ref[...].reshape(nh * ls, q_ref.shape[2])
    q_lat, q_rot = q[:, :KV_RANK], q[:, KV_RANK:KV_RANK + QK_ROPE]
    nt = (((1,), (1,)), ((), ()))

    def scores(lat, rot, pos0):
        s = (lax.dot_general(q_lat, lat, nt, preferred_element_type=F32)
             + lax.dot_general(q_rot, rot, nt, preferred_element_type=F32))
        r = lax.broadcasted_iota(jnp.int32, s.shape, 0)
        c = lax.broadcasted_iota(jnp.int32, s.shape, 1)
        q_pos = past + r % ls
        return jnp.where(((pos0 + c) // CHUNK) <= (q_pos // CHUNK), s, -jnp.inf)

    lat_p = ckv_ref[...].astype(BF16)
    lat_n = ckv_new_ref[...].astype(BF16)
    s_p = scores(lat_p, kr_ref[...].astype(BF16), 0)
    s_n = scores(lat_n, kr_new_ref[...].astype(BF16), past)
    m = jnp.maximum(jnp.max(s_p, axis=-1, keepdims=True), jnp.max(s_n, axis=-1, keepdims=True))
    p_p = jnp.exp2(s_p - m)
    p_n = jnp.exp2(s_n - m)
    l = jnp.sum(p_p, axis=-1, keepdims=True) + jnp.sum(p_n, axis=-1, keepdims=True)
    o = (jnp.dot(p_p.astype(BF16), lat_p, preferred_element_type=F32)
         + jnp.dot(p_n.astype(BF16), lat_n, preferred_element_type=F32)) / l
    o_ref[...] = o.reshape(nh, ls, KV_RANK).astype(o_ref.dtype)


def _attn_sample(q_abs, cache_ckv, cache_kr, ckv_new, kr_new, *, ls):
    nh, ts, dk = q_abs.shape
    nb, past, _ = cache_ckv.shape
    return pl.pallas_call(
        functools.partial(_attn_sample_body, ls=ls, past=past), grid=(nb,),
        in_specs=[
            pl.BlockSpec((nh, ls, dk), lambda b: (0, b, 0)),
            pl.BlockSpec((None, past, KV_RANK), lambda b: (b, 0, 0)),
            pl.BlockSpec((None, past, QK_ROPE), lambda b: (b, 0, 0)),
            pl.BlockSpec((None, ls, KV_RANK), lambda b: (b, 0, 0)),
            pl.BlockSpec((None, ls, QK_ROPE), lambda b: (b, 0, 0)),
        ],
        out_specs=pl.BlockSpec((nh, ls, KV_RANK), lambda b: (0, b, 0)),
        out_shape=jax.ShapeDtypeStruct((nh, ts, KV_RANK), BF16),
        compiler_params=_cparams(("parallel",)), name="attn_sample",
    )(q_abs, cache_ckv, cache_kr, ckv_new, kr_new)


def _rope_tables(pos):
    half = QK_ROPE // 2
    inv = ROPE_THETA ** (-np.arange(half, dtype=np.float64) / half)
    ang = np.asarray(pos, np.float64)[:, None] * inv[None, :]
    cos = np.concatenate([np.cos(ang), np.cos(ang)], axis=1)
    sin = np.concatenate([-np.sin(ang), np.sin(ang)], axis=1)
    return cos.astype(np.float32), sin.astype(np.float32)


def _swap_rope_halves(w):
    half = QK_ROPE // 2
    return jnp.concatenate([w[..., half:], w[..., :half]], axis=-1)


def _moe_layer(h, g_ffn, w_rg, b_rg, w_re, b_re, w_gate, w_up, w_down, *, tm, layer):
    t, d = h.shape
    npad = LANES - MOE_GROUPS - N_EXPERTS
    w_r = jnp.concatenate([w_rg, w_re, jnp.zeros((d, npad), F32)], axis=1)
    b_r = jnp.concatenate([b_rg, b_re, jnp.zeros((npad,), F32)]).reshape(1, LANES)
    grid = (t // tm, 1)
    x_args = [(h, pl.BlockSpec((tm, d), lambda i, j: (i, 0))),
              (g_ffn.reshape(1, d), pl.BlockSpec((1, d), lambda i, j: (0, 0)))]
    w_hi = w_r.astype(BF16)
    w_lo = (w_r - w_hi.astype(F32)).astype(BF16)
    w_args = [(jnp.concatenate([w_hi, w_hi, w_lo], axis=0), pl.BlockSpec((3 * d, LANES), lambda i, j: (0, 0)))]
    e_args = [(b_r, pl.BlockSpec((1, LANES), lambda i, j: (0, 0)))]
    tile = pl.BlockSpec((tm, LANES), lambda i, j: (i, 0))
    out_defs = [(jax.ShapeDtypeStruct((t, LANES), jnp.int32), tile),
                (jax.ShapeDtypeStruct((t, LANES), F32), tile)]
    assert d == 2 * SUBLANES * LANES
    xn_out = (jax.ShapeDtypeStruct((t * SUBLANES, LANES), jnp.uint32),
              pl.BlockSpec((tm * SUBLANES, LANES), lambda i, j: (i, 0)))
    idx_t, gates_t, xt = _fused_mm(x_args, w_args, e_args, out_defs, grid=grid, prologue=_pro_rms,
                                   epilogue=_epi_route, xn_shape=(tm, 3 * d), xn_dtype=BF16, xn_out=xn_out,
                                   xn_emit=_emit_token_tiles, xn_store=_split_hi_lo_hi, name="router")
    tc = min(tm, 128)
    blk_e, nxt_e, row_token, nused, dest_tiles = _moe_plan(idx_t[:, :TOP_K], MOE_BM, tc)
    yb = _moe_experts(xt, blk_e, nxt_e, row_token, nused, w_gate, w_up, w_down, bm=MOE_BM, layer=layer)
    return _moe_combine(h, yb, dest_tiles, gates_t, tc=tc)


def _post_rms(rows, g):
    return _rms_rows(rows) * g


def _ple_layer(h, p_rows, g_ple, w_gate, w_proj, *, tm, final=None):
    t, d = h.shape
    extra = {}
    if final is not None:
        extra = dict(post=_post_rms, post_consts=[final[0].reshape(1, d)], out_rows=final[1:])
    return _mm_resident([h], [g_ple.reshape(1, d)], [p_rows.astype(BF16)], [w_gate.astype(BF16), w_proj.astype(BF16)],
                        [0, 1], [], prologue=_pro_rms, epilogue=_epi_ple, out_dtype=F32, tm=tm, name="ple_gate",
                        res_from_x=True, **extra)


def kernel(x_prompt, x_sample, state_conv, state_ssm, cache_kv_latent, cache_k_rope, p_prompt, p_sample,
           g_mix, w_ssm_in, w_conv, b_conv, dt_bias, a_log, d_skip, g_ssm_norm, w_ssm_out,
           g_kv_in, w_dkv, g_kv, w_uk, w_uv, w_dq, g_q, w_uq, w_o,
           g_ffn, w_router_grp, b_router_grp, w_router_exp, b_router_exp, w_exp_gate, w_exp_up, w_exp_down,
           g_ple, w_ple_gate, w_ple_proj, g_final):
    bp, lp_each, d = x_prompt.shape
    bs, ls, _ = x_sample.shape
    past = cache_kv_latent.shape[1]
    assert bp == 1 and w_ssm_in.shape[0] == 1 and w_dq.shape[0] == 1 and d == D_MODEL
    assert ls >= CONV_W - 1 and ls % SUBLANES == 0 and past % CHUNK == 0
    lp = bp * lp_each
    ts = bs * ls
    t = lp + ts
    tm = min(512, math.gcd(lp, ts))
    g_, r_, n_, hd = SSM_GROUPS, HEADS_PER_GROUP, SSM_STATE, SSM_HEAD_DIM

    h = jnp.concatenate([x_prompt.reshape(lp, d), x_sample.reshape(ts, d)], axis=0)

    w_in = w_ssm_in[0]
    w_z = w_in[:, :D_INNER].astype(BF16)
    xw = SSM_IN_XW
    w_x = jnp.pad(w_in[:, D_INNER:], ((0, 0), (0, xw - CONV_DIM - SSM_HEADS))).astype(BF16)
    g0 = g_mix[0].reshape(1, d)
    tm_in = max(c for c in range(16, 1153, 16) if t % c == 0)
    mm_in = functools.partial(_mm_rows, [h], [g0], prologue=_pro_rms, tm=tm_in)
    z = mm_in(w_z, [], epilogue=_epi_silu, out_dtype=BF16, tn=1024, name="ssm_in_z")[0]
    xbc = mm_in(w_x, [], epilogue=_epi_id, out_dtype=F32, tn=xw // 5, name="ssm_in_xbc")[0]
    dt_raw = xbc[:, CONV_DIM:CONV_DIM + SSM_HEADS]

    wc = w_conv[0]
    bc = b_conv[0].reshape(1, CONV_DIM)
    st_p = jnp.zeros((1, SUBLANES, CONV_DIM), F32)
    st_s = jnp.pad(state_conv[0].astype(F32), ((0, 0), (SUBLANES - (CONV_W - 1), 0), (0, 0)))
    conv_p = xbc[lp - (CONV_W - 1):lp, :CONV_DIM].reshape(1, 1, CONV_W - 1, CONV_DIM)
    conv_s = xbc[lp:, :CONV_DIM].reshape(bs, ls, CONV_DIM)[:, ls - (CONV_W - 1):].reshape(
        1, bs, CONV_W - 1, CONV_DIM)

    bias = dt_bias[0].astype(F32).reshape(g_, 1, r_)
    a_neg = (-jnp.exp(a_log[0].astype(F32)) * math.log2(math.e)).reshape(g_, 1, r_)
    dsk = jnp.repeat(d_skip[0].astype(F32), hd).reshape(1, D_INNER)

    def dt_views(rows, nb, L):
        v = rows.reshape(nb, L, g_, r_).transpose(0, 2, 1, 3)
        return v, v.transpose(0, 1, 3, 2)

    def scan(xbc3, batch0, nb, L, st, rows, s0, q):
        dtg, dtg_t = dt_views(rows, nb, L)
        return _ssd_scan(xbc3, batch0, nb, L, st, wc, bc, dtg, dtg_t, bias, bias.transpose(0, 2, 1),
                         a_neg, a_neg.transpose(0, 2, 1), dsk, s0, q=q)

    def state_in(s):
        nb = s.shape[0]
        return s.astype(F32).reshape(nb, g_, r_ * hd, n_).transpose(0, 1, 3, 2)

    def state_out(s):
        nb = s.shape[0]
        return s.transpose(0, 1, 3, 2).reshape(1, nb, SSM_HEADS, hd, n_)

    y_p, s_p = scan(xbc.reshape(1, t, xw), 0, 1, lp, st_p, dt_raw[:lp],
                    jnp.zeros((1, g_, n_, r_ * hd), F32), min(SSD_Q, lp))
    y_s, s_s = scan(xbc.reshape(t // ls, ls, xw), lp // ls, bs, ls, st_s, dt_raw[lp:],
                    state_in(state_ssm[0]), ls)
    ssm_p, ssm_s = state_out(s_p), state_out(s_s)
    y_all = (y_p.reshape(lp, D_INNER), y_s.reshape(ts, D_INNER))

    h = _mm_resident([y_all, z], [g_ssm_norm[0].reshape(1, D_INNER)], [], [w_ssm_out[0].astype(BF16)], [0], [h],
                     prologue=_pro_gated, epilogue=_epi_res, out_dtype=F32, tm=tm // 2, name="ssm_out")

    def ffn_and_ple(h, i, final=None):
        h = _moe_layer(h, g_ffn[i], w_router_grp[i], b_router_grp[i], w_router_exp[i], b_router_exp[i],
                       w_exp_gate, w_exp_up, w_exp_down, tm=tm, layer=i)
        p_rows = jnp.concatenate([p_prompt[i].reshape(lp, -1), p_sample[i].reshape(ts, -1)], axis=0)
        return _ple_layer(h, p_rows, g_ple[i], w_ple_gate[i], w_ple_proj[i], tm=tm, final=final)

    h = ffn_and_ple(h, 0)

    pos = np.concatenate([np.arange(lp), np.tile(past + np.arange(ls), bs)])
    cos64, sin64 = _rope_tables(pos)
    gk = g_kv_in.reshape(1, d)
    w_r = w_dkv[:, KV_RANK:]
    row64 = pl.BlockSpec((tm, QK_ROPE), lambda i, j: (i, 0))
    w64 = pl.BlockSpec((d, QK_ROPE), lambda i, j: (0, 0))
    whole = lambda a: pl.BlockSpec(a.shape, lambda i, j: (0, 0))
    w_lat = w_dkv[:, :KV_RANK].astype(BF16)
    g_kv_row = g_kv.reshape(1, KV_RANK)
    ckv, kr = _fused_mm(
        [(h, pl.BlockSpec((tm, d), lambda i, j: (i, 0))), (gk, whole(gk))],
        [(w_lat, whole(w_lat)), (w_r.astype(BF16), w64), (_swap_rope_halves(w_r).astype(BF16), w64)],
        [(g_kv_row, whole(g_kv_row)), (jnp.asarray(cos64), row64), (jnp.asarray(sin64), row64)],
        [(jax.ShapeDtypeStruct((t, KV_RANK), F32), pl.BlockSpec((tm, KV_RANK), lambda i, j: (i, 0))),
         (jax.ShapeDtypeStruct((t, QK_ROPE), F32), row64)],
        grid=(t // tm, 1), prologue=_pro_rms, epilogue=_epi_latent_kv, xn_shape=(tm, d), name="kv_latent")

    ql = _mm_rows([h], [g_mix[1].reshape(1, d)], w_dq[0].astype(BF16), [], prologue=_pro_rms,
                  epilogue=_epi_rms_out, out_dtype=BF16, tm=tm, tn=Q_RANK, name="q_latent",
                  e_consts=[g_q[0].reshape(1, Q_RANK)])[0]
    nh = MLA_HEADS
    w_q = w_uq[0].reshape(Q_RANK, nh, QK_DIM)
    w_q_rope = w_q[..., QK_NOPE:]
    cos_q = np.concatenate([cos64, cos64], axis=1) * np.float32(QK_PRESCALE)
    sin_q = np.concatenate([sin64, sin64], axis=1) * np.float32(QK_PRESCALE)
    q_cat = _q_expand(ql, w_q[..., :QK_NOPE].reshape(Q_RANK, nh * QK_NOPE).astype(BF16),
                      w_q_rope.reshape(Q_RANK, nh * QK_ROPE).astype(BF16),
                      _swap_rope_halves(w_q_rope).reshape(Q_RANK, nh * QK_ROPE).astype(BF16),
                      jnp.asarray(cos_q), jnp.asarray(sin_q), tm=tm)

    lat_pad = LAT_PAD - KV_RANK - QK_ROPE
    w_uk_h = w_uk.transpose(1, 0, 2)
    w_uv_h = w_uv.transpose(1, 0, 2)
    eye_r = jnp.broadcast_to(jnp.eye(QK_ROPE, dtype=F32), (nh, QK_ROPE, QK_ROPE))
    k_cat, v_h = _kv_expand(ckv, kr, w_uk.reshape(KV_RANK, nh * QK_NOPE).astype(BF16),
                            w_uv.reshape(KV_RANK, nh * V_DIM).astype(BF16), lp, tm=tm)
    o_p = _flash_prompt(q_cat, k_cat, v_h, lp, tq=min(512, lp), hb=16)

    w_abs = jnp.concatenate([
        jnp.concatenate([w_uk_h.transpose(0, 2, 1), jnp.zeros((nh, QK_NOPE, LAT_PAD - KV_RANK), F32)], axis=2),
        jnp.concatenate([jnp.zeros((nh, QK_ROPE, KV_RANK), F32), eye_r,
                         jnp.zeros((nh, QK_ROPE, lat_pad), F32)], axis=2)], axis=1).astype(BF16)
    blk_s = lp // ts if lp % ts == 0 else None
    assert blk_s is not None
    q_abs = _fused_mm(
        [(q_cat, pl.BlockSpec((None, ts, QK_DIM), lambda i, j: (j, blk_s + i, 0)))],
        [(w_abs, pl.BlockSpec((None, QK_DIM, LAT_PAD), lambda i, j: (j, 0, 0)))],
        [],
        [(jax.ShapeDtypeStruct((nh, ts, LAT_PAD), BF16), pl.BlockSpec((None, ts, LAT_PAD), lambda i, j: (j, i, 0)))],
        grid=(1, nh), prologue=_pro_cast, epilogue=_epi_id, x_per_j=True, name="q_absorb")[0]
    o_lat = _attn_sample(q_abs, cache_kv_latent, cache_k_rope, ckv[lp:].reshape(bs, ls, KV_RANK),
                         kr[lp:].reshape(bs, ls, QK_ROPE), ls=ls)
    o_s = _fused_mm(
        [(o_lat, pl.BlockSpec((None, ts, KV_RANK), lambda i, j: (j, i, 0)))],
        [(w_uv_h.astype(BF16), pl.BlockSpec((None, KV_RANK, V_DIM), lambda i, j: (j, 0, 0)))],
        [],
        [(jax.ShapeDtypeStruct((ts, nh * V_DIM), BF16), pl.BlockSpec((ts, V_DIM), lambda i, j: (i, j)))],
        grid=(1, nh), prologue=_pro_cast, epilogue=_epi_id, x_per_j=True, name="v_absorb")[0]
    h = _mm_resident([(o_p, o_s)], [], [], [w_o[0].astype(BF16)], [0], [h], prologue=_pro_cast, epilogue=_epi_res,
                     out_dtype=F32, tm=tm, name="attn_out")
    y_p_out, y_s_out = ffn_and_ple(h, 1, final=(g_final, lp, ts))
    y_p_out = y_p_out.reshape(bp, lp_each, d)
    y_s_out = y_s_out.reshape(bs, ls, d)
    return (y_p_out, y_s_out, conv_p, ssm_p, ckv[:lp].reshape(bp, lp_each, KV_RANK),
            kr[:lp].reshape(bp, lp_each, QK_ROPE), conv_s, ssm_s, ckv[lp:].reshape(bs, ls, KV_RANK),
            kr[lp:].reshape(bs, ls, QK_ROPE))
```
